```python
import jax, jax.numpy as jnp
from jax import lax
import numpy as np

D_MODEL = 2048
BATCH = 8
SEQ = 2048
DEPTH = 2

MIX_WIDTH = D_MODEL
RET_HEADS = 4
RET_DK = MIX_WIDTH // 16
RET_DV = MIX_WIDTH // 16
RET_WIDTH = RET_HEADS * RET_DV
RET_CHUNK = 128
ROPE_THETA = 10000.0
SSD_WIDTH = MIX_WIDTH // 2
SSD_HEADDIM = 64
SSD_HEADS = SSD_WIDTH // SSD_HEADDIM
SSD_STATE = 128
SSD_GROUPS = 2
SSD_CONV = 4
SSD_CONV_DIM = SSD_WIDTH + 2 * SSD_GROUPS * SSD_STATE
SSD_CHUNK = 128
HGRN_WIDTH = MIX_WIDTH // 4
HGRN_HEADS = 4
HGRN_EXPAND = 128
HGRN_DV = HGRN_WIDTH // HGRN_HEADS
HGRN_FDIM = HGRN_HEADS * HGRN_EXPAND
HGRN_CHUNK = 64
LB_FLOOR = 1e-30
IN_SPLITS = (RET_HEADS * RET_DK, RET_HEADS * RET_DK, RET_WIDTH, RET_WIDTH,
             SSD_WIDTH, SSD_CONV_DIM, SSD_HEADS,
             HGRN_FDIM, HGRN_FDIM, HGRN_WIDTH, HGRN_WIDTH)
IN_DIM = sum(IN_SPLITS)
N_GROUPS = 4
EXPERTS_PER_GROUP = 8
N_EXPERTS = N_GROUPS * EXPERTS_PER_GROUP
TOP_K = 2
D_EXPERT = D_MODEL // 4
MOE_BLOCK = 128
EPS = 1e-6

kernel_name = "hymba_style_ret_ssd_hgrn2_hmoe"


def rms_norm(x, w):
    x32 = x.astype(jnp.float32)
    y = x32 * lax.rsqrt(jnp.mean(x32 * x32, axis=-1, keepdims=True) + EPS)
    return (y * w.astype(jnp.float32)).astype(x.dtype)


def rms_norm_plain(x):
    x32 = x.astype(jnp.float32)
    return (x32 * lax.rsqrt(jnp.mean(x32 * x32, axis=-1, keepdims=True) + EPS)).astype(x.dtype)


def rotary(x):
    t, dk = x.shape[1], x.shape[-1]
    half = dk // 2
    inv = 1.0 / (ROPE_THETA ** (jnp.arange(half, dtype=jnp.float32) / half))
    ang = jnp.arange(t, dtype=jnp.float32)[:, None] * inv[None, :]
    cos, sin = jnp.cos(ang)[:, None, :], jnp.sin(ang)[:, None, :]
    x32 = x.astype(jnp.float32)
    x1, x2 = x32[..., :half], x32[..., half:]
    return jnp.concatenate([x1 * cos - x2 * sin, x1 * sin + x2 * cos], axis=-1).astype(x.dtype)


def chunked_recurrence(q, k, v, log_a, chunk):
    bsz, t, h, dk = q.shape
    dv = v.shape[-1]
    n = t // chunk

    def blocks(a):
        return a.reshape(bsz, n, chunk, h, a.shape[-1]).transpose(1, 0, 3, 2, 4).astype(jnp.float32)

    causal = jnp.tril(jnp.ones((chunk, chunk), dtype=bool))[:, :, None]
    vector_decay = log_a.shape[-1] > 1

    def step(state, inp):
        qc, kc, vc, lc = inp
        cum = jnp.cumsum(lc, axis=2)
        last = cum[:, :, -1:, :]
        diff = cum[:, :, :, None, :] - cum[:, :, None, :, :]
        decay = jnp.where(causal, jnp.exp(jnp.where(causal, diff, 0.0)), 0.0)
        if vector_decay:
            scores = jnp.einsum('bhik,bhjk,bhijk->bhij', qc, kc, decay)
        else:
            scores = jnp.einsum('bhik,bhjk->bhij', qc, kc) * decay[..., 0]
        out = (jnp.einsum('bhij,bhjv->bhiv', scores, vc)
               + jnp.einsum('bhik,bhkv->bhiv', qc * jnp.exp(cum), state))
        state = (jnp.swapaxes(jnp.exp(last), 2, 3) * state
                 + jnp.einsum('bhjk,bhjv->bhkv', kc * jnp.exp(last - cum), vc))
        return state, out

    s0 = jnp.zeros((bsz, h, dk, dv), jnp.float32)
    _, out = lax.scan(step, s0, (blocks(q), blocks(k), blocks(v), blocks(log_a)))
    return out.transpose(1, 0, 3, 2, 4).reshape(bsz, t, h, dv).astype(v.dtype)


def retention(q, k, v, g):
    bsz, t, _ = q.shape
    q = rotary(q.reshape(bsz, t, RET_HEADS, RET_DK))
    k = rotary(k.reshape(bsz, t, RET_HEADS, RET_DK)) * (RET_DK ** -0.5)
    v = v.reshape(bsz, t, RET_HEADS, RET_DV)
    log_gamma = jnp.log(1.0 - 2.0 ** (-5.0 - jnp.arange(RET_HEADS, dtype=jnp.float32)))
    log_a = jnp.broadcast_to(log_gamma[:, None], (bsz, t, RET_HEADS, 1))
    o = chunked_recurrence(q, k, v, log_a, RET_CHUNK)
    o = rms_norm_plain(o).reshape(bsz, t, RET_WIDTH)
    return jax.nn.silu(g) * o


def ssd(z, xbc, dt_raw, conv_w, conv_b, dt_bias, a_log, d_skip, norm_w):
    bsz, t, _ = z.shape
    xbc = lax.conv_general_dilated(xbc, conv_w[:, None, :].astype(xbc.dtype), window_strides=(1,),
                                   padding=[(SSD_CONV - 1, 0)],
                                   dimension_numbers=('NWC', 'WIO', 'NWC'),
                                   feature_group_count=SSD_CONV_DIM)
    xbc = jax.nn.silu(xbc + conv_b)
    xs, bm, cm = jnp.split(xbc, [SSD_WIDTH, SSD_WIDTH + SSD_GROUPS * SSD_STATE], axis=-1)
    rep = SSD_HEADS // SSD_GROUPS
    xs = xs.reshape(bsz, t, SSD_HEADS, SSD_HEADDIM)
    bm = jnp.repeat(bm.reshape(bsz, t, SSD_GROUPS, SSD_STATE), rep, axis=2)
    cm = jnp.repeat(cm.reshape(bsz, t, SSD_GROUPS, SSD_STATE), rep, axis=2)
    dt = jax.nn.softplus(dt_raw.astype(jnp.float32) + dt_bias.astype(jnp.float32))
    log_a = (dt * -jnp.exp(a_log.astype(jnp.float32)))[..., None]
    y = chunked_recurrence(cm, bm * dt[..., None].astype(bm.dtype), xs, log_a, SSD_CHUNK)
    y = y + d_skip[:, None] * xs
    y = (y.reshape(bsz, t, SSD_WIDTH) * jax.nn.silu(z)).reshape(bsz, t, SSD_GROUPS, SSD_WIDTH // SSD_GROUPS)
    return rms_norm_plain(y).reshape(bsz, t, SSD_WIDTH) * norm_w


def hgrn2(q, f, i, g, lb, norm_w):
    bsz, t, _ = q.shape
    q = jax.nn.silu(q).reshape(bsz, t, HGRN_HEADS, HGRN_EXPAND) * (HGRN_EXPAND ** -0.5)
    f32 = f.astype(jnp.float32)
    log_lb = jnp.log(jnp.maximum(lb, LB_FLOOR))
    log_f = jnp.logaddexp(log_lb, jnp.log1p(-lb) + jax.nn.log_sigmoid(f32))
    k = (1.0 - lb) * jax.nn.sigmoid(-f32)
    log_f = log_f.reshape(bsz, t, HGRN_HEADS, HGRN_EXPAND)
    k = k.reshape(bsz, t, HGRN_HEADS, HGRN_EXPAND).astype(q.dtype)
    v = i.reshape(bsz, t, HGRN_HEADS, HGRN_DV)
    o = chunked_recurrence(q, k, v, log_f, HGRN_CHUNK)
    o = rms_norm_plain(o).reshape(bsz, t, HGRN_WIDTH) * norm_w
    return jax.nn.silu(g) * o


def hier_moe(u, wg, bg, we, be, w_gate, w_up, w_down):
    bsz, t, d = u.shape
    n = bsz * t
    uf = u.reshape(n, d)
    p_group = jax.nn.softmax((uf @ wg + bg).astype(jnp.float32), axis=-1)
    g_idx = jnp.argmax(p_group, axis=-1)
    g_gate = jnp.max(p_group, axis=-1)
    e_logits = (uf @ we + be).astype(jnp.float32).reshape(n, N_GROUPS, EXPERTS_PER_GROUP)
    e_logits = jnp.take_along_axis(e_logits, g_idx[:, None, None], axis=1)[:, 0]
    top_logits, top_idx = lax.top_k(e_logits, TOP_K)
    top_w = jax.nn.softmax(top_logits, axis=-1) * g_gate[:, None]
    expert_ids = (g_idx[:, None] * EXPERTS_PER_GROUP + top_idx).reshape(-1).astype(jnp.int32)
    token_ids = jnp.repeat(jnp.arange(n, dtype=jnp.int32), TOP_K)
    slot_w = top_w.reshape(-1)
    s = n * TOP_K
    order = jnp.argsort(expert_ids)
    sorted_e = expert_ids[order]
    counts = jnp.bincount(expert_ids, length=N_EXPERTS)
    padded = ((counts + MOE_BLOCK - 1) // MOE_BLOCK) * MOE_BLOCK
    starts = jnp.cumsum(counts) - counts
    pends = jnp.cumsum(padded)
    pstarts = pends - padded
    dest = pstarts[sorted_e] + (jnp.arange(s, dtype=jnp.int32) - starts[sorted_e])
    p_total = s + N_EXPERTS * MOE_BLOCK
    nblk = p_total // MOE_BLOCK
    buf_tok = jnp.zeros((p_total,), jnp.int32).at[dest].set(token_ids[order])
    buf_w = jnp.zeros((p_total,), jnp.float32).at[dest].set(slot_w[order])
    blk_e = jnp.clip(jnp.searchsorted(pends, jnp.arange(nblk) * MOE_BLOCK, side='right'), 0, N_EXPERTS - 1)

    def run_block(args):
        tok, e = args
        xb = uf[tok]
        hdn = jax.nn.silu(xb @ w_gate[e]) * (xb @ w_up[e])
        return hdn @ w_down[e]

    y = lax.map(run_block, (buf_tok.reshape(nblk, MOE_BLOCK), blk_e))
    y = y.reshape(p_total, d).astype(jnp.float32) * buf_w[:, None]
    out = jnp.zeros((n, d), jnp.float32).at[buf_tok].add(y)
    return out.astype(u.dtype).reshape(bsz, t, d)


def setup_inputs(seed: int = 0) -> dict:
    key = jax.random.key(seed)
    ks = jax.random.split(key, 24)
    nrm = lambda k, shape, scale: jax.random.normal(k, shape, jnp.float32) * scale
    dt = jnp.exp(jax.random.uniform(ks[5], (DEPTH, SSD_HEADS), jnp.float32)
                 * (jnp.log(0.1) - jnp.log(0.001)) + jnp.log(0.001))
    return {
        "x": nrm(ks[0], (BATCH, SEQ, D_MODEL), 1.0),
        "attn_norm_w": 1.0 + nrm(ks[1], (DEPTH, D_MODEL), 0.02),
        "w_in": nrm(ks[2], (DEPTH, D_MODEL, IN_DIM), D_MODEL ** -0.5),
        "ssd_conv_w": nrm(ks[3], (DEPTH, SSD_CONV, SSD_CONV_DIM), SSD_CONV ** -0.5),
        "ssd_conv_b": nrm(ks[4], (DEPTH, SSD_CONV_DIM), 0.02),
        "ssd_dt_bias": dt + jnp.log(-jnp.expm1(-dt)),
        "ssd_a_log": jnp.log(jax.random.uniform(ks[6], (DEPTH, SSD_HEADS), jnp.float32, 1.0, 16.0)),
        "ssd_d": 1.0 + nrm(ks[7], (DEPTH, SSD_HEADS), 0.1),
        "ssd_norm_w": 1.0 + nrm(ks[8], (DEPTH, SSD_WIDTH), 0.02),
        "hgrn_lower_bounds": 1.0 + nrm(ks[9], (DEPTH, HGRN_FDIM), 0.1),
        "hgrn_norm_w": 1.0 + nrm(ks[10], (DEPTH, HGRN_WIDTH), 0.02),
        "w_out": nrm(ks[11], (DEPTH, MIX_WIDTH, D_MODEL), MIX_WIDTH ** -0.5),
        "ffn_norm_w": 1.0 + nrm(ks[12], (DEPTH, D_MODEL), 0.02),
        "router_group_w": nrm(ks[13], (DEPTH, D_MODEL, N_GROUPS), D_MODEL ** -0.5),
        "router_group_b": nrm(ks[14], (DEPTH, N_GROUPS), 0.01),
        "router_expert_w": nrm(ks[15], (DEPTH, D_MODEL, N_EXPERTS), D_MODEL ** -0.5),
        "router_expert_b": nrm(ks[16], (DEPTH, N_EXPERTS), 0.01),
        "expert_w_gate": nrm(ks[17], (DEPTH, N_EXPERTS, D_MODEL, D_EXPERT), D_MODEL ** -0.5),
        "expert_w_up": nrm(ks[18], (DEPTH, N_EXPERTS, D_MODEL, D_EXPERT), D_MODEL ** -0.5),
        "expert_w_down": nrm(ks[19], (DEPTH, N_EXPERTS, D_EXPERT, D_MODEL), D_EXPERT ** -0.5),
        "final_norm_w": 1.0 + nrm(ks[20], (D_MODEL,), 0.02),
    }


def reference(x, attn_norm_w, w_in, ssd_conv_w, ssd_conv_b, ssd_dt_bias, ssd_a_log, ssd_d,
              ssd_norm_w, hgrn_lower_bounds, hgrn_norm_w, w_out, ffn_norm_w,
              router_group_w, router_group_b, router_expert_w, router_expert_b,
              expert_w_gate, expert_w_up, expert_w_down, final_norm_w):
    lb_sm = jax.nn.softmax(hgrn_lower_bounds.astype(jnp.float32), axis=0)
    lbs = jnp.maximum(jnp.cumsum(lb_sm, axis=0) - lb_sm[0], 0.0)
    split_points = [int(p) for p in np.cumsum(IN_SPLITS)[:-1]]
    h = x
    for l in range(DEPTH):
        u = rms_norm(h, attn_norm_w[l])
        p = u @ w_in[l]
        (rq, rk, rv, rg, sz, sxbc, sdt, hq, hf, hi, hg) = jnp.split(p, split_points, axis=-1)
        o_ret = retention(rq, rk, rv, rg)
        o_ssd = ssd(sz, sxbc, sdt, ssd_conv_w[l], ssd_conv_b[l], ssd_dt_bias[l],
                    ssd_a_log[l], ssd_d[l], ssd_norm_w[l])
        o_hgrn = hgrn2(hq, hf, hi, hg, lbs[l], hgrn_norm_w[l])
        h = h + jnp.concatenate([o_ret, o_ssd, o_hgrn], axis=-1) @ w_out[l]
        u = rms_norm(h, ffn_norm_w[l])
        h = h + hier_moe(u, router_group_w[l], router_group_b[l], router_expert_w[l],
                         router_expert_b[l], expert_w_gate[l], expert_w_up[l], expert_w_down[l])
    return rms_norm(h, final_norm_w)
```

```python
import functools
import math

import jax
import jax.numpy as jnp
from jax import lax
from jax.experimental import pallas as pl
from jax.experimental.pallas import tpu as pltpu

F32 = jnp.float32
BF16 = jnp.bfloat16
HIGHEST = lax.Precision.HIGHEST

V7X_VMEM_BYTES = 64 * 1024 * 1024
VMEM_LIMIT = V7X_VMEM_BYTES - 8 * 1024 * 1024
LANE = 128

EPS = 1e-6
LB_FLOOR = 1e-30
ROPE_THETA = 10000.0

RET_HEADS = 4
HEAD_DIM = 128
SSD_HEADDIM = 64
SSD_GROUPS = 2
SSD_HEADS_PER_GROUP = 8
SSD_CONV = 4
HGRN_HEADS = 4
CHUNK = 128
N_GROUPS = 4
EXPERTS_PER_GROUP = 8
N_EXPERTS = N_GROUPS * EXPERTS_PER_GROUP
MOE_BLOCK = 128
ROUTE_TILE = 512


def _cparams(*sem):
    return pltpu.CompilerParams(dimension_semantics=sem, vmem_limit_bytes=VMEM_LIMIT)


def _rms(x, w=None):
    y = x * lax.rsqrt(jnp.mean(x * x, axis=-1, keepdims=True) + EPS)
    return y if w is None else y * w


def _sigmoid(x):
    return 1.0 / (1.0 + jnp.exp(-x))


def _silu(x):
    return x * _sigmoid(x)


def _dot(a, b):
    return jnp.dot(a, b, preferred_element_type=F32)


def _dot_nt(a, b):
    return lax.dot_general(a, b, (((1,), (1,)), ((), ())), preferred_element_type=F32)


def _dot_tn(a, b):
    return lax.dot_general(a, b, (((0,), (0,)), ((), ())), preferred_element_type=F32)


def _tri_incl():
    ii = lax.broadcasted_iota(jnp.int32, (CHUNK, CHUNK), 0)
    jj = lax.broadcasted_iota(jnp.int32, (CHUNK, CHUNK), 1)
    return ii, jj


def _inproj_kernel(*refs, combine):
    if combine:
        h_ref, o0_ref, o1_ref, rt_ref, nw_ref, w_ref, wdt_ref, p_ref, pdt_ref, hout_ref, u_scr = refs
    else:
        h_ref, nw_ref, w_ref, wdt_ref, p_ref, pdt_ref, u_scr = refs

    @pl.when(pl.program_id(1) == 0)
    def _():
        h = h_ref[...]
        if combine:
            rt = rt_ref[...]
            h = h + rt[:, 2:3] * o0_ref[...] + rt[:, 3:4] * o1_ref[...]
            hout_ref[...] = h
        u = _rms(h, nw_ref[...]).astype(BF16)
        u_scr[...] = u
        pdt_ref[...] = _dot(u, wdt_ref[...])

    p_ref[...] = _dot(u_scr[...], w_ref[...])


def _inproj(h, nw, w_main, w_dt, moe=None):
    n, d = h.shape
    np_ = w_main.shape[1]
    combine = moe is not None
    tm = min(512 if combine else 1024, n)
    tn = 512
    row = lambda i, j: (i, 0)
    in_specs = [pl.BlockSpec((tm, d), row)]
    args = [h]
    if combine:
        o0, o1, rt = moe
        in_specs += [pl.BlockSpec((tm, d), row), pl.BlockSpec((tm, d), row), pl.BlockSpec((tm, LANE), row)]
        args += [o0, o1, rt]
    in_specs += [pl.BlockSpec((1, d), lambda i, j: (0, 0)),
                 pl.BlockSpec((d, tn), lambda i, j: (0, j)),
                 pl.BlockSpec((d, w_dt.shape[1]), lambda i, j: (0, 0))]
    args += [nw, w_main, w_dt]
    out_shape = [jax.ShapeDtypeStruct((n, np_), F32), jax.ShapeDtypeStruct((n, w_dt.shape[1]), F32)]
    out_specs = [pl.BlockSpec((tm, tn), lambda i, j: (i, j)), pl.BlockSpec((tm, w_dt.shape[1]), row)]
    if combine:
        out_shape.append(jax.ShapeDtypeStruct((n, d), F32))
        out_specs.append(pl.BlockSpec((tm, d), row))
    res = pl.pallas_call(
        functools.partial(_inproj_kernel, combine=combine),
        grid=(n // tm, np_ // tn),
        in_specs=in_specs, out_specs=out_specs, out_shape=out_shape,
        scratch_shapes=[pltpu.VMEM((tm, d), BF16)],
        compiler_params=_cparams("arbitrary", "arbitrary"),
        name="inproj",
    )(*args)
    return res


def _ret_kernel(q_ref, k_ref, v_ref, g_ref, cos_ref, sin_ref, o_ref, s_scr):
    t = q_ref.shape[0]
    hf = pl.program_id(1).astype(F32)
    lg = jnp.log(1.0 - jnp.exp2(jnp.full((1, LANE), -5.0, F32) - hf))
    ii, jj = _tri_incl()
    causal = ii >= jj
    dmat = jnp.where(causal, jnp.exp(jnp.where(causal, (ii - jj).astype(F32) * lg, 0.0)), 0.0)
    iif = ii.astype(F32)
    ecum = jnp.exp((iif + 1.0) * lg)
    wk = jnp.exp((CHUNK - 1.0 - iif) * lg)
    elast = jnp.exp(float(CHUNK) * lg)
    scale = HEAD_DIM ** -0.5
    s_scr[...] = jnp.zeros_like(s_scr)

    def step(c, carry):
        r = pl.ds(pl.multiple_of(c * CHUNK, CHUNK), CHUNK)
        cs, sn = cos_ref[r, :], sin_ref[r, :]
        q, k = q_ref[r, :], k_ref[r, :]
        qr = q * cs + pltpu.roll(q, HEAD_DIM // 2, 1) * sn
        kr = (k * cs + pltpu.roll(k, HEAD_DIM // 2, 1) * sn) * scale
        vb = v_ref[r, :].astype(BF16)
        s = s_scr[...]
        scores = _dot_nt(qr.astype(BF16), kr.astype(BF16)) * dmat
        out = _dot(scores.astype(BF16), vb) + _dot((qr * ecum).astype(BF16), s.astype(BF16))
        s_scr[...] = elast * s + _dot_tn((kr * wk).astype(BF16), vb)
        o_ref[r, :] = (_silu(g_ref[r, :]) * _rms(out)).astype(o_ref.dtype)
        return carry

    lax.fori_loop(0, t // CHUNK, step, 0)


def _retention(p, cos_t, sin_t, bsz, t):
    n = bsz * t
    blk = lambda off: pl.BlockSpec((t, HEAD_DIM), lambda b, h, off=off: (b, off + h))
    tab = pl.BlockSpec((t, HEAD_DIM), lambda b, h: (0, 0))
    return pl.pallas_call(
        _ret_kernel,
        grid=(bsz, RET_HEADS),
        in_specs=[blk(0), blk(RET_HEADS), blk(2 * RET_HEADS), blk(3 * RET_HEADS), tab, tab],
        out_specs=pl.BlockSpec((t, HEAD_DIM), lambda b, h: (b, h)),
        out_shape=jax.ShapeDtypeStruct((n, RET_HEADS * HEAD_DIM), BF16),
        scratch_shapes=[pltpu.VMEM((HEAD_DIM, HEAD_DIM), F32)],
        compiler_params=_cparams("arbitrary", "arbitrary"),
        name="retention",
    )(p, p, p, p, cos_t, sin_t)


def _ssd_kernel(z_ref, x_ref, b_ref, c_ref, dt_ref, cwx_ref, cwb_ref, cwc_ref, cbx_ref, cbb_ref, cbc_ref,
                dtb_ref, alog_ref, dsk_ref, nw_ref, o_ref, s_scr):
    t = z_ref.shape[0]
    gw = x_ref.shape[1]
    ii, jj = _tri_incl()
    causal = ii >= jj
    tri = jnp.where(causal, 1.0, 0.0).astype(F32)
    lane_lo = lax.broadcasted_iota(jnp.int32, (CHUNK, LANE), 1) < SSD_HEADDIM
    neg_a = -jnp.exp(alog_ref[...])
    s_scr[...] = jnp.zeros_like(s_scr)

    def conv(ref, w_ref, bias_ref, c):
        r = pl.ds(pl.multiple_of(c * CHUNK, CHUNK), CHUNK)
        rp = pl.ds(pl.multiple_of(jnp.maximum(c * CHUNK - 8, 0), 8), 8)
        prev = jnp.where(c > 0, ref[rp, :], 0.0)
        ext = jnp.concatenate([prev, ref[r, :]], axis=0)
        w = w_ref[...]
        acc = bias_ref[...]
        for i in range(SSD_CONV):
            off = 8 - (SSD_CONV - 1) + i
            acc = acc + w[i:i + 1, :] * ext[off:off + CHUNK, :]
        return _silu(acc)

    def step(c, carry):
        r = pl.ds(pl.multiple_of(c * CHUNK, CHUNK), CHUNK)
        xs = conv(x_ref, cwx_ref, cbx_ref, c)
        bm = conv(b_ref, cwb_ref, cbb_ref, c).astype(BF16)
        cm = conv(c_ref, cwc_ref, cbc_ref, c).astype(BF16)
        xr = dt_ref[r, :] + dtb_ref[...]
        dt = jnp.maximum(xr, 0.0) + jnp.log1p(jnp.exp(-jnp.abs(xr)))
        la = dt * neg_a
        cum = jnp.dot(tri, la, precision=HIGHEST, preferred_element_type=F32)
        last = cum[CHUNK - 1:CHUNK, :]
        wj = dt * jnp.exp(last - cum)
        ecum = jnp.exp(cum)
        cum_t = cum.T
        dt_t = dt.T
        g = _dot_nt(cm, bm)
        s = s_scr[...]
        cs = _dot(cm, s.astype(BF16))
        ys, xws, els = [], [], []
        for pr in range(SSD_HEADS_PER_GROUP // 2):
            xp = xs[:, pr * LANE:(pr + 1) * LANE]
            acc = None
            ecp, wjp, elp = None, None, None
            for half in range(2):
                hd = 2 * pr + half
                ci = jnp.broadcast_to(cum[:, hd:hd + 1], (CHUNK, CHUNK))
                dec = jnp.exp(jnp.where(causal, ci - cum_t[hd:hd + 1, :], 0.0))
                m = jnp.where(causal, g * dec * dt_t[hd:hd + 1, :], 0.0)
                sel = lane_lo if half == 0 else jnp.logical_not(lane_lo)
                y = _dot(m.astype(BF16), jnp.where(sel, xp, 0.0).astype(BF16))
                acc = y if acc is None else acc + y
                eb = jnp.broadcast_to(ecum[:, hd:hd + 1], (CHUNK, LANE))
                wb = jnp.broadcast_to(wj[:, hd:hd + 1], (CHUNK, LANE))
                lb = jnp.broadcast_to(ecum[CHUNK - 1:CHUNK, hd:hd + 1], (1, LANE))
                ecp = eb if half == 0 else jnp.where(lane_lo, ecp, eb)
                wjp = wb if half == 0 else jnp.where(lane_lo, wjp, wb)
                elp = lb if half == 0 else jnp.where(lane_lo[0:1, :], elp, lb)
            ys.append(acc + ecp * cs[:, pr * LANE:(pr + 1) * LANE])
            xws.append((wjp * xp).astype(BF16))
            els.append(elp)
        y = jnp.concatenate(ys, axis=1)
        s_scr[...] = jnp.concatenate(els, axis=1) * s + _dot_tn(bm, jnp.concatenate(xws, axis=1))
        y = (y + dsk_ref[...] * xs) * _silu(z_ref[r, :])
        o_ref[r, :] = (_rms(y) * nw_ref[...]).astype(o_ref.dtype)
        return carry

    lax.fori_loop(0, t // CHUNK, step, 0)


def _ssd(p, pdt, conv_w, conv_b, dtb, alog, dsk, nw, bsz, t):
    n = bsz * t
    gw = SSD_HEADS_PER_GROUP * SSD_HEADDIM
    st = HEAD_DIM
    z_off, x_off = 2048 // gw, 3072 // gw
    b_off, c_off = 4096 // st, 4352 // st
    xw = SSD_GROUPS * gw
    par = lambda shape, f: pl.BlockSpec(shape, f)
    in_specs = [
        par((t, gw), lambda b, g: (b, z_off + g)),
        par((t, gw), lambda b, g: (b, x_off + g)),
        par((t, st), lambda b, g: (b, b_off + g)),
        par((t, st), lambda b, g: (b, c_off + g)),
        par((t, LANE), lambda b, g: (b, g)),
        par((SSD_CONV, gw), lambda b, g: (0, g)),
        par((SSD_CONV, st), lambda b, g: (0, xw // st + g)),
        par((SSD_CONV, st), lambda b, g: (0, xw // st + SSD_GROUPS + g)),
        par((1, gw), lambda b, g: (0, g)),
        par((1, st), lambda b, g: (0, xw // st + g)),
        par((1, st), lambda b, g: (0, xw // st + SSD_GROUPS + g)),
        par((1, LANE), lambda b, g: (0, g)),
        par((1, LANE), lambda b, g: (0, g)),
        par((1, gw), lambda b, g: (0, g)),
        par((1, gw), lambda b, g: (0, g)),
    ]
    return pl.pallas_call(
        _ssd_kernel,
        grid=(bsz, SSD_GROUPS),
        in_specs=in_specs,
        out_specs=pl.BlockSpec((t, gw), lambda b, g: (b, g)),
        out_shape=jax.ShapeDtypeStruct((n, SSD_GROUPS * gw), BF16),
        scratch_shapes=[pltpu.VMEM((st, gw), F32)],
        compiler_params=_cparams("arbitrary", "arbitrary"),
        name="ssd",
    )(p, p, p, p, pdt, conv_w, conv_w, conv_w, conv_b, conv_b, conv_b, dtb, alog, dsk, nw)


def _hgrn_kernel(q_ref, f_ref, i_ref, g_ref, lb_ref, nw_ref, o_ref, st_scr, cum_scr, *, layer):
    t = q_ref.shape[0]
    lbm = lb_ref[...]
    depth = lbm.shape[0]
    mx = lbm[0:1, :]
    for i in range(1, depth):
        mx = jnp.maximum(mx, lbm[i:i + 1, :])
    ex = [jnp.exp(lbm[i:i + 1, :] - mx) for i in range(depth)]
    den = ex[0]
    for i in range(1, depth):
        den = den + ex[i]
    sm = [e / den for e in ex]
    csum = sm[0]
    for i in range(1, layer + 1):
        csum = csum + sm[i]
    lb = jnp.maximum(csum - sm[0], 0.0)
    log_lb = jnp.log(jnp.maximum(lb, LB_FLOOR))
    l1m = jnp.log1p(-lb)
    oml = 1.0 - lb

    ii, jj = _tri_incl()
    tri = jnp.where(ii >= jj, 1.0, 0.0).astype(F32)
    lvl = jnp.where(ii > jj, 31 - lax.clz(ii ^ jj), -1)
    eye = ii == jj
    row = lax.broadcasted_iota(jnp.int32, (CHUNK, LANE), 0)
    scale = HEAD_DIM ** -0.5
    nlev = int(math.log2(CHUNK))
    st_scr[...] = jnp.zeros_like(st_scr)

    def step(c, carry):
        r = pl.ds(pl.multiple_of(c * CHUNK, CHUNK), CHUNK)
        q = _silu(q_ref[r, :]) * scale
        f = f_ref[r, :]
        ls = jnp.minimum(f, 0.0) - jnp.log1p(jnp.exp(-jnp.abs(f)))
        a, b = log_lb, l1m + ls
        lf = jnp.maximum(a, b) + jnp.log1p(jnp.exp(-jnp.abs(a - b)))
        kk = oml / (1.0 + jnp.exp(f))
        vb = i_ref[r, :].astype(BF16)
        cum = jnp.dot(tri, lf, precision=HIGHEST, preferred_element_type=F32)
        cum_scr[...] = cum
        scores = jnp.where(eye, _dot_nt(q.astype(BF16), kk.astype(BF16)), 0.0)
        for lv in range(nlev):
            s = 1 << lv
            if 2 * s >= 8:
                ref = jnp.concatenate(
                    [jnp.broadcast_to(cum_scr[g0 * 2 * s + s - 1:g0 * 2 * s + s, :], (2 * s, LANE))
                     for g0 in range(CHUNK // (2 * s))], axis=0)
            elif s == 2:
                m4 = row & 3
                ref = jnp.where(m4 == 0, pltpu.roll(cum, CHUNK - 1, 0),
                                jnp.where(m4 == 1, cum,
                                          jnp.where(m4 == 2, pltpu.roll(cum, 1, 0), pltpu.roll(cum, 2, 0))))
            else:
                ref = jnp.where((row & 1) == 1, pltpu.roll(cum, 1, 0), cum)
            qs = q * jnp.exp(jnp.minimum(cum - ref, 0.0))
            ks = kk * jnp.exp(jnp.minimum(ref - cum, 0.0))
            scores = jnp.where(lvl == lv, _dot_nt(qs.astype(BF16), ks.astype(BF16)), scores)
        st = st_scr[...]
        last = cum[CHUNK - 1:CHUNK, :]
        out = _dot(scores.astype(BF16), vb) + _dot_nt((q * jnp.exp(cum)).astype(BF16), st.astype(BF16))
        st_scr[...] = st * jnp.exp(last) + _dot_tn(vb, (kk * jnp.exp(last - cum)).astype(BF16))
        o_ref[r, :] = (_silu(g_ref[r, :]) * (_rms(out) * nw_ref[...])).astype(o_ref.dtype)
        return carry

    lax.fori_loop(0, t // CHUNK, step, 0)


def _hgrn(p, lbounds, nw, layer, bsz, t):
    n = bsz * t
    base = 4608 // HEAD_DIM
    blk = lambda off: pl.BlockSpec((t, HEAD_DIM), lambda b, h, off=off: (b, base + off + h))
    return pl.pallas_call(
        functools.partial(_hgrn_kernel, layer=layer),
        grid=(bsz, HGRN_HEADS),
        in_specs=[blk(0), blk(HGRN_HEADS), blk(2 * HGRN_HEADS), blk(3 * HGRN_HEADS),
                  pl.BlockSpec((lbounds.shape[0], HEAD_DIM), lambda b, h: (0, h)),
                  pl.BlockSpec((1, HEAD_DIM), lambda b, h: (0, h))],
        out_specs=pl.BlockSpec((t, HEAD_DIM), lambda b, h: (b, h)),
        out_shape=jax.ShapeDtypeStruct((n, HGRN_HEADS * HEAD_DIM), BF16),
        scratch_shapes=[pltpu.VMEM((HEAD_DIM, HEAD_DIM), F32), pltpu.VMEM((CHUNK, HEAD_DIM), F32)],
        compiler_params=_cparams("arbitrary", "arbitrary"),
        name="hgrn2",
    )(p, p, p, p, lbounds, nw)


def _outproj_kernel(a_ref, b_ref, c_ref, h_ref, w_ref, nw_ref, wr_ref, br_ref, hout_ref, u_ref, lg_ref):
    wa, wb = a_ref.shape[1], b_ref.shape[1]
    acc = _dot(a_ref[...], w_ref[0:wa, :])
    acc = acc + _dot(b_ref[...], w_ref[wa:wa + wb, :])
    acc = acc + _dot(c_ref[...], w_ref[wa + wb:, :])
    h = h_ref[...] + acc
    hout_ref[...] = h
    u = _rms(h, nw_ref[...])
    u_ref[...] = u
    lg_ref[...] = jnp.dot(u, wr_ref[...], precision=HIGHEST, preferred_element_type=F32) + br_ref[...]


def _outproj(o_a, o_b, o_c, h, w_out, nw, w_r, b_r):
    n, d = h.shape
    tm = min(512, n)
    row = lambda i: (i, 0)
    fix = lambda i: (0, 0)
    return pl.pallas_call(
        _outproj_kernel,
        grid=(n // tm,),
        in_specs=[pl.BlockSpec((tm, o_a.shape[1]), row), pl.BlockSpec((tm, o_b.shape[1]), row),
                  pl.BlockSpec((tm, o_c.shape[1]), row), pl.BlockSpec((tm, d), row),
                  pl.BlockSpec(w_out.shape, fix), pl.BlockSpec((1, d), fix),
                  pl.BlockSpec(w_r.shape, fix), pl.BlockSpec((1, LANE), fix)],
        out_specs=[pl.BlockSpec((tm, d), row), pl.BlockSpec((tm, d), row), pl.BlockSpec((tm, LANE), row)],
        out_shape=[jax.ShapeDtypeStruct((n, d), F32), jax.ShapeDtypeStruct((n, d), F32),
                   jax.ShapeDtypeStruct((n, LANE), F32)],
        compiler_params=_cparams("arbitrary"),
        name="outproj",
    )(o_a, o_b, o_c, h, w_out, nw, w_r, b_r)


def _route_kernel(lg_ref, rt_ref, cnt_ref, carry_scr):
    tr = lg_ref.shape[0]

    @pl.when(pl.program_id(0) == 0)
    def _():
        carry_scr[...] = jnp.zeros_like(carry_scr)

    lg = lg_ref[...]
    lane = lax.broadcasted_iota(jnp.int32, (tr, LANE), 1)
    neg = -jnp.inf
    big = jnp.int32(LANE)

    def first_max(vals):
        m = jnp.max(vals, axis=-1, keepdims=True)
        idx = jnp.min(jnp.where(vals == m, lane, big), axis=-1, keepdims=True)
        return m, idx

    gl = jnp.where(lane < N_GROUPS, lg, neg)
    gmax, gidx = first_max(gl)
    gate = 1.0 / jnp.sum(jnp.exp(gl - gmax), axis=-1, keepdims=True)
    lo = N_GROUPS + EXPERTS_PER_GROUP * gidx
    el = jnp.where((lane >= lo) & (lane < lo + EXPERTS_PER_GROUP), lg, neg)
    m1, i1 = first_max(el)
    m2, i2 = first_max(jnp.where(lane == i1, neg, el))
    e21 = jnp.exp(m2 - m1)
    w1 = gate * (1.0 / (1.0 + e21))
    w2 = gate * (e21 / (1.0 + e21))

    oh1 = lane == i1
    oh2 = lane == i2
    m = jnp.where(oh1 | oh2, 1.0, 0.0)
    ti = lax.broadcasted_iota(jnp.int32, (tr, tr), 0)
    tj = lax.broadcasted_iota(jnp.int32, (tr, tr), 1)
    before = _dot(jnp.where(ti > tj, 1.0, 0.0).astype(BF16), m.astype(BF16)) + carry_scr[0:1, :]
    r1 = jnp.sum(jnp.where(oh1, before, 0.0), axis=-1, keepdims=True)
    r2 = jnp.sum(jnp.where(oh2, before, 0.0), axis=-1, keepdims=True)
    total = carry_scr[0:1, :] + jnp.sum(m, axis=0, keepdims=True)
    carry_scr[...] = jnp.broadcast_to(total, carry_scr.shape)
    cnt_ref[...] = jnp.broadcast_to(total, cnt_ref.shape)

    e1 = (i1 - N_GROUPS).astype(F32)
    e2 = (i2 - N_GROUPS).astype(F32)
    out = jnp.zeros((tr, LANE), F32)
    for pos, val in enumerate((e1, e2, w1, w2, r1, r2)):
        out = jnp.where(lane == pos, val, out)
    rt_ref[...] = out


def _route(logits):
    n = logits.shape[0]
    tr = min(ROUTE_TILE, n)
    return pl.pallas_call(
        _route_kernel,
        grid=(n // tr,),
        in_specs=[pl.BlockSpec((tr, LANE), lambda i: (i, 0))],
        out_specs=[pl.BlockSpec((tr, LANE), lambda i: (i, 0)), pl.BlockSpec((8, LANE), lambda i: (0, 0))],
        out_shape=[jax.ShapeDtypeStruct((n, LANE), F32), jax.ShapeDtypeStruct((8, LANE), F32)],
        scratch_shapes=[pltpu.VMEM((8, LANE), F32)],
        compiler_params=_cparams("arbitrary"),
        name="route",
    )(logits)


def _ffn_kernel(blk_e_ref, nv_ref, code_ref, u_hbm, wg_ref, wu_ref, wd_ref, o0_hbm, o1_hbm,
                xbuf, ybuf, wgb, wub, wdb, gsem, ssem):
    b = pl.program_id(0)
    nb = pl.num_programs(0)
    tb = ybuf.shape[0]
    slot = b % 2

    def gather_copy(tok, r, sl):
        return pltpu.make_async_copy(u_hbm.at[pl.ds(tok, 1), :], xbuf.at[sl, pl.ds(r, 1), :], gsem.at[sl])

    def issue_gather(blk, sl):
        def body(r, c):
            tok = jnp.maximum(code_ref[blk * tb + r], 0) >> 1
            gather_copy(tok, r, sl).start()
            return c
        lax.fori_loop(0, tb, body, 0)

    def scatter_copy(out_hbm, tok, r):
        return pltpu.make_async_copy(ybuf.at[pl.ds(r, 1), :], out_hbm.at[pl.ds(tok, 1), :], ssem)

    @pl.when((b == 0) & (nv_ref[0] > 0))
    def _():
        issue_gather(0, 0)

    nxt = jnp.minimum(b + 1, nb - 1)

    @pl.when((b + 1 < nb) & (nv_ref[nxt] > 0))
    def _():
        issue_gather(b + 1, 1 - slot)

    prev_e = blk_e_ref[jnp.maximum(b - 1, 0)]

    @pl.when((b == 0) | (blk_e_ref[b] != prev_e))
    def _():
        wgb[...] = wg_ref[0].astype(BF16)
        wub[...] = wu_ref[0].astype(BF16)
        wdb[...] = wd_ref[0].astype(BF16)

    nv = nv_ref[b]

    @pl.when(nv > 0)
    def _():
        def wait_body(r, c):
            gather_copy(0, r, slot).wait()
            return c
        lax.fori_loop(0, tb, wait_body, 0)
        x = xbuf[slot].astype(BF16)
        hdn = _silu(_dot(x, wgb[...])) * _dot(x, wub[...])
        ybuf[...] = _dot(hdn.astype(BF16), wdb[...])

        def send(r, c):
            code = code_ref[b * tb + r]

            @pl.when((code & 1) == 0)
            def _():
                scatter_copy(o0_hbm, code >> 1, r).start()

            @pl.when((code & 1) == 1)
            def _():
                scatter_copy(o1_hbm, code >> 1, r).start()
            return c
        lax.fori_loop(0, nv, send, 0)

        def drain(r, c):
            scatter_copy(o0_hbm, 0, r).wait()
            return c
        lax.fori_loop(0, nv, drain, 0)


def _ffn(blk_e, nv, code, u, wg, wu, wd):
    n, d = u.shape
    de = wg.shape[2]
    nblk = blk_e.shape[0]
    tb = MOE_BLOCK
    grid_spec = pltpu.PrefetchScalarGridSpec(
        num_scalar_prefetch=3,
        grid=(nblk,),
        in_specs=[pl.BlockSpec(memory_space=pl.ANY),
                  pl.BlockSpec((1, d, de), lambda b, be, nv, cd: (be[b], 0, 0)),
                  pl.BlockSpec((1, d, de), lambda b, be, nv, cd: (be[b], 0, 0)),
                  pl.BlockSpec((1, de, d), lambda b, be, nv, cd: (be[b], 0, 0))],
        out_specs=[pl.BlockSpec(memory_space=pl.ANY), pl.BlockSpec(memory_space=pl.ANY)],
        scratch_shapes=[pltpu.VMEM((2, tb, d), F32), pltpu.VMEM((tb, d), F32),
                        pltpu.VMEM((d, de), BF16), pltpu.VMEM((d, de), BF16), pltpu.VMEM((de, d), BF16),
                        pltpu.SemaphoreType.DMA((2,)), pltpu.SemaphoreType.DMA],
    )
    return pl.pallas_call(
        _ffn_kernel,
        grid_spec=grid_spec,
        out_shape=[jax.ShapeDtypeStruct((n, d), F32), jax.ShapeDtypeStruct((n, d), F32)],
        compiler_params=_cparams("arbitrary"),
        name="moe_ffn",
    )(blk_e, nv, code, u, wg, wu, wd)


def _dispatch_tables(route, counts, n):
    tb = MOE_BLOCK
    e = route[:, 0:2].astype(jnp.int32)
    rank = route[:, 4:6].astype(jnp.int32)
    cnt = counts[0, N_GROUPS:N_GROUPS + N_EXPERTS].astype(jnp.int32)
    padded = ((cnt + tb - 1) // tb) * tb
    pends = jnp.cumsum(padded)
    pstarts = pends - padded
    dest = (pstarts[e] + rank).reshape(-1)
    p_total = 2 * n + N_EXPERTS * tb
    nblk = p_total // tb
    code = jnp.full((p_total,), -1, jnp.int32).at[dest].set(jnp.arange(2 * n, dtype=jnp.int32))
    bstart = jnp.arange(nblk, dtype=jnp.int32) * tb
    blk_e = jnp.clip(jnp.searchsorted(pends, bstart, side="right"), 0, N_EXPERTS - 1).astype(jnp.int32)
    nv = jnp.clip(cnt[blk_e] - (bstart - pstarts[blk_e]), 0, tb)
    nv = jnp.where(bstart < pends[-1], nv, 0).astype(jnp.int32)
    return blk_e, nv, code


def _final_kernel(h_ref, o0_ref, o1_ref, rt_ref, nw_ref, out_ref):
    rt = rt_ref[...]
    h = h_ref[...] + rt[:, 2:3] * o0_ref[...] + rt[:, 3:4] * o1_ref[...]
    out_ref[...] = _rms(h, nw_ref[...])


def _final(h, o0, o1, rt, nw):
    n, d = h.shape
    tm = min(512, n)
    row = lambda i: (i, 0)
    return pl.pallas_call(
        _final_kernel,
        grid=(n // tm,),
        in_specs=[pl.BlockSpec((tm, d), row), pl.BlockSpec((tm, d), row), pl.BlockSpec((tm, d), row),
                  pl.BlockSpec((tm, LANE), row), pl.BlockSpec((1, d), lambda i: (0, 0))],
        out_specs=pl.BlockSpec((tm, d), row),
        out_shape=jax.ShapeDtypeStruct((n, d), F32),
        compiler_params=_cparams("arbitrary"),
        name="final_norm",
    )(h, o0, o1, rt, nw)


def _pad_lanes(v, width):
    return jnp.pad(v, ((0, 0), (0, width - v.shape[1])))


def kernel(x, attn_norm_w, w_in, ssd_conv_w, ssd_conv_b, ssd_dt_bias, ssd_a_log, ssd_d, ssd_norm_w, hgrn_lower_bounds, hgrn_norm_w, w_out, ffn_norm_w, router_group_w, router_group_b, router_expert_w, router_expert_b, expert_w_gate, expert_w_up, expert_w_down, final_norm_w):
    bsz, t, d = x.shape
    n = bsz * t
    depth = w_in.shape[0]
    dt_lo, dt_hi = 4608, 4608 + SSD_GROUPS * SSD_HEADS_PER_GROUP

    half = HEAD_DIM // 2
    inv = 1.0 / (ROPE_THETA ** (jnp.arange(half, dtype=F32) / half))
    ang = jnp.arange(t, dtype=F32)[:, None] * inv[None, :]
    cos_t = jnp.concatenate([jnp.cos(ang), jnp.cos(ang)], axis=1)
    sin_t = jnp.concatenate([-jnp.sin(ang), jnp.sin(ang)], axis=1)

    def group_lanes(v):
        v = v.reshape(SSD_GROUPS, SSD_HEADS_PER_GROUP)
        return _pad_lanes(v, LANE).reshape(1, SSD_GROUPS * LANE)

    h = x.reshape(n, d)
    moe = None
    for l in range(depth):
        wl = w_in[l]
        w_main = jnp.concatenate([wl[:, :dt_lo], wl[:, dt_hi:]], axis=1).astype(BF16)
        w_dt = wl[:, dt_lo:dt_hi].reshape(d, SSD_GROUPS, SSD_HEADS_PER_GROUP)
        w_dt = jnp.pad(w_dt, ((0, 0), (0, 0), (0, LANE - SSD_HEADS_PER_GROUP))).reshape(d, SSD_GROUPS * LANE)
        res = _inproj(h, attn_norm_w[l][None, :], w_main, w_dt.astype(BF16), moe)
        if moe is None:
            p, pdt = res
        else:
            p, pdt, h = res

        o_ret = _retention(p, cos_t, sin_t, bsz, t)
        o_ssd = _ssd(p, pdt, ssd_conv_w[l], ssd_conv_b[l][None, :], group_lanes(ssd_dt_bias[l]),
                     group_lanes(ssd_a_log[l]), jnp.repeat(ssd_d[l], SSD_HEADDIM)[None, :],
                     ssd_norm_w[l][None, :], bsz, t)
        o_hgrn = _hgrn(p, hgrn_lower_bounds, hgrn_norm_w[l][None, :], l, bsz, t)

        w_r = _pad_lanes(jnp.concatenate([router_group_w[l], router_expert_w[l]], axis=1), LANE)
        b_r = _pad_lanes(jnp.concatenate([router_group_b[l], router_expert_b[l]])[None, :], LANE)
        h, u, logits = _outproj(o_ret, o_ssd, o_hgrn, h, w_out[l].astype(BF16), ffn_norm_w[l][None, :], w_r, b_r)
        route, counts = _route(logits)
        blk_e, nv, code = _dispatch_tables(route, counts, n)
        o0, o1 = _ffn(blk_e, nv, code, u, expert_w_gate[l], expert_w_up[l], expert_w_down[l])
        moe = (o0, o1, route)

    out = _final(h, moe[0], moe[1], moe[2], final_norm_w[None, :])
    return out.reshape(bsz, t, d)
```

```python
import functools
import math

import jax
import jax.numpy as jnp
from jax import lax
from jax.experimental import pallas as pl
from jax.experimental.pallas import tpu as pltpu

F32 = jnp.float32
BF16 = jnp.bfloat16
HIGHEST = lax.Precision.HIGHEST

V7X_VMEM_BYTES = 64 * 1024 * 1024
VMEM_LIMIT = V7X_VMEM_BYTES - 8 * 1024 * 1024
LANE = 128

EPS = 1e-6
LB_FLOOR = 1e-30
ROPE_THETA = 10000.0

RET_HEADS = 4
HEAD_DIM = 128
SSD_HEADDIM = 64
SSD_GROUPS = 2
SSD_HEADS_PER_GROUP = 8
SSD_CONV = 4
HGRN_HEADS = 4
CHUNK = 128
N_GROUPS = 4
EXPERTS_PER_GROUP = 8
N_EXPERTS = N_GROUPS * EXPERTS_PER_GROUP
MOE_BLOCK = 128
DMA_UNROLL = 8
ROUTE_TILE = 512
DT_LO = 4608
DT_HI = DT_LO + SSD_GROUPS * SSD_HEADS_PER_GROUP


def _cparams(*sem):
    return pltpu.CompilerParams(dimension_semantics=sem, vmem_limit_bytes=VMEM_LIMIT)


def _rms(x, w=None):
    y = x * lax.rsqrt(jnp.mean(x * x, axis=-1, keepdims=True) + EPS)
    return y if w is None else y * w


def _sigmoid(x):
    return 1.0 / (1.0 + jnp.exp(-x))


def _silu(x):
    return x * _sigmoid(x)


def _dot(a, b):
    return jnp.dot(a, b, preferred_element_type=F32)


def _dot_nt(a, b):
    return lax.dot_general(a, b, (((1,), (1,)), ((), ())), preferred_element_type=F32)


def _dot_tn(a, b):
    return lax.dot_general(a, b, (((0,), (0,)), ((), ())), preferred_element_type=F32)


def _tri_incl():
    ii = lax.broadcasted_iota(jnp.int32, (CHUNK, CHUNK), 0)
    jj = lax.broadcasted_iota(jnp.int32, (CHUNK, CHUNK), 1)
    return ii, jj


def _slab_rows(d):
    return d // LANE


def _slabs_to_rows(ref2d, base, rows, srows):
    return jnp.concatenate([ref2d[pl.ds(base + c, rows, stride=srows), :] for c in range(srows)], axis=1)


def _rows_to_slabs(ref2d, base, val, srows):
    rows = val.shape[0]
    for c in range(srows):
        ref2d[pl.ds(base + c, rows, stride=srows), :] = val[:, c * LANE:(c + 1) * LANE]


def _wprep_kernel(w_ref, wm_ref, wdt_ref):
    wm_ref[:, 0:DT_LO] = w_ref[:, 0:DT_LO].astype(BF16)
    wm_ref[:, DT_LO:] = w_ref[:, DT_HI:].astype(BF16)
    rows = w_ref.shape[0]
    pad = jnp.zeros((rows, LANE - SSD_HEADS_PER_GROUP), BF16)
    parts = []
    for g in range(SSD_GROUPS):
        lo = DT_LO + g * SSD_HEADS_PER_GROUP
        parts += [w_ref[:, lo:lo + SSD_HEADS_PER_GROUP].astype(BF16), pad]
    wdt_ref[...] = jnp.concatenate(parts, axis=1)


def _wprep(w):
    d, nin = w.shape
    tr = 256
    nm = nin - (DT_HI - DT_LO)
    return pl.pallas_call(
        _wprep_kernel,
        grid=(d // tr,),
        in_specs=[pl.BlockSpec((tr, nin), lambda i: (i, 0))],
        out_specs=[pl.BlockSpec((tr, nm), lambda i: (i, 0)), pl.BlockSpec((tr, SSD_GROUPS * LANE), lambda i: (i, 0))],
        out_shape=[jax.ShapeDtypeStruct((d, nm), BF16), jax.ShapeDtypeStruct((d, SSD_GROUPS * LANE), BF16)],
        compiler_params=_cparams("arbitrary"),
        name="wprep",
    )(w)


def _moe_combine(h, o0_ref, o1_ref, rt_ref):
    rows = h.shape[0]
    srows = _slab_rows(h.shape[1])
    rt = rt_ref[...]
    y0 = _slabs_to_rows(o0_ref, 0, rows, srows)
    y1 = _slabs_to_rows(o1_ref, 0, rows, srows)
    return h + rt[:, 2:3] * y0 + rt[:, 3:4] * y1


def _inproj_kernel(*refs, combine):
    if combine:
        h_ref, o0_ref, o1_ref, rt_ref, nw_ref, w_ref, wdt_ref, p_ref, pdt_ref, hout_ref, u_scr = refs
    else:
        h_ref, nw_ref, w_ref, wdt_ref, p_ref, pdt_ref, u_scr = refs

    @pl.when(pl.program_id(1) == 0)
    def _():
        h = h_ref[...]
        if combine:
            h = _moe_combine(h, o0_ref, o1_ref, rt_ref)
            hout_ref[...] = h
        u = _rms(h, nw_ref[...]).astype(BF16)
        u_scr[...] = u
        pdt_ref[...] = _dot(u, wdt_ref[...])

    p_ref[...] = _dot(u_scr[...], w_ref[...])


def _moe_specs(o2d, n, tm, srows, nidx):
    nt = n // tm
    if nidx == 2:
        return [pl.BlockSpec((tm * srows, LANE), lambda i, j: (i, 0)),
                pl.BlockSpec((tm * srows, LANE), lambda i, j: (nt + i, 0))]
    return [pl.BlockSpec((tm * srows, LANE), lambda i: (i, 0)),
            pl.BlockSpec((tm * srows, LANE), lambda i: (nt + i, 0))]


def _inproj(h, nw, w_main, w_dt, moe=None):
    n, d = h.shape
    np_ = w_main.shape[1]
    combine = moe is not None
    tm = min(512 if combine else 1024, n)
    tn = 512
    row = lambda i, j: (i, 0)
    in_specs = [pl.BlockSpec((tm, d), row)]
    args = [h]
    if combine:
        o2d, rt = moe
        in_specs += _moe_specs(o2d, n, tm, _slab_rows(d), 2) + [pl.BlockSpec((tm, LANE), row)]
        args += [o2d, o2d, rt]
    in_specs += [pl.BlockSpec((1, d), lambda i, j: (0, 0)),
                 pl.BlockSpec((d, tn), lambda i, j: (0, j)),
                 pl.BlockSpec((d, w_dt.shape[1]), lambda i, j: (0, 0))]
    args += [nw, w_main, w_dt]
    out_shape = [jax.ShapeDtypeStruct((n, np_), F32), jax.ShapeDtypeStruct((n, w_dt.shape[1]), F32)]
    out_specs = [pl.BlockSpec((tm, tn), lambda i, j: (i, j)), pl.BlockSpec((tm, w_dt.shape[1]), row)]
    if combine:
        out_shape.append(jax.ShapeDtypeStruct((n, d), F32))
        out_specs.append(pl.BlockSpec((tm, d), row))
    res = pl.pallas_call(
        functools.partial(_inproj_kernel, combine=combine),
        grid=(n // tm, np_ // tn),
        in_specs=in_specs, out_specs=out_specs, out_shape=out_shape,
        scratch_shapes=[pltpu.VMEM((tm, d), BF16)],
        compiler_params=_cparams("arbitrary", "arbitrary"),
        name="inproj",
    )(*args)
    return res


def _ret_kernel(q_ref, k_ref, v_ref, g_ref, cos_ref, sin_ref, o_ref, s_scr):
    t = q_ref.shape[0]
    hf = pl.program_id(1).astype(F32)
    lg = jnp.log(1.0 - jnp.exp2(jnp.full((1, LANE), -5.0, F32) - hf))
    ii, jj = _tri_incl()
    causal = ii >= jj
    dmat = jnp.where(causal, jnp.exp(jnp.where(causal, (ii - jj).astype(F32) * lg, 0.0)), 0.0)
    iif = ii.astype(F32)
    ecum = jnp.exp((iif + 1.0) * lg)
    wk = jnp.exp((CHUNK - 1.0 - iif) * lg)
    elast = jnp.exp(float(CHUNK) * lg)
    scale = HEAD_DIM ** -0.5
    s_scr[...] = jnp.zeros_like(s_scr)

    def step(c, carry):
        r = pl.ds(pl.multiple_of(c * CHUNK, CHUNK), CHUNK)
        cs, sn = cos_ref[r, :], sin_ref[r, :]
        q, k = q_ref[r, :], k_ref[r, :]
        qr = q * cs + pltpu.roll(q, HEAD_DIM // 2, 1) * sn
        kr = (k * cs + pltpu.roll(k, HEAD_DIM // 2, 1) * sn) * scale
        vb = v_ref[r, :].astype(BF16)
        s = s_scr[...]
        scores = _dot_nt(qr.astype(BF16), kr.astype(BF16)) * dmat
        out = _dot(scores.astype(BF16), vb) + _dot((qr * ecum).astype(BF16), s.astype(BF16))
        s_scr[...] = elast * s + _dot_tn((kr * wk).astype(BF16), vb)
        o_ref[r, :] = (_silu(g_ref[r, :]) * _rms(out)).astype(o_ref.dtype)
        return carry

    lax.fori_loop(0, t // CHUNK, step, 0)


def _retention(p, cos_t, sin_t, bsz, t):
    n = bsz * t
    blk = lambda off: pl.BlockSpec((t, HEAD_DIM), lambda b, h, off=off: (b, off + h))
    tab = pl.BlockSpec((t, HEAD_DIM), lambda b, h: (0, 0))
    return pl.pallas_call(
        _ret_kernel,
        grid=(bsz, RET_HEADS),
        in_specs=[blk(0), blk(RET_HEADS), blk(2 * RET_HEADS), blk(3 * RET_HEADS), tab, tab],
        out_specs=pl.BlockSpec((t, HEAD_DIM), lambda b, h: (b, h)),
        out_shape=jax.ShapeDtypeStruct((n, RET_HEADS * HEAD_DIM), BF16),
        scratch_shapes=[pltpu.VMEM((HEAD_DIM, HEAD_DIM), F32)],
        compiler_params=_cparams("arbitrary", "arbitrary"),
        name="retention",
    )(p, p, p, p, cos_t, sin_t)


def _ssd_kernel(z_ref, x_ref, b_ref, c_ref, dt_ref, cwx_ref, cwb_ref, cwc_ref, cbx_ref, cbb_ref, cbc_ref,
                dtb_ref, alog_ref, dsk_ref, nw_ref, o_ref, s_scr):
    t = z_ref.shape[0]
    ii, jj = _tri_incl()
    causal = ii >= jj
    tri = jnp.where(causal, 1.0, 0.0).astype(F32)
    lane_lo = lax.broadcasted_iota(jnp.int32, (CHUNK, LANE), 1) < SSD_HEADDIM
    neg_a = -jnp.exp(alog_ref[...])
    s_scr[...] = jnp.zeros_like(s_scr)

    def conv(ref, w_ref, bias_ref, c):
        r = pl.ds(pl.multiple_of(c * CHUNK, CHUNK), CHUNK)
        rp = pl.ds(pl.multiple_of(jnp.maximum(c * CHUNK - 8, 0), 8), 8)
        prev = jnp.where(c > 0, ref[rp, :], 0.0)
        ext = jnp.concatenate([prev, ref[r, :]], axis=0)
        w = w_ref[...]
        acc = bias_ref[...]
        for i in range(SSD_CONV):
            off = 8 - (SSD_CONV - 1) + i
            acc = acc + w[i:i + 1, :] * ext[off:off + CHUNK, :]
        return _silu(acc)

    def step(c, carry):
        r = pl.ds(pl.multiple_of(c * CHUNK, CHUNK), CHUNK)
        xs = conv(x_ref, cwx_ref, cbx_ref, c)
        bm = conv(b_ref, cwb_ref, cbb_ref, c).astype(BF16)
        cm = conv(c_ref, cwc_ref, cbc_ref, c).astype(BF16)
        xr = dt_ref[r, :] + dtb_ref[...]
        dt = jnp.maximum(xr, 0.0) + jnp.log1p(jnp.exp(-jnp.abs(xr)))
        la = dt * neg_a
        cum = jnp.dot(tri, la, precision=HIGHEST, preferred_element_type=F32)
        last = cum[CHUNK - 1:CHUNK, :]
        wj = dt * jnp.exp(last - cum)
        ecum = jnp.exp(cum)
        cum_t = cum.T
        dt_t = dt.T
        g = _dot_nt(cm, bm)
        s = s_scr[...]
        cs = _dot(cm, s.astype(BF16))
        ys, xws, els = [], [], []
        for pr in range(SSD_HEADS_PER_GROUP // 2):
            xp = xs[:, pr * LANE:(pr + 1) * LANE]
            acc = None
            ecp, wjp, elp = None, None, None
            for half in range(2):
                hd = 2 * pr + half
                ci = jnp.broadcast_to(cum[:, hd:hd + 1], (CHUNK, CHUNK))
                dec = jnp.exp(jnp.where(causal, ci - cum_t[hd:hd + 1, :], 0.0))
                m = jnp.where(causal, g * dec * dt_t[hd:hd + 1, :], 0.0)
                sel = lane_lo if half == 0 else jnp.logical_not(lane_lo)
                y = _dot(m.astype(BF16), jnp.where(sel, xp, 0.0).astype(BF16))
                acc = y if acc is None else acc + y
                eb = jnp.broadcast_to(ecum[:, hd:hd + 1], (CHUNK, LANE))
                wb = jnp.broadcast_to(wj[:, hd:hd + 1], (CHUNK, LANE))
                lb = jnp.broadcast_to(ecum[CHUNK - 1:CHUNK, hd:hd + 1], (1, LANE))
                ecp = eb if half == 0 else jnp.where(lane_lo, ecp, eb)
                wjp = wb if half == 0 else jnp.where(lane_lo, wjp, wb)
                elp = lb if half == 0 else jnp.where(lane_lo[0:1, :], elp, lb)
            ys.append(acc + ecp * cs[:, pr * LANE:(pr + 1) * LANE])
            xws.append((wjp * xp).astype(BF16))
            els.append(elp)
        y = jnp.concatenate(ys, axis=1)
        s_scr[...] = jnp.concatenate(els, axis=1) * s + _dot_tn(bm, jnp.concatenate(xws, axis=1))
        y = (y + dsk_ref[...] * xs) * _silu(z_ref[r, :])
        o_ref[r, :] = (_rms(y) * nw_ref[...]).astype(o_ref.dtype)
        return carry

    lax.fori_loop(0, t // CHUNK, step, 0)


def _ssd(p, pdt, conv_w, conv_b, dtb, alog, dsk, nw, bsz, t):
    n = bsz * t
    gw = SSD_HEADS_PER_GROUP * SSD_HEADDIM
    st = HEAD_DIM
    z_off, x_off = 2048 // gw, 3072 // gw
    b_off, c_off = 4096 // st, 4352 // st
    xw = SSD_GROUPS * gw
    par = lambda shape, f: pl.BlockSpec(shape, f)
    in_specs = [
        par((t, gw), lambda b, g: (b, z_off + g)),
        par((t, gw), lambda b, g: (b, x_off + g)),
        par((t, st), lambda b, g: (b, b_off + g)),
        par((t, st), lambda b, g: (b, c_off + g)),
        par((t, LANE), lambda b, g: (b, g)),
        par((SSD_CONV, gw), lambda b, g: (0, g)),
        par((SSD_CONV, st), lambda b, g: (0, xw // st + g)),
        par((SSD_CONV, st), lambda b, g: (0, xw // st + SSD_GROUPS + g)),
        par((1, gw), lambda b, g: (0, g)),
        par((1, st), lambda b, g: (0, xw // st + g)),
        par((1, st), lambda b, g: (0, xw // st + SSD_GROUPS + g)),
        par((1, LANE), lambda b, g: (0, g)),
        par((1, LANE), lambda b, g: (0, g)),
        par((1, gw), lambda b, g: (0, g)),
        par((1, gw), lambda b, g: (0, g)),
    ]
    return pl.pallas_call(
        _ssd_kernel,
        grid=(bsz, SSD_GROUPS),
        in_specs=in_specs,
        out_specs=pl.BlockSpec((t, gw), lambda b, g: (b, g)),
        out_shape=jax.ShapeDtypeStruct((n, SSD_GROUPS * gw), BF16),
        scratch_shapes=[pltpu.VMEM((st, gw), F32)],
        compiler_params=_cparams("arbitrary", "arbitrary"),
        name="ssd",
    )(p, p, p, p, pdt, conv_w, conv_w, conv_w, conv_b, conv_b, conv_b, dtb, alog, dsk, nw)


def _hgrn_kernel(q_ref, f_ref, i_ref, g_ref, lb_ref, nw_ref, o_ref, st_scr, cum_scr, *, layer):
    t = q_ref.shape[0]
    lbm = lb_ref[...]
    depth = lbm.shape[0]
    mx = lbm[0:1, :]
    for i in range(1, depth):
        mx = jnp.maximum(mx, lbm[i:i + 1, :])
    ex = [jnp.exp(lbm[i:i + 1, :] - mx) for i in range(depth)]
    den = ex[0]
    for i in range(1, depth):
        den = den + ex[i]
    sm = [e / den for e in ex]
    csum = sm[0]
    for i in range(1, layer + 1):
        csum = csum + sm[i]
    lb = jnp.maximum(csum - sm[0], 0.0)
    log_lb = jnp.log(jnp.maximum(lb, LB_FLOOR))
    l1m = jnp.log1p(-lb)
    oml = 1.0 - lb

    ii, jj = _tri_incl()
    tri = jnp.where(ii >= jj, 1.0, 0.0).astype(F32)
    lvl = jnp.where(ii > jj, 31 - lax.clz(ii ^ jj), -1)
    eye = ii == jj
    row = lax.broadcasted_iota(jnp.int32, (CHUNK, LANE), 0)
    scale = HEAD_DIM ** -0.5
    nlev = int(math.log2(CHUNK))
    st_scr[...] = jnp.zeros_like(st_scr)

    def step(c, carry):
        r = pl.ds(pl.multiple_of(c * CHUNK, CHUNK), CHUNK)
        q = _silu(q_ref[r, :]) * scale
        f = f_ref[r, :]
        ls = jnp.minimum(f, 0.0) - jnp.log1p(jnp.exp(-jnp.abs(f)))
        a, b = log_lb, l1m + ls
        lf = jnp.maximum(a, b) + jnp.log1p(jnp.exp(-jnp.abs(a - b)))
        kk = oml / (1.0 + jnp.exp(f))
        vb = i_ref[r, :].astype(BF16)
        cum = jnp.dot(tri, lf, precision=HIGHEST, preferred_element_type=F32)
        cum_scr[...] = cum
        scores = jnp.where(eye, _dot_nt(q.astype(BF16), kk.astype(BF16)), 0.0)
        for lv in range(nlev):
            s = 1 << lv
            if 2 * s >= 8:
                ref = jnp.concatenate(
                    [jnp.broadcast_to(cum_scr[g0 * 2 * s + s - 1:g0 * 2 * s + s, :], (2 * s, LANE))
                     for g0 in range(CHUNK // (2 * s))], axis=0)
            elif s == 2:
                m4 = row & 3
                ref = jnp.where(m4 == 0, pltpu.roll(cum, CHUNK - 1, 0),
                                jnp.where(m4 == 1, cum,
                                          jnp.where(m4 == 2, pltpu.roll(cum, 1, 0), pltpu.roll(cum, 2, 0))))
            else:
                ref = jnp.where((row & 1) == 1, pltpu.roll(cum, 1, 0), cum)
            qs = q * jnp.exp(jnp.minimum(cum - ref, 0.0))
            ks = kk * jnp.exp(jnp.minimum(ref - cum, 0.0))
            scores = jnp.where(lvl == lv, _dot_nt(qs.astype(BF16), ks.astype(BF16)), scores)
        st = st_scr[...]
        last = cum[CHUNK - 1:CHUNK, :]
        out = _dot(scores.astype(BF16), vb) + _dot_nt((q * jnp.exp(cum)).astype(BF16), st.astype(BF16))
        st_scr[...] = st * jnp.exp(last) + _dot_tn(vb, (kk * jnp.exp(last - cum)).astype(BF16))
        o_ref[r, :] = (_silu(g_ref[r, :]) * (_rms(out) * nw_ref[...])).astype(o_ref.dtype)
        return carry

    lax.fori_loop(0, t // CHUNK, step, 0)


def _hgrn(p, lbounds, nw, layer, bsz, t):
    n = bsz * t
    base = DT_LO // HEAD_DIM
    blk = lambda off: pl.BlockSpec((t, HEAD_DIM), lambda b, h, off=off: (b, base + off + h))
    return pl.pallas_call(
        functools.partial(_hgrn_kernel, layer=layer),
        grid=(bsz, HGRN_HEADS),
        in_specs=[blk(0), blk(HGRN_HEADS), blk(2 * HGRN_HEADS), blk(3 * HGRN_HEADS),
                  pl.BlockSpec((lbounds.shape[0], HEAD_DIM), lambda b, h: (0, h)),
                  pl.BlockSpec((1, HEAD_DIM), lambda b, h: (0, h))],
        out_specs=pl.BlockSpec((t, HEAD_DIM), lambda b, h: (b, h)),
        out_shape=jax.ShapeDtypeStruct((n, HGRN_HEADS * HEAD_DIM), BF16),
        scratch_shapes=[pltpu.VMEM((HEAD_DIM, HEAD_DIM), F32), pltpu.VMEM((CHUNK, HEAD_DIM), F32)],
        compiler_params=_cparams("arbitrary", "arbitrary"),
        name="hgrn2",
    )(p, p, p, p, lbounds, nw)


def _outproj_kernel(a_ref, b_ref, c_ref, h_ref, w_ref, nw_ref, wr_ref, br_ref, hout_ref, u_ref, lg_ref):
    wa, wb = a_ref.shape[1], b_ref.shape[1]
    acc = _dot(a_ref[...], w_ref[0:wa, :])
    acc = acc + _dot(b_ref[...], w_ref[wa:wa + wb, :])
    acc = acc + _dot(c_ref[...], w_ref[wa + wb:, :])
    h = h_ref[...] + acc
    hout_ref[...] = h
    u = _rms(h, nw_ref[...])
    _rows_to_slabs(u_ref, 0, u, _slab_rows(u.shape[1]))
    lg_ref[...] = jnp.dot(u, wr_ref[...], precision=HIGHEST, preferred_element_type=F32) + br_ref[...]


def _outproj(o_a, o_b, o_c, h, w_out, nw, w_r, b_r):
    n, d = h.shape
    tm = min(512, n)
    srows = _slab_rows(d)
    row = lambda i: (i, 0)
    fix = lambda i: (0, 0)
    return pl.pallas_call(
        _outproj_kernel,
        grid=(n // tm,),
        in_specs=[pl.BlockSpec((tm, o_a.shape[1]), row), pl.BlockSpec((tm, o_b.shape[1]), row),
                  pl.BlockSpec((tm, o_c.shape[1]), row), pl.BlockSpec((tm, d), row),
                  pl.BlockSpec(w_out.shape, fix), pl.BlockSpec((1, d), fix),
                  pl.BlockSpec(w_r.shape, fix), pl.BlockSpec((1, LANE), fix)],
        out_specs=[pl.BlockSpec((tm, d), row), pl.BlockSpec((tm * srows, LANE), row),
                   pl.BlockSpec((tm, LANE), row)],
        out_shape=[jax.ShapeDtypeStruct((n, d), F32), jax.ShapeDtypeStruct((n * srows, LANE), F32),
                   jax.ShapeDtypeStruct((n, LANE), F32)],
        compiler_params=_cparams("arbitrary"),
        name="outproj",
    )(o_a, o_b, o_c, h, w_out, nw, w_r, b_r)


def _route_kernel(lg_ref, rt_ref, cnt_ref, carry_scr):
    tr = lg_ref.shape[0]

    @pl.when(pl.program_id(0) == 0)
    def _():
        carry_scr[...] = jnp.zeros_like(carry_scr)

    lg = lg_ref[...]
    lane = lax.broadcasted_iota(jnp.int32, (tr, LANE), 1)
    neg = -jnp.inf
    big = jnp.int32(LANE)

    def first_max(vals):
        m = jnp.max(vals, axis=-1, keepdims=True)
        idx = jnp.min(jnp.where(vals == m, lane, big), axis=-1, keepdims=True)
        return m, idx

    gl = jnp.where(lane < N_GROUPS, lg, neg)
    gmax, gidx = first_max(gl)
    gate = 1.0 / jnp.sum(jnp.exp(gl - gmax), axis=-1, keepdims=True)
    lo = N_GROUPS + EXPERTS_PER_GROUP * gidx
    el = jnp.where((lane >= lo) & (lane < lo + EXPERTS_PER_GROUP), lg, neg)
    m1, i1 = first_max(el)
    m2, i2 = first_max(jnp.where(lane == i1, neg, el))
    e21 = jnp.exp(m2 - m1)
    w1 = gate * (1.0 / (1.0 + e21))
    w2 = gate * (e21 / (1.0 + e21))

    oh1 = lane == i1
    oh2 = lane == i2
    m = jnp.where(oh1 | oh2, 1.0, 0.0)
    ti = lax.broadcasted_iota(jnp.int32, (tr, tr), 0)
    tj = lax.broadcasted_iota(jnp.int32, (tr, tr), 1)
    before = _dot(jnp.where(ti > tj, 1.0, 0.0).astype(BF16), m.astype(BF16)) + carry_scr[0:1, :]
    r1 = jnp.sum(jnp.where(oh1, before, 0.0), axis=-1, keepdims=True)
    r2 = jnp.sum(jnp.where(oh2, before, 0.0), axis=-1, keepdims=True)
    total = carry_scr[0:1, :] + jnp.sum(m, axis=0, keepdims=True)
    carry_scr[...] = jnp.broadcast_to(total, carry_scr.shape)
    cnt_ref[...] = jnp.broadcast_to(total, cnt_ref.shape)

    e1 = (i1 - N_GROUPS).astype(F32)
    e2 = (i2 - N_GROUPS).astype(F32)
    out = jnp.zeros((tr, LANE), F32)
    for pos, val in enumerate((e1, e2, w1, w2, r1, r2)):
        out = jnp.where(lane == pos, val, out)
    rt_ref[...] = out


def _route(logits):
    n = logits.shape[0]
    tr = min(ROUTE_TILE, n)
    return pl.pallas_call(
        _route_kernel,
        grid=(n // tr,),
        in_specs=[pl.BlockSpec((tr, LANE), lambda i: (i, 0))],
        out_specs=[pl.BlockSpec((tr, LANE), lambda i: (i, 0)), pl.BlockSpec((8, LANE), lambda i: (0, 0))],
        out_shape=[jax.ShapeDtypeStruct((n, LANE), F32), jax.ShapeDtypeStruct((8, LANE), F32)],
        scratch_shapes=[pltpu.VMEM((8, LANE), F32)],
        compiler_params=_cparams("arbitrary"),
        name="route",
    )(logits)


def _ffn_kernel(blk_e_ref, nv_ref, code_ref, u_hbm, wg_ref, wu_ref, wd_ref, o_hbm,
                xbuf, ybuf, wgb, wub, wdb, gsem, ssem, *, tb, srows):
    b = pl.program_id(0)
    nb = pl.num_programs(0)
    n_tok = u_hbm.shape[0] // srows
    slot = b % 2

    def slab(ref, idx, count=1):
        return ref.at[pl.ds(pl.multiple_of(idx * srows, srows), count * srows)]

    def row_loop(count, fn):
        full = count // DMA_UNROLL

        def grp(i, c):
            for k in range(DMA_UNROLL):
                fn(i * DMA_UNROLL + k)
            return c
        lax.fori_loop(0, full, grp, 0)

        def one(r, c):
            fn(r)
            return c
        lax.fori_loop(full * DMA_UNROLL, count, one, 0)

    def issue_gather(blk, sl, count):
        def fn(r):
            tok = code_ref[blk * tb + r] >> 1
            pltpu.make_async_copy(slab(u_hbm, tok), slab(xbuf, sl * tb + r), gsem.at[sl]).start()
        row_loop(count, fn)

    def wait_gather(sl, count):
        pltpu.make_async_copy(slab(u_hbm, 0, count), slab(xbuf, sl * tb, count), gsem.at[sl]).wait()

    def wait_scatter(sl, count):
        pltpu.make_async_copy(slab(ybuf, sl * tb, count), slab(o_hbm, 0, count), ssem.at[sl]).wait()

    @pl.when(b == 0)
    def _():
        xbuf[...] = jnp.zeros_like(xbuf)

        @pl.when(nv_ref[0] > 0)
        def _():
            issue_gather(0, 0, nv_ref[0])

    nxt = jnp.minimum(b + 1, nb - 1)
    nv_next = jnp.where(b + 1 < nb, nv_ref[nxt], 0)

    @pl.when(nv_next > 0)
    def _():
        issue_gather(b + 1, 1 - slot, nv_next)

    prev = jnp.maximum(b - 1, 0)

    @pl.when((b == 0) | (blk_e_ref[b] != blk_e_ref[prev]))
    def _():
        wgb[...] = wg_ref[0].astype(BF16)
        wub[...] = wu_ref[0].astype(BF16)
        wdb[...] = wd_ref[0].astype(BF16)

    nv = nv_ref[b]

    @pl.when(nv > 0)
    def _():
        wait_gather(slot, nv)
        x = _slabs_to_rows(xbuf, slot * tb * srows, tb, srows).astype(BF16)
        hdn = _silu(_dot(x, wgb[...])) * _dot(x, wub[...])
        y = _dot(hdn.astype(BF16), wdb[...])
        _rows_to_slabs(ybuf, slot * tb * srows, y, srows)

        def send(r):
            code = code_ref[b * tb + r]
            dst = (code & 1) * n_tok + (code >> 1)
            pltpu.make_async_copy(slab(ybuf, slot * tb + r), slab(o_hbm, dst), ssem.at[slot]).start()
        row_loop(nv, send)

    nv_prev = jnp.where(b > 0, nv_ref[prev], 0)

    @pl.when(nv_prev > 0)
    def _():
        wait_scatter(1 - slot, nv_prev)

    @pl.when((b == nb - 1) & (nv > 0))
    def _():
        wait_scatter(slot, nv)


def _ffn(blk_e, nv, code, u2d, wg, wu, wd):
    d, de = wg.shape[1], wg.shape[2]
    srows = _slab_rows(d)
    n = u2d.shape[0] // srows
    nblk = blk_e.shape[0]
    tb = MOE_BLOCK
    grid_spec = pltpu.PrefetchScalarGridSpec(
        num_scalar_prefetch=3,
        grid=(nblk,),
        in_specs=[pl.BlockSpec(memory_space=pl.ANY),
                  pl.BlockSpec((1, d, de), lambda b, be, nv, cd: (be[b], 0, 0)),
                  pl.BlockSpec((1, d, de), lambda b, be, nv, cd: (be[b], 0, 0)),
                  pl.BlockSpec((1, de, d), lambda b, be, nv, cd: (be[b], 0, 0))],
        out_specs=pl.BlockSpec(memory_space=pl.ANY),
        scratch_shapes=[pltpu.VMEM((2 * tb * srows, LANE), F32), pltpu.VMEM((2 * tb * srows, LANE), F32),
                        pltpu.VMEM((d, de), BF16), pltpu.VMEM((d, de), BF16), pltpu.VMEM((de, d), BF16),
                        pltpu.SemaphoreType.DMA((2,)), pltpu.SemaphoreType.DMA((2,))],
    )
    return pl.pallas_call(
        functools.partial(_ffn_kernel, tb=tb, srows=srows),
        grid_spec=grid_spec,
        out_shape=jax.ShapeDtypeStruct((2 * n * srows, LANE), F32),
        compiler_params=_cparams("arbitrary"),
        name="moe_ffn",
    )(blk_e, nv, code, u2d, wg, wu, wd)


def _dispatch_tables(route, counts, n):
    tb = MOE_BLOCK
    e = route[:, 0:2].astype(jnp.int32)
    rank = route[:, 4:6].astype(jnp.int32)
    cnt = counts[0, N_GROUPS:N_GROUPS + N_EXPERTS].astype(jnp.int32)
    padded = ((cnt + tb - 1) // tb) * tb
    pends = jnp.cumsum(padded)
    pstarts = pends - padded
    dest = (pstarts[e] + rank).reshape(-1)
    p_total = 2 * n + N_EXPERTS * tb
    nblk = p_total // tb
    code = jnp.full((p_total,), -1, jnp.int32).at[dest].set(
        jnp.arange(2 * n, dtype=jnp.int32), unique_indices=True, mode="promise_in_bounds")
    bstart = jnp.arange(nblk, dtype=jnp.int32) * tb
    blk_e = jnp.minimum(jnp.sum((bstart[:, None] >= pends[None, :]).astype(jnp.int32), axis=1), N_EXPERTS - 1)
    nv = jnp.clip(cnt[blk_e] - (bstart - pstarts[blk_e]), 0, tb)
    nv = jnp.where(bstart < pends[-1], nv, 0).astype(jnp.int32)
    return blk_e, nv, code


def _final_kernel(h_ref, o0_ref, o1_ref, rt_ref, nw_ref, out_ref):
    h = _moe_combine(h_ref[...], o0_ref, o1_ref, rt_ref)
    out_ref[...] = _rms(h, nw_ref[...])


def _final(h, o2d, rt, nw):
    n, d = h.shape
    tm = min(512, n)
    row = lambda i: (i, 0)
    return pl.pallas_call(
        _final_kernel,
        grid=(n // tm,),
        in_specs=[pl.BlockSpec((tm, d), row)] + _moe_specs(o2d, n, tm, _slab_rows(d), 1)
                 + [pl.BlockSpec((tm, LANE), row), pl.BlockSpec((1, d), lambda i: (0, 0))],
        out_specs=pl.BlockSpec((tm, d), row),
        out_shape=jax.ShapeDtypeStruct((n, d), F32),
        compiler_params=_cparams("arbitrary"),
        name="final_norm",
    )(h, o2d, o2d, rt, nw)


def _pad_lanes(v, width):
    return jnp.pad(v, ((0, 0), (0, width - v.shape[1])))


def kernel(x, attn_norm_w, w_in, ssd_conv_w, ssd_conv_b, ssd_dt_bias, ssd_a_log, ssd_d, ssd_norm_w, hgrn_lower_bounds, hgrn_norm_w, w_out, ffn_norm_w, router_group_w, router_group_b, router_expert_w, router_expert_b, expert_w_gate, expert_w_up, expert_w_down, final_norm_w):
    bsz, t, d = x.shape
    n = bsz * t
    depth = w_in.shape[0]
    srows = _slab_rows(d)

    half = HEAD_DIM // 2
    inv = 1.0 / (ROPE_THETA ** (jnp.arange(half, dtype=F32) / half))
    ang = jnp.arange(t, dtype=F32)[:, None] * inv[None, :]
    cos_t = jnp.concatenate([jnp.cos(ang), jnp.cos(ang)], axis=1)
    sin_t = jnp.concatenate([-jnp.sin(ang), jnp.sin(ang)], axis=1)

    def group_lanes(v):
        v = v.reshape(SSD_GROUPS, SSD_HEADS_PER_GROUP)
        return _pad_lanes(v, LANE).reshape(1, SSD_GROUPS * LANE)

    h = x.reshape(n, d)
    moe = None
    for l in range(depth):
        w_main, w_dt = _wprep(w_in[l])
        res = _inproj(h, attn_norm_w[l][None, :], w_main, w_dt, moe)
        if moe is None:
            p, pdt = res
        else:
            p, pdt, h = res

        o_ret = _retention(p, cos_t, sin_t, bsz, t)
        o_ssd = _ssd(p, pdt, ssd_conv_w[l], ssd_conv_b[l][None, :], group_lanes(ssd_dt_bias[l]),
                     group_lanes(ssd_a_log[l]), jnp.repeat(ssd_d[l], SSD_HEADDIM)[None, :],
                     ssd_norm_w[l][None, :], bsz, t)
        o_hgrn = _hgrn(p, hgrn_lower_bounds, hgrn_norm_w[l][None, :], l, bsz, t)

        w_r = _pad_lanes(jnp.concatenate([router_group_w[l], router_expert_w[l]], axis=1), LANE)
        b_r = _pad_lanes(jnp.concatenate([router_group_b[l], router_expert_b[l]])[None, :], LANE)
        h, u2d, logits = _outproj(o_ret, o_ssd, o_hgrn, h, w_out[l].astype(BF16), ffn_norm_w[l][None, :], w_r, b_r)
        route, counts = _route(logits)
        blk_e, nv, code = _dispatch_tables(route, counts, n)
        o2d = _ffn(blk_e, nv, code, u2d, expert_w_gate[l], expert_w_up[l], expert_w_down[l])
        moe = (o2d, route)

    out = _final(h, moe[0], moe[1], final_norm_w[None, :])
    return out.reshape(bsz, t, d)
```

```python
import functools
import math

import jax
import jax.numpy as jnp
from jax import lax
from jax.experimental import pallas as pl
from jax.experimental.pallas import tpu as pltpu

F32 = jnp.float32
BF16 = jnp.bfloat16
HIGHEST = lax.Precision.HIGHEST

V7X_VMEM_BYTES = 64 * 1024 * 1024
VMEM_LIMIT = V7X_VMEM_BYTES - 8 * 1024 * 1024
LANE = 128

EPS = 1e-6
LB_FLOOR = 1e-30
ROPE_THETA = 10000.0

RET_HEADS = 4
HEAD_DIM = 128
SSD_HEADDIM = 64
SSD_GROUPS = 2
SSD_HEADS_PER_GROUP = 8
SSD_CONV = 4
HGRN_HEADS = 4
CHUNK = 128
N_GROUPS = 4
EXPERTS_PER_GROUP = 8
N_EXPERTS = N_GROUPS * EXPERTS_PER_GROUP
MOE_BLOCK = 128
DMA_UNROLL = 8
ROUTE_TILE = 512
DT_LO = 4608
DT_HI = DT_LO + SSD_GROUPS * SSD_HEADS_PER_GROUP


def _cparams(*sem):
    return pltpu.CompilerParams(dimension_semantics=sem, vmem_limit_bytes=VMEM_LIMIT)


def _rms(x, w=None):
    y = x * lax.rsqrt(jnp.mean(x * x, axis=-1, keepdims=True) + EPS)
    return y if w is None else y * w


def _sigmoid(x):
    return 1.0 / (1.0 + jnp.exp(-x))


def _silu(x):
    return x * _sigmoid(x)


def _dot(a, b):
    return jnp.dot(a, b, preferred_element_type=F32)


def _dot_nt(a, b):
    return lax.dot_general(a, b, (((1,), (1,)), ((), ())), preferred_element_type=F32)


def _dot_tn(a, b):
    return lax.dot_general(a, b, (((0,), (0,)), ((), ())), preferred_element_type=F32)


def _tri_incl():
    ii = lax.broadcasted_iota(jnp.int32, (CHUNK, CHUNK), 0)
    jj = lax.broadcasted_iota(jnp.int32, (CHUNK, CHUNK), 1)
    return ii, jj


def _slab_rows(d):
    return d // LANE


def _slabs_to_rows(ref2d, base, rows, srows):
    return jnp.concatenate([ref2d[pl.ds(base + c, rows, stride=srows), :] for c in range(srows)], axis=1)


def _rows_to_slabs(ref2d, base, val, srows):
    rows = val.shape[0]
    for c in range(srows):
        ref2d[pl.ds(base + c, rows, stride=srows), :] = val[:, c * LANE:(c + 1) * LANE]


def _wprep_kernel(w_ref, wm_ref):
    nm = w_ref.shape[1] - (DT_HI - DT_LO)
    wm_ref[:, 0:DT_LO] = w_ref[:, 0:DT_LO].astype(BF16)
    wm_ref[:, DT_LO:nm] = w_ref[:, DT_HI:].astype(BF16)
    rows = w_ref.shape[0]
    pad = jnp.zeros((rows, LANE - SSD_HEADS_PER_GROUP), BF16)
    parts = []
    for g in range(SSD_GROUPS):
        lo = DT_LO + g * SSD_HEADS_PER_GROUP
        parts += [w_ref[:, lo:lo + SSD_HEADS_PER_GROUP].astype(BF16), pad]
    wm_ref[:, nm:] = jnp.concatenate(parts, axis=1)


def _wprep(w, layer):
    _, d, nin = w.shape
    tr = 256
    nout = nin - (DT_HI - DT_LO) + SSD_GROUPS * LANE
    return pl.pallas_call(
        _wprep_kernel,
        grid=(d // tr,),
        in_specs=[pl.BlockSpec((None, tr, nin), lambda i: (layer, i, 0))],
        out_specs=pl.BlockSpec((tr, nout), lambda i: (i, 0)),
        out_shape=jax.ShapeDtypeStruct((d, nout), BF16),
        compiler_params=_cparams("arbitrary"),
        name="wprep",
    )(w)


def _norm_kernel(*refs, combine, write_h):
    if combine:
        h_ref, o0_ref, o1_ref, rt_ref, nw_ref = refs[:5]
        outs = refs[5:]
        rows = h_ref.shape[0]
        srows = _slab_rows(h_ref.shape[1])
        rt = rt_ref[...]
        h = (h_ref[...] + rt[:, 2:3] * _slabs_to_rows(o0_ref, 0, rows, srows)
             + rt[:, 3:4] * _slabs_to_rows(o1_ref, 0, rows, srows))
    else:
        h_ref, nw_ref = refs[:2]
        outs = refs[2:]
        h = h_ref[...]
    u_ref = outs[0]
    u_ref[...] = _rms(h, nw_ref[...]).astype(u_ref.dtype)
    if write_h:
        outs[1][...] = h


def _norm_pass(h, nw, moe, out_dtype, write_h):
    n, d = h.shape
    combine = moe is not None
    tm = min(512, n)
    row = lambda i: (i, 0)
    in_specs = [pl.BlockSpec((tm, d), row)]
    args = [h]
    if combine:
        o2d, rt = moe
        srows = _slab_rows(d)
        nt = n // tm
        in_specs += [pl.BlockSpec((tm * srows, LANE), row),
                     pl.BlockSpec((tm * srows, LANE), lambda i: (nt + i, 0)),
                     pl.BlockSpec((tm, LANE), row)]
        args += [o2d, o2d, rt]
    in_specs.append(pl.BlockSpec((1, d), lambda i: (0, 0)))
    args.append(nw)
    out_shape = [jax.ShapeDtypeStruct((n, d), out_dtype)]
    out_specs = [pl.BlockSpec((tm, d), row)]
    if write_h:
        out_shape.append(jax.ShapeDtypeStruct((n, d), F32))
        out_specs.append(pl.BlockSpec((tm, d), row))
    return pl.pallas_call(
        functools.partial(_norm_kernel, combine=combine, write_h=write_h),
        grid=(n // tm,),
        in_specs=in_specs, out_specs=out_specs, out_shape=out_shape,
        compiler_params=_cparams("arbitrary"),
        name="norm_pass",
    )(*args)


def _inproj_kernel(u_ref, w_ref, p_ref):
    p_ref[...] = _dot(u_ref[...], w_ref[...])


def _inproj(u, w):
    n, d = u.shape
    np_ = w.shape[1]
    tm = min(1024, n)
    tn = np_ // 3
    return pl.pallas_call(
        _inproj_kernel,
        grid=(np_ // tn, n // tm),
        in_specs=[pl.BlockSpec((tm, d), lambda j, i: (i, 0)), pl.BlockSpec((d, tn), lambda j, i: (0, j))],
        out_specs=pl.BlockSpec((tm, tn), lambda j, i: (i, j)),
        out_shape=jax.ShapeDtypeStruct((n, np_), F32),
        compiler_params=_cparams("arbitrary", "arbitrary"),
        name="inproj",
    )(u, w)


def _ret_kernel(q_ref, k_ref, v_ref, g_ref, cos_ref, sin_ref, o_ref, s_scr):
    t = q_ref.shape[0]
    hf = pl.program_id(1).astype(F32)
    lg = jnp.log(1.0 - jnp.exp2(jnp.full((1, LANE), -5.0, F32) - hf))
    ii, jj = _tri_incl()
    causal = ii >= jj
    dmat = jnp.where(causal, jnp.exp(jnp.where(causal, (ii - jj).astype(F32) * lg, 0.0)), 0.0)
    iif = ii.astype(F32)
    ecum = jnp.exp((iif + 1.0) * lg)
    wk = jnp.exp((CHUNK - 1.0 - iif) * lg)
    elast = jnp.exp(float(CHUNK) * lg)
    scale = HEAD_DIM ** -0.5
    s_scr[...] = jnp.zeros_like(s_scr)

    def step(c, carry):
        r = pl.ds(pl.multiple_of(c * CHUNK, CHUNK), CHUNK)
        cs, sn = cos_ref[r, :], sin_ref[r, :]
        q, k = q_ref[r, :], k_ref[r, :]
        qr = q * cs + pltpu.roll(q, HEAD_DIM // 2, 1) * sn
        kr = (k * cs + pltpu.roll(k, HEAD_DIM // 2, 1) * sn) * scale
        vb = v_ref[r, :].astype(BF16)
        s = s_scr[...]
        scores = _dot_nt(qr.astype(BF16), kr.astype(BF16)) * dmat
        out = _dot(scores.astype(BF16), vb) + _dot((qr * ecum).astype(BF16), s.astype(BF16))
        s_scr[...] = elast * s + _dot_tn((kr * wk).astype(BF16), vb)
        o_ref[r, :] = (_silu(g_ref[r, :]) * _rms(out)).astype(o_ref.dtype)
        return carry

    lax.fori_loop(0, t // CHUNK, step, 0)


def _retention(p, cos_t, sin_t, bsz, t):
    n = bsz * t
    blk = lambda off: pl.BlockSpec((t, HEAD_DIM), lambda b, h, off=off: (b, off + h))
    tab = pl.BlockSpec((t, HEAD_DIM), lambda b, h: (0, 0))
    return pl.pallas_call(
        _ret_kernel,
        grid=(bsz, RET_HEADS),
        in_specs=[blk(0), blk(RET_HEADS), blk(2 * RET_HEADS), blk(3 * RET_HEADS), tab, tab],
        out_specs=pl.BlockSpec((t, HEAD_DIM), lambda b, h: (b, h)),
        out_shape=jax.ShapeDtypeStruct((n, RET_HEADS * HEAD_DIM), BF16),
        scratch_shapes=[pltpu.VMEM((HEAD_DIM, HEAD_DIM), F32)],
        compiler_params=_cparams("arbitrary", "arbitrary"),
        name="retention",
    )(p, p, p, p, cos_t, sin_t)


def _ssd_kernel(z_ref, x_ref, b_ref, c_ref, dt_ref, cwx_ref, cwb_ref, cwc_ref, cbx_ref, cbb_ref, cbc_ref,
                dtb_ref, alog_ref, dsk_ref, nw_ref, o_ref, s_scr):
    t = z_ref.shape[0]
    ii, jj = _tri_incl()
    causal = ii >= jj
    tri = jnp.where(causal, 1.0, 0.0).astype(F32)
    lane_lo = lax.broadcasted_iota(jnp.int32, (CHUNK, LANE), 1) < SSD_HEADDIM
    neg_a = -jnp.exp(alog_ref[...])
    s_scr[...] = jnp.zeros_like(s_scr)

    def conv(ref, w_ref, bias_ref, c):
        r = pl.ds(pl.multiple_of(c * CHUNK, CHUNK), CHUNK)
        rp = pl.ds(pl.multiple_of(jnp.maximum(c * CHUNK - 8, 0), 8), 8)
        prev = jnp.where(c > 0, ref[rp, :], 0.0)
        ext = jnp.concatenate([prev, ref[r, :]], axis=0)
        w = w_ref[...]
        acc = bias_ref[...]
        for i in range(SSD_CONV):
            off = 8 - (SSD_CONV - 1) + i
            acc = acc + w[i:i + 1, :] * ext[off:off + CHUNK, :]
        return _silu(acc)

    def step(c, carry):
        r = pl.ds(pl.multiple_of(c * CHUNK, CHUNK), CHUNK)
        xs = conv(x_ref, cwx_ref, cbx_ref, c)
        bm = conv(b_ref, cwb_ref, cbb_ref, c).astype(BF16)
        cm = conv(c_ref, cwc_ref, cbc_ref, c).astype(BF16)
        xr = dt_ref[r, :] + dtb_ref[...]
        dt = jnp.maximum(xr, 0.0) + jnp.log1p(jnp.exp(-jnp.abs(xr)))
        la = dt * neg_a
        cum = jnp.dot(tri, la, precision=HIGHEST, preferred_element_type=F32)
        last = cum[CHUNK - 1:CHUNK, :]
        wj = dt * jnp.exp(last - cum)
        ecum = jnp.exp(cum)
        cum_t = cum.T
        dt_t = dt.T
        g = _dot_nt(cm, bm)
        s = s_scr[...]
        cs = _dot(cm, s.astype(BF16))
        ys, xws, els = [], [], []
        for pr in range(SSD_HEADS_PER_GROUP // 2):
            xp = xs[:, pr * LANE:(pr + 1) * LANE]
            acc = None
            ecp, wjp, elp = None, None, None
            for half in range(2):
                hd = 2 * pr + half
                ci = jnp.broadcast_to(cum[:, hd:hd + 1], (CHUNK, CHUNK))
                dec = jnp.exp(jnp.where(causal, ci - cum_t[hd:hd + 1, :], 0.0))
                m = jnp.where(causal, g * dec * dt_t[hd:hd + 1, :], 0.0)
                sel = lane_lo if half == 0 else jnp.logical_not(lane_lo)
                y = _dot(m.astype(BF16), jnp.where(sel, xp, 0.0).astype(BF16))
                acc = y if acc is None else acc + y
                eb = jnp.broadcast_to(ecum[:, hd:hd + 1], (CHUNK, LANE))
                wb = jnp.broadcast_to(wj[:, hd:hd + 1], (CHUNK, LANE))
                lb = jnp.broadcast_to(ecum[CHUNK - 1:CHUNK, hd:hd + 1], (1, LANE))
                ecp = eb if half == 0 else jnp.where(lane_lo, ecp, eb)
                wjp = wb if half == 0 else jnp.where(lane_lo, wjp, wb)
                elp = lb if half == 0 else jnp.where(lane_lo[0:1, :], elp, lb)
            ys.append(acc + ecp * cs[:, pr * LANE:(pr + 1) * LANE])
            xws.append((wjp * xp).astype(BF16))
            els.append(elp)
        y = jnp.concatenate(ys, axis=1)
        s_scr[...] = jnp.concatenate(els, axis=1) * s + _dot_tn(bm, jnp.concatenate(xws, axis=1))
        y = (y + dsk_ref[...] * xs) * _silu(z_ref[r, :])
        o_ref[r, :] = (_rms(y) * nw_ref[...]).astype(o_ref.dtype)
        return carry

    lax.fori_loop(0, t // CHUNK, step, 0)


def _ssd(p, conv_w, conv_b, dtb, alog, dsk, nw, bsz, t):
    n = bsz * t
    gw = SSD_HEADS_PER_GROUP * SSD_HEADDIM
    st = HEAD_DIM
    z_off, x_off = 2048 // gw, 3072 // gw
    b_off, c_off = 4096 // st, 4352 // st
    dt_off = (p.shape[1] - SSD_GROUPS * LANE) // LANE
    xw = SSD_GROUPS * gw
    par = lambda shape, f: pl.BlockSpec(shape, f)
    in_specs = [
        par((t, gw), lambda b, g: (b, z_off + g)),
        par((t, gw), lambda b, g: (b, x_off + g)),
        par((t, st), lambda b, g: (b, b_off + g)),
        par((t, st), lambda b, g: (b, c_off + g)),
        par((t, LANE), lambda b, g: (b, dt_off + g)),
        par((SSD_CONV, gw), lambda b, g: (0, g)),
        par((SSD_CONV, st), lambda b, g: (0, xw // st + g)),
        par((SSD_CONV, st), lambda b, g: (0, xw // st + SSD_GROUPS + g)),
        par((1, gw), lambda b, g: (0, g)),
        par((1, st), lambda b, g: (0, xw // st + g)),
        par((1, st), lambda b, g: (0, xw // st + SSD_GROUPS + g)),
        par((1, LANE), lambda b, g: (0, g)),
        par((1, LANE), lambda b, g: (0, g)),
        par((1, gw), lambda b, g: (0, g)),
        par((1, gw), lambda b, g: (0, g)),
    ]
    return pl.pallas_call(
        _ssd_kernel,
        grid=(bsz, SSD_GROUPS),
        in_specs=in_specs,
        out_specs=pl.BlockSpec((t, gw), lambda b, g: (b, g)),
        out_shape=jax.ShapeDtypeStruct((n, SSD_GROUPS * gw), BF16),
        scratch_shapes=[pltpu.VMEM((st, gw), F32)],
        compiler_params=_cparams("arbitrary", "arbitrary"),
        name="ssd",
    )(p, p, p, p, p, conv_w, conv_w, conv_w, conv_b, conv_b, conv_b, dtb, alog, dsk, nw)


def _hgrn_kernel(q_ref, f_ref, i_ref, g_ref, lb_ref, nw_ref, o_ref, st_scr, cum_scr, *, layer):
    t = q_ref.shape[0]
    lbm = lb_ref[...]
    depth = lbm.shape[0]
    mx = lbm[0:1, :]
    for i in range(1, depth):
        mx = jnp.maximum(mx, lbm[i:i + 1, :])
    ex = [jnp.exp(lbm[i:i + 1, :] - mx) for i in range(depth)]
    den = ex[0]
    for i in range(1, depth):
        den = den + ex[i]
    sm = [e / den for e in ex]
    csum = sm[0]
    for i in range(1, layer + 1):
        csum = csum + sm[i]
    lb = jnp.maximum(csum - sm[0], 0.0)
    log_lb = jnp.log(jnp.maximum(lb, LB_FLOOR))
    l1m = jnp.log1p(-lb)
    oml = 1.0 - lb

    ii, jj = _tri_incl()
    tri = jnp.where(ii >= jj, 1.0, 0.0).astype(F32)
    lvl = jnp.where(ii > jj, 31 - lax.clz(ii ^ jj), -1)
    eye = ii == jj
    row = lax.broadcasted_iota(jnp.int32, (CHUNK, LANE), 0)
    scale = HEAD_DIM ** -0.5
    nlev = int(math.log2(CHUNK))
    st_scr[...] = jnp.zeros_like(st_scr)

    def step(c, carry):
        r = pl.ds(pl.multiple_of(c * CHUNK, CHUNK), CHUNK)
        q = _silu(q_ref[r, :]) * scale
        f = f_ref[r, :]
        ls = jnp.minimum(f, 0.0) - jnp.log1p(jnp.exp(-jnp.abs(f)))
        a, b = log_lb, l1m + ls
        lf = jnp.maximum(a, b) + jnp.log1p(jnp.exp(-jnp.abs(a - b)))
        kk = oml / (1.0 + jnp.exp(f))
        vb = i_ref[r, :].astype(BF16)
        cum = jnp.dot(tri, lf, precision=HIGHEST, preferred_element_type=F32)
        cum_scr[...] = cum
        scores = jnp.where(eye, _dot_nt(q.astype(BF16), kk.astype(BF16)), 0.0)
        for lv in range(nlev):
            s = 1 << lv
            if 2 * s >= 8:
                ref = jnp.concatenate(
                    [jnp.broadcast_to(cum_scr[g0 * 2 * s + s - 1:g0 * 2 * s + s, :], (2 * s, LANE))
                     for g0 in range(CHUNK // (2 * s))], axis=0)
            elif s == 2:
                m4 = row & 3
                ref = jnp.where(m4 == 0, pltpu.roll(cum, CHUNK - 1, 0),
                                jnp.where(m4 == 1, cum,
                                          jnp.where(m4 == 2, pltpu.roll(cum, 1, 0), pltpu.roll(cum, 2, 0))))
            else:
                ref = jnp.where((row & 1) == 1, pltpu.roll(cum, 1, 0), cum)
            qs = q * jnp.exp(jnp.minimum(cum - ref, 0.0))
            ks = kk * jnp.exp(jnp.minimum(ref - cum, 0.0))
            scores = jnp.where(lvl == lv, _dot_nt(qs.astype(BF16), ks.astype(BF16)), scores)
        st = st_scr[...]
        last = cum[CHUNK - 1:CHUNK, :]
        out = _dot(scores.astype(BF16), vb) + _dot_nt((q * jnp.exp(cum)).astype(BF16), st.astype(BF16))
        st_scr[...] = st * jnp.exp(last) + _dot_tn(vb, (kk * jnp.exp(last - cum)).astype(BF16))
        o_ref[r, :] = (_silu(g_ref[r, :]) * (_rms(out) * nw_ref[...])).astype(o_ref.dtype)
        return carry

    lax.fori_loop(0, t // CHUNK, step, 0)


def _hgrn(p, lbounds, nw, layer, bsz, t):
    n = bsz * t
    base = DT_LO // HEAD_DIM
    blk = lambda off: pl.BlockSpec((t, HEAD_DIM), lambda b, h, off=off: (b, base + off + h))
    return pl.pallas_call(
        functools.partial(_hgrn_kernel, layer=layer),
        grid=(bsz, HGRN_HEADS),
        in_specs=[blk(0), blk(HGRN_HEADS), blk(2 * HGRN_HEADS), blk(3 * HGRN_HEADS),
                  pl.BlockSpec((lbounds.shape[0], HEAD_DIM), lambda b, h: (0, h)),
                  pl.BlockSpec((1, HEAD_DIM), lambda b, h: (0, h))],
        out_specs=pl.BlockSpec((t, HEAD_DIM), lambda b, h: (b, h)),
        out_shape=jax.ShapeDtypeStruct((n, HGRN_HEADS * HEAD_DIM), BF16),
        scratch_shapes=[pltpu.VMEM((HEAD_DIM, HEAD_DIM), F32), pltpu.VMEM((CHUNK, HEAD_DIM), F32)],
        compiler_params=_cparams("arbitrary", "arbitrary"),
        name="hgrn2",
    )(p, p, p, p, lbounds, nw)


def _outproj_kernel(a_ref, b_ref, c_ref, h_ref, w_ref, nw_ref, wr_ref, br_ref, hout_ref, u_ref, lg_ref):
    wa, wb = a_ref.shape[1], b_ref.shape[1]
    acc = _dot(a_ref[...], w_ref[0:wa, :])
    acc = acc + _dot(b_ref[...], w_ref[wa:wa + wb, :])
    acc = acc + _dot(c_ref[...], w_ref[wa + wb:, :])
    h = h_ref[...] + acc
    hout_ref[...] = h
    u = _rms(h, nw_ref[...])
    u_hi = u.astype(BF16)
    u_lo = (u - u_hi.astype(F32)).astype(BF16)
    t2 = _dot(u_hi, wr_ref[...])
    lg_ref[...] = t2[:, :LANE] + t2[:, LANE:] + _dot(u_lo, wr_ref[:, :LANE]) + br_ref[...]
    half = u.shape[1] // 2
    bits = lax.bitcast_convert_type(u_hi.astype(F32), jnp.uint32)
    packed = (bits[:, :half] >> 16) | (bits[:, half:] & jnp.uint32(0xFFFF0000))
    _rows_to_slabs(u_ref, 0, packed, half // LANE)


def _outproj(o_a, o_b, o_c, h, w_out, nw, w_r, b_r):
    n, d = h.shape
    tm = min(512, n)
    srows = _slab_rows(d // 2)
    row = lambda i: (i, 0)
    fix = lambda i: (0, 0)
    return pl.pallas_call(
        _outproj_kernel,
        grid=(n // tm,),
        in_specs=[pl.BlockSpec((tm, o_a.shape[1]), row), pl.BlockSpec((tm, o_b.shape[1]), row),
                  pl.BlockSpec((tm, o_c.shape[1]), row), pl.BlockSpec((tm, d), row),
                  pl.BlockSpec(w_out.shape, fix), pl.BlockSpec((1, d), fix),
                  pl.BlockSpec(w_r.shape, fix), pl.BlockSpec((1, LANE), fix)],
        out_specs=[pl.BlockSpec((tm, d), row), pl.BlockSpec((tm * srows, LANE), row),
                   pl.BlockSpec((tm, LANE), row)],
        out_shape=[jax.ShapeDtypeStruct((n, d), F32), jax.ShapeDtypeStruct((n * srows, LANE), jnp.uint32),
                   jax.ShapeDtypeStruct((n, LANE), F32)],
        compiler_params=_cparams("arbitrary"),
        name="outproj",
    )(o_a, o_b, o_c, h, w_out, nw, w_r, b_r)


def _route_kernel(lg_ref, rt_ref, cnt_ref, carry_scr):
    tr = lg_ref.shape[0]

    @pl.when(pl.program_id(0) == 0)
    def _():
        carry_scr[...] = jnp.zeros_like(carry_scr)

    lg = lg_ref[...]
    lane = lax.broadcasted_iota(jnp.int32, (tr, LANE), 1)
    neg = -jnp.inf
    big = jnp.int32(LANE)

    def first_max(vals):
        m = jnp.max(vals, axis=-1, keepdims=True)
        idx = jnp.min(jnp.where(vals == m, lane, big), axis=-1, keepdims=True)
        return m, idx

    gl = jnp.where(lane < N_GROUPS, lg, neg)
    gmax, gidx = first_max(gl)
    gate = 1.0 / jnp.sum(jnp.exp(gl - gmax), axis=-1, keepdims=True)
    lo = N_GROUPS + EXPERTS_PER_GROUP * gidx
    el = jnp.where((lane >= lo) & (lane < lo + EXPERTS_PER_GROUP), lg, neg)
    m1, i1 = first_max(el)
    m2, i2 = first_max(jnp.where(lane == i1, neg, el))
    e21 = jnp.exp(m2 - m1)
    w1 = gate * (1.0 / (1.0 + e21))
    w2 = gate * (e21 / (1.0 + e21))

    oh1 = lane == i1
    oh2 = lane == i2
    m = jnp.where(oh1 | oh2, 1.0, 0.0)
    ti = lax.broadcasted_iota(jnp.int32, (tr, tr), 0)
    tj = lax.broadcasted_iota(jnp.int32, (tr, tr), 1)
    before = _dot(jnp.where(ti > tj, 1.0, 0.0).astype(BF16), m.astype(BF16)) + carry_scr[0:1, :]
    r1 = jnp.sum(jnp.where(oh1, before, 0.0), axis=-1, keepdims=True)
    r2 = jnp.sum(jnp.where(oh2, before, 0.0), axis=-1, keepdims=True)
    total = carry_scr[0:1, :] + jnp.sum(m, axis=0, keepdims=True)
    carry_scr[...] = jnp.broadcast_to(total, carry_scr.shape)
    cnt_ref[...] = jnp.broadcast_to(total, cnt_ref.shape)

    e1 = (i1 - N_GROUPS).astype(F32)
    e2 = (i2 - N_GROUPS).astype(F32)
    out = jnp.zeros((tr, LANE), F32)
    for pos, val in enumerate((e1, e2, w1, w2, r1, r2)):
        out = jnp.where(lane == pos, val, out)
    rt_ref[...] = out


def _route(logits):
    n = logits.shape[0]
    tr = min(ROUTE_TILE, n)
    return pl.pallas_call(
        _route_kernel,
        grid=(n // tr,),
        in_specs=[pl.BlockSpec((tr, LANE), lambda i: (i, 0))],
        out_specs=[pl.BlockSpec((tr, LANE), lambda i: (i, 0)), pl.BlockSpec((8, LANE), lambda i: (0, 0))],
        out_shape=[jax.ShapeDtypeStruct((n, LANE), F32), jax.ShapeDtypeStruct((8, LANE), F32)],
        scratch_shapes=[pltpu.VMEM((8, LANE), F32)],
        compiler_params=_cparams("arbitrary"),
        name="route",
    )(logits)


def _ffn_kernel(blk_e_ref, nv_ref, code_ref, u_hbm, wg_ref, wu_ref, wd_ref, o_hbm,
                xbuf, ybuf, wgb, wub, wdb, gsem, ssem, *, tb, xrows, yrows):
    b = pl.program_id(0)
    nb = pl.num_programs(0)
    n_tok = u_hbm.shape[0] // xrows
    slot = b % 2

    def slab(ref, srows, idx, count=1):
        return ref.at[pl.ds(pl.multiple_of(idx * srows, srows), count * srows)]

    def row_loop(count, fn):
        full = count // DMA_UNROLL

        def grp(i, c):
            for k in range(DMA_UNROLL):
                fn(i * DMA_UNROLL + k)
            return c
        lax.fori_loop(0, full, grp, 0)

        def one(r, c):
            fn(r)
            return c
        lax.fori_loop(full * DMA_UNROLL, count, one, 0)

    def issue_gather(blk, sl, count):
        def fn(r):
            tok = code_ref[blk * tb + r] >> 1
            pltpu.make_async_copy(slab(u_hbm, xrows, tok), slab(xbuf, xrows, sl * tb + r), gsem.at[sl]).start()
        row_loop(count, fn)

    def wait_gather(sl, count):
        pltpu.make_async_copy(slab(u_hbm, xrows, 0, count), slab(xbuf, xrows, sl * tb, count), gsem.at[sl]).wait()

    def wait_scatter(sl, count):
        pltpu.make_async_copy(slab(ybuf, yrows, sl * tb, count), slab(o_hbm, yrows, 0, count), ssem.at[sl]).wait()

    @pl.when(b == 0)
    def _():
        xbuf[...] = jnp.zeros_like(xbuf)

        @pl.when(nv_ref[0] > 0)
        def _():
            issue_gather(0, 0, nv_ref[0])

    nxt = jnp.minimum(b + 1, nb - 1)
    nv_next = jnp.where(b + 1 < nb, nv_ref[nxt], 0)

    @pl.when(nv_next > 0)
    def _():
        issue_gather(b + 1, 1 - slot, nv_next)

    prev = jnp.maximum(b - 1, 0)

    @pl.when((b == 0) | (blk_e_ref[b] != blk_e_ref[prev]))
    def _():
        wgb[...] = wg_ref[...].astype(BF16)
        wub[...] = wu_ref[...].astype(BF16)
        wdb[...] = wd_ref[...].astype(BF16)

    nv = nv_ref[b]

    @pl.when(nv > 0)
    def _():
        wait_gather(slot, nv)
        words = _slabs_to_rows(xbuf, slot * tb * xrows, tb, xrows)
        x = jnp.concatenate(
            [lax.bitcast_convert_type(words << 16, F32).astype(BF16),
             lax.bitcast_convert_type(words & jnp.uint32(0xFFFF0000), F32).astype(BF16)], axis=1)
        hdn = _silu(_dot(x, wgb[...])) * _dot(x, wub[...])
        y = _dot(hdn.astype(BF16), wdb[...])
        _rows_to_slabs(ybuf, slot * tb * yrows, y, yrows)

        def send(r):
            code = code_ref[b * tb + r]
            dst = (code & 1) * n_tok + (code >> 1)
            pltpu.make_async_copy(slab(ybuf, yrows, slot * tb + r), slab(o_hbm, yrows, dst), ssem.at[slot]).start()
        row_loop(nv, send)

    nv_prev = jnp.where(b > 0, nv_ref[prev], 0)

    @pl.when(nv_prev > 0)
    def _():
        wait_scatter(1 - slot, nv_prev)

    @pl.when((b == nb - 1) & (nv > 0))
    def _():
        wait_scatter(slot, nv)


def _ffn(blk_e, nv, code, u2d, wg, wu, wd, layer):
    d, de = wg.shape[2], wg.shape[3]
    xrows, yrows = _slab_rows(d // 2), _slab_rows(d)
    n = u2d.shape[0] // xrows
    nblk = blk_e.shape[0]
    tb = MOE_BLOCK
    grid_spec = pltpu.PrefetchScalarGridSpec(
        num_scalar_prefetch=3,
        grid=(nblk,),
        in_specs=[pl.BlockSpec(memory_space=pl.ANY),
                  pl.BlockSpec((None, None, d, de), lambda b, be, nv, cd: (layer, be[b], 0, 0)),
                  pl.BlockSpec((None, None, d, de), lambda b, be, nv, cd: (layer, be[b], 0, 0)),
                  pl.BlockSpec((None, None, de, d), lambda b, be, nv, cd: (layer, be[b], 0, 0))],
        out_specs=pl.BlockSpec(memory_space=pl.ANY),
        scratch_shapes=[pltpu.VMEM((2 * tb * xrows, LANE), jnp.uint32), pltpu.VMEM((2 * tb * yrows, LANE), F32),
                        pltpu.VMEM((d, de), BF16), pltpu.VMEM((d, de), BF16), pltpu.VMEM((de, d), BF16),
                        pltpu.SemaphoreType.DMA((2,)), pltpu.SemaphoreType.DMA((2,))],
    )
    return pl.pallas_call(
        functools.partial(_ffn_kernel, tb=tb, xrows=xrows, yrows=yrows),
        grid_spec=grid_spec,
        out_shape=jax.ShapeDtypeStruct((2 * n * yrows, LANE), F32),
        compiler_params=_cparams("arbitrary"),
        name="moe_ffn",
    )(blk_e, nv, code, u2d, wg, wu, wd)


def _dispatch_tables(route, counts, n):
    tb = MOE_BLOCK
    e = route[:, 0:2].astype(jnp.int32)
    rank = route[:, 4:6].astype(jnp.int32)
    cnt = counts[0, N_GROUPS:N_GROUPS + N_EXPERTS].astype(jnp.int32)
    padded = ((cnt + tb - 1) // tb) * tb
    pends = jnp.cumsum(padded)
    pstarts = pends - padded
    dest = (pstarts[e] + rank).reshape(-1)
    p_total = 2 * n + N_EXPERTS * tb
    nblk = p_total // tb
    code = jnp.full((p_total,), -1, jnp.int32).at[dest].set(
        jnp.arange(2 * n, dtype=jnp.int32), unique_indices=True, mode="promise_in_bounds")
    bstart = jnp.arange(nblk, dtype=jnp.int32) * tb
    blk_e = jnp.minimum(jnp.sum((bstart[:, None] >= pends[None, :]).astype(jnp.int32), axis=1), N_EXPERTS - 1)
    nv = jnp.clip(cnt[blk_e] - (bstart - pstarts[blk_e]), 0, tb)
    nv = jnp.where(bstart < pends[-1], nv, 0).astype(jnp.int32)
    return blk_e, nv, code


def _pad_lanes(v, width):
    return jnp.pad(v, ((0, 0), (0, width - v.shape[1])))


def kernel(x, attn_norm_w, w_in, ssd_conv_w, ssd_conv_b, ssd_dt_bias, ssd_a_log, ssd_d, ssd_norm_w, hgrn_lower_bounds, hgrn_norm_w, w_out, ffn_norm_w, router_group_w, router_group_b, router_expert_w, router_expert_b, expert_w_gate, expert_w_up, expert_w_down, final_norm_w):
    bsz, t, d = x.shape
    n = bsz * t
    depth = w_in.shape[0]

    half = HEAD_DIM // 2
    inv = 1.0 / (ROPE_THETA ** (jnp.arange(half, dtype=F32) / half))
    ang = jnp.arange(t, dtype=F32)[:, None] * inv[None, :]
    cos_t = jnp.concatenate([jnp.cos(ang), jnp.cos(ang)], axis=1)
    sin_t = jnp.concatenate([-jnp.sin(ang), jnp.sin(ang)], axis=1)

    def group_lanes(v):
        v = v.reshape(SSD_GROUPS, SSD_HEADS_PER_GROUP)
        return _pad_lanes(v, LANE).reshape(1, SSD_GROUPS * LANE)

    h = x.reshape(n, d)
    moe = None
    for l in range(depth):
        if moe is None:
            (u,) = _norm_pass(h, attn_norm_w[l][None, :], None, BF16, False)
        else:
            u, h = _norm_pass(h, attn_norm_w[l][None, :], moe, BF16, True)
        p = _inproj(u, _wprep(w_in, l))

        o_ret = _retention(p, cos_t, sin_t, bsz, t)
        o_ssd = _ssd(p, ssd_conv_w[l], ssd_conv_b[l][None, :], group_lanes(ssd_dt_bias[l]),
                     group_lanes(ssd_a_log[l]), jnp.repeat(ssd_d[l], SSD_HEADDIM)[None, :],
                     ssd_norm_w[l][None, :], bsz, t)
        o_hgrn = _hgrn(p, hgrn_lower_bounds, hgrn_norm_w[l][None, :], l, bsz, t)

        w_r = _pad_lanes(jnp.concatenate([router_group_w[l], router_expert_w[l]], axis=1), LANE)
        w_r_hi = w_r.astype(BF16)
        w_r = jnp.concatenate([w_r_hi, (w_r - w_r_hi.astype(F32)).astype(BF16)], axis=1)
        b_r = _pad_lanes(jnp.concatenate([router_group_b[l], router_expert_b[l]])[None, :], LANE)
        h, u2d, logits = _outproj(o_ret, o_ssd, o_hgrn, h, w_out[l].astype(BF16), ffn_norm_w[l][None, :], w_r, b_r)
        route, counts = _route(logits)
        blk_e, nv, code = _dispatch_tables(route, counts, n)
        o2d = _ffn(blk_e, nv, code, u2d, expert_w_gate, expert_w_up, expert_w_down, l)
        moe = (o2d, route)

    (out,) = _norm_pass(h, final_norm_w[None, :], moe, F32, False)
    return out.reshape(bsz, t, d)
```

```python
import functools
import math

import jax
import jax.numpy as jnp
from jax import lax
from jax.experimental import pallas as pl
from jax.experimental.pallas import tpu as pltpu

F32 = jnp.float32
BF16 = jnp.bfloat16
HIGHEST = lax.Precision.HIGHEST

V7X_VMEM_BYTES = 64 * 1024 * 1024
VMEM_LIMIT = V7X_VMEM_BYTES - 8 * 1024 * 1024
LANE = 128

EPS = 1e-6
LB_FLOOR = 1e-30
ROPE_THETA = 10000.0

RET_HEADS = 4
HEAD_DIM = 128
SSD_HEADDIM = 64
SSD_GROUPS = 2
SSD_HEADS_PER_GROUP = 8
SSD_CONV = 4
HGRN_HEADS = 4
CHUNK = 128
N_GROUPS = 4
EXPERTS_PER_GROUP = 8
N_EXPERTS = N_GROUPS * EXPERTS_PER_GROUP
MOE_BLOCK = 128
DMA_UNROLL = 8
ROUTE_TILE = 512
DT_LO = 4608
DT_HI = DT_LO + SSD_GROUPS * SSD_HEADS_PER_GROUP


def _cparams(*sem):
    return pltpu.CompilerParams(dimension_semantics=sem, vmem_limit_bytes=VMEM_LIMIT)


def _rms(x, w=None):
    y = x * lax.rsqrt(jnp.mean(x * x, axis=-1, keepdims=True) + EPS)
    return y if w is None else y * w


def _sigmoid(x):
    return 1.0 / (1.0 + jnp.exp(-x))


def _silu(x):
    return x * _sigmoid(x)


def _dot(a, b):
    return jnp.dot(a, b, preferred_element_type=F32)


def _dot_nt(a, b):
    return lax.dot_general(a, b, (((1,), (1,)), ((), ())), preferred_element_type=F32)


def _dot_tn(a, b):
    return lax.dot_general(a, b, (((0,), (0,)), ((), ())), preferred_element_type=F32)


def _tri_incl():
    ii = lax.broadcasted_iota(jnp.int32, (CHUNK, CHUNK), 0)
    jj = lax.broadcasted_iota(jnp.int32, (CHUNK, CHUNK), 1)
    return ii, jj


def _slab_rows(d):
    return d // LANE


def _slabs_to_rows(ref2d, base, rows, srows):
    return jnp.concatenate([ref2d[pl.ds(base + c, rows, stride=srows), :] for c in range(srows)], axis=1)


def _rows_to_slabs(ref2d, base, val, srows):
    rows = val.shape[0]
    for c in range(srows):
        ref2d[pl.ds(base + c, rows, stride=srows), :] = val[:, c * LANE:(c + 1) * LANE]


def _wprep_kernel(w_ref, wm_ref):
    nm = w_ref.shape[1] - (DT_HI - DT_LO)
    wm_ref[:, 0:DT_LO] = w_ref[:, 0:DT_LO].astype(BF16)
    wm_ref[:, DT_LO:nm] = w_ref[:, DT_HI:].astype(BF16)
    rows = w_ref.shape[0]
    pad = jnp.zeros((rows, LANE - SSD_HEADS_PER_GROUP), BF16)
    parts = []
    for g in range(SSD_GROUPS):
        lo = DT_LO + g * SSD_HEADS_PER_GROUP
        parts += [w_ref[:, lo:lo + SSD_HEADS_PER_GROUP].astype(BF16), pad]
    wm_ref[:, nm:] = jnp.concatenate(parts, axis=1)


def _wprep(w):
    depth, d, nin = w.shape
    tr = 256
    nout = nin - (DT_HI - DT_LO) + SSD_GROUPS * LANE
    return pl.pallas_call(
        _wprep_kernel,
        grid=(depth, d // tr),
        in_specs=[pl.BlockSpec((None, tr, nin), lambda l, i: (l, i, 0))],
        out_specs=pl.BlockSpec((None, tr, nout), lambda l, i: (l, i, 0)),
        out_shape=jax.ShapeDtypeStruct((depth, d, nout), BF16),
        compiler_params=_cparams("arbitrary", "arbitrary"),
        name="wprep",
    )(w)


def _norm_kernel(*refs, combine, write_h):
    if combine:
        h_ref, o0_ref, o1_ref, rt_ref, nw_ref = refs[:5]
        outs = refs[5:]
        rows = h_ref.shape[0]
        srows = _slab_rows(h_ref.shape[1])
        rt = rt_ref[...]
        h = (h_ref[...] + rt[:, 2:3] * _slabs_to_rows(o0_ref, 0, rows, srows)
             + rt[:, 3:4] * _slabs_to_rows(o1_ref, 0, rows, srows))
    else:
        h_ref, nw_ref = refs[:2]
        outs = refs[2:]
        h = h_ref[...]
    u_ref = outs[0]
    u_ref[...] = _rms(h, nw_ref[...]).astype(u_ref.dtype)
    if write_h:
        outs[1][...] = h


def _norm_pass(h, nw, moe, out_dtype, write_h):
    n, d = h.shape
    combine = moe is not None
    tm = min(512, n)
    row = lambda i: (i, 0)
    in_specs = [pl.BlockSpec((tm, d), row)]
    args = [h]
    if combine:
        o2d, rt = moe
        srows = _slab_rows(d)
        nt = n // tm
        in_specs += [pl.BlockSpec((tm * srows, LANE), row),
                     pl.BlockSpec((tm * srows, LANE), lambda i: (nt + i, 0)),
                     pl.BlockSpec((tm, LANE), row)]
        args += [o2d, o2d, rt]
    in_specs.append(pl.BlockSpec((1, d), lambda i: (0, 0)))
    args.append(nw)
    out_shape = [jax.ShapeDtypeStruct((n, d), out_dtype)]
    out_specs = [pl.BlockSpec((tm, d), row)]
    if write_h:
        out_shape.append(jax.ShapeDtypeStruct((n, d), F32))
        out_specs.append(pl.BlockSpec((tm, d), row))
    return pl.pallas_call(
        functools.partial(_norm_kernel, combine=combine, write_h=write_h),
        grid=(n // tm,),
        in_specs=in_specs, out_specs=out_specs, out_shape=out_shape,
        compiler_params=_cparams("arbitrary"),
        name="norm_pass",
    )(*args)


def _inproj_kernel(u_ref, w_ref, p_ref):
    p_ref[...] = _dot(u_ref[...], w_ref[...])


def _inproj(u, w, layer):
    n, d = u.shape
    np_ = w.shape[2]
    tm = min(1024, n)
    tn = np_ // 3
    return pl.pallas_call(
        _inproj_kernel,
        grid=(np_ // tn, n // tm),
        in_specs=[pl.BlockSpec((tm, d), lambda j, i: (i, 0)),
                  pl.BlockSpec((None, d, tn), lambda j, i: (layer, 0, j))],
        out_specs=pl.BlockSpec((tm, tn), lambda j, i: (i, j)),
        out_shape=jax.ShapeDtypeStruct((n, np_), F32),
        compiler_params=_cparams("arbitrary", "arbitrary"),
        name="inproj",
    )(u, w)


def _ret_kernel(q_ref, k_ref, v_ref, g_ref, cos_ref, sin_ref, o_ref, s_scr):
    t = q_ref.shape[0]
    ii, jj = _tri_incl()
    causal = ii >= jj
    dist = (ii - jj).astype(F32)
    iif = ii.astype(F32)
    consts = []
    for hd in range(RET_HEADS):
        lg = math.log(1.0 - 2.0 ** (-5.0 - hd))
        dmat = jnp.where(causal, jnp.exp(jnp.where(causal, dist * lg, 0.0)), 0.0)
        ecum = jnp.exp((iif + 1.0) * lg)
        wk = jnp.exp((CHUNK - 1.0 - iif) * lg)
        consts.append((dmat, ecum, wk, math.exp(CHUNK * lg)))
    scale = HEAD_DIM ** -0.5
    s_scr[...] = jnp.zeros_like(s_scr)

    def step(c, carry):
        r = pl.ds(pl.multiple_of(c * CHUNK, CHUNK), CHUNK)
        cs, sn = cos_ref[r, :], sin_ref[r, :]
        for hd in range(RET_HEADS):
            dmat, ecum, wk, elast = consts[hd]
            cols = slice(hd * HEAD_DIM, (hd + 1) * HEAD_DIM)
            q, k = q_ref[r, cols], k_ref[r, cols]
            qr = q * cs + pltpu.roll(q, HEAD_DIM // 2, 1) * sn
            kr = (k * cs + pltpu.roll(k, HEAD_DIM // 2, 1) * sn) * scale
            vb = v_ref[r, cols].astype(BF16)
            s = s_scr[hd]
            scores = _dot_nt(qr.astype(BF16), kr.astype(BF16)) * dmat
            out = _dot(scores.astype(BF16), vb) + _dot((qr * ecum).astype(BF16), s.astype(BF16))
            s_scr[hd] = elast * s + _dot_tn((kr * wk).astype(BF16), vb)
            o_ref[r, cols] = (_silu(g_ref[r, cols]) * _rms(out)).astype(o_ref.dtype)
        return carry

    lax.fori_loop(0, t // CHUNK, step, 0)


def _retention(p, cos_t, sin_t, bsz, t):
    n = bsz * t
    width = RET_HEADS * HEAD_DIM
    blk = lambda off: pl.BlockSpec((t, width), lambda b, off=off: (b, off))
    tab = pl.BlockSpec((t, HEAD_DIM), lambda b: (0, 0))
    return pl.pallas_call(
        _ret_kernel,
        grid=(bsz,),
        in_specs=[blk(0), blk(1), blk(2), blk(3), tab, tab],
        out_specs=pl.BlockSpec((t, width), lambda b: (b, 0)),
        out_shape=jax.ShapeDtypeStruct((n, width), BF16),
        scratch_shapes=[pltpu.VMEM((RET_HEADS, HEAD_DIM, HEAD_DIM), F32)],
        compiler_params=_cparams("arbitrary"),
        name="retention",
    )(p, p, p, p, cos_t, sin_t)


def _ssd_kernel(z_ref, x_ref, b_ref, c_ref, dt_ref, cwx_ref, cwb_ref, cwc_ref, cbx_ref, cbb_ref, cbc_ref,
                dtb_ref, alog_ref, dsk_ref, nw_ref, o_ref, s_scr):
    t = z_ref.shape[0]
    ii, jj = _tri_incl()
    causal = ii >= jj
    tri = jnp.where(causal, 1.0, 0.0).astype(F32)
    lane_lo = lax.broadcasted_iota(jnp.int32, (CHUNK, LANE), 1) < SSD_HEADDIM
    neg_a = -jnp.exp(alog_ref[...])
    s_scr[...] = jnp.zeros_like(s_scr)

    def conv(ref, w_ref, bias_ref, c):
        r = pl.ds(pl.multiple_of(c * CHUNK, CHUNK), CHUNK)
        rp = pl.ds(pl.multiple_of(jnp.maximum(c * CHUNK - 8, 0), 8), 8)
        cur = ref[r, :]
        ext = jnp.concatenate([jnp.where(c > 0, ref[rp, :], 0.0), cur], axis=0)
        w = w_ref[...]
        acc = bias_ref[...] + w[SSD_CONV - 1:SSD_CONV, :] * cur
        for lag in range(1, SSD_CONV):
            shifted = pltpu.roll(ext, lag, 0)[8:8 + CHUNK, :]
            acc = acc + w[SSD_CONV - 1 - lag:SSD_CONV - lag, :] * shifted
        return _silu(acc)

    def step(c, carry):
        r = pl.ds(pl.multiple_of(c * CHUNK, CHUNK), CHUNK)
        xs = conv(x_ref, cwx_ref, cbx_ref, c)
        bm = conv(b_ref, cwb_ref, cbb_ref, c).astype(BF16)
        cm = conv(c_ref, cwc_ref, cbc_ref, c).astype(BF16)
        xr = dt_ref[r, :] + dtb_ref[...]
        dt = jnp.maximum(xr, 0.0) + jnp.log1p(jnp.exp(-jnp.abs(xr)))
        la = dt * neg_a
        cum = jnp.dot(tri, la, precision=HIGHEST, preferred_element_type=F32)
        last = cum[CHUNK - 1:CHUNK, :]
        wj = dt * jnp.exp(last - cum)
        ecum = jnp.exp(cum)
        cum_t = cum.T
        dt_t = dt.T
        gm = jnp.where(causal, _dot_nt(cm, bm), 0.0)
        s = s_scr[...]
        cs = _dot(cm, s.astype(BF16))
        ys, xws, els = [], [], []
        for pr in range(SSD_HEADS_PER_GROUP // 2):
            xp = xs[:, pr * LANE:(pr + 1) * LANE]
            acc = None
            ecp, wjp, elp = None, None, None
            for half in range(2):
                hd = 2 * pr + half
                ci = jnp.broadcast_to(cum[:, hd:hd + 1], (CHUNK, CHUNK))
                dec = jnp.exp(jnp.minimum(ci - cum_t[hd:hd + 1, :], 0.0))
                m = gm * dec * dt_t[hd:hd + 1, :]
                sel = lane_lo if half == 0 else jnp.logical_not(lane_lo)
                y = _dot(m.astype(BF16), jnp.where(sel, xp, 0.0).astype(BF16))
                acc = y if acc is None else acc + y
                eb = jnp.broadcast_to(ecum[:, hd:hd + 1], (CHUNK, LANE))
                wb = jnp.broadcast_to(wj[:, hd:hd + 1], (CHUNK, LANE))
                lb = jnp.broadcast_to(ecum[CHUNK - 1:CHUNK, hd:hd + 1], (1, LANE))
                ecp = eb if half == 0 else jnp.where(lane_lo, ecp, eb)
                wjp = wb if half == 0 else jnp.where(lane_lo, wjp, wb)
                elp = lb if half == 0 else jnp.where(lane_lo[0:1, :], elp, lb)
            ys.append(acc + ecp * cs[:, pr * LANE:(pr + 1) * LANE])
            xws.append((wjp * xp).astype(BF16))
            els.append(elp)
        y = jnp.concatenate(ys, axis=1)
        s_scr[...] = jnp.concatenate(els, axis=1) * s + _dot_tn(bm, jnp.concatenate(xws, axis=1))
        y = (y + dsk_ref[...] * xs) * _silu(z_ref[r, :])
        o_ref[r, :] = (_rms(y) * nw_ref[...]).astype(o_ref.dtype)
        return carry

    lax.fori_loop(0, t // CHUNK, step, 0)


def _ssd(p, conv_w, conv_b, dtb, alog, dsk, nw, bsz, t):
    n = bsz * t
    gw = SSD_HEADS_PER_GROUP * SSD_HEADDIM
    st = HEAD_DIM
    z_off, x_off = 2048 // gw, 3072 // gw
    b_off, c_off = 4096 // st, 4352 // st
    dt_off = (p.shape[1] - SSD_GROUPS * LANE) // LANE
    xw = SSD_GROUPS * gw
    par = lambda shape, f: pl.BlockSpec(shape, f)
    in_specs = [
        par((t, gw), lambda b, g: (b, z_off + g)),
        par((t, gw), lambda b, g: (b, x_off + g)),
        par((t, st), lambda b, g: (b, b_off + g)),
        par((t, st), lambda b, g: (b, c_off + g)),
        par((t, LANE), lambda b, g: (b, dt_off + g)),
        par((SSD_CONV, gw), lambda b, g: (0, g)),
        par((SSD_CONV, st), lambda b, g: (0, xw // st + g)),
        par((SSD_CONV, st), lambda b, g: (0, xw // st + SSD_GROUPS + g)),
        par((1, gw), lambda b, g: (0, g)),
        par((1, st), lambda b, g: (0, xw // st + g)),
        par((1, st), lambda b, g: (0, xw // st + SSD_GROUPS + g)),
        par((1, LANE), lambda b, g: (0, g)),
        par((1, LANE), lambda b, g: (0, g)),
        par((1, gw), lambda b, g: (0, g)),
        par((1, gw), lambda b, g: (0, g)),
    ]
    return pl.pallas_call(
        _ssd_kernel,
        grid=(bsz, SSD_GROUPS),
        in_specs=in_specs,
        out_specs=pl.BlockSpec((t, gw), lambda b, g: (b, g)),
        out_shape=jax.ShapeDtypeStruct((n, SSD_GROUPS * gw), BF16),
        scratch_shapes=[pltpu.VMEM((st, gw), F32)],
        compiler_params=_cparams("arbitrary", "arbitrary"),
        name="ssd",
    )(p, p, p, p, p, conv_w, conv_w, conv_w, conv_b, conv_b, conv_b, dtb, alog, dsk, nw)


def _hgrn_kernel(q_ref, f_ref, i_ref, g_ref, lb_ref, nw_ref, o_ref, st_scr, cum_scr, *, layer):
    t = q_ref.shape[0]
    lbm = lb_ref[...]
    depth = lbm.shape[0]
    mx = lbm[0:1, :]
    for i in range(1, depth):
        mx = jnp.maximum(mx, lbm[i:i + 1, :])
    ex = [jnp.exp(lbm[i:i + 1, :] - mx) for i in range(depth)]
    den = ex[0]
    for i in range(1, depth):
        den = den + ex[i]
    sm = [e / den for e in ex]
    csum = sm[0]
    for i in range(1, layer + 1):
        csum = csum + sm[i]
    lb = jnp.maximum(csum - sm[0], 0.0)
    log_lb = jnp.log(jnp.maximum(lb, LB_FLOOR))
    l1m = jnp.log1p(-lb)
    oml = 1.0 - lb

    ii, jj = _tri_incl()
    tri = jnp.where(ii >= jj, 1.0, 0.0).astype(F32)
    lvl = jnp.where(ii > jj, 31 - lax.clz(ii ^ jj), -1)
    eye = ii == jj
    width = q_ref.shape[1]
    hcols = [slice(c0, c0 + HEAD_DIM) for c0 in range(0, width, HEAD_DIM)]
    row = lax.broadcasted_iota(jnp.int32, (CHUNK, width), 0)
    scale = HEAD_DIM ** -0.5
    nlev = int(math.log2(CHUNK))
    st_scr[...] = jnp.zeros_like(st_scr)

    def step(c, carry):
        r = pl.ds(pl.multiple_of(c * CHUNK, CHUNK), CHUNK)
        q = _silu(q_ref[r, :]) * scale
        f = f_ref[r, :]
        ls = jnp.minimum(f, 0.0) - jnp.log1p(jnp.exp(-jnp.abs(f)))
        a, b = log_lb, l1m + ls
        lf = jnp.maximum(a, b) + jnp.log1p(jnp.exp(-jnp.abs(a - b)))
        kk = oml / (1.0 + jnp.exp(f))
        vb = i_ref[r, :].astype(BF16)
        cum = jnp.dot(tri, lf, precision=HIGHEST, preferred_element_type=F32)
        cum_scr[...] = cum
        qb, kb = q.astype(BF16), kk.astype(BF16)
        scores = [jnp.where(eye, _dot_nt(qb[:, cs], kb[:, cs]), 0.0) for cs in hcols]
        for lv in range(nlev):
            s = 1 << lv
            if 2 * s >= 8:
                ref = jnp.concatenate(
                    [jnp.broadcast_to(cum_scr[g0 * 2 * s + s - 1:g0 * 2 * s + s, :], (2 * s, width))
                     for g0 in range(CHUNK // (2 * s))], axis=0)
            elif s == 2:
                m4 = row & 3
                ref = jnp.where(m4 == 0, pltpu.roll(cum, CHUNK - 1, 0),
                                jnp.where(m4 == 1, cum,
                                          jnp.where(m4 == 2, pltpu.roll(cum, 1, 0), pltpu.roll(cum, 2, 0))))
            else:
                ref = jnp.where((row & 1) == 1, pltpu.roll(cum, 1, 0), cum)
            qs = (q * jnp.exp(jnp.minimum(cum - ref, 0.0))).astype(BF16)
            ks = (kk * jnp.exp(jnp.minimum(ref - cum, 0.0))).astype(BF16)
            scores = [jnp.where(lvl == lv, _dot_nt(qs[:, cs], ks[:, cs]), sc) for cs, sc in zip(hcols, scores)]
        last = cum[CHUNK - 1:CHUNK, :]
        qe = (q * jnp.exp(cum)).astype(BF16)
        kw = (kk * jnp.exp(last - cum)).astype(BF16)
        elast = jnp.exp(last)
        outs = []
        for hd, cs in enumerate(hcols):
            st = st_scr[hd]
            out = _dot(scores[hd].astype(BF16), vb[:, cs]) + _dot_nt(qe[:, cs], st.astype(BF16))
            st_scr[hd] = st * elast[:, cs] + _dot_tn(vb[:, cs], kw[:, cs])
            outs.append(_rms(out))
        o = jnp.concatenate(outs, axis=1) * nw_ref[...]
        o_ref[r, :] = (_silu(g_ref[r, :]) * o).astype(o_ref.dtype)
        return carry

    lax.fori_loop(0, t // CHUNK, step, 0)


def _hgrn(p, lbounds, nw, layer, bsz, t):
    n = bsz * t
    width = HGRN_HEADS * HEAD_DIM
    base = DT_LO // width
    blk = lambda off: pl.BlockSpec((t, width), lambda b, off=off: (b, base + off))
    return pl.pallas_call(
        functools.partial(_hgrn_kernel, layer=layer),
        grid=(bsz,),
        in_specs=[blk(0), blk(1), blk(2), blk(3),
                  pl.BlockSpec((lbounds.shape[0], width), lambda b: (0, 0)),
                  pl.BlockSpec((1, width), lambda b: (0, 0))],
        out_specs=pl.BlockSpec((t, width), lambda b: (b, 0)),
        out_shape=jax.ShapeDtypeStruct((n, width), BF16),
        scratch_shapes=[pltpu.VMEM((HGRN_HEADS, HEAD_DIM, HEAD_DIM), F32), pltpu.VMEM((CHUNK, width), F32)],
        compiler_params=_cparams("arbitrary"),
        name="hgrn2",
    )(p, p, p, p, lbounds, nw)


def _outproj_kernel(a_ref, b_ref, c_ref, h_ref, w_ref, nw_ref, wr_ref, br_ref, hout_ref, u_ref, lg_ref):
    wa, wb = a_ref.shape[1], b_ref.shape[1]
    acc = _dot(a_ref[...], w_ref[0:wa, :])
    acc = acc + _dot(b_ref[...], w_ref[wa:wa + wb, :])
    acc = acc + _dot(c_ref[...], w_ref[wa + wb:, :])
    h = h_ref[...] + acc
    hout_ref[...] = h
    u = _rms(h, nw_ref[...])
    u_hi = u.astype(BF16)
    u_lo = (u - u_hi.astype(F32)).astype(BF16)
    t2 = _dot(u_hi, wr_ref[...])
    lg_ref[...] = t2[:, :LANE] + t2[:, LANE:] + _dot(u_lo, wr_ref[:, :LANE]) + br_ref[...]
    half = u.shape[1] // 2
    bits = lax.bitcast_convert_type(u_hi.astype(F32), jnp.uint32)
    packed = (bits[:, :half] >> 16) | (bits[:, half:] & jnp.uint32(0xFFFF0000))
    _rows_to_slabs(u_ref, 0, packed, half // LANE)


def _outproj(o_a, o_b, o_c, h, w_out, nw, w_r, b_r):
    n, d = h.shape
    tm = min(512, n)
    srows = _slab_rows(d // 2)
    row = lambda i: (i, 0)
    fix = lambda i: (0, 0)
    return pl.pallas_call(
        _outproj_kernel,
        grid=(n // tm,),
        in_specs=[pl.BlockSpec((tm, o_a.shape[1]), row), pl.BlockSpec((tm, o_b.shape[1]), row),
                  pl.BlockSpec((tm, o_c.shape[1]), row), pl.BlockSpec((tm, d), row),
                  pl.BlockSpec(w_out.shape, fix), pl.BlockSpec((1, d), fix),
                  pl.BlockSpec(w_r.shape, fix), pl.BlockSpec((1, LANE), fix)],
        out_specs=[pl.BlockSpec((tm, d), row), pl.BlockSpec((tm * srows, LANE), row),
                   pl.BlockSpec((tm, LANE), row)],
        out_shape=[jax.ShapeDtypeStruct((n, d), F32), jax.ShapeDtypeStruct((n * srows, LANE), jnp.uint32),
                   jax.ShapeDtypeStruct((n, LANE), F32)],
        compiler_params=_cparams("arbitrary"),
        name="outproj",
    )(o_a, o_b, o_c, h, w_out, nw, w_r, b_r)


def _route_kernel(lg_ref, rt_ref, cnt_ref, carry_scr):
    tr = lg_ref.shape[0]

    @pl.when(pl.program_id(0) == 0)
    def _():
        carry_scr[...] = jnp.zeros_like(carry_scr)

    lg = lg_ref[...]
    lane = lax.broadcasted_iota(jnp.int32, (tr, LANE), 1)
    neg = -jnp.inf
    big = jnp.int32(LANE)

    def first_max(vals):
        m = jnp.max(vals, axis=-1, keepdims=True)
        idx = jnp.min(jnp.where(vals == m, lane, big), axis=-1, keepdims=True)
        return m, idx

    gl = jnp.where(lane < N_GROUPS, lg, neg)
    gmax, gidx = first_max(gl)
    gate = 1.0 / jnp.sum(jnp.exp(gl - gmax), axis=-1, keepdims=True)
    lo = N_GROUPS + EXPERTS_PER_GROUP * gidx
    el = jnp.where((lane >= lo) & (lane < lo + EXPERTS_PER_GROUP), lg, neg)
    m1, i1 = first_max(el)
    m2, i2 = first_max(jnp.where(lane == i1, neg, el))
    e21 = jnp.exp(m2 - m1)
    w1 = gate * (1.0 / (1.0 + e21))
    w2 = gate * (e21 / (1.0 + e21))

    oh1 = lane == i1
    oh2 = lane == i2
    m = jnp.where(oh1 | oh2, 1.0, 0.0)
    ti = lax.broadcasted_iota(jnp.int32, (tr, tr), 0)
    tj = lax.broadcasted_iota(jnp.int32, (tr, tr), 1)
    before = _dot(jnp.where(ti > tj, 1.0, 0.0).astype(BF16), m.astype(BF16)) + carry_scr[0:1, :]
    r1 = jnp.sum(jnp.where(oh1, before, 0.0), axis=-1, keepdims=True)
    r2 = jnp.sum(jnp.where(oh2, before, 0.0), axis=-1, keepdims=True)
    total = carry_scr[0:1, :] + jnp.sum(m, axis=0, keepdims=True)
    carry_scr[...] = jnp.broadcast_to(total, carry_scr.shape)
    cnt_ref[...] = jnp.broadcast_to(total, cnt_ref.shape)

    e1 = (i1 - N_GROUPS).astype(F32)
    e2 = (i2 - N_GROUPS).astype(F32)
    out = jnp.zeros((tr, LANE), F32)
    for pos, val in enumerate((e1, e2, w1, w2, r1, r2)):
        out = jnp.where(lane == pos, val, out)
    rt_ref[...] = out


def _route(logits):
    n = logits.shape[0]
    tr = min(ROUTE_TILE, n)
    return pl.pallas_call(
        _route_kernel,
        grid=(n // tr,),
        in_specs=[pl.BlockSpec((tr, LANE), lambda i: (i, 0))],
        out_specs=[pl.BlockSpec((tr, LANE), lambda i: (i, 0)), pl.BlockSpec((8, LANE), lambda i: (0, 0))],
        out_shape=[jax.ShapeDtypeStruct((n, LANE), F32), jax.ShapeDtypeStruct((8, LANE), F32)],
        scratch_shapes=[pltpu.VMEM((8, LANE), F32)],
        compiler_params=_cparams("arbitrary"),
        name="route",
    )(logits)


def _ffn_kernel(blk_e_ref, nv_ref, code_ref, u_hbm, wg_ref, wu_ref, wd_ref, o_hbm,
                xbuf, ybuf, wgb, wub, wdb, gsem, ssem, *, tb, xrows, yrows):
    b = pl.program_id(0)
    nb = pl.num_programs(0)
    n_tok = u_hbm.shape[0] // xrows
    slot = b % 2

    def slab(ref, srows, idx, count=1):
        return ref.at[pl.ds(pl.multiple_of(idx * srows, srows), count * srows)]

    def row_loop(count, fn):
        full = count // DMA_UNROLL

        def grp(i, c):
            for k in range(DMA_UNROLL):
                fn(i * DMA_UNROLL + k)
            return c
        lax.fori_loop(0, full, grp, 0)

        def one(r, c):
            fn(r)
            return c
        lax.fori_loop(full * DMA_UNROLL, count, one, 0)

    def issue_gather(blk, sl, count):
        def fn(r):
            tok = code_ref[blk * tb + r] >> 1
            pltpu.make_async_copy(slab(u_hbm, xrows, tok), slab(xbuf, xrows, sl * tb + r), gsem.at[sl]).start()
        row_loop(count, fn)

    def wait_gather(sl, count):
        pltpu.make_async_copy(slab(u_hbm, xrows, 0, count), slab(xbuf, xrows, sl * tb, count), gsem.at[sl]).wait()

    def wait_scatter(sl, count):
        pltpu.make_async_copy(slab(ybuf, yrows, sl * tb, count), slab(o_hbm, yrows, 0, count), ssem.at[sl]).wait()

    @pl.when(b == 0)
    def _():
        xbuf[...] = jnp.zeros_like(xbuf)

        @pl.when(nv_ref[0] > 0)
        def _():
            issue_gather(0, 0, nv_ref[0])

    nxt = jnp.minimum(b + 1, nb - 1)
    nv_next = jnp.where(b + 1 < nb, nv_ref[nxt], 0)

    @pl.when(nv_next > 0)
    def _():
        issue_gather(b + 1, 1 - slot, nv_next)

    prev = jnp.maximum(b - 1, 0)

    @pl.when((b == 0) | (blk_e_ref[b] != blk_e_ref[prev]))
    def _():
        wgb[...] = wg_ref[...].astype(BF16)
        wub[...] = wu_ref[...].astype(BF16)
        wdb[...] = wd_ref[...].astype(BF16)

    nv = nv_ref[b]

    @pl.when(nv > 0)
    def _():
        wait_gather(slot, nv)
        words = _slabs_to_rows(xbuf, slot * tb * xrows, tb, xrows)
        x = jnp.concatenate(
            [lax.bitcast_convert_type(words << 16, F32).astype(BF16),
             lax.bitcast_convert_type(words & jnp.uint32(0xFFFF0000), F32).astype(BF16)], axis=1)
        hdn = _silu(_dot(x, wgb[...])) * _dot(x, wub[...])
        y = _dot(hdn.astype(BF16), wdb[...])
        _rows_to_slabs(ybuf, slot * tb * yrows, y, yrows)

        def send(r):
            code = code_ref[b * tb + r]
            dst = (code & 1) * n_tok + (code >> 1)
            pltpu.make_async_copy(slab(ybuf, yrows, slot * tb + r), slab(o_hbm, yrows, dst), ssem.at[slot]).start()
        row_loop(nv, send)

    nv_prev = jnp.where(b > 0, nv_ref[prev], 0)

    @pl.when(nv_prev > 0)
    def _():
        wait_scatter(1 - slot, nv_prev)

    @pl.when((b == nb - 1) & (nv > 0))
    def _():
        wait_scatter(slot, nv)


def _ffn(blk_e, nv, code, u2d, wg, wu, wd, layer):
    d, de = wg.shape[2], wg.shape[3]
    xrows, yrows = _slab_rows(d // 2), _slab_rows(d)
    n = u2d.shape[0] // xrows
    nblk = blk_e.shape[0]
    tb = MOE_BLOCK
    grid_spec = pltpu.PrefetchScalarGridSpec(
        num_scalar_prefetch=3,
        grid=(nblk,),
        in_specs=[pl.BlockSpec(memory_space=pl.ANY),
                  pl.BlockSpec((None, None, d, de), lambda b, be, nv, cd: (layer, be[b], 0, 0)),
                  pl.BlockSpec((None, None, d, de), lambda b, be, nv, cd: (layer, be[b], 0, 0)),
                  pl.BlockSpec((None, None, de, d), lambda b, be, nv, cd: (layer, be[b], 0, 0))],
        out_specs=pl.BlockSpec(memory_space=pl.ANY),
        scratch_shapes=[pltpu.VMEM((2 * tb * xrows, LANE), jnp.uint32), pltpu.VMEM((2 * tb * yrows, LANE), F32),
                        pltpu.VMEM((d, de), BF16), pltpu.VMEM((d, de), BF16), pltpu.VMEM((de, d), BF16),
                        pltpu.SemaphoreType.DMA((2,)), pltpu.SemaphoreType.DMA((2,))],
    )
    return pl.pallas_call(
        functools.partial(_ffn_kernel, tb=tb, xrows=xrows, yrows=yrows),
        grid_spec=grid_spec,
        out_shape=jax.ShapeDtypeStruct((2 * n * yrows, LANE), F32),
        compiler_params=_cparams("arbitrary"),
        name="moe_ffn",
    )(blk_e, nv, code, u2d, wg, wu, wd)


def _dispatch_tables(route, counts, n):
    tb = MOE_BLOCK
    e = route[:, 0:2].astype(jnp.int32)
    rank = route[:, 4:6].astype(jnp.int32)
    cnt = counts[0, N_GROUPS:N_GROUPS + N_EXPERTS].astype(jnp.int32)
    padded = ((cnt + tb - 1) // tb) * tb
    pends = jnp.cumsum(padded)
    pstarts = pends - padded
    dest = (pstarts[e] + rank).reshape(-1)
    p_total = 2 * n + N_EXPERTS * tb
    nblk = p_total // tb
    code = jnp.full((p_total,), -1, jnp.int32).at[dest].set(
        jnp.arange(2 * n, dtype=jnp.int32), unique_indices=True, mode="promise_in_bounds")
    bstart = jnp.arange(nblk, dtype=jnp.int32) * tb
    blk_e = jnp.minimum(jnp.sum((bstart[:, None] >= pends[None, :]).astype(jnp.int32), axis=1), N_EXPERTS - 1)
    nv = jnp.clip(cnt[blk_e] - (bstart - pstarts[blk_e]), 0, tb)
    nv = jnp.where(bstart < pends[-1], nv, 0).astype(jnp.int32)
    return blk_e, nv, code


def _pad_lanes(v, width):
    return jnp.pad(v, ((0, 0), (0, width - v.shape[1])))


def kernel(x, attn_norm_w, w_in, ssd_conv_w, ssd_conv_b, ssd_dt_bias, ssd_a_log, ssd_d, ssd_norm_w, hgrn_lower_bounds, hgrn_norm_w, w_out, ffn_norm_w, router_group_w, router_group_b, router_expert_w, router_expert_b, expert_w_gate, expert_w_up, expert_w_down, final_norm_w):
    bsz, t, d = x.shape
    n = bsz * t
    depth = w_in.shape[0]

    half = HEAD_DIM // 2
    inv = 1.0 / (ROPE_THETA ** (jnp.arange(half, dtype=F32) / half))
    ang = jnp.arange(t, dtype=F32)[:, None] * inv[None, :]
    cos_t = jnp.concatenate([jnp.cos(ang), jnp.cos(ang)], axis=1)
    sin_t = jnp.concatenate([-jnp.sin(ang), jnp.sin(ang)], axis=1)

    def group_lanes(v):
        v = v.reshape(SSD_GROUPS, SSD_HEADS_PER_GROUP)
        return _pad_lanes(v, LANE).reshape(1, SSD_GROUPS * LANE)

    h = x.reshape(n, d)
    w_in_b = _wprep(w_in)
    moe = None
    for l in range(depth):
        if moe is None:
            (u,) = _norm_pass(h, attn_norm_w[l][None, :], None, BF16, False)
        else:
            u, h = _norm_pass(h, attn_norm_w[l][None, :], moe, BF16, True)
        p = _inproj(u, w_in_b, l)

        o_ret = _retention(p, cos_t, sin_t, bsz, t)
        o_ssd = _ssd(p, ssd_conv_w[l], ssd_conv_b[l][None, :], group_lanes(ssd_dt_bias[l]),
                     group_lanes(ssd_a_log[l]), jnp.repeat(ssd_d[l], SSD_HEADDIM)[None, :],
                     ssd_norm_w[l][None, :], bsz, t)
        o_hgrn = _hgrn(p, hgrn_lower_bounds, hgrn_norm_w[l][None, :], l, bsz, t)

        w_r = _pad_lanes(jnp.concatenate([router_group_w[l], router_expert_w[l]], axis=1), LANE)
        w_r_hi = w_r.astype(BF16)
        w_r = jnp.concatenate([w_r_hi, (w_r - w_r_hi.astype(F32)).astype(BF16)], axis=1)
        b_r = _pad_lanes(jnp.concatenate([router_group_b[l], router_expert_b[l]])[None, :], LANE)
        h, u2d, logits = _outproj(o_ret, o_ssd, o_hgrn, h, w_out[l].astype(BF16), ffn_norm_w[l][None, :], w_r, b_r)
        route, counts = _route(logits)
        blk_e, nv, code = _dispatch_tables(route, counts, n)
        o2d = _ffn(blk_e, nv, code, u2d, expert_w_gate, expert_w_up, expert_w_down, l)
        moe = (o2d, route)

    (out,) = _norm_pass(h, final_norm_w[None, :], moe, F32, False)
    return out.reshape(bsz, t, d)
```

```python
import functools
import math

import jax
import jax.numpy as jnp
from jax import lax
from jax.experimental import pallas as pl
from jax.experimental.pallas import tpu as pltpu

F32 = jnp.float32
BF16 = jnp.bfloat16
HIGHEST = lax.Precision.HIGHEST

V7X_VMEM_BYTES = 64 * 1024 * 1024
VMEM_LIMIT = V7X_VMEM_BYTES - 8 * 1024 * 1024
LANE = 128

EPS = 1e-6
LB_FLOOR = 1e-30
ROPE_THETA = 10000.0

RET_HEADS = 4
HEAD_DIM = 128
SSD_HEADDIM = 64
SSD_GROUPS = 2
SSD_HEADS_PER_GROUP = 8
SSD_CONV = 4
HGRN_HEADS = 4
CHUNK = 128
N_GROUPS = 4
EXPERTS_PER_GROUP = 8
N_EXPERTS = N_GROUPS * EXPERTS_PER_GROUP
MOE_BLOCK = 128
DMA_UNROLL = 8
ROUTE_TILE = 512
DT_LO = 4608
DT_HI = DT_LO + SSD_GROUPS * SSD_HEADS_PER_GROUP


def _cparams(*sem):
    return pltpu.CompilerParams(dimension_semantics=sem, vmem_limit_bytes=VMEM_LIMIT)


def _rms(x, w=None):
    y = x * lax.rsqrt(jnp.mean(x * x, axis=-1, keepdims=True) + EPS)
    return y if w is None else y * w


def _sigmoid(x):
    return 1.0 / (1.0 + jnp.exp(-x))


def _silu(x):
    return x * _sigmoid(x)


def _dot(a, b):
    return jnp.dot(a, b, preferred_element_type=F32)


def _dot_nt(a, b):
    return lax.dot_general(a, b, (((1,), (1,)), ((), ())), preferred_element_type=F32)


def _dot_tn(a, b):
    return lax.dot_general(a, b, (((0,), (0,)), ((), ())), preferred_element_type=F32)


def _tri_incl():
    ii = lax.broadcasted_iota(jnp.int32, (CHUNK, CHUNK), 0)
    jj = lax.broadcasted_iota(jnp.int32, (CHUNK, CHUNK), 1)
    return ii, jj


def _slab_rows(d):
    return d // LANE


def _slabs_to_rows(ref2d, base, rows, srows):
    return jnp.concatenate([ref2d[pl.ds(base + c, rows, stride=srows), :] for c in range(srows)], axis=1)


def _rows_to_slabs(ref2d, base, val, srows):
    rows = val.shape[0]
    for c in range(srows):
        ref2d[pl.ds(base + c, rows, stride=srows), :] = val[:, c * LANE:(c + 1) * LANE]


def _wprep_kernel(w_ref, wm_ref):
    nm = w_ref.shape[1] - (DT_HI - DT_LO)
    wm_ref[:, 0:DT_LO] = w_ref[:, 0:DT_LO].astype(BF16)
    wm_ref[:, DT_LO:nm] = w_ref[:, DT_HI:].astype(BF16)
    rows = w_ref.shape[0]
    pad = jnp.zeros((rows, LANE - SSD_HEADS_PER_GROUP), BF16)
    parts = []
    for g in range(SSD_GROUPS):
        lo = DT_LO + g * SSD_HEADS_PER_GROUP
        parts += [w_ref[:, lo:lo + SSD_HEADS_PER_GROUP].astype(BF16), pad]
    wm_ref[:, nm:] = jnp.concatenate(parts, axis=1)


def _wprep(w):
    depth, d, nin = w.shape
    tr = 256
    nout = nin - (DT_HI - DT_LO) + SSD_GROUPS * LANE
    return pl.pallas_call(
        _wprep_kernel,
        grid=(depth, d // tr),
        in_specs=[pl.BlockSpec((None, tr, nin), lambda l, i: (l, i, 0))],
        out_specs=pl.BlockSpec((None, tr, nout), lambda l, i: (l, i, 0)),
        out_shape=jax.ShapeDtypeStruct((depth, d, nout), BF16),
        compiler_params=_cparams("arbitrary", "arbitrary"),
        name="wprep",
    )(w)


def _norm_kernel(*refs, combine, write_h):
    if combine:
        h_ref, o0_ref, o1_ref, rt_ref, nw_ref = refs[:5]
        outs = refs[5:]
        rows = h_ref.shape[0]
        srows = _slab_rows(h_ref.shape[1])
        rt = rt_ref[...]
        h = (h_ref[...] + rt[:, 2:3] * _slabs_to_rows(o0_ref, 0, rows, srows)
             + rt[:, 3:4] * _slabs_to_rows(o1_ref, 0, rows, srows))
    else:
        h_ref, nw_ref = refs[:2]
        outs = refs[2:]
        h = h_ref[...]
    u_ref = outs[0]
    u_ref[...] = _rms(h, nw_ref[...]).astype(u_ref.dtype)
    if write_h:
        outs[1][...] = h


def _norm_pass(h, nw, moe, out_dtype, write_h):
    n, d = h.shape
    combine = moe is not None
    tm = min(512, n)
    row = lambda i: (i, 0)
    in_specs = [pl.BlockSpec((tm, d), row)]
    args = [h]
    if combine:
        o2d, rt = moe
        srows = _slab_rows(d)
        nt = n // tm
        in_specs += [pl.BlockSpec((tm * srows, LANE), row),
                     pl.BlockSpec((tm * srows, LANE), lambda i: (nt + i, 0)),
                     pl.BlockSpec((tm, LANE), row)]
        args += [o2d, o2d, rt]
    in_specs.append(pl.BlockSpec((1, d), lambda i: (0, 0)))
    args.append(nw)
    out_shape = [jax.ShapeDtypeStruct((n, d), out_dtype)]
    out_specs = [pl.BlockSpec((tm, d), row)]
    if write_h:
        out_shape.append(jax.ShapeDtypeStruct((n, d), F32))
        out_specs.append(pl.BlockSpec((tm, d), row))
    return pl.pallas_call(
        functools.partial(_norm_kernel, combine=combine, write_h=write_h),
        grid=(n // tm,),
        in_specs=in_specs, out_specs=out_specs, out_shape=out_shape,
        compiler_params=_cparams("arbitrary"),
        name="norm_pass",
    )(*args)


def _inproj_kernel(u_ref, w_ref, p_ref):
    p_ref[...] = _dot(u_ref[...], w_ref[...])


def _inproj(u, w, layer):
    n, d = u.shape
    np_ = w.shape[2]
    tm = min(1024, n)
    tn = np_ // 3
    return pl.pallas_call(
        _inproj_kernel,
        grid=(np_ // tn, n // tm),
        in_specs=[pl.BlockSpec((tm, d), lambda j, i: (i, 0)),
                  pl.BlockSpec((None, d, tn), lambda j, i: (layer, 0, j))],
        out_specs=pl.BlockSpec((tm, tn), lambda j, i: (i, j)),
        out_shape=jax.ShapeDtypeStruct((n, np_), F32),
        compiler_params=_cparams("arbitrary", "arbitrary"),
        name="inproj",
    )(u, w)


def _ret_kernel(q_ref, k_ref, v_ref, g_ref, cos_ref, sin_ref, o_ref, s_scr):
    t = q_ref.shape[0]
    ii, jj = _tri_incl()
    causal = ii >= jj
    dist = (ii - jj).astype(F32)
    iif = ii.astype(F32)
    consts = []
    for hd in range(RET_HEADS):
        lg = math.log(1.0 - 2.0 ** (-5.0 - hd))
        dmat = jnp.where(causal, jnp.exp(jnp.where(causal, dist * lg, 0.0)), 0.0)
        ecum = jnp.exp((iif + 1.0) * lg)
        wk = jnp.exp((CHUNK - 1.0 - iif) * lg)
        consts.append((dmat, ecum, wk, math.exp(CHUNK * lg)))
    scale = HEAD_DIM ** -0.5
    s_scr[...] = jnp.zeros_like(s_scr)

    def step(c, carry):
        r = pl.ds(pl.multiple_of(c * CHUNK, CHUNK), CHUNK)
        cs, sn = cos_ref[r, :], sin_ref[r, :]
        for hd in range(RET_HEADS):
            dmat, ecum, wk, elast = consts[hd]
            cols = slice(hd * HEAD_DIM, (hd + 1) * HEAD_DIM)
            q, k = q_ref[r, cols], k_ref[r, cols]
            qr = q * cs + pltpu.roll(q, HEAD_DIM // 2, 1) * sn
            kr = (k * cs + pltpu.roll(k, HEAD_DIM // 2, 1) * sn) * scale
            vb = v_ref[r, cols].astype(BF16)
            s = s_scr[hd]
            scores = _dot_nt(qr.astype(BF16), kr.astype(BF16)) * dmat
            out = _dot(scores.astype(BF16), vb) + _dot((qr * ecum).astype(BF16), s.astype(BF16))
            s_scr[hd] = elast * s + _dot_tn((kr * wk).astype(BF16), vb)
            o_ref[r, cols] = (_silu(g_ref[r, cols]) * _rms(out)).astype(o_ref.dtype)
        return carry

    lax.fori_loop(0, t // CHUNK, step, 0)


def _retention(p, cos_t, sin_t, bsz, t):
    n = bsz * t
    width = RET_HEADS * HEAD_DIM
    blk = lambda off: pl.BlockSpec((t, width), lambda b, off=off: (b, off))
    tab = pl.BlockSpec((t, HEAD_DIM), lambda b: (0, 0))
    return pl.pallas_call(
        _ret_kernel,
        grid=(bsz,),
        in_specs=[blk(0), blk(1), blk(2), blk(3), tab, tab],
        out_specs=pl.BlockSpec((t, width), lambda b: (b, 0)),
        out_shape=jax.ShapeDtypeStruct((n, width), BF16),
        scratch_shapes=[pltpu.VMEM((RET_HEADS, HEAD_DIM, HEAD_DIM), F32)],
        compiler_params=_cparams("arbitrary"),
        name="retention",
    )(p, p, p, p, cos_t, sin_t)


def _ssd_kernel(z_ref, x_ref, b_ref, c_ref, dt_ref, cwx_ref, cwb_ref, cwc_ref, cbx_ref, cbb_ref, cbc_ref,
                dtb_ref, alog_ref, dsk_ref, nw_ref, o_ref, s_scr):
    t = z_ref.shape[0]
    ii, jj = _tri_incl()
    causal = ii >= jj
    tri = jnp.where(causal, 1.0, 0.0).astype(F32)
    lane_lo = lax.broadcasted_iota(jnp.int32, (CHUNK, LANE), 1) < SSD_HEADDIM
    neg_a = -jnp.exp(alog_ref[...])
    s_scr[...] = jnp.zeros_like(s_scr)

    def conv(ref, w_ref, bias_ref, c):
        r = pl.ds(pl.multiple_of(c * CHUNK, CHUNK), CHUNK)
        rp = pl.ds(pl.multiple_of(jnp.maximum(c * CHUNK - 8, 0), 8), 8)
        cur = ref[r, :]
        ext = jnp.concatenate([jnp.where(c > 0, ref[rp, :], 0.0), cur], axis=0)
        w = w_ref[...]
        acc = bias_ref[...] + w[SSD_CONV - 1:SSD_CONV, :] * cur
        for lag in range(1, SSD_CONV):
            shifted = pltpu.roll(ext, lag, 0)[8:8 + CHUNK, :]
            acc = acc + w[SSD_CONV - 1 - lag:SSD_CONV - lag, :] * shifted
        return _silu(acc)

    def step(c, carry):
        r = pl.ds(pl.multiple_of(c * CHUNK, CHUNK), CHUNK)
        xs = conv(x_ref, cwx_ref, cbx_ref, c)
        bm = conv(b_ref, cwb_ref, cbb_ref, c).astype(BF16)
        cm = conv(c_ref, cwc_ref, cbc_ref, c).astype(BF16)
        xr = dt_ref[r, :] + dtb_ref[...]
        dt = jnp.maximum(xr, 0.0) + jnp.log1p(jnp.exp(-jnp.abs(xr)))
        la = dt * neg_a
        cum = jnp.dot(tri, la, precision=HIGHEST, preferred_element_type=F32)
        last = cum[CHUNK - 1:CHUNK, :]
        wj = dt * jnp.exp(last - cum)
        ecum = jnp.exp(cum)
        cum_t = cum.T
        dt_t = dt.T
        gm = jnp.where(causal, _dot_nt(cm, bm), 0.0)
        s = s_scr[...]
        cs = _dot(cm, s.astype(BF16))
        ys, xws, els = [], [], []
        for pr in range(SSD_HEADS_PER_GROUP // 2):
            xp = xs[:, pr * LANE:(pr + 1) * LANE]
            acc = None
            ecp, wjp, elp = None, None, None
            for half in range(2):
                hd = 2 * pr + half
                ci = jnp.broadcast_to(cum[:, hd:hd + 1], (CHUNK, CHUNK))
                dec = jnp.exp(jnp.minimum(ci - cum_t[hd:hd + 1, :], 0.0))
                m = gm * dec * dt_t[hd:hd + 1, :]
                sel = lane_lo if half == 0 else jnp.logical_not(lane_lo)
                y = _dot(m.astype(BF16), jnp.where(sel, xp, 0.0).astype(BF16))
                acc = y if acc is None else acc + y
                eb = jnp.broadcast_to(ecum[:, hd:hd + 1], (CHUNK, LANE))
                wb = jnp.broadcast_to(wj[:, hd:hd + 1], (CHUNK, LANE))
                lb = jnp.broadcast_to(ecum[CHUNK - 1:CHUNK, hd:hd + 1], (1, LANE))
                ecp = eb if half == 0 else jnp.where(lane_lo, ecp, eb)
                wjp = wb if half == 0 else jnp.where(lane_lo, wjp, wb)
                elp = lb if half == 0 else jnp.where(lane_lo[0:1, :], elp, lb)
            ys.append(acc + ecp * cs[:, pr * LANE:(pr + 1) * LANE])
            xws.append((wjp * xp).astype(BF16))
            els.append(elp)
        y = jnp.concatenate(ys, axis=1)
        s_scr[...] = jnp.concatenate(els, axis=1) * s + _dot_tn(bm, jnp.concatenate(xws, axis=1))
        y = (y + dsk_ref[...] * xs) * _silu(z_ref[r, :])
        o_ref[r, :] = (_rms(y) * nw_ref[...]).astype(o_ref.dtype)
        return carry

    lax.fori_loop(0, t // CHUNK, step, 0)


def _ssd(p, conv_w, conv_b, dtb, alog, dsk, nw, bsz, t):
    n = bsz * t
    gw = SSD_HEADS_PER_GROUP * SSD_HEADDIM
    st = HEAD_DIM
    z_off, x_off = 2048 // gw, 3072 // gw
    b_off, c_off = 4096 // st, 4352 // st
    dt_off = (p.shape[1] - SSD_GROUPS * LANE) // LANE
    xw = SSD_GROUPS * gw
    par = lambda shape, f: pl.BlockSpec(shape, f)
    in_specs = [
        par((t, gw), lambda b, g: (b, z_off + g)),
        par((t, gw), lambda b, g: (b, x_off + g)),
        par((t, st), lambda b, g: (b, b_off + g)),
        par((t, st), lambda b, g: (b, c_off + g)),
        par((t, LANE), lambda b, g: (b, dt_off + g)),
        par((SSD_CONV, gw), lambda b, g: (0, g)),
        par((SSD_CONV, st), lambda b, g: (0, xw // st + g)),
        par((SSD_CONV, st), lambda b, g: (0, xw // st + SSD_GROUPS + g)),
        par((1, gw), lambda b, g: (0, g)),
        par((1, st), lambda b, g: (0, xw // st + g)),
        par((1, st), lambda b, g: (0, xw // st + SSD_GROUPS + g)),
        par((1, LANE), lambda b, g: (0, g)),
        par((1, LANE), lambda b, g: (0, g)),
        par((1, gw), lambda b, g: (0, g)),
        par((1, gw), lambda b, g: (0, g)),
    ]
    return pl.pallas_call(
        _ssd_kernel,
        grid=(bsz, SSD_GROUPS),
        in_specs=in_specs,
        out_specs=pl.BlockSpec((t, gw), lambda b, g: (b, g)),
        out_shape=jax.ShapeDtypeStruct((n, SSD_GROUPS * gw), BF16),
        scratch_shapes=[pltpu.VMEM((st, gw), F32)],
        compiler_params=_cparams("arbitrary", "arbitrary"),
        name="ssd",
    )(p, p, p, p, p, conv_w, conv_w, conv_w, conv_b, conv_b, conv_b, dtb, alog, dsk, nw)


def _hgrn_kernel(q_ref, f_ref, i_ref, g_ref, lb_ref, nw_ref, o_ref, st_scr, cum_scr, *, layer):
    t = q_ref.shape[0]
    lbm = lb_ref[...]
    depth = lbm.shape[0]
    mx = lbm[0:1, :]
    for i in range(1, depth):
        mx = jnp.maximum(mx, lbm[i:i + 1, :])
    ex = [jnp.exp(lbm[i:i + 1, :] - mx) for i in range(depth)]
    den = ex[0]
    for i in range(1, depth):
        den = den + ex[i]
    sm = [e / den for e in ex]
    csum = sm[0]
    for i in range(1, layer + 1):
        csum = csum + sm[i]
    lb = jnp.maximum(csum - sm[0], 0.0)
    log_lb = jnp.log(jnp.maximum(lb, LB_FLOOR))
    l1m = jnp.log1p(-lb)
    oml = 1.0 - lb

    ii, jj = _tri_incl()
    tri = jnp.where(ii >= jj, 1.0, 0.0).astype(F32)
    lvl = jnp.where(ii > jj, 31 - lax.clz(ii ^ jj), -1)
    eye = ii == jj
    width = q_ref.shape[1]
    hcols = [slice(c0, c0 + HEAD_DIM) for c0 in range(0, width, HEAD_DIM)]
    row = lax.broadcasted_iota(jnp.int32, (CHUNK, width), 0)
    scale = HEAD_DIM ** -0.5
    nlev = int(math.log2(CHUNK))
    st_scr[...] = jnp.zeros_like(st_scr)

    def step(c, carry):
        r = pl.ds(pl.multiple_of(c * CHUNK, CHUNK), CHUNK)
        q = _silu(q_ref[r, :]) * scale
        f = f_ref[r, :]
        ls = jnp.minimum(f, 0.0) - jnp.log1p(jnp.exp(-jnp.abs(f)))
        a, b = log_lb, l1m + ls
        lf = jnp.maximum(a, b) + jnp.log1p(jnp.exp(-jnp.abs(a - b)))
        kk = oml / (1.0 + jnp.exp(f))
        vb = i_ref[r, :].astype(BF16)
        cum = jnp.dot(tri, lf, precision=HIGHEST, preferred_element_type=F32)
        cum_scr[...] = cum
        qb, kb = q.astype(BF16), kk.astype(BF16)
        scores = [jnp.where(eye, _dot_nt(qb[:, cs], kb[:, cs]), 0.0) for cs in hcols]
        for lv in range(nlev):
            s = 1 << lv
            if 2 * s >= 8:
                ref = jnp.concatenate(
                    [jnp.broadcast_to(cum_scr[g0 * 2 * s + s - 1:g0 * 2 * s + s, :], (2 * s, width))
                     for g0 in range(CHUNK // (2 * s))], axis=0)
            elif s == 2:
                m4 = row & 3
                ref = jnp.where(m4 == 0, pltpu.roll(cum, CHUNK - 1, 0),
                                jnp.where(m4 == 1, cum,
                                          jnp.where(m4 == 2, pltpu.roll(cum, 1, 0), pltpu.roll(cum, 2, 0))))
            else:
                ref = jnp.where((row & 1) == 1, pltpu.roll(cum, 1, 0), cum)
            qs = (q * jnp.exp(jnp.minimum(cum - ref, 0.0))).astype(BF16)
            ks = (kk * jnp.exp(jnp.minimum(ref - cum, 0.0))).astype(BF16)
            scores = [jnp.where(lvl == lv, _dot_nt(qs[:, cs], ks[:, cs]), sc) for cs, sc in zip(hcols, scores)]
        last = cum[CHUNK - 1:CHUNK, :]
        qe = (q * jnp.exp(cum)).astype(BF16)
        kw = (kk * jnp.exp(last - cum)).astype(BF16)
        elast = jnp.exp(last)
        outs = []
        for hd, cs in enumerate(hcols):
            st = st_scr[hd]
            out = _dot(scores[hd].astype(BF16), vb[:, cs]) + _dot_nt(qe[:, cs], st.astype(BF16))
            st_scr[hd] = st * elast[:, cs] + _dot_tn(vb[:, cs], kw[:, cs])
            outs.append(_rms(out))
        o = jnp.concatenate(outs, axis=1) * nw_ref[...]
        o_ref[r, :] = (_silu(g_ref[r, :]) * o).astype(o_ref.dtype)
        return carry

    lax.fori_loop(0, t // CHUNK, step, 0)


def _hgrn(p, lbounds, nw, layer, bsz, t):
    n = bsz * t
    width = HGRN_HEADS * HEAD_DIM
    base = DT_LO // width
    blk = lambda off: pl.BlockSpec((t, width), lambda b, off=off: (b, base + off))
    return pl.pallas_call(
        functools.partial(_hgrn_kernel, layer=layer),
        grid=(bsz,),
        in_specs=[blk(0), blk(1), blk(2), blk(3),
                  pl.BlockSpec((lbounds.shape[0], width), lambda b: (0, 0)),
                  pl.BlockSpec((1, width), lambda b: (0, 0))],
        out_specs=pl.BlockSpec((t, width), lambda b: (b, 0)),
        out_shape=jax.ShapeDtypeStruct((n, width), BF16),
        scratch_shapes=[pltpu.VMEM((HGRN_HEADS, HEAD_DIM, HEAD_DIM), F32), pltpu.VMEM((CHUNK, width), F32)],
        compiler_params=_cparams("arbitrary"),
        name="hgrn2",
    )(p, p, p, p, lbounds, nw)


def _outproj_kernel(a_ref, b_ref, c_ref, h_ref, w_ref, nw_ref, wr_ref, br_ref, hout_ref, u_ref, lg_ref):
    wa, wb = a_ref.shape[1], b_ref.shape[1]
    acc = _dot(a_ref[...], w_ref[0:wa, :])
    acc = acc + _dot(b_ref[...], w_ref[wa:wa + wb, :])
    acc = acc + _dot(c_ref[...], w_ref[wa + wb:, :])
    h = h_ref[...] + acc
    hout_ref[...] = h
    u = _rms(h, nw_ref[...])
    u_hi = u.astype(BF16)
    u_lo = (u - u_hi.astype(F32)).astype(BF16)
    t2 = _dot(u_hi, wr_ref[...])
    lg_ref[...] = t2[:, :LANE] + t2[:, LANE:] + _dot(u_lo, wr_ref[:, :LANE]) + br_ref[...]
    half = u.shape[1] // 2
    bits = lax.bitcast_convert_type(u_hi.astype(F32), jnp.uint32)
    packed = (bits[:, :half] >> 16) | (bits[:, half:] & jnp.uint32(0xFFFF0000))
    _rows_to_slabs(u_ref, 0, packed, half // LANE)


def _outproj(o_a, o_b, o_c, h, w_out, nw, w_r, b_r):
    n, d = h.shape
    tm = min(512, n)
    srows = _slab_rows(d // 2)
    row = lambda i: (i, 0)
    fix = lambda i: (0, 0)
    return pl.pallas_call(
        _outproj_kernel,
        grid=(n // tm,),
        in_specs=[pl.BlockSpec((tm, o_a.shape[1]), row), pl.BlockSpec((tm, o_b.shape[1]), row),
                  pl.BlockSpec((tm, o_c.shape[1]), row), pl.BlockSpec((tm, d), row),
                  pl.BlockSpec(w_out.shape, fix), pl.BlockSpec((1, d), fix),
                  pl.BlockSpec(w_r.shape, fix), pl.BlockSpec((1, LANE), fix)],
        out_specs=[pl.BlockSpec((tm, d), row), pl.BlockSpec((tm * srows, LANE), row),
                   pl.BlockSpec((tm, LANE), row)],
        out_shape=[jax.ShapeDtypeStruct((n, d), F32), jax.ShapeDtypeStruct((n * srows, LANE), jnp.uint32),
                   jax.ShapeDtypeStruct((n, LANE), F32)],
        compiler_params=_cparams("arbitrary"),
        name="outproj",
    )(o_a, o_b, o_c, h, w_out, nw, w_r, b_r)


def _route_kernel(lg_ref, rt_ref, cnt_ref, carry_scr):
    tr = lg_ref.shape[0]

    @pl.when(pl.program_id(0) == 0)
    def _():
        carry_scr[...] = jnp.zeros_like(carry_scr)

    lg = lg_ref[...]
    lane = lax.broadcasted_iota(jnp.int32, (tr, LANE), 1)
    neg = -jnp.inf
    big = jnp.int32(LANE)

    def first_max(vals):
        m = jnp.max(vals, axis=-1, keepdims=True)
        idx = jnp.min(jnp.where(vals == m, lane, big), axis=-1, keepdims=True)
        return m, idx

    gl = jnp.where(lane < N_GROUPS, lg, neg)
    gmax, gidx = first_max(gl)
    gate = 1.0 / jnp.sum(jnp.exp(gl - gmax), axis=-1, keepdims=True)
    lo = N_GROUPS + EXPERTS_PER_GROUP * gidx
    el = jnp.where((lane >= lo) & (lane < lo + EXPERTS_PER_GROUP), lg, neg)
    m1, i1 = first_max(el)
    m2, i2 = first_max(jnp.where(lane == i1, neg, el))
    e21 = jnp.exp(m2 - m1)
    w1 = gate * (1.0 / (1.0 + e21))
    w2 = gate * (e21 / (1.0 + e21))

    oh1 = lane == i1
    oh2 = lane == i2
    m = jnp.where(oh1 | oh2, 1.0, 0.0)
    ti = lax.broadcasted_iota(jnp.int32, (tr, tr), 0)
    tj = lax.broadcasted_iota(jnp.int32, (tr, tr), 1)
    before = _dot(jnp.where(ti > tj, 1.0, 0.0).astype(BF16), m.astype(BF16)) + carry_scr[0:1, :]
    r1 = jnp.sum(jnp.where(oh1, before, 0.0), axis=-1, keepdims=True)
    r2 = jnp.sum(jnp.where(oh2, before, 0.0), axis=-1, keepdims=True)
    total = carry_scr[0:1, :] + jnp.sum(m, axis=0, keepdims=True)
    carry_scr[...] = jnp.broadcast_to(total, carry_scr.shape)
    cnt_ref[...] = jnp.broadcast_to(total, cnt_ref.shape)

    e1 = (i1 - N_GROUPS).astype(F32)
    e2 = (i2 - N_GROUPS).astype(F32)
    out = jnp.zeros((tr, LANE), F32)
    for pos, val in enumerate((e1, e2, w1, w2, r1, r2)):
        out = jnp.where(lane == pos, val, out)
    rt_ref[...] = out


def _route(logits):
    n = logits.shape[0]
    tr = min(ROUTE_TILE, n)
    return pl.pallas_call(
        _route_kernel,
        grid=(n // tr,),
        in_specs=[pl.BlockSpec((tr, LANE), lambda i: (i, 0))],
        out_specs=[pl.BlockSpec((tr, LANE), lambda i: (i, 0)), pl.BlockSpec((8, LANE), lambda i: (0, 0))],
        out_shape=[jax.ShapeDtypeStruct((n, LANE), F32), jax.ShapeDtypeStruct((8, LANE), F32)],
        scratch_shapes=[pltpu.VMEM((8, LANE), F32)],
        compiler_params=_cparams("arbitrary"),
        name="route",
    )(logits)


def _ffn_kernel(blk_e_ref, nv_ref, code_ref, u_hbm, wg_ref, wu_ref, wd_ref, o_hbm,
                xbuf, ybuf, wgb, wub, wdb, gsem, ssem, *, tb, xrows, yrows):
    b = pl.program_id(0)
    n_tok = u_hbm.shape[0] // xrows
    slot = b % 2
    prev = jnp.maximum(b - 1, 0)
    nv = nv_ref[b]
    nv_prev = jnp.where(b > 0, nv_ref[prev], 0)

    def slab(ref, srows, idx, count=1):
        return ref.at[pl.ds(pl.multiple_of(idx * srows, srows), count * srows)]

    def gather_row(blk, sl, r):
        tok = code_ref[blk * tb + r] & (n_tok - 1)
        pltpu.make_async_copy(slab(u_hbm, xrows, tok), slab(xbuf, xrows, sl * tb + r), gsem.at[sl]).start()

    def scatter_row(blk, sl, r, dst):
        pltpu.make_async_copy(slab(ybuf, yrows, sl * tb + r), slab(o_hbm, yrows, dst), ssem.at[sl]).start()

    def wait_gather(sl):
        pltpu.make_async_copy(slab(u_hbm, xrows, 0, tb), slab(xbuf, xrows, sl * tb, tb), gsem.at[sl]).wait()

    def wait_scatter(sl):
        pltpu.make_async_copy(slab(ybuf, yrows, sl * tb, tb), slab(o_hbm, yrows, 0, tb), ssem.at[sl]).wait()

    def row_loop(fn):
        def grp(i, c):
            for k in range(DMA_UNROLL):
                fn(i * DMA_UNROLL + k)
            return c
        lax.fori_loop(0, tb // DMA_UNROLL, grp, 0)

    @pl.when(b == 0)
    def _():
        ybuf[...] = jnp.zeros_like(ybuf)
        row_loop(lambda r: gather_row(0, 0, r))

    @pl.when((nv > 0) & ((b == 0) | (blk_e_ref[b] != blk_e_ref[prev])))
    def _():
        wgb[...] = wg_ref[...].astype(BF16)
        wub[...] = wu_ref[...].astype(BF16)
        wdb[...] = wd_ref[...].astype(BF16)

    @pl.when(nv > 0)
    def _():
        wait_gather(slot)
        words = _slabs_to_rows(xbuf, slot * tb * xrows, tb, xrows)
        x = jnp.concatenate(
            [lax.bitcast_convert_type(words << 16, F32).astype(BF16),
             lax.bitcast_convert_type(words & jnp.uint32(0xFFFF0000), F32).astype(BF16)], axis=1)
        has_prev = b > 0
        for r in range(tb):
            gather_row(b + 1, 1 - slot, r)
            dst = jnp.where(has_prev, code_ref[prev * tb + r], 2 * n_tok + r)
            scatter_row(prev, 1 - slot, r, dst)
        hdn = _silu(_dot(x, wgb[...])) * _dot(x, wub[...])
        y = _dot(hdn.astype(BF16), wdb[...])
        _rows_to_slabs(ybuf, slot * tb * yrows, y, yrows)
        wait_scatter(1 - slot)

    @pl.when((nv == 0) & (nv_prev > 0))
    def _():
        wait_gather(slot)
        row_loop(lambda r: scatter_row(prev, 1 - slot, r, code_ref[prev * tb + r]))
        wait_scatter(1 - slot)


def _ffn(blk_e, nv, code, u2d, wg, wu, wd, layer):
    d, de = wg.shape[2], wg.shape[3]
    xrows, yrows = _slab_rows(d // 2), _slab_rows(d)
    n = u2d.shape[0] // xrows
    assert n & (n - 1) == 0, "token index is taken from the slot code by masking"
    nblk = blk_e.shape[0]
    tb = MOE_BLOCK
    grid_spec = pltpu.PrefetchScalarGridSpec(
        num_scalar_prefetch=3,
        grid=(nblk,),
        in_specs=[pl.BlockSpec(memory_space=pl.ANY),
                  pl.BlockSpec((None, None, d, de), lambda b, be, nv, cd: (layer, be[b], 0, 0)),
                  pl.BlockSpec((None, None, d, de), lambda b, be, nv, cd: (layer, be[b], 0, 0)),
                  pl.BlockSpec((None, None, de, d), lambda b, be, nv, cd: (layer, be[b], 0, 0))],
        out_specs=pl.BlockSpec(memory_space=pl.ANY),
        scratch_shapes=[pltpu.VMEM((2 * tb * xrows, LANE), jnp.uint32), pltpu.VMEM((2 * tb * yrows, LANE), F32),
                        pltpu.VMEM((d, de), BF16), pltpu.VMEM((d, de), BF16), pltpu.VMEM((de, d), BF16),
                        pltpu.SemaphoreType.DMA((2,)), pltpu.SemaphoreType.DMA((2,))],
    )
    return pl.pallas_call(
        functools.partial(_ffn_kernel, tb=tb, xrows=xrows, yrows=yrows),
        grid_spec=grid_spec,
        out_shape=jax.ShapeDtypeStruct(((2 * n + tb) * yrows, LANE), F32),
        compiler_params=_cparams("arbitrary"),
        name="moe_ffn",
    )(blk_e, nv, code, u2d, wg, wu, wd)


def _dispatch_tables(route, counts, n):
    tb = MOE_BLOCK
    e = route[:, 0:2].astype(jnp.int32)
    rank = route[:, 4:6].astype(jnp.int32)
    cnt = counts[0, N_GROUPS:N_GROUPS + N_EXPERTS].astype(jnp.int32)
    padded = ((cnt + tb - 1) // tb) * tb
    pends = jnp.cumsum(padded)
    pstarts = pends - padded
    dest = (pstarts[e] + rank).reshape(-1)
    p_total = 2 * n + N_EXPERTS * tb
    nblk = p_total // tb
    slot_id = jnp.arange(2 * n, dtype=jnp.int32)
    code = (2 * n + jnp.arange(p_total, dtype=jnp.int32) % tb).at[dest].set(
        (slot_id & 1) * n + (slot_id >> 1), unique_indices=True, mode="promise_in_bounds")
    bstart = jnp.arange(nblk, dtype=jnp.int32) * tb
    blk_e = jnp.minimum(jnp.sum((bstart[:, None] >= pends[None, :]).astype(jnp.int32), axis=1), N_EXPERTS - 1)
    nv = jnp.clip(cnt[blk_e] - (bstart - pstarts[blk_e]), 0, tb)
    nv = jnp.where(bstart < pends[-1], nv, 0).astype(jnp.int32)
    return blk_e, nv, code


def _pad_lanes(v, width):
    return jnp.pad(v, ((0, 0), (0, width - v.shape[1])))


def kernel(x, attn_norm_w, w_in, ssd_conv_w, ssd_conv_b, ssd_dt_bias, ssd_a_log, ssd_d, ssd_norm_w, hgrn_lower_bounds, hgrn_norm_w, w_out, ffn_norm_w, router_group_w, router_group_b, router_expert_w, router_expert_b, expert_w_gate, expert_w_up, expert_w_down, final_norm_w):
    bsz, t, d = x.shape
    n = bsz * t
    depth = w_in.shape[0]

    half = HEAD_DIM // 2
    inv = 1.0 / (ROPE_THETA ** (jnp.arange(half, dtype=F32) / half))
    ang = jnp.arange(t, dtype=F32)[:, None] * inv[None, :]
    cos_t = jnp.concatenate([jnp.cos(ang), jnp.cos(ang)], axis=1)
    sin_t = jnp.concatenate([-jnp.sin(ang), jnp.sin(ang)], axis=1)

    def group_lanes(v):
        v = v.reshape(SSD_GROUPS, SSD_HEADS_PER_GROUP)
        return _pad_lanes(v, LANE).reshape(1, SSD_GROUPS * LANE)

    h = x.reshape(n, d)
    w_in_b = _wprep(w_in)
    moe = None
    for l in range(depth):
        if moe is None:
            (u,) = _norm_pass(h, attn_norm_w[l][None, :], None, BF16, False)
        else:
            u, h = _norm_pass(h, attn_norm_w[l][None, :], moe, BF16, True)
        p = _inproj(u, w_in_b, l)

        o_ret = _retention(p, cos_t, sin_t, bsz, t)
        o_ssd = _ssd(p, ssd_conv_w[l], ssd_conv_b[l][None, :], group_lanes(ssd_dt_bias[l]),
                     group_lanes(ssd_a_log[l]), jnp.repeat(ssd_d[l], SSD_HEADDIM)[None, :],
                     ssd_norm_w[l][None, :], bsz, t)
        o_hgrn = _hgrn(p, hgrn_lower_bounds, hgrn_norm_w[l][None, :], l, bsz, t)

        w_r = _pad_lanes(jnp.concatenate([router_group_w[l], router_expert_w[l]], axis=1), LANE)
        w_r_hi = w_r.astype(BF16)
        w_r = jnp.concatenate([w_r_hi, (w_r - w_r_hi.astype(F32)).astype(BF16)], axis=1)
        b_r = _pad_lanes(jnp.concatenate([router_group_b[l], router_expert_b[l]])[None, :], LANE)
        h, u2d, logits = _outproj(o_ret, o_ssd, o_hgrn, h, w_out[l].astype(BF16), ffn_norm_w[l][None, :], w_r, b_r)
        route, counts = _route(logits)
        blk_e, nv, code = _dispatch_tables(route, counts, n)
        o2d = _ffn(blk_e, nv, code, u2d, expert_w_gate, expert_w_up, expert_w_down, l)
        moe = (o2d, route)

    (out,) = _norm_pass(h, final_norm_w[None, :], moe, F32, False)
    return out.reshape(bsz, t, d)
```

```python
import functools
import math

import jax
import jax.numpy as jnp
from jax import lax
from jax.experimental import pallas as pl
from jax.experimental.pallas import tpu as pltpu

F32 = jnp.float32
BF16 = jnp.bfloat16
HIGHEST = lax.Precision.HIGHEST

V7X_VMEM_BYTES = 64 * 1024 * 1024
VMEM_LIMIT = V7X_VMEM_BYTES - 8 * 1024 * 1024
LANE = 128

EPS = 1e-6
LB_FLOOR = 1e-30
ROPE_THETA = 10000.0

RET_HEADS = 4
HEAD_DIM = 128
SSD_HEADDIM = 64
SSD_GROUPS = 2
SSD_HEADS_PER_GROUP = 8
SSD_CONV = 4
HGRN_HEADS = 4
CHUNK = 128
N_GROUPS = 4
EXPERTS_PER_GROUP = 8
N_EXPERTS = N_GROUPS * EXPERTS_PER_GROUP
MOE_BLOCK = 256
DMA_UNROLL = 8
ROUTE_TILE = 512
DT_LO = 4608
DT_HI = DT_LO + SSD_GROUPS * SSD_HEADS_PER_GROUP


def _cparams(*sem):
    return pltpu.CompilerParams(dimension_semantics=sem, vmem_limit_bytes=VMEM_LIMIT)


def _rms(x, w=None):
    y = x * lax.rsqrt(jnp.mean(x * x, axis=-1, keepdims=True) + EPS)
    return y if w is None else y * w


def _sigmoid(x):
    return 1.0 / (1.0 + jnp.exp(-x))


def _silu(x):
    return x * _sigmoid(x)


def _dot(a, b):
    return jnp.dot(a, b, preferred_element_type=F32)


def _dot_nt(a, b):
    return lax.dot_general(a, b, (((1,), (1,)), ((), ())), preferred_element_type=F32)


def _dot_tn(a, b):
    return lax.dot_general(a, b, (((0,), (0,)), ((), ())), preferred_element_type=F32)


def _tri_incl():
    ii = lax.broadcasted_iota(jnp.int32, (CHUNK, CHUNK), 0)
    jj = lax.broadcasted_iota(jnp.int32, (CHUNK, CHUNK), 1)
    return ii, jj


def _slab_rows(d):
    return d // LANE


def _slabs_to_rows(ref2d, base, rows, srows):
    return jnp.concatenate([ref2d[pl.ds(base + c, rows, stride=srows), :] for c in range(srows)], axis=1)


def _rows_to_slabs(ref2d, base, val, srows):
    rows = val.shape[0]
    for c in range(srows):
        ref2d[pl.ds(base + c, rows, stride=srows), :] = val[:, c * LANE:(c + 1) * LANE]


def _wprep_kernel(w_ref, wm_ref):
    nm = w_ref.shape[1] - (DT_HI - DT_LO)
    wm_ref[:, 0:DT_LO] = w_ref[:, 0:DT_LO].astype(BF16)
    wm_ref[:, DT_LO:nm] = w_ref[:, DT_HI:].astype(BF16)
    rows = w_ref.shape[0]
    pad = jnp.zeros((rows, LANE - SSD_HEADS_PER_GROUP), BF16)
    parts = []
    for g in range(SSD_GROUPS):
        lo = DT_LO + g * SSD_HEADS_PER_GROUP
        parts += [w_ref[:, lo:lo + SSD_HEADS_PER_GROUP].astype(BF16), pad]
    wm_ref[:, nm:] = jnp.concatenate(parts, axis=1)


def _wprep(w):
    depth, d, nin = w.shape
    tr = 256
    nout = nin - (DT_HI - DT_LO) + SSD_GROUPS * LANE
    return pl.pallas_call(
        _wprep_kernel,
        grid=(depth, d // tr),
        in_specs=[pl.BlockSpec((None, tr, nin), lambda l, i: (l, i, 0))],
        out_specs=pl.BlockSpec((None, tr, nout), lambda l, i: (l, i, 0)),
        out_shape=jax.ShapeDtypeStruct((depth, d, nout), BF16),
        compiler_params=_cparams("arbitrary", "arbitrary"),
        name="wprep",
    )(w)


def _norm_kernel(*refs, combine, write_h):
    if combine:
        h_ref, o0_ref, o1_ref, rt_ref, nw_ref = refs[:5]
        outs = refs[5:]
        rows = h_ref.shape[0]
        srows = _slab_rows(h_ref.shape[1])
        rt = rt_ref[...]
        h = (h_ref[...] + rt[:, 2:3] * _slabs_to_rows(o0_ref, 0, rows, srows)
             + rt[:, 3:4] * _slabs_to_rows(o1_ref, 0, rows, srows))
    else:
        h_ref, nw_ref = refs[:2]
        outs = refs[2:]
        h = h_ref[...]
    u_ref = outs[0]
    u_ref[...] = _rms(h, nw_ref[...]).astype(u_ref.dtype)
    if write_h:
        outs[1][...] = h


def _norm_pass(h, nw, moe, out_dtype, write_h):
    n, d = h.shape
    combine = moe is not None
    tm = min(512, n)
    row = lambda i: (i, 0)
    in_specs = [pl.BlockSpec((tm, d), row)]
    args = [h]
    if combine:
        o2d, rt = moe
        srows = _slab_rows(d)
        nt = n // tm
        in_specs += [pl.BlockSpec((tm * srows, LANE), row),
                     pl.BlockSpec((tm * srows, LANE), lambda i: (nt + i, 0)),
                     pl.BlockSpec((tm, LANE), row)]
        args += [o2d, o2d, rt]
    in_specs.append(pl.BlockSpec((1, d), lambda i: (0, 0)))
    args.append(nw)
    out_shape = [jax.ShapeDtypeStruct((n, d), out_dtype)]
    out_specs = [pl.BlockSpec((tm, d), row)]
    if write_h:
        out_shape.append(jax.ShapeDtypeStruct((n, d), F32))
        out_specs.append(pl.BlockSpec((tm, d), row))
    return pl.pallas_call(
        functools.partial(_norm_kernel, combine=combine, write_h=write_h),
        grid=(n // tm,),
        in_specs=in_specs, out_specs=out_specs, out_shape=out_shape,
        compiler_params=_cparams("arbitrary"),
        name="norm_pass",
    )(*args)


def _inproj_kernel(u_ref, w_ref, p_ref):
    p_ref[...] = _dot(u_ref[...], w_ref[...])


def _inproj(u, w, layer):
    n, d = u.shape
    np_ = w.shape[2]
    tm = min(1024, n)
    tn = np_ // 3
    return pl.pallas_call(
        _inproj_kernel,
        grid=(np_ // tn, n // tm),
        in_specs=[pl.BlockSpec((tm, d), lambda j, i: (i, 0)),
                  pl.BlockSpec((None, d, tn), lambda j, i: (layer, 0, j))],
        out_specs=pl.BlockSpec((tm, tn), lambda j, i: (i, j)),
        out_shape=jax.ShapeDtypeStruct((n, np_), F32),
        compiler_params=_cparams("arbitrary", "arbitrary"),
        name="inproj",
    )(u, w)


def _ret_kernel(q_ref, k_ref, v_ref, g_ref, cos_ref, sin_ref, o_ref, s_scr):
    t = q_ref.shape[0]
    ii, jj = _tri_incl()
    causal = ii >= jj
    dist = (ii - jj).astype(F32)
    iif = ii.astype(F32)
    consts = []
    for hd in range(RET_HEADS):
        lg = math.log(1.0 - 2.0 ** (-5.0 - hd))
        dmat = jnp.where(causal, jnp.exp(jnp.where(causal, dist * lg, 0.0)), 0.0)
        ecum = jnp.exp((iif + 1.0) * lg)
        wk = jnp.exp((CHUNK - 1.0 - iif) * lg)
        consts.append((dmat, ecum, wk, math.exp(CHUNK * lg)))
    scale = HEAD_DIM ** -0.5
    s_scr[...] = jnp.zeros_like(s_scr)

    def step(c, carry):
        r = pl.ds(pl.multiple_of(c * CHUNK, CHUNK), CHUNK)
        cs, sn = cos_ref[r, :], sin_ref[r, :]
        for hd in range(RET_HEADS):
            dmat, ecum, wk, elast = consts[hd]
            cols = slice(hd * HEAD_DIM, (hd + 1) * HEAD_DIM)
            q, k = q_ref[r, cols], k_ref[r, cols]
            qr = q * cs + pltpu.roll(q, HEAD_DIM // 2, 1) * sn
            kr = (k * cs + pltpu.roll(k, HEAD_DIM // 2, 1) * sn) * scale
            vb = v_ref[r, cols].astype(BF16)
            s = s_scr[hd]
            scores = _dot_nt(qr.astype(BF16), kr.astype(BF16)) * dmat
            out = _dot(scores.astype(BF16), vb) + _dot((qr * ecum).astype(BF16), s.astype(BF16))
            s_scr[hd] = elast * s + _dot_tn((kr * wk).astype(BF16), vb)
            o_ref[r, cols] = (_silu(g_ref[r, cols]) * _rms(out)).astype(o_ref.dtype)
        return carry

    lax.fori_loop(0, t // CHUNK, step, 0)


def _retention(p, cos_t, sin_t, bsz, t):
    n = bsz * t
    width = RET_HEADS * HEAD_DIM
    blk = lambda off: pl.BlockSpec((t, width), lambda b, off=off: (b, off))
    tab = pl.BlockSpec((t, HEAD_DIM), lambda b: (0, 0))
    return pl.pallas_call(
        _ret_kernel,
        grid=(bsz,),
        in_specs=[blk(0), blk(1), blk(2), blk(3), tab, tab],
        out_specs=pl.BlockSpec((t, width), lambda b: (b, 0)),
        out_shape=jax.ShapeDtypeStruct((n, width), BF16),
        scratch_shapes=[pltpu.VMEM((RET_HEADS, HEAD_DIM, HEAD_DIM), F32)],
        compiler_params=_cparams("arbitrary"),
        name="retention",
    )(p, p, p, p, cos_t, sin_t)


def _ssd_kernel(z_ref, x_ref, b_ref, c_ref, dt_ref, cwx_ref, cwb_ref, cwc_ref, cbx_ref, cbb_ref, cbc_ref,
                dtb_ref, alog_ref, dsk_ref, nw_ref, o_ref, s_scr):
    t = z_ref.shape[0]
    ii, jj = _tri_incl()
    causal = ii >= jj
    tri = jnp.where(causal, 1.0, 0.0).astype(F32)
    lane_lo = lax.broadcasted_iota(jnp.int32, (CHUNK, LANE), 1) < SSD_HEADDIM
    neg_a = -jnp.exp(alog_ref[...])
    s_scr[...] = jnp.zeros_like(s_scr)

    def conv(ref, w_ref, bias_ref, c):
        r = pl.ds(pl.multiple_of(c * CHUNK, CHUNK), CHUNK)
        rp = pl.ds(pl.multiple_of(jnp.maximum(c * CHUNK - 8, 0), 8), 8)
        cur = ref[r, :]
        ext = jnp.concatenate([jnp.where(c > 0, ref[rp, :], 0.0), cur], axis=0)
        w = w_ref[...]
        acc = bias_ref[...] + w[SSD_CONV - 1:SSD_CONV, :] * cur
        for lag in range(1, SSD_CONV):
            shifted = pltpu.roll(ext, lag, 0)[8:8 + CHUNK, :]
            acc = acc + w[SSD_CONV - 1 - lag:SSD_CONV - lag, :] * shifted
        return _silu(acc)

    def step(c, carry):
        r = pl.ds(pl.multiple_of(c * CHUNK, CHUNK), CHUNK)
        xs = conv(x_ref, cwx_ref, cbx_ref, c)
        bm = conv(b_ref, cwb_ref, cbb_ref, c).astype(BF16)
        cm = conv(c_ref, cwc_ref, cbc_ref, c).astype(BF16)
        xr = dt_ref[r, :] + dtb_ref[...]
        dt = jnp.maximum(xr, 0.0) + jnp.log1p(jnp.exp(-jnp.abs(xr)))
        la = dt * neg_a
        cum = jnp.dot(tri, la, precision=HIGHEST, preferred_element_type=F32)
        last = cum[CHUNK - 1:CHUNK, :]
        wj = dt * jnp.exp(last - cum)
        ecum = jnp.exp(cum)
        cum_t = cum.T
        dt_t = dt.T
        gm = jnp.where(causal, _dot_nt(cm, bm), 0.0)
        s = s_scr[...]
        cs = _dot(cm, s.astype(BF16))
        ys, xws, els = [], [], []
        for pr in range(SSD_HEADS_PER_GROUP // 2):
            xp = xs[:, pr * LANE:(pr + 1) * LANE]
            acc = None
            ecp, wjp, elp = None, None, None
            for half in range(2):
                hd = 2 * pr + half
                ci = jnp.broadcast_to(cum[:, hd:hd + 1], (CHUNK, CHUNK))
                dec = jnp.exp(jnp.minimum(ci - cum_t[hd:hd + 1, :], 0.0))
                m = gm * dec * dt_t[hd:hd + 1, :]
                sel = lane_lo if half == 0 else jnp.logical_not(lane_lo)
                y = _dot(m.astype(BF16), jnp.where(sel, xp, 0.0).astype(BF16))
                acc = y if acc is None else acc + y
                eb = jnp.broadcast_to(ecum[:, hd:hd + 1], (CHUNK, LANE))
                wb = jnp.broadcast_to(wj[:, hd:hd + 1], (CHUNK, LANE))
                lb = jnp.broadcast_to(ecum[CHUNK - 1:CHUNK, hd:hd + 1], (1, LANE))
                ecp = eb if half == 0 else jnp.where(lane_lo, ecp, eb)
                wjp = wb if half == 0 else jnp.where(lane_lo, wjp, wb)
                elp = lb if half == 0 else jnp.where(lane_lo[0:1, :], elp, lb)
            ys.append(acc + ecp * cs[:, pr * LANE:(pr + 1) * LANE])
            xws.append((wjp * xp).astype(BF16))
            els.append(elp)
        y = jnp.concatenate(ys, axis=1)
        s_scr[...] = jnp.concatenate(els, axis=1) * s + _dot_tn(bm, jnp.concatenate(xws, axis=1))
        y = (y + dsk_ref[...] * xs) * _silu(z_ref[r, :])
        o_ref[r, :] = (_rms(y) * nw_ref[...]).astype(o_ref.dtype)
        return carry

    lax.fori_loop(0, t // CHUNK, step, 0)


def _ssd(p, conv_w, conv_b, dtb, alog, dsk, nw, bsz, t):
    n = bsz * t
    gw = SSD_HEADS_PER_GROUP * SSD_HEADDIM
    st = HEAD_DIM
    z_off, x_off = 2048 // gw, 3072 // gw
    b_off, c_off = 4096 // st, 4352 // st
    dt_off = (p.shape[1] - SSD_GROUPS * LANE) // LANE
    xw = SSD_GROUPS * gw
    par = lambda shape, f: pl.BlockSpec(shape, f)
    in_specs = [
        par((t, gw), lambda b, g: (b, z_off + g)),
        par((t, gw), lambda b, g: (b, x_off + g)),
        par((t, st), lambda b, g: (b, b_off + g)),
        par((t, st), lambda b, g: (b, c_off + g)),
        par((t, LANE), lambda b, g: (b, dt_off + g)),
        par((SSD_CONV, gw), lambda b, g: (0, g)),
        par((SSD_CONV, st), lambda b, g: (0, xw // st + g)),
        par((SSD_CONV, st), lambda b, g: (0, xw // st + SSD_GROUPS + g)),
        par((1, gw), lambda b, g: (0, g)),
        par((1, st), lambda b, g: (0, xw // st + g)),
        par((1, st), lambda b, g: (0, xw // st + SSD_GROUPS + g)),
        par((1, LANE), lambda b, g: (0, g)),
        par((1, LANE), lambda b, g: (0, g)),
        par((1, gw), lambda b, g: (0, g)),
        par((1, gw), lambda b, g: (0, g)),
    ]
    return pl.pallas_call(
        _ssd_kernel,
        grid=(bsz, SSD_GROUPS),
        in_specs=in_specs,
        out_specs=pl.BlockSpec((t, gw), lambda b, g: (b, g)),
        out_shape=jax.ShapeDtypeStruct((n, SSD_GROUPS * gw), BF16),
        scratch_shapes=[pltpu.VMEM((st, gw), F32)],
        compiler_params=_cparams("arbitrary", "arbitrary"),
        name="ssd",
    )(p, p, p, p, p, conv_w, conv_w, conv_w, conv_b, conv_b, conv_b, dtb, alog, dsk, nw)


def _hgrn_kernel(q_ref, f_ref, i_ref, g_ref, lb_ref, nw_ref, o_ref, st_scr, cum_scr, *, layer):
    t = q_ref.shape[0]
    lbm = lb_ref[...]
    depth = lbm.shape[0]
    mx = lbm[0:1, :]
    for i in range(1, depth):
        mx = jnp.maximum(mx, lbm[i:i + 1, :])
    ex = [jnp.exp(lbm[i:i + 1, :] - mx) for i in range(depth)]
    den = ex[0]
    for i in range(1, depth):
        den = den + ex[i]
    sm = [e / den for e in ex]
    csum = sm[0]
    for i in range(1, layer + 1):
        csum = csum + sm[i]
    lb = jnp.maximum(csum - sm[0], 0.0)
    log_lb = jnp.log(jnp.maximum(lb, LB_FLOOR))
    l1m = jnp.log1p(-lb)
    oml = 1.0 - lb

    ii, jj = _tri_incl()
    tri = jnp.where(ii >= jj, 1.0, 0.0).astype(F32)
    lvl = jnp.where(ii > jj, 31 - lax.clz(ii ^ jj), -1)
    eye = ii == jj
    width = q_ref.shape[1]
    hcols = [slice(c0, c0 + HEAD_DIM) for c0 in range(0, width, HEAD_DIM)]
    row = lax.broadcasted_iota(jnp.int32, (CHUNK, width), 0)
    scale = HEAD_DIM ** -0.5
    nlev = int(math.log2(CHUNK))
    st_scr[...] = jnp.zeros_like(st_scr)

    def step(c, carry):
        r = pl.ds(pl.multiple_of(c * CHUNK, CHUNK), CHUNK)
        q = _silu(q_ref[r, :]) * scale
        f = f_ref[r, :]
        ls = jnp.minimum(f, 0.0) - jnp.log1p(jnp.exp(-jnp.abs(f)))
        a, b = log_lb, l1m + ls
        lf = jnp.maximum(a, b) + jnp.log1p(jnp.exp(-jnp.abs(a - b)))
        kk = oml / (1.0 + jnp.exp(f))
        vb = i_ref[r, :].astype(BF16)
        cum = jnp.dot(tri, lf, precision=HIGHEST, preferred_element_type=F32)
        cum_scr[...] = cum
        qb, kb = q.astype(BF16), kk.astype(BF16)
        scores = [jnp.where(eye, _dot_nt(qb[:, cs], kb[:, cs]), 0.0) for cs in hcols]
        for lv in range(nlev):
            s = 1 << lv
            if 2 * s >= 8:
                ref = jnp.concatenate(
                    [jnp.broadcast_to(cum_scr[g0 * 2 * s + s - 1:g0 * 2 * s + s, :], (2 * s, width))
                     for g0 in range(CHUNK // (2 * s))], axis=0)
            elif s == 2:
                m4 = row & 3
                ref = jnp.where(m4 == 0, pltpu.roll(cum, CHUNK - 1, 0),
                                jnp.where(m4 == 1, cum,
                                          jnp.where(m4 == 2, pltpu.roll(cum, 1, 0), pltpu.roll(cum, 2, 0))))
            else:
                ref = jnp.where((row & 1) == 1, pltpu.roll(cum, 1, 0), cum)
            qs = (q * jnp.exp(jnp.minimum(cum - ref, 0.0))).astype(BF16)
            ks = (kk * jnp.exp(jnp.minimum(ref - cum, 0.0))).astype(BF16)
            scores = [jnp.where(lvl == lv, _dot_nt(qs[:, cs], ks[:, cs]), sc) for cs, sc in zip(hcols, scores)]
        last = cum[CHUNK - 1:CHUNK, :]
        qe = (q * jnp.exp(cum)).astype(BF16)
        kw = (kk * jnp.exp(last - cum)).astype(BF16)
        elast = jnp.exp(last)
        outs = []
        for hd, cs in enumerate(hcols):
            st = st_scr[hd]
            out = _dot(scores[hd].astype(BF16), vb[:, cs]) + _dot_nt(qe[:, cs], st.astype(BF16))
            st_scr[hd] = st * elast[:, cs] + _dot_tn(vb[:, cs], kw[:, cs])
            outs.append(_rms(out))
        o = jnp.concatenate(outs, axis=1) * nw_ref[...]
        o_ref[r, :] = (_silu(g_ref[r, :]) * o).astype(o_ref.dtype)
        return carry

    lax.fori_loop(0, t // CHUNK, step, 0)


def _hgrn(p, lbounds, nw, layer, bsz, t):
    n = bsz * t
    width = HGRN_HEADS * HEAD_DIM
    base = DT_LO // width
    blk = lambda off: pl.BlockSpec((t, width), lambda b, off=off: (b, base + off))
    return pl.pallas_call(
        functools.partial(_hgrn_kernel, layer=layer),
        grid=(bsz,),
        in_specs=[blk(0), blk(1), blk(2), blk(3),
                  pl.BlockSpec((lbounds.shape[0], width), lambda b: (0, 0)),
                  pl.BlockSpec((1, width), lambda b: (0, 0))],
        out_specs=pl.BlockSpec((t, width), lambda b: (b, 0)),
        out_shape=jax.ShapeDtypeStruct((n, width), BF16),
        scratch_shapes=[pltpu.VMEM((HGRN_HEADS, HEAD_DIM, HEAD_DIM), F32), pltpu.VMEM((CHUNK, width), F32)],
        compiler_params=_cparams("arbitrary"),
        name="hgrn2",
    )(p, p, p, p, lbounds, nw)


def _outproj_kernel(a_ref, b_ref, c_ref, h_ref, w_ref, nw_ref, wr_ref, br_ref, hout_ref, u_ref, lg_ref):
    wa, wb = a_ref.shape[1], b_ref.shape[1]
    acc = _dot(a_ref[...], w_ref[0:wa, :])
    acc = acc + _dot(b_ref[...], w_ref[wa:wa + wb, :])
    acc = acc + _dot(c_ref[...], w_ref[wa + wb:, :])
    h = h_ref[...] + acc
    hout_ref[...] = h
    u = _rms(h, nw_ref[...])
    u_hi = u.astype(BF16)
    u_lo = (u - u_hi.astype(F32)).astype(BF16)
    t2 = _dot(u_hi, wr_ref[...])
    lg_ref[...] = t2[:, :LANE] + t2[:, LANE:] + _dot(u_lo, wr_ref[:, :LANE]) + br_ref[...]
    half = u.shape[1] // 2
    bits = lax.bitcast_convert_type(u_hi.astype(F32), jnp.uint32)
    packed = (bits[:, :half] >> 16) | (bits[:, half:] & jnp.uint32(0xFFFF0000))
    _rows_to_slabs(u_ref, 0, packed, half // LANE)


def _outproj(o_a, o_b, o_c, h, w_out, nw, w_r, b_r):
    n, d = h.shape
    tm = min(512, n)
    srows = _slab_rows(d // 2)
    row = lambda i: (i, 0)
    fix = lambda i: (0, 0)
    return pl.pallas_call(
        _outproj_kernel,
        grid=(n // tm,),
        in_specs=[pl.BlockSpec((tm, o_a.shape[1]), row), pl.BlockSpec((tm, o_b.shape[1]), row),
                  pl.BlockSpec((tm, o_c.shape[1]), row), pl.BlockSpec((tm, d), row),
                  pl.BlockSpec(w_out.shape, fix), pl.BlockSpec((1, d), fix),
                  pl.BlockSpec(w_r.shape, fix), pl.BlockSpec((1, LANE), fix)],
        out_specs=[pl.BlockSpec((tm, d), row), pl.BlockSpec((tm * srows, LANE), row),
                   pl.BlockSpec((tm, LANE), row)],
        out_shape=[jax.ShapeDtypeStruct((n, d), F32), jax.ShapeDtypeStruct((n * srows, LANE), jnp.uint32),
                   jax.ShapeDtypeStruct((n, LANE), F32)],
        compiler_params=_cparams("arbitrary"),
        name="outproj",
    )(o_a, o_b, o_c, h, w_out, nw, w_r, b_r)


def _route_kernel(lg_ref, rt_ref, cnt_ref, carry_scr):
    tr = lg_ref.shape[0]

    @pl.when(pl.program_id(0) == 0)
    def _():
        carry_scr[...] = jnp.zeros_like(carry_scr)

    lg = lg_ref[...]
    lane = lax.broadcasted_iota(jnp.int32, (tr, LANE), 1)
    neg = -jnp.inf
    big = jnp.int32(LANE)

    def first_max(vals):
        m = jnp.max(vals, axis=-1, keepdims=True)
        idx = jnp.min(jnp.where(vals == m, lane, big), axis=-1, keepdims=True)
        return m, idx

    gl = jnp.where(lane < N_GROUPS, lg, neg)
    gmax, gidx = first_max(gl)
    gate = 1.0 / jnp.sum(jnp.exp(gl - gmax), axis=-1, keepdims=True)
    lo = N_GROUPS + EXPERTS_PER_GROUP * gidx
    el = jnp.where((lane >= lo) & (lane < lo + EXPERTS_PER_GROUP), lg, neg)
    m1, i1 = first_max(el)
    m2, i2 = first_max(jnp.where(lane == i1, neg, el))
    e21 = jnp.exp(m2 - m1)
    w1 = gate * (1.0 / (1.0 + e21))
    w2 = gate * (e21 / (1.0 + e21))

    oh1 = lane == i1
    oh2 = lane == i2
    m = jnp.where(oh1 | oh2, 1.0, 0.0)
    ti = lax.broadcasted_iota(jnp.int32, (tr, tr), 0)
    tj = lax.broadcasted_iota(jnp.int32, (tr, tr), 1)
    before = _dot(jnp.where(ti > tj, 1.0, 0.0).astype(BF16), m.astype(BF16)) + carry_scr[0:1, :]
    r1 = jnp.sum(jnp.where(oh1, before, 0.0), axis=-1, keepdims=True)
    r2 = jnp.sum(jnp.where(oh2, before, 0.0), axis=-1, keepdims=True)
    total = carry_scr[0:1, :] + jnp.sum(m, axis=0, keepdims=True)
    carry_scr[...] = jnp.broadcast_to(total, carry_scr.shape)
    cnt_ref[...] = jnp.broadcast_to(total, cnt_ref.shape)

    e1 = (i1 - N_GROUPS).astype(F32)
    e2 = (i2 - N_GROUPS).astype(F32)
    out = jnp.zeros((tr, LANE), F32)
    for pos, val in enumerate((e1, e2, w1, w2, r1, r2)):
        out = jnp.where(lane == pos, val, out)
    rt_ref[...] = out


def _route(logits):
    n = logits.shape[0]
    tr = min(ROUTE_TILE, n)
    return pl.pallas_call(
        _route_kernel,
        grid=(n // tr,),
        in_specs=[pl.BlockSpec((tr, LANE), lambda i: (i, 0))],
        out_specs=[pl.BlockSpec((tr, LANE), lambda i: (i, 0)), pl.BlockSpec((8, LANE), lambda i: (0, 0))],
        out_shape=[jax.ShapeDtypeStruct((n, LANE), F32), jax.ShapeDtypeStruct((8, LANE), F32)],
        scratch_shapes=[pltpu.VMEM((8, LANE), F32)],
        compiler_params=_cparams("arbitrary"),
        name="route",
    )(logits)


def _ffn_kernel(blk_e_ref, nv_ref, code_ref, u_hbm, wg_ref, wu_ref, wd_ref, o_hbm,
                xbuf, ybuf, wgb, wub, wdb, gsem, ssem, *, tb, xrows, yrows):
    b = pl.program_id(0)
    n_tok = u_hbm.shape[0] // xrows
    slot = b % 2
    prev = jnp.maximum(b - 1, 0)
    nv = nv_ref[b]
    nv_prev = jnp.where(b > 0, nv_ref[prev], 0)

    def slab(ref, srows, idx, count=1):
        return ref.at[pl.ds(pl.multiple_of(idx * srows, srows), count * srows)]

    def gather_row(blk, sl, r, lane):
        tok = code_ref[blk * tb + r] & (n_tok - 1)
        pltpu.make_async_copy(slab(u_hbm, xrows, tok), slab(xbuf, xrows, sl * tb + r),
                              gsem.at[sl]).start(priority=lane % 2)

    def scatter_row(blk, sl, r, dst, lane):
        pltpu.make_async_copy(slab(ybuf, yrows, sl * tb + r), slab(o_hbm, yrows, dst),
                              ssem.at[sl]).start(priority=lane % 2)

    def wait_gather(sl):
        pltpu.make_async_copy(slab(u_hbm, xrows, 0, tb), slab(xbuf, xrows, sl * tb, tb), gsem.at[sl]).wait()

    def wait_scatter(sl):
        pltpu.make_async_copy(slab(ybuf, yrows, sl * tb, tb), slab(o_hbm, yrows, 0, tb), ssem.at[sl]).wait()

    def row_loop(fn):
        def grp(i, c):
            for k in range(DMA_UNROLL):
                fn(i * DMA_UNROLL + k, k)
            return c
        lax.fori_loop(0, tb // DMA_UNROLL, grp, 0)

    @pl.when(b == 0)
    def _():
        ybuf[...] = jnp.zeros_like(ybuf)
        row_loop(lambda r, k: gather_row(0, 0, r, k))

    @pl.when((nv > 0) & ((b == 0) | (blk_e_ref[b] != blk_e_ref[prev])))
    def _():
        wgb[...] = wg_ref[...].astype(BF16)
        wub[...] = wu_ref[...].astype(BF16)
        wdb[...] = wd_ref[...].astype(BF16)

    @pl.when(nv > 0)
    def _():
        wait_gather(slot)
        words = _slabs_to_rows(xbuf, slot * tb * xrows, tb, xrows)
        x = jnp.concatenate(
            [lax.bitcast_convert_type(words << 16, F32).astype(BF16),
             lax.bitcast_convert_type(words & jnp.uint32(0xFFFF0000), F32).astype(BF16)], axis=1)
        has_prev = b > 0
        for r in range(tb):
            gather_row(b + 1, 1 - slot, r, r)
            dst = jnp.where(has_prev, code_ref[prev * tb + r], 2 * n_tok + r)
            scatter_row(prev, 1 - slot, r, dst, r)
        hdn = _silu(_dot(x, wgb[...])) * _dot(x, wub[...])
        y = _dot(hdn.astype(BF16), wdb[...])
        _rows_to_slabs(ybuf, slot * tb * yrows, y, yrows)
        wait_scatter(1 - slot)

    @pl.when((nv == 0) & (nv_prev > 0))
    def _():
        wait_gather(slot)
        row_loop(lambda r, k: scatter_row(prev, 1 - slot, r, code_ref[prev * tb + r], k))
        wait_scatter(1 - slot)


def _ffn(blk_e, nv, code, u2d, wg, wu, wd, layer):
    d, de = wg.shape[2], wg.shape[3]
    xrows, yrows = _slab_rows(d // 2), _slab_rows(d)
    n = u2d.shape[0] // xrows
    assert n & (n - 1) == 0, "token index is taken from the slot code by masking"
    nblk = blk_e.shape[0]
    tb = MOE_BLOCK
    grid_spec = pltpu.PrefetchScalarGridSpec(
        num_scalar_prefetch=3,
        grid=(nblk,),
        in_specs=[pl.BlockSpec(memory_space=pl.ANY),
                  pl.BlockSpec((None, None, d, de), lambda b, be, nv, cd: (layer, be[b], 0, 0)),
                  pl.BlockSpec((None, None, d, de), lambda b, be, nv, cd: (layer, be[b], 0, 0)),
                  pl.BlockSpec((None, None, de, d), lambda b, be, nv, cd: (layer, be[b], 0, 0))],
        out_specs=pl.BlockSpec(memory_space=pl.ANY),
        scratch_shapes=[pltpu.VMEM((2 * tb * xrows, LANE), jnp.uint32), pltpu.VMEM((2 * tb * yrows, LANE), F32),
                        pltpu.VMEM((d, de), BF16), pltpu.VMEM((d, de), BF16), pltpu.VMEM((de, d), BF16),
                        pltpu.SemaphoreType.DMA((2,)), pltpu.SemaphoreType.DMA((2,))],
    )
    return pl.pallas_call(
        functools.partial(_ffn_kernel, tb=tb, xrows=xrows, yrows=yrows),
        grid_spec=grid_spec,
        out_shape=jax.ShapeDtypeStruct(((2 * n + tb) * yrows, LANE), F32),
        compiler_params=_cparams("arbitrary"),
        name="moe_ffn",
    )(blk_e, nv, code, u2d, wg, wu, wd)


def _dispatch_tables(route, counts, n):
    tb = MOE_BLOCK
    e = route[:, 0:2].astype(jnp.int32)
    rank = route[:, 4:6].astype(jnp.int32)
    cnt = counts[0, N_GROUPS:N_GROUPS + N_EXPERTS].astype(jnp.int32)
    padded = ((cnt + tb - 1) // tb) * tb
    pends = jnp.cumsum(padded)
    pstarts = pends - padded
    dest = (pstarts[e] + rank).reshape(-1)
    p_total = 2 * n + N_EXPERTS * tb
    nblk = p_total // tb
    slot_id = jnp.arange(2 * n, dtype=jnp.int32)
    code = (2 * n + jnp.arange(p_total, dtype=jnp.int32) % tb).at[dest].set(
        (slot_id & 1) * n + (slot_id >> 1), unique_indices=True, mode="promise_in_bounds")
    bstart = jnp.arange(nblk, dtype=jnp.int32) * tb
    blk_e = jnp.minimum(jnp.sum((bstart[:, None] >= pends[None, :]).astype(jnp.int32), axis=1), N_EXPERTS - 1)
    nv = jnp.clip(cnt[blk_e] - (bstart - pstarts[blk_e]), 0, tb)
    nv = jnp.where(bstart < pends[-1], nv, 0).astype(jnp.int32)
    return blk_e, nv, code


def _pad_lanes(v, width):
    return jnp.pad(v, ((0, 0), (0, width - v.shape[1])))


def kernel(x, attn_norm_w, w_in, ssd_conv_w, ssd_conv_b, ssd_dt_bias, ssd_a_log, ssd_d, ssd_norm_w, hgrn_lower_bounds, hgrn_norm_w, w_out, ffn_norm_w, router_group_w, router_group_b, router_expert_w, router_expert_b, expert_w_gate, expert_w_up, expert_w_down, final_norm_w):
    bsz, t, d = x.shape
    n = bsz * t
    depth = w_in.shape[0]

    half = HEAD_DIM // 2
    inv = 1.0 / (ROPE_THETA ** (jnp.arange(half, dtype=F32) / half))
    ang = jnp.arange(t, dtype=F32)[:, None] * inv[None, :]
    cos_t = jnp.concatenate([jnp.cos(ang), jnp.cos(ang)], axis=1)
    sin_t = jnp.concatenate([-jnp.sin(ang), jnp.sin(ang)], axis=1)

    def group_lanes(v):
        v = v.reshape(SSD_GROUPS, SSD_HEADS_PER_GROUP)
        return _pad_lanes(v, LANE).reshape(1, SSD_GROUPS * LANE)

    h = x.reshape(n, d)
    w_in_b = _wprep(w_in)
    moe = None
    for l in range(depth):
        if moe is None:
            (u,) = _norm_pass(h, attn_norm_w[l][None, :], None, BF16, False)
        else:
            u, h = _norm_pass(h, attn_norm_w[l][None, :], moe, BF16, True)
        p = _inproj(u, w_in_b, l)

        o_ret = _retention(p, cos_t, sin_t, bsz, t)
        o_ssd = _ssd(p, ssd_conv_w[l], ssd_conv_b[l][None, :], group_lanes(ssd_dt_bias[l]),
                     group_lanes(ssd_a_log[l]), jnp.repeat(ssd_d[l], SSD_HEADDIM)[None, :],
                     ssd_norm_w[l][None, :], bsz, t)
        o_hgrn = _hgrn(p, hgrn_lower_bounds, hgrn_norm_w[l][None, :], l, bsz, t)

        w_r = _pad_lanes(jnp.concatenate([router_group_w[l], router_expert_w[l]], axis=1), LANE)
        w_r_hi = w_r.astype(BF16)
        w_r = jnp.concatenate([w_r_hi, (w_r - w_r_hi.astype(F32)).astype(BF16)], axis=1)
        b_r = _pad_lanes(jnp.concatenate([router_group_b[l], router_expert_b[l]])[None, :], LANE)
        h, u2d, logits = _outproj(o_ret, o_ssd, o_hgrn, h, w_out[l].astype(BF16), ffn_norm_w[l][None, :], w_r, b_r)
        route, counts = _route(logits)
        blk_e, nv, code = _dispatch_tables(route, counts, n)
        o2d = _ffn(blk_e, nv, code, u2d, expert_w_gate, expert_w_up, expert_w_down, l)
        moe = (o2d, route)

    (out,) = _norm_pass(h, final_norm_w[None, :], moe, F32, False)
    return out.reshape(bsz, t, d)
```

```python
import functools
import math

import jax
import jax.numpy as jnp
from jax import lax
from jax.experimental import pallas as pl
from jax.experimental.pallas import tpu as pltpu

F32 = jnp.float32
BF16 = jnp.bfloat16
HIGHEST = lax.Precision.HIGHEST

V7X_VMEM_BYTES = 64 * 1024 * 1024
VMEM_LIMIT = V7X_VMEM_BYTES - 8 * 1024 * 1024
LANE = 128

EPS = 1e-6
LOG2E = 1.4426950408889634
LB_FLOOR = 1e-30
ROPE_THETA = 10000.0

RET_HEADS = 4
HEAD_DIM = 128
SSD_HEADDIM = 64
SSD_GROUPS = 2
SSD_HEADS_PER_GROUP = 8
SSD_CONV = 4
HGRN_HEADS = 4
CHUNK = 128
N_GROUPS = 4
EXPERTS_PER_GROUP = 8
N_EXPERTS = N_GROUPS * EXPERTS_PER_GROUP
MOE_BLOCK = 256
DMA_UNROLL = 8
ROUTE_TILE = 512
DT_LO = 4608
DT_HI = DT_LO + SSD_GROUPS * SSD_HEADS_PER_GROUP


def _cparams(*sem):
    return pltpu.CompilerParams(dimension_semantics=sem, vmem_limit_bytes=VMEM_LIMIT)


def _rms(x, w=None):
    y = x * lax.rsqrt(jnp.mean(x * x, axis=-1, keepdims=True) + EPS)
    return y if w is None else y * w


def _sigmoid(x):
    return 1.0 / (1.0 + jnp.exp(-x))


def _silu(x):
    return x * _sigmoid(x)


def _dot(a, b):
    return jnp.dot(a, b, preferred_element_type=F32)


def _dot_nt(a, b):
    return lax.dot_general(a, b, (((1,), (1,)), ((), ())), preferred_element_type=F32)


def _dot_tn(a, b):
    return lax.dot_general(a, b, (((0,), (0,)), ((), ())), preferred_element_type=F32)


def _tri_incl():
    ii = lax.broadcasted_iota(jnp.int32, (CHUNK, CHUNK), 0)
    jj = lax.broadcasted_iota(jnp.int32, (CHUNK, CHUNK), 1)
    return ii, jj


def _slab_rows(d):
    return d // LANE


def _slabs_to_rows(ref2d, base, rows, srows):
    return jnp.concatenate([ref2d[pl.ds(base + c, rows, stride=srows), :] for c in range(srows)], axis=1)


def _rows_to_slabs(ref2d, base, val, srows):
    rows = val.shape[0]
    for c in range(srows):
        ref2d[pl.ds(base + c, rows, stride=srows), :] = val[:, c * LANE:(c + 1) * LANE]


WPREP_ROWS = 256


def _wprep_kernel(wt_hbm, out_ref, buf, sem):
    l, j = pl.program_id(0), pl.program_id(1)
    nj = pl.num_programs(1)
    step = l * nj + j
    slot = step % 2
    n_lo = DT_LO // WPREP_ROWS
    n_main = n_lo + (wt_hbm.shape[1] - DT_HI) // WPREP_ROWS

    def block_copy(st, sl):
        jj = st % nj
        src = jnp.where(jj < n_lo, jj * WPREP_ROWS,
                        jnp.where(jj < n_main, DT_HI + (jj - n_lo) * WPREP_ROWS, DT_LO))
        return pltpu.make_async_copy(wt_hbm.at[st // nj, pl.ds(pl.multiple_of(src, 8), WPREP_ROWS), :],
                                     buf.at[sl], sem.at[sl])

    @pl.when(step == 0)
    def _():
        block_copy(0, 0).start()

    @pl.when(step + 1 < pl.num_programs(0) * nj)
    def _():
        block_copy(step + 1, 1 - slot).start()

    block_copy(step, slot).wait()

    @pl.when(j < n_main)
    def _():
        out_ref[...] = buf[slot].astype(BF16)

    @pl.when(j >= n_main)
    def _():
        pad = jnp.zeros((LANE - SSD_HEADS_PER_GROUP, out_ref.shape[1]), F32)
        parts = []
        for g in range(SSD_GROUPS):
            parts += [buf[slot, g * SSD_HEADS_PER_GROUP:(g + 1) * SSD_HEADS_PER_GROUP, :], pad]
        out_ref[...] = jnp.concatenate(parts, axis=0).astype(BF16)


def _wprep(wt):
    depth, nin, d = wt.shape
    nout = nin - (DT_HI - DT_LO) + SSD_GROUPS * LANE
    assert DT_LO % WPREP_ROWS == 0 and (nin - DT_HI) % WPREP_ROWS == 0 and SSD_GROUPS * LANE == WPREP_ROWS
    return pl.pallas_call(
        _wprep_kernel,
        grid=(depth, nout // WPREP_ROWS),
        in_specs=[pl.BlockSpec(memory_space=pl.ANY)],
        out_specs=pl.BlockSpec((None, WPREP_ROWS, d), lambda l, j: (l, j, 0)),
        out_shape=jax.ShapeDtypeStruct((depth, nout, d), BF16),
        scratch_shapes=[pltpu.VMEM((2, WPREP_ROWS, d), F32), pltpu.SemaphoreType.DMA((2,))],
        compiler_params=_cparams("arbitrary", "arbitrary"),
        name="wprep",
    )(wt)


def _norm_kernel(*refs, combine, write_h):
    if combine:
        h_ref, o0_ref, o1_ref, rt_ref, nw_ref = refs[:5]
        outs = refs[5:]
        rows = h_ref.shape[0]
        srows = _slab_rows(h_ref.shape[1])
        rt = rt_ref[...]
        h = (h_ref[...] + rt[:, 2:3] * _slabs_to_rows(o0_ref, 0, rows, srows)
             + rt[:, 3:4] * _slabs_to_rows(o1_ref, 0, rows, srows))
    else:
        h_ref, nw_ref = refs[:2]
        outs = refs[2:]
        h = h_ref[...]
    u_ref = outs[0]
    u_ref[...] = _rms(h, nw_ref[...]).astype(u_ref.dtype)
    if write_h:
        outs[1][...] = h


def _norm_pass(h, nw, moe, out_dtype, write_h):
    n, d = h.shape
    combine = moe is not None
    tm = min(512, n)
    row = lambda i: (i, 0)
    in_specs = [pl.BlockSpec((tm, d), row)]
    args = [h]
    if combine:
        o2d, rt = moe
        srows = _slab_rows(d)
        nt = n // tm
        in_specs += [pl.BlockSpec((tm * srows, LANE), row),
                     pl.BlockSpec((tm * srows, LANE), lambda i: (nt + i, 0)),
                     pl.BlockSpec((tm, LANE), row)]
        args += [o2d, o2d, rt]
    in_specs.append(pl.BlockSpec((1, d), lambda i: (0, 0)))
    args.append(nw)
    out_shape = [jax.ShapeDtypeStruct((n, d), out_dtype)]
    out_specs = [pl.BlockSpec((tm, d), row)]
    if write_h:
        out_shape.append(jax.ShapeDtypeStruct((n, d), F32))
        out_specs.append(pl.BlockSpec((tm, d), row))
    return pl.pallas_call(
        functools.partial(_norm_kernel, combine=combine, write_h=write_h),
        grid=(n // tm,),
        in_specs=in_specs, out_specs=out_specs, out_shape=out_shape,
        compiler_params=_cparams("arbitrary"),
        name="norm_pass",
    )(*args)


def _inproj_kernel(u_ref, wt_ref, p_ref):
    p_ref[...] = _dot_nt(u_ref[...], wt_ref[...])


def _inproj(u, wt, layer):
    n, d = u.shape
    np_ = wt.shape[1]
    tm = min(1024, n)
    tn = np_ // 3
    return pl.pallas_call(
        _inproj_kernel,
        grid=(np_ // tn, n // tm),
        in_specs=[pl.BlockSpec((tm, d), lambda j, i: (i, 0)),
                  pl.BlockSpec((None, tn, d), lambda j, i: (layer, j, 0))],
        out_specs=pl.BlockSpec((tm, tn), lambda j, i: (i, j)),
        out_shape=jax.ShapeDtypeStruct((n, np_), F32),
        compiler_params=_cparams("arbitrary", "arbitrary"),
        name="inproj",
    )(u, wt)


def _ret_kernel(q_ref, k_ref, v_ref, g_ref, cos_ref, sin_ref, o_ref, s_scr):
    t = q_ref.shape[0]
    ii, jj = _tri_incl()
    causal = ii >= jj
    dist = (ii - jj).astype(F32)
    iif = ii.astype(F32)
    consts = []
    for hd in range(RET_HEADS):
        lg = math.log(1.0 - 2.0 ** (-5.0 - hd))
        dmat = jnp.where(causal, jnp.exp(jnp.where(causal, dist * lg, 0.0)), 0.0)
        ecum = jnp.exp((iif + 1.0) * lg)
        wk = jnp.exp((CHUNK - 1.0 - iif) * lg)
        consts.append((dmat, ecum, wk, math.exp(CHUNK * lg)))
    scale = HEAD_DIM ** -0.5
    s_scr[...] = jnp.zeros_like(s_scr)

    def step(c, carry):
        r = pl.ds(pl.multiple_of(c * CHUNK, CHUNK), CHUNK)
        cs, sn = cos_ref[r, :], sin_ref[r, :]
        for hd in range(RET_HEADS):
            dmat, ecum, wk, elast = consts[hd]
            cols = slice(hd * HEAD_DIM, (hd + 1) * HEAD_DIM)
            q, k = q_ref[r, cols], k_ref[r, cols]
            qr = q * cs + pltpu.roll(q, HEAD_DIM // 2, 1) * sn
            kr = (k * cs + pltpu.roll(k, HEAD_DIM // 2, 1) * sn) * scale
            vb = v_ref[r, cols].astype(BF16)
            s = s_scr[hd]
            scores = _dot_nt(qr.astype(BF16), kr.astype(BF16)) * dmat
            out = _dot(scores.astype(BF16), vb) + _dot((qr * ecum).astype(BF16), s.astype(BF16))
            s_scr[hd] = elast * s + _dot_tn((kr * wk).astype(BF16), vb)
            o_ref[r, cols] = (_silu(g_ref[r, cols]) * _rms(out)).astype(o_ref.dtype)
        return carry

    lax.fori_loop(0, t // CHUNK, step, 0)


def _retention(p, cos_t, sin_t, bsz, t):
    n = bsz * t
    width = RET_HEADS * HEAD_DIM
    blk = lambda off: pl.BlockSpec((t, width), lambda b, off=off: (b, off))
    tab = pl.BlockSpec((t, HEAD_DIM), lambda b: (0, 0))
    return pl.pallas_call(
        _ret_kernel,
        grid=(bsz,),
        in_specs=[blk(0), blk(1), blk(2), blk(3), tab, tab],
        out_specs=pl.BlockSpec((t, width), lambda b: (b, 0)),
        out_shape=jax.ShapeDtypeStruct((n, width), BF16),
        scratch_shapes=[pltpu.VMEM((RET_HEADS, HEAD_DIM, HEAD_DIM), F32)],
        compiler_params=_cparams("arbitrary"),
        name="retention",
    )(p, p, p, p, cos_t, sin_t)


def _ssd_kernel(z_ref, x_ref, b_ref, c_ref, dt_ref, cwx_ref, cwb_ref, cwc_ref, cbx_ref, cbb_ref, cbc_ref,
                dtb_ref, alog_ref, dsk_ref, nw_ref, o_ref, s_scr):
    t = z_ref.shape[0]
    ii, jj = _tri_incl()
    causal = ii >= jj
    tri = jnp.where(causal, 1.0, 0.0).astype(F32)
    lane_lo = lax.broadcasted_iota(jnp.int32, (CHUNK, LANE), 1) < SSD_HEADDIM
    neg_a = -jnp.exp(alog_ref[...])
    s_scr[...] = jnp.zeros_like(s_scr)

    def conv(ref, w_ref, bias_ref, c):
        r = pl.ds(pl.multiple_of(c * CHUNK, CHUNK), CHUNK)
        rp = pl.ds(pl.multiple_of(jnp.maximum(c * CHUNK - 8, 0), 8), 8)
        cur = ref[r, :]
        ext = jnp.concatenate([jnp.where(c > 0, ref[rp, :], 0.0), cur], axis=0)
        w = w_ref[...]
        acc = bias_ref[...] + w[SSD_CONV - 1:SSD_CONV, :] * cur
        for lag in range(1, SSD_CONV):
            shifted = pltpu.roll(ext, lag, 0)[8:8 + CHUNK, :]
            acc = acc + w[SSD_CONV - 1 - lag:SSD_CONV - lag, :] * shifted
        return _silu(acc)

    def step(c, carry):
        r = pl.ds(pl.multiple_of(c * CHUNK, CHUNK), CHUNK)
        xs = conv(x_ref, cwx_ref, cbx_ref, c)
        bm = conv(b_ref, cwb_ref, cbb_ref, c).astype(BF16)
        cm = conv(c_ref, cwc_ref, cbc_ref, c).astype(BF16)
        xr = dt_ref[r, :] + dtb_ref[...]
        dt = jnp.maximum(xr, 0.0) + jnp.log1p(jnp.exp(-jnp.abs(xr)))
        la = dt * neg_a
        cum = jnp.dot(tri, la, precision=HIGHEST, preferred_element_type=F32)
        last = cum[CHUNK - 1:CHUNK, :]
        wj = dt * jnp.exp(last - cum)
        ecum = jnp.exp(cum)
        cum_t = cum.T
        dt_t = dt.T
        gm = jnp.where(causal, _dot_nt(cm, bm), 0.0)
        s = s_scr[...]
        cs = _dot(cm, s.astype(BF16))
        ys, xws, els = [], [], []
        for pr in range(SSD_HEADS_PER_GROUP // 2):
            xp = xs[:, pr * LANE:(pr + 1) * LANE]
            acc = None
            ecp, wjp, elp = None, None, None
            for half in range(2):
                hd = 2 * pr + half
                ci = jnp.broadcast_to(cum[:, hd:hd + 1], (CHUNK, CHUNK))
                dec = jnp.exp(jnp.minimum(ci - cum_t[hd:hd + 1, :], 0.0))
                m = gm * dec * dt_t[hd:hd + 1, :]
                sel = lane_lo if half == 0 else jnp.logical_not(lane_lo)
                y = _dot(m.astype(BF16), jnp.where(sel, xp, 0.0).astype(BF16))
                acc = y if acc is None else acc + y
                eb = jnp.broadcast_to(ecum[:, hd:hd + 1], (CHUNK, LANE))
                wb = jnp.broadcast_to(wj[:, hd:hd + 1], (CHUNK, LANE))
                lb = jnp.broadcast_to(ecum[CHUNK - 1:CHUNK, hd:hd + 1], (1, LANE))
                ecp = eb if half == 0 else jnp.where(lane_lo, ecp, eb)
                wjp = wb if half == 0 else jnp.where(lane_lo, wjp, wb)
                elp = lb if half == 0 else jnp.where(lane_lo[0:1, :], elp, lb)
            ys.append(acc + ecp * cs[:, pr * LANE:(pr + 1) * LANE])
            xws.append((wjp * xp).astype(BF16))
            els.append(elp)
        y = jnp.concatenate(ys, axis=1)
        s_scr[...] = jnp.concatenate(els, axis=1) * s + _dot_tn(bm, jnp.concatenate(xws, axis=1))
        y = (y + dsk_ref[...] * xs) * _silu(z_ref[r, :])
        o_ref[r, :] = (_rms(y) * nw_ref[...]).astype(o_ref.dtype)
        return carry

    lax.fori_loop(0, t // CHUNK, step, 0)


def _ssd(p, conv_w, conv_b, dtb, alog, dsk, nw, bsz, t):
    n = bsz * t
    gw = SSD_HEADS_PER_GROUP * SSD_HEADDIM
    st = HEAD_DIM
    z_off, x_off = 2048 // gw, 3072 // gw
    b_off, c_off = 4096 // st, 4352 // st
    dt_off = (p.shape[1] - SSD_GROUPS * LANE) // LANE
    xw = SSD_GROUPS * gw
    par = lambda shape, f: pl.BlockSpec(shape, f)
    in_specs = [
        par((t, gw), lambda b, g: (b, z_off + g)),
        par((t, gw), lambda b, g: (b, x_off + g)),
        par((t, st), lambda b, g: (b, b_off + g)),
        par((t, st), lambda b, g: (b, c_off + g)),
        par((t, LANE), lambda b, g: (b, dt_off + g)),
        par((SSD_CONV, gw), lambda b, g: (0, g)),
        par((SSD_CONV, st), lambda b, g: (0, xw // st + g)),
        par((SSD_CONV, st), lambda b, g: (0, xw // st + SSD_GROUPS + g)),
        par((1, gw), lambda b, g: (0, g)),
        par((1, st), lambda b, g: (0, xw // st + g)),
        par((1, st), lambda b, g: (0, xw // st + SSD_GROUPS + g)),
        par((1, LANE), lambda b, g: (0, g)),
        par((1, LANE), lambda b, g: (0, g)),
        par((1, gw), lambda b, g: (0, g)),
        par((1, gw), lambda b, g: (0, g)),
    ]
    return pl.pallas_call(
        _ssd_kernel,
        grid=(bsz, SSD_GROUPS),
        in_specs=in_specs,
        out_specs=pl.BlockSpec((t, gw), lambda b, g: (b, g)),
        out_shape=jax.ShapeDtypeStruct((n, SSD_GROUPS * gw), BF16),
        scratch_shapes=[pltpu.VMEM((st, gw), F32)],
        compiler_params=_cparams("arbitrary", "arbitrary"),
        name="ssd",
    )(p, p, p, p, p, conv_w, conv_w, conv_w, conv_b, conv_b, conv_b, dtb, alog, dsk, nw)


def _hgrn_kernel(q_ref, f_ref, i_ref, g_ref, lb_ref, nw_ref, o_ref, st_scr, cum_scr, *, layer):
    t = q_ref.shape[0]
    lbm = lb_ref[...]
    depth = lbm.shape[0]
    mx = lbm[0:1, :]
    for i in range(1, depth):
        mx = jnp.maximum(mx, lbm[i:i + 1, :])
    ex = [jnp.exp(lbm[i:i + 1, :] - mx) for i in range(depth)]
    den = ex[0]
    for i in range(1, depth):
        den = den + ex[i]
    sm = [e / den for e in ex]
    csum = sm[0]
    for i in range(1, layer + 1):
        csum = csum + sm[i]
    lb = jnp.maximum(csum - sm[0], 0.0)
    log_lb = jnp.log(jnp.maximum(lb, LB_FLOOR))
    l1m = jnp.log1p(-lb)
    oml = 1.0 - lb

    ii, jj = _tri_incl()
    tri = jnp.where(ii >= jj, 1.0, 0.0).astype(F32)
    lvl = jnp.where(ii > jj, 31 - lax.clz(ii ^ jj), -1)
    eye = ii == jj
    width = q_ref.shape[1]
    hcols = [slice(c0, c0 + HEAD_DIM) for c0 in range(0, width, HEAD_DIM)]
    row = lax.broadcasted_iota(jnp.int32, (CHUNK, width), 0)
    scale = HEAD_DIM ** -0.5
    nlev = int(math.log2(CHUNK))
    st_scr[...] = jnp.zeros_like(st_scr)

    def step(c, carry):
        r = pl.ds(pl.multiple_of(c * CHUNK, CHUNK), CHUNK)
        q = _silu(q_ref[r, :]) * scale
        f = f_ref[r, :]
        ls = jnp.minimum(f, 0.0) - jnp.log1p(jnp.exp(-jnp.abs(f)))
        a, b = log_lb, l1m + ls
        lf = jnp.maximum(a, b) + jnp.log1p(jnp.exp(-jnp.abs(a - b)))
        kk = oml / (1.0 + jnp.exp(f))
        vb = i_ref[r, :].astype(BF16)
        cum = jnp.dot(tri, lf, precision=HIGHEST, preferred_element_type=F32) * LOG2E
        cum_scr[...] = cum
        qb, kb = q.astype(BF16), kk.astype(BF16)
        scores = [jnp.where(eye, _dot_nt(qb[:, cs], kb[:, cs]), 0.0) for cs in hcols]
        for lv in range(nlev):
            s = 1 << lv
            if 2 * s >= 8:
                ref = jnp.concatenate(
                    [jnp.broadcast_to(cum_scr[g0 * 2 * s + s - 1:g0 * 2 * s + s, :], (2 * s, width))
                     for g0 in range(CHUNK // (2 * s))], axis=0)
            elif s == 2:
                m4 = row & 3
                ref = jnp.where(m4 == 0, pltpu.roll(cum, CHUNK - 1, 0),
                                jnp.where(m4 == 1, cum,
                                          jnp.where(m4 == 2, pltpu.roll(cum, 1, 0), pltpu.roll(cum, 2, 0))))
            else:
                ref = jnp.where((row & 1) == 1, pltpu.roll(cum, 1, 0), cum)
            dlt = cum - ref
            e = jnp.exp2(jnp.minimum(dlt, -dlt))
            qs = (q * e).astype(BF16)
            ks = (kk * e).astype(BF16)
            scores = [jnp.where(lvl == lv, _dot_nt(qs[:, cs], ks[:, cs]), sc) for cs, sc in zip(hcols, scores)]
        last = cum[CHUNK - 1:CHUNK, :]
        qe = (q * jnp.exp2(cum)).astype(BF16)
        kw = (kk * jnp.exp2(last - cum)).astype(BF16)
        elast = jnp.exp2(last)
        outs = []
        for hd, cs in enumerate(hcols):
            st = st_scr[hd]
            out = _dot(scores[hd].astype(BF16), vb[:, cs]) + _dot_nt(qe[:, cs], st.astype(BF16))
            st_scr[hd] = st * elast[:, cs] + _dot_tn(vb[:, cs], kw[:, cs])
            outs.append(_rms(out))
        o = jnp.concatenate(outs, axis=1) * nw_ref[...]
        o_ref[r, :] = (_silu(g_ref[r, :]) * o).astype(o_ref.dtype)
        return carry

    lax.fori_loop(0, t // CHUNK, step, 0)


def _hgrn(p, lbounds, nw, layer, bsz, t):
    n = bsz * t
    width = HGRN_HEADS * HEAD_DIM
    base = DT_LO // width
    blk = lambda off: pl.BlockSpec((t, width), lambda b, off=off: (b, base + off))
    return pl.pallas_call(
        functools.partial(_hgrn_kernel, layer=layer),
        grid=(bsz,),
        in_specs=[blk(0), blk(1), blk(2), blk(3),
                  pl.BlockSpec((lbounds.shape[0], width), lambda b: (0, 0)),
                  pl.BlockSpec((1, width), lambda b: (0, 0))],
        out_specs=pl.BlockSpec((t, width), lambda b: (b, 0)),
        out_shape=jax.ShapeDtypeStruct((n, width), BF16),
        scratch_shapes=[pltpu.VMEM((HGRN_HEADS, HEAD_DIM, HEAD_DIM), F32), pltpu.VMEM((CHUNK, width), F32)],
        compiler_params=_cparams("arbitrary"),
        name="hgrn2",
    )(p, p, p, p, lbounds, nw)


def _outproj_kernel(a_ref, b_ref, c_ref, h_ref, w_ref, nw_ref, wr_ref, br_ref, hout_ref, u_ref, lg_ref):
    wa, wb = a_ref.shape[1], b_ref.shape[1]
    acc = _dot(a_ref[...], w_ref[0:wa, :])
    acc = acc + _dot(b_ref[...], w_ref[wa:wa + wb, :])
    acc = acc + _dot(c_ref[...], w_ref[wa + wb:, :])
    h = h_ref[...] + acc
    hout_ref[...] = h
    u = _rms(h, nw_ref[...])
    u_hi = u.astype(BF16)
    u_lo = (u - u_hi.astype(F32)).astype(BF16)
    t2 = _dot(u_hi, wr_ref[...])
    lg_ref[...] = t2[:, :LANE] + t2[:, LANE:] + _dot(u_lo, wr_ref[:, :LANE]) + br_ref[...]
    half = u.shape[1] // 2
    bits = lax.bitcast_convert_type(u_hi.astype(F32), jnp.uint32)
    packed = (bits[:, :half] >> 16) | (bits[:, half:] & jnp.uint32(0xFFFF0000))
    _rows_to_slabs(u_ref, 0, packed, half // LANE)


def _outproj(o_a, o_b, o_c, h, w_out, nw, w_r, b_r):
    n, d = h.shape
    tm = min(512, n)
    srows = _slab_rows(d // 2)
    row = lambda i: (i, 0)
    fix = lambda i: (0, 0)
    return pl.pallas_call(
        _outproj_kernel,
        grid=(n // tm,),
        in_specs=[pl.BlockSpec((tm, o_a.shape[1]), row), pl.BlockSpec((tm, o_b.shape[1]), row),
                  pl.BlockSpec((tm, o_c.shape[1]), row), pl.BlockSpec((tm, d), row),
                  pl.BlockSpec(w_out.shape, fix), pl.BlockSpec((1, d), fix),
                  pl.BlockSpec(w_r.shape, fix), pl.BlockSpec((1, LANE), fix)],
        out_specs=[pl.BlockSpec((tm, d), row), pl.BlockSpec((tm * srows, LANE), row),
                   pl.BlockSpec((tm, LANE), row)],
        out_shape=[jax.ShapeDtypeStruct((n, d), F32), jax.ShapeDtypeStruct((n * srows, LANE), jnp.uint32),
                   jax.ShapeDtypeStruct((n, LANE), F32)],
        compiler_params=_cparams("arbitrary"),
        name="outproj",
    )(o_a, o_b, o_c, h, w_out, nw, w_r, b_r)


def _route_kernel(lg_ref, rt_ref, cnt_ref, carry_scr):
    tr = lg_ref.shape[0]

    @pl.when(pl.program_id(0) == 0)
    def _():
        carry_scr[...] = jnp.zeros_like(carry_scr)

    lg = lg_ref[...]
    lane = lax.broadcasted_iota(jnp.int32, (tr, LANE), 1)
    neg = -jnp.inf
    big = jnp.int32(LANE)

    def first_max(vals):
        m = jnp.max(vals, axis=-1, keepdims=True)
        idx = jnp.min(jnp.where(vals == m, lane, big), axis=-1, keepdims=True)
        return m, idx

    gl = jnp.where(lane < N_GROUPS, lg, neg)
    gmax, gidx = first_max(gl)
    gate = 1.0 / jnp.sum(jnp.exp(gl - gmax), axis=-1, keepdims=True)
    lo = N_GROUPS + EXPERTS_PER_GROUP * gidx
    el = jnp.where((lane >= lo) & (lane < lo + EXPERTS_PER_GROUP), lg, neg)
    m1, i1 = first_max(el)
    m2, i2 = first_max(jnp.where(lane == i1, neg, el))
    e21 = jnp.exp(m2 - m1)
    w1 = gate * (1.0 / (1.0 + e21))
    w2 = gate * (e21 / (1.0 + e21))

    oh1 = lane == i1
    oh2 = lane == i2
    m = jnp.where(oh1 | oh2, 1.0, 0.0)
    ti = lax.broadcasted_iota(jnp.int32, (tr, tr), 0)
    tj = lax.broadcasted_iota(jnp.int32, (tr, tr), 1)
    before = _dot(jnp.where(ti > tj, 1.0, 0.0).astype(BF16), m.astype(BF16)) + carry_scr[0:1, :]
    r1 = jnp.sum(jnp.where(oh1, before, 0.0), axis=-1, keepdims=True)
    r2 = jnp.sum(jnp.where(oh2, before, 0.0), axis=-1, keepdims=True)
    total = carry_scr[0:1, :] + jnp.sum(m, axis=0, keepdims=True)
    carry_scr[...] = jnp.broadcast_to(total, carry_scr.shape)
    cnt_ref[...] = jnp.broadcast_to(total, cnt_ref.shape)

    e1 = (i1 - N_GROUPS).astype(F32)
    e2 = (i2 - N_GROUPS).astype(F32)
    out = jnp.zeros((tr, LANE), F32)
    for pos, val in enumerate((e1, e2, w1, w2, r1, r2)):
        out = jnp.where(lane == pos, val, out)
    rt_ref[...] = out


def _route(logits):
    n = logits.shape[0]
    tr = min(ROUTE_TILE, n)
    return pl.pallas_call(
        _route_kernel,
        grid=(n // tr,),
        in_specs=[pl.BlockSpec((tr, LANE), lambda i: (i, 0))],
        out_specs=[pl.BlockSpec((tr, LANE), lambda i: (i, 0)), pl.BlockSpec((8, LANE), lambda i: (0, 0))],
        out_shape=[jax.ShapeDtypeStruct((n, LANE), F32), jax.ShapeDtypeStruct((8, LANE), F32)],
        scratch_shapes=[pltpu.VMEM((8, LANE), F32)],
        compiler_params=_cparams("arbitrary"),
        name="route",
    )(logits)


def _ffn_kernel(blk_e_ref, nv_ref, code_ref, u_hbm, wg_ref, wu_ref, wd_ref, o_hbm,
                xbuf, ybuf, wgb, wub, wdb, gsem, ssem, *, tb, xrows, yrows):
    b = pl.program_id(0)
    n_tok = u_hbm.shape[0] // xrows
    slot = b % 2
    prev = jnp.maximum(b - 1, 0)
    nv = nv_ref[b]
    nv_prev = jnp.where(b > 0, nv_ref[prev], 0)

    def slab(ref, srows, idx, count=1):
        return ref.at[pl.ds(pl.multiple_of(idx * srows, srows), count * srows)]

    def gather_row(blk, sl, r, lane):
        tok = code_ref[blk * tb + r] & (n_tok - 1)
        pltpu.make_async_copy(slab(u_hbm, xrows, tok), slab(xbuf, xrows, sl * tb + r),
                              gsem.at[sl]).start(priority=lane % 2)

    def scatter_row(blk, sl, r, dst, lane):
        pltpu.make_async_copy(slab(ybuf, yrows, sl * tb + r), slab(o_hbm, yrows, dst),
                              ssem.at[sl]).start(priority=lane % 2)

    def wait_gather(sl):
        pltpu.make_async_copy(slab(u_hbm, xrows, 0, tb), slab(xbuf, xrows, sl * tb, tb), gsem.at[sl]).wait()

    def wait_scatter(sl):
        pltpu.make_async_copy(slab(ybuf, yrows, sl * tb, tb), slab(o_hbm, yrows, 0, tb), ssem.at[sl]).wait()

    def row_loop(fn):
        def grp(i, c):
            for k in range(DMA_UNROLL):
                fn(i * DMA_UNROLL + k, k)
            return c
        lax.fori_loop(0, tb // DMA_UNROLL, grp, 0)

    @pl.when(b == 0)
    def _():
        ybuf[...] = jnp.zeros_like(ybuf)
        row_loop(lambda r, k: gather_row(0, 0, r, k))

    @pl.when((nv > 0) & ((b == 0) | (blk_e_ref[b] != blk_e_ref[prev])))
    def _():
        wgb[...] = wg_ref[...].astype(BF16)
        wub[...] = wu_ref[...].astype(BF16)
        wdb[...] = wd_ref[...].astype(BF16)

    @pl.when(nv > 0)
    def _():
        wait_gather(slot)
        words = _slabs_to_rows(xbuf, slot * tb * xrows, tb, xrows)
        x = jnp.concatenate(
            [lax.bitcast_convert_type(words << 16, F32).astype(BF16),
             lax.bitcast_convert_type(words & jnp.uint32(0xFFFF0000), F32).astype(BF16)], axis=1)
        has_prev = b > 0
        for r in range(tb):
            gather_row(b + 1, 1 - slot, r, r)
            dst = jnp.where(has_prev, code_ref[prev * tb + r], 2 * n_tok + r)
            scatter_row(prev, 1 - slot, r, dst, r)
        hdn = _silu(_dot(x, wgb[...])) * _dot(x, wub[...])
        y = _dot(hdn.astype(BF16), wdb[...])
        _rows_to_slabs(ybuf, slot * tb * yrows, y, yrows)
        wait_scatter(1 - slot)

    @pl.when((nv == 0) & (nv_prev > 0))
    def _():
        wait_gather(slot)
        row_loop(lambda r, k: scatter_row(prev, 1 - slot, r, code_ref[prev * tb + r], k))
        wait_scatter(1 - slot)


def _ffn(blk_e, nv, code, u2d, wg, wu, wd, layer):
    d, de = wg.shape[2], wg.shape[3]
    xrows, yrows = _slab_rows(d // 2), _slab_rows(d)
    n = u2d.shape[0] // xrows
    assert n & (n - 1) == 0, "token index is taken from the slot code by masking"
    nblk = blk_e.shape[0]
    tb = MOE_BLOCK
    grid_spec = pltpu.PrefetchScalarGridSpec(
        num_scalar_prefetch=3,
        grid=(nblk,),
        in_specs=[pl.BlockSpec(memory_space=pl.ANY),
                  pl.BlockSpec((None, None, d, de), lambda b, be, nv, cd: (layer, be[b], 0, 0)),
                  pl.BlockSpec((None, None, d, de), lambda b, be, nv, cd: (layer, be[b], 0, 0)),
                  pl.BlockSpec((None, None, de, d), lambda b, be, nv, cd: (layer, be[b], 0, 0))],
        out_specs=pl.BlockSpec(memory_space=pl.ANY),
        scratch_shapes=[pltpu.VMEM((2 * tb * xrows, LANE), jnp.uint32), pltpu.VMEM((2 * tb * yrows, LANE), F32),
                        pltpu.VMEM((d, de), BF16), pltpu.VMEM((d, de), BF16), pltpu.VMEM((de, d), BF16),
                        pltpu.SemaphoreType.DMA((2,)), pltpu.SemaphoreType.DMA((2,))],
    )
    return pl.pallas_call(
        functools.partial(_ffn_kernel, tb=tb, xrows=xrows, yrows=yrows),
        grid_spec=grid_spec,
        out_shape=jax.ShapeDtypeStruct(((2 * n + tb) * yrows, LANE), F32),
        compiler_params=_cparams("arbitrary"),
        name="moe_ffn",
    )(blk_e, nv, code, u2d, wg, wu, wd)


def _dispatch_tables(route, counts, n):
    tb = MOE_BLOCK
    e = route[:, 0:2].astype(jnp.int32)
    rank = route[:, 4:6].astype(jnp.int32)
    cnt = counts[0, N_GROUPS:N_GROUPS + N_EXPERTS].astype(jnp.int32)
    padded = ((cnt + tb - 1) // tb) * tb
    pends = jnp.cumsum(padded)
    pstarts = pends - padded
    start_of = jnp.sum(jnp.where(e[..., None] == jnp.arange(N_EXPERTS, dtype=jnp.int32), pstarts, 0), axis=-1)
    dest = (start_of + rank).reshape(-1)
    p_total = 2 * n + N_EXPERTS * tb
    nblk = p_total // tb
    slot_id = jnp.arange(2 * n, dtype=jnp.int32)
    code = (2 * n + jnp.arange(p_total, dtype=jnp.int32) % tb).at[dest].set(
        (slot_id & 1) * n + (slot_id >> 1), unique_indices=True, mode="promise_in_bounds")
    bstart = jnp.arange(nblk, dtype=jnp.int32) * tb
    blk_e = jnp.minimum(jnp.sum((bstart[:, None] >= pends[None, :]).astype(jnp.int32), axis=1), N_EXPERTS - 1)
    nv = jnp.clip(cnt[blk_e] - (bstart - pstarts[blk_e]), 0, tb)
    nv = jnp.where(bstart < pends[-1], nv, 0).astype(jnp.int32)
    return blk_e, nv, code


def _pad_lanes(v, width):
    return jnp.pad(v, ((0, 0), (0, width - v.shape[1])))


def kernel(x, attn_norm_w, w_in, ssd_conv_w, ssd_conv_b, ssd_dt_bias, ssd_a_log, ssd_d, ssd_norm_w, hgrn_lower_bounds, hgrn_norm_w, w_out, ffn_norm_w, router_group_w, router_group_b, router_expert_w, router_expert_b, expert_w_gate, expert_w_up, expert_w_down, final_norm_w):
    bsz, t, d = x.shape
    n = bsz * t
    depth = w_in.shape[0]

    half = HEAD_DIM // 2
    inv = 1.0 / (ROPE_THETA ** (jnp.arange(half, dtype=F32) / half))
    ang = jnp.arange(t, dtype=F32)[:, None] * inv[None, :]
    cos_t = jnp.concatenate([jnp.cos(ang), jnp.cos(ang)], axis=1)
    sin_t = jnp.concatenate([-jnp.sin(ang), jnp.sin(ang)], axis=1)

    def group_lanes(v):
        v = v.reshape(SSD_GROUPS, SSD_HEADS_PER_GROUP)
        return _pad_lanes(v, LANE).reshape(1, SSD_GROUPS * LANE)

    h = x.reshape(n, d)
    w_in_b = _wprep(jnp.swapaxes(w_in, 1, 2))
    moe = None
    for l in range(depth):
        if moe is None:
            (u,) = _norm_pass(h, attn_norm_w[l][None, :], None, BF16, False)
        else:
            u, h = _norm_pass(h, attn_norm_w[l][None, :], moe, BF16, True)
        p = _inproj(u, w_in_b, l)

        o_ret = _retention(p, cos_t, sin_t, bsz, t)
        o_ssd = _ssd(p, ssd_conv_w[l], ssd_conv_b[l][None, :], group_lanes(ssd_dt_bias[l]),
                     group_lanes(ssd_a_log[l]), jnp.repeat(ssd_d[l], SSD_HEADDIM)[None, :],
                     ssd_norm_w[l][None, :], bsz, t)
        o_hgrn = _hgrn(p, hgrn_lower_bounds, hgrn_norm_w[l][None, :], l, bsz, t)

        w_r = _pad_lanes(jnp.concatenate([router_group_w[l], router_expert_w[l]], axis=1), LANE)
        w_r_hi = w_r.astype(BF16)
        w_r = jnp.concatenate([w_r_hi, (w_r - w_r_hi.astype(F32)).astype(BF16)], axis=1)
        b_r = _pad_lanes(jnp.concatenate([router_group_b[l], router_expert_b[l]])[None, :], LANE)
        h, u2d, logits = _outproj(o_ret, o_ssd, o_hgrn, h, w_out[l].astype(BF16), ffn_norm_w[l][None, :], w_r, b_r)
        route, counts = _route(logits)
        blk_e, nv, code = _dispatch_tables(route, counts, n)
        o2d = _ffn(blk_e, nv, code, u2d, expert_w_gate, expert_w_up, expert_w_down, l)
        moe = (o2d, route)

    (out,) = _norm_pass(h, final_norm_w[None, :], moe, F32, False)
    return out.reshape(bsz, t, d)
```

```python
import functools
import math

import jax
import jax.numpy as jnp
from jax import lax
from jax.experimental import pallas as pl
from jax.experimental.pallas import tpu as pltpu

F32 = jnp.float32
BF16 = jnp.bfloat16
HIGHEST = lax.Precision.HIGHEST

V7X_VMEM_BYTES = 64 * 1024 * 1024
VMEM_LIMIT = V7X_VMEM_BYTES - 8 * 1024 * 1024
LANE = 128

EPS = 1e-6
LOG2E = 1.4426950408889634
LB_FLOOR = 1e-30
ROPE_THETA = 10000.0

RET_HEADS = 4
HEAD_DIM = 128
SSD_HEADDIM = 64
SSD_GROUPS = 2
SSD_HEADS_PER_GROUP = 8
SSD_CONV = 4
HGRN_HEADS = 4
CHUNK = 128
N_GROUPS = 4
EXPERTS_PER_GROUP = 8
N_EXPERTS = N_GROUPS * EXPERTS_PER_GROUP
MOE_BLOCK = 256
DMA_UNROLL = 8
ROUTE_TILE = 512
DT_LO = 4608
DT_HI = DT_LO + SSD_GROUPS * SSD_HEADS_PER_GROUP


def _cparams(*sem):
    return pltpu.CompilerParams(dimension_semantics=sem, vmem_limit_bytes=VMEM_LIMIT)


def _rms(x, w=None):
    y = x * lax.rsqrt(jnp.mean(x * x, axis=-1, keepdims=True) + EPS)
    return y if w is None else y * w


def _sigmoid(x):
    return 1.0 / (1.0 + jnp.exp(-x))


def _silu(x):
    return x * _sigmoid(x)


def _dot(a, b):
    return jnp.dot(a, b, preferred_element_type=F32)


def _dot_nt(a, b):
    return lax.dot_general(a, b, (((1,), (1,)), ((), ())), preferred_element_type=F32)


def _dot_tn(a, b):
    return lax.dot_general(a, b, (((0,), (0,)), ((), ())), preferred_element_type=F32)


def _tri_incl():
    ii = lax.broadcasted_iota(jnp.int32, (CHUNK, CHUNK), 0)
    jj = lax.broadcasted_iota(jnp.int32, (CHUNK, CHUNK), 1)
    return ii, jj


def _slab_rows(d):
    return d // LANE


def _pack_bf16_pairs(v):
    half = v.shape[1] // 2
    bits = lax.bitcast_convert_type(v.astype(BF16).astype(F32), jnp.uint32)
    return (bits[:, :half] >> 16) | (bits[:, half:] & jnp.uint32(0xFFFF0000))


def _unpack_bf16_pairs(words):
    return jnp.concatenate([lax.bitcast_convert_type(words << 16, F32),
                            lax.bitcast_convert_type(words & jnp.uint32(0xFFFF0000), F32)], axis=1)


def _slabs_to_rows(ref2d, base, rows, srows):
    return jnp.concatenate([ref2d[pl.ds(base + c, rows, stride=srows), :] for c in range(srows)], axis=1)


def _rows_to_slabs(ref2d, base, val, srows):
    rows = val.shape[0]
    for c in range(srows):
        ref2d[pl.ds(base + c, rows, stride=srows), :] = val[:, c * LANE:(c + 1) * LANE]


WPREP_ROWS = 256


def _wprep_kernel(wt_hbm, out_ref, buf, sem):
    l, j = pl.program_id(0), pl.program_id(1)
    nj = pl.num_programs(1)
    step = l * nj + j
    slot = step % 2
    n_lo = DT_LO // WPREP_ROWS
    n_main = n_lo + (wt_hbm.shape[1] - DT_HI) // WPREP_ROWS

    def block_copy(st, sl):
        jj = st % nj
        src = jnp.where(jj < n_lo, jj * WPREP_ROWS,
                        jnp.where(jj < n_main, DT_HI + (jj - n_lo) * WPREP_ROWS, DT_LO))
        return pltpu.make_async_copy(wt_hbm.at[st // nj, pl.ds(pl.multiple_of(src, 8), WPREP_ROWS), :],
                                     buf.at[sl], sem.at[sl])

    @pl.when(step == 0)
    def _():
        block_copy(0, 0).start()

    @pl.when(step + 1 < pl.num_programs(0) * nj)
    def _():
        block_copy(step + 1, 1 - slot).start()

    block_copy(step, slot).wait()

    @pl.when(j < n_main)
    def _():
        out_ref[...] = buf[slot].astype(BF16)

    @pl.when(j >= n_main)
    def _():
        pad = jnp.zeros((LANE - SSD_HEADS_PER_GROUP, out_ref.shape[1]), F32)
        parts = []
        for g in range(SSD_GROUPS):
            parts += [buf[slot, g * SSD_HEADS_PER_GROUP:(g + 1) * SSD_HEADS_PER_GROUP, :], pad]
        out_ref[...] = jnp.concatenate(parts, axis=0).astype(BF16)


def _wprep(wt):
    depth, nin, d = wt.shape
    nout = nin - (DT_HI - DT_LO) + SSD_GROUPS * LANE
    assert DT_LO % WPREP_ROWS == 0 and (nin - DT_HI) % WPREP_ROWS == 0 and SSD_GROUPS * LANE == WPREP_ROWS
    return pl.pallas_call(
        _wprep_kernel,
        grid=(depth, nout // WPREP_ROWS),
        in_specs=[pl.BlockSpec(memory_space=pl.ANY)],
        out_specs=pl.BlockSpec((None, WPREP_ROWS, d), lambda l, j: (l, j, 0)),
        out_shape=jax.ShapeDtypeStruct((depth, nout, d), BF16),
        scratch_shapes=[pltpu.VMEM((2, WPREP_ROWS, d), F32), pltpu.SemaphoreType.DMA((2,))],
        compiler_params=_cparams("arbitrary", "arbitrary"),
        name="wprep",
    )(wt)


def _norm_kernel(*refs, combine, write_h):
    if combine:
        h_ref, o0_ref, o1_ref, rt_ref, nw_ref = refs[:5]
        outs = refs[5:]
        rows = h_ref.shape[0]
        srows = _slab_rows(h_ref.shape[1] // 2)
        rt = rt_ref[...]
        h = (h_ref[...] + rt[:, 2:3] * _unpack_bf16_pairs(_slabs_to_rows(o0_ref, 0, rows, srows))
             + rt[:, 3:4] * _unpack_bf16_pairs(_slabs_to_rows(o1_ref, 0, rows, srows)))
    else:
        h_ref, nw_ref = refs[:2]
        outs = refs[2:]
        h = h_ref[...]
    u_ref = outs[0]
    u_ref[...] = _rms(h, nw_ref[...]).astype(u_ref.dtype)
    if write_h:
        outs[1][...] = h


def _norm_pass(h, nw, moe, out_dtype, write_h):
    n, d = h.shape
    combine = moe is not None
    tm = min(512, n)
    row = lambda i: (i, 0)
    in_specs = [pl.BlockSpec((tm, d), row)]
    args = [h]
    if combine:
        o2d, rt = moe
        srows = _slab_rows(d // 2)
        nt = n // tm
        in_specs += [pl.BlockSpec((tm * srows, LANE), row),
                     pl.BlockSpec((tm * srows, LANE), lambda i: (nt + i, 0)),
                     pl.BlockSpec((tm, LANE), row)]
        args += [o2d, o2d, rt]
    in_specs.append(pl.BlockSpec((1, d), lambda i: (0, 0)))
    args.append(nw)
    out_shape = [jax.ShapeDtypeStruct((n, d), out_dtype)]
    out_specs = [pl.BlockSpec((tm, d), row)]
    if write_h:
        out_shape.append(jax.ShapeDtypeStruct((n, d), F32))
        out_specs.append(pl.BlockSpec((tm, d), row))
    return pl.pallas_call(
        functools.partial(_norm_kernel, combine=combine, write_h=write_h),
        grid=(n // tm,),
        in_specs=in_specs, out_specs=out_specs, out_shape=out_shape,
        compiler_params=_cparams("arbitrary"),
        name="norm_pass",
    )(*args)


def _inproj_kernel(u_ref, wt_ref, p_ref):
    p_ref[...] = _dot_nt(u_ref[...], wt_ref[...])


def _inproj(u, wt, layer):
    n, d = u.shape
    np_ = wt.shape[1]
    tm = min(1024, n)
    tn = np_ // 3
    return pl.pallas_call(
        _inproj_kernel,
        grid=(np_ // tn, n // tm),
        in_specs=[pl.BlockSpec((tm, d), lambda j, i: (i, 0)),
                  pl.BlockSpec((None, tn, d), lambda j, i: (layer, j, 0))],
        out_specs=pl.BlockSpec((tm, tn), lambda j, i: (i, j)),
        out_shape=jax.ShapeDtypeStruct((n, np_), F32),
        compiler_params=_cparams("arbitrary", "arbitrary"),
        name="inproj",
    )(u, wt)


def _ret_kernel(q_ref, k_ref, v_ref, g_ref, cos_ref, sin_ref, o_ref, s_scr):
    t = q_ref.shape[0]
    ii, jj = _tri_incl()
    causal = ii >= jj
    dist = (ii - jj).astype(F32)
    iif = ii.astype(F32)
    consts = []
    for hd in range(RET_HEADS):
        lg = math.log(1.0 - 2.0 ** (-5.0 - hd))
        dmat = jnp.where(causal, jnp.exp(jnp.where(causal, dist * lg, 0.0)), 0.0)
        ecum = jnp.exp((iif + 1.0) * lg)
        wk = jnp.exp((CHUNK - 1.0 - iif) * lg)
        consts.append((dmat, ecum, wk, math.exp(CHUNK * lg)))
    scale = HEAD_DIM ** -0.5
    s_scr[...] = jnp.zeros_like(s_scr)

    def step(c, carry):
        r = pl.ds(pl.multiple_of(c * CHUNK, CHUNK), CHUNK)
        cs, sn = cos_ref[r, :], sin_ref[r, :]
        for hd in range(RET_HEADS):
            dmat, ecum, wk, elast = consts[hd]
            cols = slice(hd * HEAD_DIM, (hd + 1) * HEAD_DIM)
            q, k = q_ref[r, cols], k_ref[r, cols]
            qr = q * cs + pltpu.roll(q, HEAD_DIM // 2, 1) * sn
            kr = (k * cs + pltpu.roll(k, HEAD_DIM // 2, 1) * sn) * scale
            vb = v_ref[r, cols].astype(BF16)
            s = s_scr[hd]
            scores = _dot_nt(qr.astype(BF16), kr.astype(BF16)) * dmat
            out = _dot(scores.astype(BF16), vb) + _dot((qr * ecum).astype(BF16), s.astype(BF16))
            s_scr[hd] = elast * s + _dot_tn((kr * wk).astype(BF16), vb)
            o_ref[r, cols] = (_silu(g_ref[r, cols]) * _rms(out)).astype(o_ref.dtype)
        return carry

    lax.fori_loop(0, t // CHUNK, step, 0, unroll=2)


def _retention(p, cos_t, sin_t, bsz, t):
    n = bsz * t
    width = RET_HEADS * HEAD_DIM
    blk = lambda off: pl.BlockSpec((t, width), lambda b, off=off: (b, off))
    tab = pl.BlockSpec((t, HEAD_DIM), lambda b: (0, 0))
    return pl.pallas_call(
        _ret_kernel,
        grid=(bsz,),
        in_specs=[blk(0), blk(1), blk(2), blk(3), tab, tab],
        out_specs=pl.BlockSpec((t, width), lambda b: (b, 0)),
        out_shape=jax.ShapeDtypeStruct((n, width), BF16),
        scratch_shapes=[pltpu.VMEM((RET_HEADS, HEAD_DIM, HEAD_DIM), F32)],
        compiler_params=_cparams("arbitrary"),
        name="retention",
    )(p, p, p, p, cos_t, sin_t)


def _ssd_kernel(z_ref, x_ref, b_ref, c_ref, dt_ref, cwx_ref, cwb_ref, cwc_ref, cbx_ref, cbb_ref, cbc_ref,
                dtb_ref, alog_ref, dsk_ref, nw_ref, o_ref, s_scr):
    t = z_ref.shape[0]
    ii, jj = _tri_incl()
    causal = ii >= jj
    tri = jnp.where(causal, 1.0, 0.0).astype(F32)
    lane_lo = lax.broadcasted_iota(jnp.int32, (CHUNK, LANE), 1) < SSD_HEADDIM
    neg_a = -jnp.exp(alog_ref[...])
    s_scr[...] = jnp.zeros_like(s_scr)

    def conv(ref, w_ref, bias_ref, c):
        r = pl.ds(pl.multiple_of(c * CHUNK, CHUNK), CHUNK)
        rp = pl.ds(pl.multiple_of(jnp.maximum(c * CHUNK - 8, 0), 8), 8)
        cur = ref[r, :]
        ext = jnp.concatenate([jnp.where(c > 0, ref[rp, :], 0.0), cur], axis=0)
        w = w_ref[...]
        acc = bias_ref[...] + w[SSD_CONV - 1:SSD_CONV, :] * cur
        for lag in range(1, SSD_CONV):
            shifted = pltpu.roll(ext, lag, 0)[8:8 + CHUNK, :]
            acc = acc + w[SSD_CONV - 1 - lag:SSD_CONV - lag, :] * shifted
        return _silu(acc)

    def step(c, carry):
        r = pl.ds(pl.multiple_of(c * CHUNK, CHUNK), CHUNK)
        xs = conv(x_ref, cwx_ref, cbx_ref, c)
        bm = conv(b_ref, cwb_ref, cbb_ref, c).astype(BF16)
        cm = conv(c_ref, cwc_ref, cbc_ref, c).astype(BF16)
        xr = dt_ref[r, :] + dtb_ref[...]
        dt = jnp.maximum(xr, 0.0) + jnp.log1p(jnp.exp(-jnp.abs(xr)))
        la = dt * neg_a
        cum = jnp.dot(tri, la, precision=HIGHEST, preferred_element_type=F32)
        last = cum[CHUNK - 1:CHUNK, :]
        wj = dt * jnp.exp(last - cum)
        ecum = jnp.exp(cum)
        cum_t = cum.T
        dt_t = dt.T
        gm = jnp.where(causal, _dot_nt(cm, bm), 0.0)
        s = s_scr[...]
        cs = _dot(cm, s.astype(BF16))
        ys, xws, els = [], [], []
        for pr in range(SSD_HEADS_PER_GROUP // 2):
            xp = xs[:, pr * LANE:(pr + 1) * LANE]
            acc = None
            ecp, wjp, elp = None, None, None
            for half in range(2):
                hd = 2 * pr + half
                ci = jnp.broadcast_to(cum[:, hd:hd + 1], (CHUNK, CHUNK))
                dec = jnp.exp(jnp.minimum(ci - cum_t[hd:hd + 1, :], 0.0))
                m = gm * dec * dt_t[hd:hd + 1, :]
                sel = lane_lo if half == 0 else jnp.logical_not(lane_lo)
                y = _dot(m.astype(BF16), jnp.where(sel, xp, 0.0).astype(BF16))
                acc = y if acc is None else acc + y
                eb = jnp.broadcast_to(ecum[:, hd:hd + 1], (CHUNK, LANE))
                wb = jnp.broadcast_to(wj[:, hd:hd + 1], (CHUNK, LANE))
                lb = jnp.broadcast_to(ecum[CHUNK - 1:CHUNK, hd:hd + 1], (1, LANE))
                ecp = eb if half == 0 else jnp.where(lane_lo, ecp, eb)
                wjp = wb if half == 0 else jnp.where(lane_lo, wjp, wb)
                elp = lb if half == 0 else jnp.where(lane_lo[0:1, :], elp, lb)
            ys.append(acc + ecp * cs[:, pr * LANE:(pr + 1) * LANE])
            xws.append((wjp * xp).astype(BF16))
            els.append(elp)
        y = jnp.concatenate(ys, axis=1)
        s_scr[...] = jnp.concatenate(els, axis=1) * s + _dot_tn(bm, jnp.concatenate(xws, axis=1))
        y = (y + dsk_ref[...] * xs) * _silu(z_ref[r, :])
        o_ref[r, :] = (_rms(y) * nw_ref[...]).astype(o_ref.dtype)
        return carry

    lax.fori_loop(0, t // CHUNK, step, 0, unroll=2)


def _ssd(p, conv_w, conv_b, dtb, alog, dsk, nw, bsz, t):
    n = bsz * t
    gw = SSD_HEADS_PER_GROUP * SSD_HEADDIM
    st = HEAD_DIM
    z_off, x_off = 2048 // gw, 3072 // gw
    b_off, c_off = 4096 // st, 4352 // st
    dt_off = (p.shape[1] - SSD_GROUPS * LANE) // LANE
    xw = SSD_GROUPS * gw
    par = lambda shape, f: pl.BlockSpec(shape, f)
    in_specs = [
        par((t, gw), lambda b, g: (b, z_off + g)),
        par((t, gw), lambda b, g: (b, x_off + g)),
        par((t, st), lambda b, g: (b, b_off + g)),
        par((t, st), lambda b, g: (b, c_off + g)),
        par((t, LANE), lambda b, g: (b, dt_off + g)),
        par((SSD_CONV, gw), lambda b, g: (0, g)),
        par((SSD_CONV, st), lambda b, g: (0, xw // st + g)),
        par((SSD_CONV, st), lambda b, g: (0, xw // st + SSD_GROUPS + g)),
        par((1, gw), lambda b, g: (0, g)),
        par((1, st), lambda b, g: (0, xw // st + g)),
        par((1, st), lambda b, g: (0, xw // st + SSD_GROUPS + g)),
        par((1, LANE), lambda b, g: (0, g)),
        par((1, LANE), lambda b, g: (0, g)),
        par((1, gw), lambda b, g: (0, g)),
        par((1, gw), lambda b, g: (0, g)),
    ]
    return pl.pallas_call(
        _ssd_kernel,
        grid=(bsz, SSD_GROUPS),
        in_specs=in_specs,
        out_specs=pl.BlockSpec((t, gw), lambda b, g: (b, g)),
        out_shape=jax.ShapeDtypeStruct((n, SSD_GROUPS * gw), BF16),
        scratch_shapes=[pltpu.VMEM((st, gw), F32)],
        compiler_params=_cparams("arbitrary", "arbitrary"),
        name="ssd",
    )(p, p, p, p, p, conv_w, conv_w, conv_w, conv_b, conv_b, conv_b, dtb, alog, dsk, nw)


def _hgrn_kernel(q_ref, f_ref, i_ref, g_ref, lb_ref, nw_ref, o_ref, st_scr, cum_scr, *, layer):
    t = q_ref.shape[0]
    lbm = lb_ref[...]
    depth = lbm.shape[0]
    mx = lbm[0:1, :]
    for i in range(1, depth):
        mx = jnp.maximum(mx, lbm[i:i + 1, :])
    ex = [jnp.exp(lbm[i:i + 1, :] - mx) for i in range(depth)]
    den = ex[0]
    for i in range(1, depth):
        den = den + ex[i]
    sm = [e / den for e in ex]
    csum = sm[0]
    for i in range(1, layer + 1):
        csum = csum + sm[i]
    lb = jnp.maximum(csum - sm[0], 0.0)
    log_lb = jnp.log(jnp.maximum(lb, LB_FLOOR))
    l1m = jnp.log1p(-lb)
    oml = 1.0 - lb

    ii, jj = _tri_incl()
    tri = jnp.where(ii >= jj, 1.0, 0.0).astype(F32)
    lvl = jnp.where(ii > jj, 31 - lax.clz(ii ^ jj), -1)
    eye = ii == jj
    width = q_ref.shape[1]
    hcols = [slice(c0, c0 + HEAD_DIM) for c0 in range(0, width, HEAD_DIM)]
    row = lax.broadcasted_iota(jnp.int32, (CHUNK, width), 0)
    scale = HEAD_DIM ** -0.5
    nlev = int(math.log2(CHUNK))
    st_scr[...] = jnp.zeros_like(st_scr)

    def step(c, carry):
        r = pl.ds(pl.multiple_of(c * CHUNK, CHUNK), CHUNK)
        q = _silu(q_ref[r, :]) * scale
        f = f_ref[r, :]
        ls = jnp.minimum(f, 0.0) - jnp.log1p(jnp.exp(-jnp.abs(f)))
        a, b = log_lb, l1m + ls
        lf = jnp.maximum(a, b) + jnp.log1p(jnp.exp(-jnp.abs(a - b)))
        kk = oml / (1.0 + jnp.exp(f))
        vb = i_ref[r, :].astype(BF16)
        cum = jnp.dot(tri, lf, precision=HIGHEST, preferred_element_type=F32) * LOG2E
        cum_scr[...] = cum
        qb, kb = q.astype(BF16), kk.astype(BF16)
        scores = [jnp.where(eye, _dot_nt(qb[:, cs], kb[:, cs]), 0.0) for cs in hcols]
        for lv in range(nlev):
            s = 1 << lv
            if 2 * s >= 8:
                ref = jnp.concatenate(
                    [jnp.broadcast_to(cum_scr[g0 * 2 * s + s - 1:g0 * 2 * s + s, :], (2 * s, width))
                     for g0 in range(CHUNK // (2 * s))], axis=0)
            elif s == 2:
                m4 = row & 3
                ref = jnp.where(m4 == 0, pltpu.roll(cum, CHUNK - 1, 0),
                                jnp.where(m4 == 1, cum,
                                          jnp.where(m4 == 2, pltpu.roll(cum, 1, 0), pltpu.roll(cum, 2, 0))))
            else:
                ref = jnp.where((row & 1) == 1, pltpu.roll(cum, 1, 0), cum)
            dlt = cum - ref
            e = jnp.exp2(jnp.minimum(dlt, -dlt))
            qs = (q * e).astype(BF16)
            ks = (kk * e).astype(BF16)
            scores = [jnp.where(lvl == lv, _dot_nt(qs[:, cs], ks[:, cs]), sc) for cs, sc in zip(hcols, scores)]
        last = cum[CHUNK - 1:CHUNK, :]
        qe = (q * jnp.exp2(cum)).astype(BF16)
        kw = (kk * jnp.exp2(last - cum)).astype(BF16)
        elast = jnp.exp2(last)
        outs = []
        for hd, cs in enumerate(hcols):
            st = st_scr[hd]
            out = _dot(scores[hd].astype(BF16), vb[:, cs]) + _dot_nt(qe[:, cs], st.astype(BF16))
            st_scr[hd] = st * elast[:, cs] + _dot_tn(vb[:, cs], kw[:, cs])
            outs.append(_rms(out))
        o = jnp.concatenate(outs, axis=1) * nw_ref[...]
        o_ref[r, :] = (_silu(g_ref[r, :]) * o).astype(o_ref.dtype)
        return carry

    lax.fori_loop(0, t // CHUNK, step, 0, unroll=2)


def _hgrn(p, lbounds, nw, layer, bsz, t):
    n = bsz * t
    width = HGRN_HEADS * HEAD_DIM
    base = DT_LO // width
    blk = lambda off: pl.BlockSpec((t, width), lambda b, off=off: (b, base + off))
    return pl.pallas_call(
        functools.partial(_hgrn_kernel, layer=layer),
        grid=(bsz,),
        in_specs=[blk(0), blk(1), blk(2), blk(3),
                  pl.BlockSpec((lbounds.shape[0], width), lambda b: (0, 0)),
                  pl.BlockSpec((1, width), lambda b: (0, 0))],
        out_specs=pl.BlockSpec((t, width), lambda b: (b, 0)),
        out_shape=jax.ShapeDtypeStruct((n, width), BF16),
        scratch_shapes=[pltpu.VMEM((HGRN_HEADS, HEAD_DIM, HEAD_DIM), F32), pltpu.VMEM((CHUNK, width), F32)],
        compiler_params=_cparams("arbitrary"),
        name="hgrn2",
    )(p, p, p, p, lbounds, nw)


def _outproj_kernel(a_ref, b_ref, c_ref, h_ref, w_ref, nw_ref, wr_ref, br_ref, hout_ref, u_ref, lg_ref):
    wa, wb = a_ref.shape[1], b_ref.shape[1]
    acc = _dot(a_ref[...], w_ref[0:wa, :])
    acc = acc + _dot(b_ref[...], w_ref[wa:wa + wb, :])
    acc = acc + _dot(c_ref[...], w_ref[wa + wb:, :])
    h = h_ref[...] + acc
    hout_ref[...] = h
    u = _rms(h, nw_ref[...])
    u_hi = u.astype(BF16)
    u_lo = (u - u_hi.astype(F32)).astype(BF16)
    t2 = _dot(u_hi, wr_ref[...])
    lg_ref[...] = t2[:, :LANE] + t2[:, LANE:] + _dot(u_lo, wr_ref[:, :LANE]) + br_ref[...]
    _rows_to_slabs(u_ref, 0, _pack_bf16_pairs(u), _slab_rows(u.shape[1] // 2))


def _outproj(o_a, o_b, o_c, h, w_out, nw, w_r, b_r):
    n, d = h.shape
    tm = min(512, n)
    srows = _slab_rows(d // 2)
    row = lambda i: (i, 0)
    fix = lambda i: (0, 0)
    return pl.pallas_call(
        _outproj_kernel,
        grid=(n // tm,),
        in_specs=[pl.BlockSpec((tm, o_a.shape[1]), row), pl.BlockSpec((tm, o_b.shape[1]), row),
                  pl.BlockSpec((tm, o_c.shape[1]), row), pl.BlockSpec((tm, d), row),
                  pl.BlockSpec(w_out.shape, fix), pl.BlockSpec((1, d), fix),
                  pl.BlockSpec(w_r.shape, fix), pl.BlockSpec((1, LANE), fix)],
        out_specs=[pl.BlockSpec((tm, d), row), pl.BlockSpec((tm * srows, LANE), row),
                   pl.BlockSpec((tm, LANE), row)],
        out_shape=[jax.ShapeDtypeStruct((n, d), F32), jax.ShapeDtypeStruct((n * srows, LANE), jnp.uint32),
                   jax.ShapeDtypeStruct((n, LANE), F32)],
        compiler_params=_cparams("arbitrary"),
        name="outproj",
    )(o_a, o_b, o_c, h, w_out, nw, w_r, b_r)


def _route_kernel(lg_ref, rt_ref, cnt_ref, carry_scr):
    tr = lg_ref.shape[0]

    @pl.when(pl.program_id(0) == 0)
    def _():
        carry_scr[...] = jnp.zeros_like(carry_scr)

    lg = lg_ref[...]
    lane = lax.broadcasted_iota(jnp.int32, (tr, LANE), 1)
    neg = -jnp.inf
    big = jnp.int32(LANE)

    def first_max(vals):
        m = jnp.max(vals, axis=-1, keepdims=True)
        idx = jnp.min(jnp.where(vals == m, lane, big), axis=-1, keepdims=True)
        return m, idx

    gl = jnp.where(lane < N_GROUPS, lg, neg)
    gmax, gidx = first_max(gl)
    gate = 1.0 / jnp.sum(jnp.exp(gl - gmax), axis=-1, keepdims=True)
    lo = N_GROUPS + EXPERTS_PER_GROUP * gidx
    el = jnp.where((lane >= lo) & (lane < lo + EXPERTS_PER_GROUP), lg, neg)
    m1, i1 = first_max(el)
    m2, i2 = first_max(jnp.where(lane == i1, neg, el))
    e21 = jnp.exp(m2 - m1)
    w1 = gate * (1.0 / (1.0 + e21))
    w2 = gate * (e21 / (1.0 + e21))

    oh1 = lane == i1
    oh2 = lane == i2
    m = jnp.where(oh1 | oh2, 1.0, 0.0)
    ti = lax.broadcasted_iota(jnp.int32, (tr, tr), 0)
    tj = lax.broadcasted_iota(jnp.int32, (tr, tr), 1)
    before = _dot(jnp.where(ti > tj, 1.0, 0.0).astype(BF16), m.astype(BF16)) + carry_scr[0:1, :]
    r1 = jnp.sum(jnp.where(oh1, before, 0.0), axis=-1, keepdims=True)
    r2 = jnp.sum(jnp.where(oh2, before, 0.0), axis=-1, keepdims=True)
    total = carry_scr[0:1, :] + jnp.sum(m, axis=0, keepdims=True)
    carry_scr[...] = jnp.broadcast_to(total, carry_scr.shape)
    cnt_ref[...] = jnp.broadcast_to(total, cnt_ref.shape)

    e1 = (i1 - N_GROUPS).astype(F32)
    e2 = (i2 - N_GROUPS).astype(F32)
    out = jnp.zeros((tr, LANE), F32)
    for pos, val in enumerate((e1, e2, w1, w2, r1, r2)):
        out = jnp.where(lane == pos, val, out)
    rt_ref[...] = out


def _route(logits):
    n = logits.shape[0]
    tr = min(ROUTE_TILE, n)
    return pl.pallas_call(
        _route_kernel,
        grid=(n // tr,),
        in_specs=[pl.BlockSpec((tr, LANE), lambda i: (i, 0))],
        out_specs=[pl.BlockSpec((tr, LANE), lambda i: (i, 0)), pl.BlockSpec((8, LANE), lambda i: (0, 0))],
        out_shape=[jax.ShapeDtypeStruct((n, LANE), F32), jax.ShapeDtypeStruct((8, LANE), F32)],
        scratch_shapes=[pltpu.VMEM((8, LANE), F32)],
        compiler_params=_cparams("arbitrary"),
        name="route",
    )(logits)


def _ffn_kernel(blk_e_ref, nv_ref, code_ref, u_hbm, wg_ref, wu_ref, wd_ref, o_hbm,
                xbuf, ybuf, wgb, wub, wdb, gsem, ssem, *, tb, xrows, yrows):
    b = pl.program_id(0)
    n_tok = u_hbm.shape[0] // xrows
    slot = b % 2
    prev = jnp.maximum(b - 1, 0)
    nv = nv_ref[b]
    nv_prev = jnp.where(b > 0, nv_ref[prev], 0)

    def slab(ref, srows, idx, count=1):
        return ref.at[pl.ds(pl.multiple_of(idx * srows, srows), count * srows)]

    def gather_row(blk, sl, r, lane):
        tok = code_ref[blk * tb + r] & (n_tok - 1)
        pltpu.make_async_copy(slab(u_hbm, xrows, tok), slab(xbuf, xrows, sl * tb + r),
                              gsem.at[sl]).start(priority=lane % 2)

    def scatter_row(blk, sl, r, dst, lane):
        pltpu.make_async_copy(slab(ybuf, yrows, sl * tb + r), slab(o_hbm, yrows, dst),
                              ssem.at[sl]).start(priority=lane % 2)

    def wait_gather(sl):
        pltpu.make_async_copy(slab(u_hbm, xrows, 0, tb), slab(xbuf, xrows, sl * tb, tb), gsem.at[sl]).wait()

    def wait_scatter(sl):
        pltpu.make_async_copy(slab(ybuf, yrows, sl * tb, tb), slab(o_hbm, yrows, 0, tb), ssem.at[sl]).wait()

    def row_loop(fn):
        def grp(i, c):
            for k in range(DMA_UNROLL):
                fn(i * DMA_UNROLL + k, k)
            return c
        lax.fori_loop(0, tb // DMA_UNROLL, grp, 0)

    @pl.when(b == 0)
    def _():
        ybuf[...] = jnp.zeros_like(ybuf)
        row_loop(lambda r, k: gather_row(0, 0, r, k))

    @pl.when((nv > 0) & ((b == 0) | (blk_e_ref[b] != blk_e_ref[prev])))
    def _():
        wgb[...] = wg_ref[...].astype(BF16)
        wub[...] = wu_ref[...].astype(BF16)
        wdb[...] = wd_ref[...].astype(BF16)

    @pl.when(nv > 0)
    def _():
        wait_gather(slot)
        words = _slabs_to_rows(xbuf, slot * tb * xrows, tb, xrows)
        x = _unpack_bf16_pairs(words).astype(BF16)
        has_prev = b > 0
        for r in range(tb):
            gather_row(b + 1, 1 - slot, r, r)
            dst = jnp.where(has_prev, code_ref[prev * tb + r], 2 * n_tok + r)
            scatter_row(prev, 1 - slot, r, dst, r)
        hdn = _silu(_dot(x, wgb[...])) * _dot(x, wub[...])
        y = _dot(hdn.astype(BF16), wdb[...])
        _rows_to_slabs(ybuf, slot * tb * yrows, _pack_bf16_pairs(y), yrows)
        wait_scatter(1 - slot)

    @pl.when((nv == 0) & (nv_prev > 0))
    def _():
        wait_gather(slot)
        row_loop(lambda r, k: scatter_row(prev, 1 - slot, r, code_ref[prev * tb + r], k))
        wait_scatter(1 - slot)


def _ffn(blk_e, nv, code, u2d, wg, wu, wd, layer):
    d, de = wg.shape[2], wg.shape[3]
    xrows = yrows = _slab_rows(d // 2)
    n = u2d.shape[0] // xrows
    assert n & (n - 1) == 0, "token index is taken from the slot code by masking"
    nblk = blk_e.shape[0]
    tb = MOE_BLOCK
    grid_spec = pltpu.PrefetchScalarGridSpec(
        num_scalar_prefetch=3,
        grid=(nblk,),
        in_specs=[pl.BlockSpec(memory_space=pl.ANY),
                  pl.BlockSpec((None, None, d, de), lambda b, be, nv, cd: (layer, be[b], 0, 0)),
                  pl.BlockSpec((None, None, d, de), lambda b, be, nv, cd: (layer, be[b], 0, 0)),
                  pl.BlockSpec((None, None, de, d), lambda b, be, nv, cd: (layer, be[b], 0, 0))],
        out_specs=pl.BlockSpec(memory_space=pl.ANY),
        scratch_shapes=[pltpu.VMEM((2 * tb * xrows, LANE), jnp.uint32),
                        pltpu.VMEM((2 * tb * yrows, LANE), jnp.uint32),
                        pltpu.VMEM((d, de), BF16), pltpu.VMEM((d, de), BF16), pltpu.VMEM((de, d), BF16),
                        pltpu.SemaphoreType.DMA((2,)), pltpu.SemaphoreType.DMA((2,))],
    )
    return pl.pallas_call(
        functools.partial(_ffn_kernel, tb=tb, xrows=xrows, yrows=yrows),
        grid_spec=grid_spec,
        out_shape=jax.ShapeDtypeStruct(((2 * n + tb) * yrows, LANE), jnp.uint32),
        compiler_params=_cparams("arbitrary"),
        name="moe_ffn",
    )(blk_e, nv, code, u2d, wg, wu, wd)


def _dispatch_tables(route, counts, n):
    tb = MOE_BLOCK
    e = route[:, 0:2].astype(jnp.int32)
    rank = route[:, 4:6].astype(jnp.int32)
    cnt = counts[0, N_GROUPS:N_GROUPS + N_EXPERTS].astype(jnp.int32)
    padded = ((cnt + tb - 1) // tb) * tb
    pends = jnp.cumsum(padded)
    pstarts = pends - padded
    start_of = jnp.sum(jnp.where(e[..., None] == jnp.arange(N_EXPERTS, dtype=jnp.int32), pstarts, 0), axis=-1)
    dest = (start_of + rank).reshape(-1)
    p_total = 2 * n + N_EXPERTS * tb
    nblk = p_total // tb
    slot_id = jnp.arange(2 * n, dtype=jnp.int32)
    code = (2 * n + jnp.arange(p_total, dtype=jnp.int32) % tb).at[dest].set(
        (slot_id & 1) * n + (slot_id >> 1), unique_indices=True, mode="promise_in_bounds")
    bstart = jnp.arange(nblk, dtype=jnp.int32) * tb
    blk_e = jnp.minimum(jnp.sum((bstart[:, None] >= pends[None, :]).astype(jnp.int32), axis=1), N_EXPERTS - 1)
    nv = jnp.clip(cnt[blk_e] - (bstart - pstarts[blk_e]), 0, tb)
    nv = jnp.where(bstart < pends[-1], nv, 0).astype(jnp.int32)
    return blk_e, nv, code


def _pad_lanes(v, width):
    return jnp.pad(v, ((0, 0), (0, width - v.shape[1])))


def kernel(x, attn_norm_w, w_in, ssd_conv_w, ssd_conv_b, ssd_dt_bias, ssd_a_log, ssd_d, ssd_norm_w, hgrn_lower_bounds, hgrn_norm_w, w_out, ffn_norm_w, router_group_w, router_group_b, router_expert_w, router_expert_b, expert_w_gate, expert_w_up, expert_w_down, final_norm_w):
    bsz, t, d = x.shape
    n = bsz * t
    depth = w_in.shape[0]

    half = HEAD_DIM // 2
    inv = 1.0 / (ROPE_THETA ** (jnp.arange(half, dtype=F32) / half))
    ang = jnp.arange(t, dtype=F32)[:, None] * inv[None, :]
    cos_t = jnp.concatenate([jnp.cos(ang), jnp.cos(ang)], axis=1)
    sin_t = jnp.concatenate([-jnp.sin(ang), jnp.sin(ang)], axis=1)

    def group_lanes(v):
        v = v.reshape(SSD_GROUPS, SSD_HEADS_PER_GROUP)
        return _pad_lanes(v, LANE).reshape(1, SSD_GROUPS * LANE)

    h = x.reshape(n, d)
    w_in_b = _wprep(jnp.swapaxes(w_in, 1, 2))
    moe = None
    for l in range(depth):
        if moe is None:
            (u,) = _norm_pass(h, attn_norm_w[l][None, :], None, BF16, False)
        else:
            u, h = _norm_pass(h, attn_norm_w[l][None, :], moe, BF16, True)
        p = _inproj(u, w_in_b, l)

        o_ret = _retention(p, cos_t, sin_t, bsz, t)
        o_ssd = _ssd(p, ssd_conv_w[l], ssd_conv_b[l][None, :], group_lanes(ssd_dt_bias[l]),
                     group_lanes(ssd_a_log[l]), jnp.repeat(ssd_d[l], SSD_HEADDIM)[None, :],
                     ssd_norm_w[l][None, :], bsz, t)
        o_hgrn = _hgrn(p, hgrn_lower_bounds, hgrn_norm_w[l][None, :], l, bsz, t)

        w_r = _pad_lanes(jnp.concatenate([router_group_w[l], router_expert_w[l]], axis=1), LANE)
        w_r_hi = w_r.astype(BF16)
        w_r = jnp.concatenate([w_r_hi, (w_r - w_r_hi.astype(F32)).astype(BF16)], axis=1)
        b_r = _pad_lanes(jnp.concatenate([router_group_b[l], router_expert_b[l]])[None, :], LANE)
        h, u2d, logits = _outproj(o_ret, o_ssd, o_hgrn, h, w_out[l].astype(BF16), ffn_norm_w[l][None, :], w_r, b_r)
        route, counts = _route(logits)
        blk_e, nv, code = _dispatch_tables(route, counts, n)
        o2d = _ffn(blk_e, nv, code, u2d, expert_w_gate, expert_w_up, expert_w_down, l)
        moe = (o2d, route)

    (out,) = _norm_pass(h, final_norm_w[None, :], moe, F32, False)
    return out.reshape(bsz, t, d)
```

```python
import functools
import math

import jax
import jax.numpy as jnp
from jax import lax
from jax.experimental import pallas as pl
from jax.experimental.pallas import tpu as pltpu

F32 = jnp.float32
BF16 = jnp.bfloat16
HIGHEST = lax.Precision.HIGHEST

V7X_VMEM_BYTES = 64 * 1024 * 1024
VMEM_LIMIT = V7X_VMEM_BYTES - 8 * 1024 * 1024
LANE = 128

EPS = 1e-6
LOG2E = 1.4426950408889634
LB_FLOOR = 1e-30
ROPE_THETA = 10000.0

RET_HEADS = 4
HEAD_DIM = 128
SSD_HEADDIM = 64
SSD_GROUPS = 2
SSD_HEADS_PER_GROUP = 8
SSD_CONV = 4
HGRN_HEADS = 4
CHUNK = 128
N_GROUPS = 4
EXPERTS_PER_GROUP = 8
N_EXPERTS = N_GROUPS * EXPERTS_PER_GROUP
MOE_BLOCK = 256
DMA_UNROLL = 8
ROUTE_TILE = 512
DT_LO = 4608
DT_HI = DT_LO + SSD_GROUPS * SSD_HEADS_PER_GROUP


def _cparams(*sem):
    return pltpu.CompilerParams(dimension_semantics=sem, vmem_limit_bytes=VMEM_LIMIT)


def _rms(x, w=None):
    y = x * lax.rsqrt(jnp.mean(x * x, axis=-1, keepdims=True) + EPS)
    return y if w is None else y * w


def _sigmoid(x):
    return 1.0 / (1.0 + jnp.exp(-x))


def _silu(x):
    return x * _sigmoid(x)


def _dot(a, b):
    return jnp.dot(a, b, preferred_element_type=F32)


def _dot_nt(a, b):
    return lax.dot_general(a, b, (((1,), (1,)), ((), ())), preferred_element_type=F32)


def _dot_tn(a, b):
    return lax.dot_general(a, b, (((0,), (0,)), ((), ())), preferred_element_type=F32)


def _tri_incl():
    ii = lax.broadcasted_iota(jnp.int32, (CHUNK, CHUNK), 0)
    jj = lax.broadcasted_iota(jnp.int32, (CHUNK, CHUNK), 1)
    return ii, jj


def _slab_rows(d):
    return d // LANE


def _pack_bf16_pairs(v):
    half = v.shape[1] // 2
    bits = lax.bitcast_convert_type(v.astype(BF16).astype(F32), jnp.uint32)
    return (bits[:, :half] >> 16) | (bits[:, half:] & jnp.uint32(0xFFFF0000))


def _unpack_bf16_pairs(words):
    return jnp.concatenate([lax.bitcast_convert_type(words << 16, F32),
                            lax.bitcast_convert_type(words & jnp.uint32(0xFFFF0000), F32)], axis=1)


def _slabs_to_rows(ref2d, base, rows, srows):
    return jnp.concatenate([ref2d[pl.ds(base + c, rows, stride=srows), :] for c in range(srows)], axis=1)


def _rows_to_slabs(ref2d, base, val, srows):
    rows = val.shape[0]
    for c in range(srows):
        ref2d[pl.ds(base + c, rows, stride=srows), :] = val[:, c * LANE:(c + 1) * LANE]


WPREP_ROWS = 256


def _wprep_kernel(wt_hbm, out_ref, buf, sem):
    l, j = pl.program_id(0), pl.program_id(1)
    nj = pl.num_programs(1)
    step = l * nj + j
    slot = step % 2
    n_lo = DT_LO // WPREP_ROWS
    n_main = n_lo + (wt_hbm.shape[1] - DT_HI) // WPREP_ROWS

    def block_copy(st, sl):
        jj = st % nj
        src = jnp.where(jj < n_lo, jj * WPREP_ROWS,
                        jnp.where(jj < n_main, DT_HI + (jj - n_lo) * WPREP_ROWS, DT_LO))
        return pltpu.make_async_copy(wt_hbm.at[st // nj, pl.ds(pl.multiple_of(src, 8), WPREP_ROWS), :],
                                     buf.at[sl], sem.at[sl])

    @pl.when(step == 0)
    def _():
        block_copy(0, 0).start()

    @pl.when(step + 1 < pl.num_programs(0) * nj)
    def _():
        block_copy(step + 1, 1 - slot).start()

    block_copy(step, slot).wait()

    @pl.when(j < n_main)
    def _():
        out_ref[...] = buf[slot].astype(BF16)

    @pl.when(j >= n_main)
    def _():
        pad = jnp.zeros((LANE - SSD_HEADS_PER_GROUP, out_ref.shape[1]), F32)
        parts = []
        for g in range(SSD_GROUPS):
            parts += [buf[slot, g * SSD_HEADS_PER_GROUP:(g + 1) * SSD_HEADS_PER_GROUP, :], pad]
        out_ref[...] = jnp.concatenate(parts, axis=0).astype(BF16)


def _wprep(wt):
    depth, nin, d = wt.shape
    nout = nin - (DT_HI - DT_LO) + SSD_GROUPS * LANE
    assert DT_LO % WPREP_ROWS == 0 and (nin - DT_HI) % WPREP_ROWS == 0 and SSD_GROUPS * LANE == WPREP_ROWS
    return pl.pallas_call(
        _wprep_kernel,
        grid=(depth, nout // WPREP_ROWS),
        in_specs=[pl.BlockSpec(memory_space=pl.ANY)],
        out_specs=pl.BlockSpec((None, WPREP_ROWS, d), lambda l, j: (l, j, 0)),
        out_shape=jax.ShapeDtypeStruct((depth, nout, d), BF16),
        scratch_shapes=[pltpu.VMEM((2, WPREP_ROWS, d), F32), pltpu.SemaphoreType.DMA((2,))],
        compiler_params=_cparams("arbitrary", "arbitrary"),
        name="wprep",
    )(wt)


def _norm_kernel(*refs, combine, write_h):
    if combine:
        h_ref, o0_ref, o1_ref, rt_ref, nw_ref = refs[:5]
        outs = refs[5:]
        rows = h_ref.shape[0]
        srows = _slab_rows(h_ref.shape[1] // 2)
        rt = rt_ref[...]
        h = (h_ref[...] + rt[:, 2:3] * _unpack_bf16_pairs(_slabs_to_rows(o0_ref, 0, rows, srows))
             + rt[:, 3:4] * _unpack_bf16_pairs(_slabs_to_rows(o1_ref, 0, rows, srows)))
    else:
        h_ref, nw_ref = refs[:2]
        outs = refs[2:]
        h = h_ref[...]
    u_ref = outs[0]
    u_ref[...] = _rms(h, nw_ref[...]).astype(u_ref.dtype)
    if write_h:
        outs[1][...] = h


def _norm_pass(h, nw, moe, out_dtype, write_h):
    n, d = h.shape
    combine = moe is not None
    tm = min(512, n)
    row = lambda i: (i, 0)
    in_specs = [pl.BlockSpec((tm, d), row)]
    args = [h]
    if combine:
        o2d, rt = moe
        srows = _slab_rows(d // 2)
        nt = n // tm
        in_specs += [pl.BlockSpec((tm * srows, LANE), row),
                     pl.BlockSpec((tm * srows, LANE), lambda i: (nt + i, 0)),
                     pl.BlockSpec((tm, LANE), row)]
        args += [o2d, o2d, rt]
    in_specs.append(pl.BlockSpec((1, d), lambda i: (0, 0)))
    args.append(nw)
    out_shape = [jax.ShapeDtypeStruct((n, d), out_dtype)]
    out_specs = [pl.BlockSpec((tm, d), row)]
    if write_h:
        out_shape.append(jax.ShapeDtypeStruct((n, d), F32))
        out_specs.append(pl.BlockSpec((tm, d), row))
    return pl.pallas_call(
        functools.partial(_norm_kernel, combine=combine, write_h=write_h),
        grid=(n // tm,),
        in_specs=in_specs, out_specs=out_specs, out_shape=out_shape,
        compiler_params=_cparams("arbitrary"),
        name="norm_pass",
    )(*args)


def _inproj_kernel(u_ref, wt_ref, p_ref):
    p_ref[...] = _dot_nt(u_ref[...], wt_ref[...])


def _inproj(u, wt, layer):
    n, d = u.shape
    np_ = wt.shape[1]
    tm = min(1024, n)
    tn = np_ // 3
    return pl.pallas_call(
        _inproj_kernel,
        grid=(np_ // tn, n // tm),
        in_specs=[pl.BlockSpec((tm, d), lambda j, i: (i, 0)),
                  pl.BlockSpec((None, tn, d), lambda j, i: (layer, j, 0))],
        out_specs=pl.BlockSpec((tm, tn), lambda j, i: (i, j)),
        out_shape=jax.ShapeDtypeStruct((n, np_), F32),
        compiler_params=_cparams("arbitrary", "arbitrary"),
        name="inproj",
    )(u, wt)


def _ret_kernel(q_ref, k_ref, v_ref, g_ref, cos_ref, sin_ref, o_ref, s_scr):
    t = q_ref.shape[0]
    ii, jj = _tri_incl()
    causal = ii >= jj
    dist = (ii - jj).astype(F32)
    iif = ii.astype(F32)
    consts = []
    for hd in range(RET_HEADS):
        lg = math.log(1.0 - 2.0 ** (-5.0 - hd))
        dmat = jnp.where(causal, jnp.exp(jnp.where(causal, dist * lg, 0.0)), 0.0)
        ecum = jnp.exp((iif + 1.0) * lg)
        wk = jnp.exp((CHUNK - 1.0 - iif) * lg)
        consts.append((dmat, ecum, wk, math.exp(CHUNK * lg)))
    scale = HEAD_DIM ** -0.5
    s_scr[...] = jnp.zeros_like(s_scr)

    def step(c, carry):
        r = pl.ds(pl.multiple_of(c * CHUNK, CHUNK), CHUNK)
        cs, sn = cos_ref[r, :], sin_ref[r, :]
        for hd in range(RET_HEADS):
            dmat, ecum, wk, elast = consts[hd]
            cols = slice(hd * HEAD_DIM, (hd + 1) * HEAD_DIM)
            q, k = q_ref[r, cols], k_ref[r, cols]
            qr = q * cs + pltpu.roll(q, HEAD_DIM // 2, 1) * sn
            kr = (k * cs + pltpu.roll(k, HEAD_DIM // 2, 1) * sn) * scale
            vb = v_ref[r, cols].astype(BF16)
            s = s_scr[hd]
            scores = _dot_nt(qr.astype(BF16), kr.astype(BF16)) * dmat
            out = _dot(scores.astype(BF16), vb) + _dot((qr * ecum).astype(BF16), s.astype(BF16))
            s_scr[hd] = elast * s + _dot_tn((kr * wk).astype(BF16), vb)
            o_ref[r, cols] = (_silu(g_ref[r, cols]) * _rms(out)).astype(o_ref.dtype)
        return carry

    lax.fori_loop(0, t // CHUNK, step, 0, unroll=2)


def _retention(p, cos_t, sin_t, bsz, t):
    n = bsz * t
    width = RET_HEADS * HEAD_DIM
    blk = lambda off: pl.BlockSpec((t, width), lambda b, off=off: (b, off))
    tab = pl.BlockSpec((t, HEAD_DIM), lambda b: (0, 0))
    return pl.pallas_call(
        _ret_kernel,
        grid=(bsz,),
        in_specs=[blk(0), blk(1), blk(2), blk(3), tab, tab],
        out_specs=pl.BlockSpec((t, width), lambda b: (b, 0)),
        out_shape=jax.ShapeDtypeStruct((n, width), BF16),
        scratch_shapes=[pltpu.VMEM((RET_HEADS, HEAD_DIM, HEAD_DIM), F32)],
        compiler_params=_cparams("arbitrary"),
        name="retention",
    )(p, p, p, p, cos_t, sin_t)


def _ssd_kernel(z_ref, x_ref, b_ref, c_ref, dt_ref, cwx_ref, cwb_ref, cwc_ref, cbx_ref, cbb_ref, cbc_ref,
                dtb_ref, alog_ref, dsk_ref, nw_ref, o_ref, s_scr):
    t = z_ref.shape[0]
    ii, jj = _tri_incl()
    causal = ii >= jj
    tri = jnp.where(causal, 1.0, 0.0).astype(F32)
    lane_lo = lax.broadcasted_iota(jnp.int32, (CHUNK, LANE), 1) < SSD_HEADDIM
    neg_a = -jnp.exp(alog_ref[...])
    s_scr[...] = jnp.zeros_like(s_scr)

    def conv(ref, w_ref, bias_ref, c):
        r = pl.ds(pl.multiple_of(c * CHUNK, CHUNK), CHUNK)
        rp = pl.ds(pl.multiple_of(jnp.maximum(c * CHUNK - 8, 0), 8), 8)
        cur = ref[r, :]
        ext = jnp.concatenate([jnp.where(c > 0, ref[rp, :], 0.0), cur], axis=0)
        w = w_ref[...]
        acc = bias_ref[...] + w[SSD_CONV - 1:SSD_CONV, :] * cur
        for lag in range(1, SSD_CONV):
            shifted = pltpu.roll(ext, lag, 0)[8:8 + CHUNK, :]
            acc = acc + w[SSD_CONV - 1 - lag:SSD_CONV - lag, :] * shifted
        return _silu(acc)

    def step(c, carry):
        r = pl.ds(pl.multiple_of(c * CHUNK, CHUNK), CHUNK)
        xs = conv(x_ref, cwx_ref, cbx_ref, c)
        bm = conv(b_ref, cwb_ref, cbb_ref, c).astype(BF16)
        cm = conv(c_ref, cwc_ref, cbc_ref, c).astype(BF16)
        xr = dt_ref[r, :] + dtb_ref[...]
        dt = jnp.maximum(xr, 0.0) + jnp.log1p(jnp.exp(-jnp.abs(xr)))
        la = dt * neg_a
        cum = jnp.dot(tri, la, precision=HIGHEST, preferred_element_type=F32) * LOG2E
        last = cum[CHUNK - 1:CHUNK, :]
        wj = dt * jnp.exp2(last - cum)
        ecum = jnp.exp2(cum)
        cum_t = cum.T
        dt_t = dt.T
        gm = jnp.where(causal, _dot_nt(cm, bm), 0.0)
        s = s_scr[...]
        cs = _dot(cm, s.astype(BF16))
        ys, xws, els = [], [], []
        for pr in range(SSD_HEADS_PER_GROUP // 2):
            xp = xs[:, pr * LANE:(pr + 1) * LANE]
            acc = None
            ecp, wjp, elp = None, None, None
            for half in range(2):
                hd = 2 * pr + half
                ci = jnp.broadcast_to(cum[:, hd:hd + 1], (CHUNK, CHUNK))
                dec = jnp.exp2(jnp.minimum(ci - cum_t[hd:hd + 1, :], 0.0))
                m = gm * dec * dt_t[hd:hd + 1, :]
                sel = lane_lo if half == 0 else jnp.logical_not(lane_lo)
                y = _dot(m.astype(BF16), jnp.where(sel, xp, 0.0).astype(BF16))
                acc = y if acc is None else acc + y
                eb = jnp.broadcast_to(ecum[:, hd:hd + 1], (CHUNK, LANE))
                wb = jnp.broadcast_to(wj[:, hd:hd + 1], (CHUNK, LANE))
                lb = jnp.broadcast_to(ecum[CHUNK - 1:CHUNK, hd:hd + 1], (1, LANE))
                ecp = eb if half == 0 else jnp.where(lane_lo, ecp, eb)
                wjp = wb if half == 0 else jnp.where(lane_lo, wjp, wb)
                elp = lb if half == 0 else jnp.where(lane_lo[0:1, :], elp, lb)
            ys.append(acc + ecp * cs[:, pr * LANE:(pr + 1) * LANE])
            xws.append((wjp * xp).astype(BF16))
            els.append(elp)
        y = jnp.concatenate(ys, axis=1)
        s_scr[...] = jnp.concatenate(els, axis=1) * s + _dot_tn(bm, jnp.concatenate(xws, axis=1))
        y = (y + dsk_ref[...] * xs) * _silu(z_ref[r, :])
        o_ref[r, :] = (_rms(y) * nw_ref[...]).astype(o_ref.dtype)
        return carry

    lax.fori_loop(0, t // CHUNK, step, 0, unroll=2)


def _ssd(p, conv_w, conv_b, dtb, alog, dsk, nw, bsz, t):
    n = bsz * t
    gw = SSD_HEADS_PER_GROUP * SSD_HEADDIM
    st = HEAD_DIM
    z_off, x_off = 2048 // gw, 3072 // gw
    b_off, c_off = 4096 // st, 4352 // st
    dt_off = (p.shape[1] - SSD_GROUPS * LANE) // LANE
    xw = SSD_GROUPS * gw
    par = lambda shape, f: pl.BlockSpec(shape, f)
    in_specs = [
        par((t, gw), lambda b, g: (b, z_off + g)),
        par((t, gw), lambda b, g: (b, x_off + g)),
        par((t, st), lambda b, g: (b, b_off + g)),
        par((t, st), lambda b, g: (b, c_off + g)),
        par((t, LANE), lambda b, g: (b, dt_off + g)),
        par((SSD_CONV, gw), lambda b, g: (0, g)),
        par((SSD_CONV, st), lambda b, g: (0, xw // st + g)),
        par((SSD_CONV, st), lambda b, g: (0, xw // st + SSD_GROUPS + g)),
        par((1, gw), lambda b, g: (0, g)),
        par((1, st), lambda b, g: (0, xw // st + g)),
        par((1, st), lambda b, g: (0, xw // st + SSD_GROUPS + g)),
        par((1, LANE), lambda b, g: (0, g)),
        par((1, LANE), lambda b, g: (0, g)),
        par((1, gw), lambda b, g: (0, g)),
        par((1, gw), lambda b, g: (0, g)),
    ]
    return pl.pallas_call(
        _ssd_kernel,
        grid=(bsz, SSD_GROUPS),
        in_specs=in_specs,
        out_specs=pl.BlockSpec((t, gw), lambda b, g: (b, g)),
        out_shape=jax.ShapeDtypeStruct((n, SSD_GROUPS * gw), BF16),
        scratch_shapes=[pltpu.VMEM((st, gw), F32)],
        compiler_params=_cparams("arbitrary", "arbitrary"),
        name="ssd",
    )(p, p, p, p, p, conv_w, conv_w, conv_w, conv_b, conv_b, conv_b, dtb, alog, dsk, nw)


def _hgrn_kernel(q_ref, f_ref, i_ref, g_ref, lb_ref, nw_ref, o_ref, st_scr, cum_scr, *, layer):
    t = q_ref.shape[0]
    lbm = lb_ref[...]
    depth = lbm.shape[0]
    mx = lbm[0:1, :]
    for i in range(1, depth):
        mx = jnp.maximum(mx, lbm[i:i + 1, :])
    ex = [jnp.exp(lbm[i:i + 1, :] - mx) for i in range(depth)]
    den = ex[0]
    for i in range(1, depth):
        den = den + ex[i]
    sm = [e / den for e in ex]
    csum = sm[0]
    for i in range(1, layer + 1):
        csum = csum + sm[i]
    lb = jnp.maximum(csum - sm[0], 0.0)
    log_lb = jnp.log(jnp.maximum(lb, LB_FLOOR))
    l1m = jnp.log1p(-lb)
    oml = 1.0 - lb

    ii, jj = _tri_incl()
    tri = jnp.where(ii >= jj, 1.0, 0.0).astype(F32)
    lvl = jnp.where(ii > jj, 31 - lax.clz(ii ^ jj), -1)
    eye = ii == jj
    width = q_ref.shape[1]
    hcols = [slice(c0, c0 + HEAD_DIM) for c0 in range(0, width, HEAD_DIM)]
    row = lax.broadcasted_iota(jnp.int32, (CHUNK, width), 0)
    scale = HEAD_DIM ** -0.5
    nlev = int(math.log2(CHUNK))
    st_scr[...] = jnp.zeros_like(st_scr)

    def step(c, carry):
        r = pl.ds(pl.multiple_of(c * CHUNK, CHUNK), CHUNK)
        q = _silu(q_ref[r, :]) * scale
        f = f_ref[r, :]
        ls = jnp.minimum(f, 0.0) - jnp.log1p(jnp.exp(-jnp.abs(f)))
        a, b = log_lb, l1m + ls
        lf = jnp.maximum(a, b) + jnp.log1p(jnp.exp(-jnp.abs(a - b)))
        kk = oml / (1.0 + jnp.exp(f))
        vb = i_ref[r, :].astype(BF16)
        cum = jnp.dot(tri, lf, precision=HIGHEST, preferred_element_type=F32) * LOG2E
        cum_scr[...] = cum
        qb, kb = q.astype(BF16), kk.astype(BF16)
        scores = [jnp.where(eye, _dot_nt(qb[:, cs], kb[:, cs]), 0.0) for cs in hcols]
        for lv in range(nlev):
            s = 1 << lv
            if 2 * s >= 8:
                ref = jnp.concatenate(
                    [jnp.broadcast_to(cum_scr[g0 * 2 * s + s - 1:g0 * 2 * s + s, :], (2 * s, width))
                     for g0 in range(CHUNK // (2 * s))], axis=0)
            elif s == 2:
                m4 = row & 3
                ref = jnp.where(m4 == 0, pltpu.roll(cum, CHUNK - 1, 0),
                                jnp.where(m4 == 1, cum,
                                          jnp.where(m4 == 2, pltpu.roll(cum, 1, 0), pltpu.roll(cum, 2, 0))))
            else:
                ref = jnp.where((row & 1) == 1, pltpu.roll(cum, 1, 0), cum)
            dlt = cum - ref
            e = jnp.exp2(jnp.minimum(dlt, -dlt))
            qs = (q * e).astype(BF16)
            ks = (kk * e).astype(BF16)
            scores = [jnp.where(lvl == lv, _dot_nt(qs[:, cs], ks[:, cs]), sc) for cs, sc in zip(hcols, scores)]
        last = cum[CHUNK - 1:CHUNK, :]
        qe = (q * jnp.exp2(cum)).astype(BF16)
        kw = (kk * jnp.exp2(last - cum)).astype(BF16)
        elast = jnp.exp2(last)
        outs = []
        for hd, cs in enumerate(hcols):
            st = st_scr[hd]
            out = _dot(scores[hd].astype(BF16), vb[:, cs]) + _dot_nt(qe[:, cs], st.astype(BF16))
            st_scr[hd] = st * elast[:, cs] + _dot_tn(vb[:, cs], kw[:, cs])
            outs.append(_rms(out))
        o = jnp.concatenate(outs, axis=1) * nw_ref[...]
        o_ref[r, :] = (_silu(g_ref[r, :]) * o).astype(o_ref.dtype)
        return carry

    lax.fori_loop(0, t // CHUNK, step, 0, unroll=2)


def _hgrn(p, lbounds, nw, layer, bsz, t):
    n = bsz * t
    width = HGRN_HEADS * HEAD_DIM
    base = DT_LO // width
    blk = lambda off: pl.BlockSpec((t, width), lambda b, off=off: (b, base + off))
    return pl.pallas_call(
        functools.partial(_hgrn_kernel, layer=layer),
        grid=(bsz,),
        in_specs=[blk(0), blk(1), blk(2), blk(3),
                  pl.BlockSpec((lbounds.shape[0], width), lambda b: (0, 0)),
                  pl.BlockSpec((1, width), lambda b: (0, 0))],
        out_specs=pl.BlockSpec((t, width), lambda b: (b, 0)),
        out_shape=jax.ShapeDtypeStruct((n, width), BF16),
        scratch_shapes=[pltpu.VMEM((HGRN_HEADS, HEAD_DIM, HEAD_DIM), F32), pltpu.VMEM((CHUNK, width), F32)],
        compiler_params=_cparams("arbitrary"),
        name="hgrn2",
    )(p, p, p, p, lbounds, nw)


def _outproj_kernel(a_ref, b_ref, c_ref, h_ref, w_ref, nw_ref, wr_ref, br_ref, hout_ref, u_ref, lg_ref):
    xrows = _slab_rows(h_ref.shape[1] // 2)
    half = h_ref.shape[0] // 2
    accs = []
    for r0 in (0, half):
        rs = slice(r0, r0 + half)
        mix = jnp.concatenate([a_ref[rs, :], b_ref[rs, :], c_ref[rs, :]], axis=1)
        accs.append(_dot(mix, w_ref[...]))
    for r0, acc in zip((0, half), accs):
        rs = slice(r0, r0 + half)
        h = h_ref[rs, :] + acc
        hout_ref[rs, :] = h
        u = _rms(h, nw_ref[...])
        u_hi = u.astype(BF16)
        u_lo = (u - u_hi.astype(F32)).astype(BF16)
        t2 = _dot(u_hi, wr_ref[...])
        lg_ref[rs, :] = t2[:, :LANE] + t2[:, LANE:] + _dot(u_lo, wr_ref[:, :LANE]) + br_ref[...]
        _rows_to_slabs(u_ref, r0 * xrows, _pack_bf16_pairs(u), xrows)


def _outproj(o_a, o_b, o_c, h, w_out, nw, w_r, b_r):
    n, d = h.shape
    tm = min(512, n)
    srows = _slab_rows(d // 2)
    row = lambda i: (i, 0)
    fix = lambda i: (0, 0)
    return pl.pallas_call(
        _outproj_kernel,
        grid=(n // tm,),
        in_specs=[pl.BlockSpec((tm, o_a.shape[1]), row), pl.BlockSpec((tm, o_b.shape[1]), row),
                  pl.BlockSpec((tm, o_c.shape[1]), row), pl.BlockSpec((tm, d), row),
                  pl.BlockSpec(w_out.shape, fix), pl.BlockSpec((1, d), fix),
                  pl.BlockSpec(w_r.shape, fix), pl.BlockSpec((1, LANE), fix)],
        out_specs=[pl.BlockSpec((tm, d), row), pl.BlockSpec((tm * srows, LANE), row),
                   pl.BlockSpec((tm, LANE), row)],
        out_shape=[jax.ShapeDtypeStruct((n, d), F32), jax.ShapeDtypeStruct((n * srows, LANE), jnp.uint32),
                   jax.ShapeDtypeStruct((n, LANE), F32)],
        compiler_params=_cparams("arbitrary"),
        name="outproj",
    )(o_a, o_b, o_c, h, w_out, nw, w_r, b_r)


def _route_kernel(lg_ref, rt_ref, cnt_ref, carry_scr):
    tr = lg_ref.shape[0]

    @pl.when(pl.program_id(0) == 0)
    def _():
        carry_scr[...] = jnp.zeros_like(carry_scr)

    lg = lg_ref[...]
    lane = lax.broadcasted_iota(jnp.int32, (tr, LANE), 1)
    neg = -jnp.inf
    big = jnp.int32(LANE)

    def first_max(vals):
        m = jnp.max(vals, axis=-1, keepdims=True)
        idx = jnp.min(jnp.where(vals == m, lane, big), axis=-1, keepdims=True)
        return m, idx

    gl = jnp.where(lane < N_GROUPS, lg, neg)
    gmax, gidx = first_max(gl)
    gate = 1.0 / jnp.sum(jnp.exp(gl - gmax), axis=-1, keepdims=True)
    lo = N_GROUPS + EXPERTS_PER_GROUP * gidx
    el = jnp.where((lane >= lo) & (lane < lo + EXPERTS_PER_GROUP), lg, neg)
    m1, i1 = first_max(el)
    m2, i2 = first_max(jnp.where(lane == i1, neg, el))
    e21 = jnp.exp(m2 - m1)
    w1 = gate * (1.0 / (1.0 + e21))
    w2 = gate * (e21 / (1.0 + e21))

    oh1 = lane == i1
    oh2 = lane == i2
    m = jnp.where(oh1 | oh2, 1.0, 0.0)
    ti = lax.broadcasted_iota(jnp.int32, (tr, tr), 0)
    tj = lax.broadcasted_iota(jnp.int32, (tr, tr), 1)
    before = _dot(jnp.where(ti > tj, 1.0, 0.0).astype(BF16), m.astype(BF16)) + carry_scr[0:1, :]
    r1 = jnp.sum(jnp.where(oh1, before, 0.0), axis=-1, keepdims=True)
    r2 = jnp.sum(jnp.where(oh2, before, 0.0), axis=-1, keepdims=True)
    total = carry_scr[0:1, :] + jnp.sum(m, axis=0, keepdims=True)
    carry_scr[...] = jnp.broadcast_to(total, carry_scr.shape)
    cnt_ref[...] = jnp.broadcast_to(total, cnt_ref.shape)

    e1 = (i1 - N_GROUPS).astype(F32)
    e2 = (i2 - N_GROUPS).astype(F32)
    out = jnp.zeros((tr, LANE), F32)
    for pos, val in enumerate((e1, e2, w1, w2, r1, r2)):
        out = jnp.where(lane == pos, val, out)
    rt_ref[...] = out


def _route(logits):
    n = logits.shape[0]
    tr = min(ROUTE_TILE, n)
    return pl.pallas_call(
        _route_kernel,
        grid=(n // tr,),
        in_specs=[pl.BlockSpec((tr, LANE), lambda i: (i, 0))],
        out_specs=[pl.BlockSpec((tr, LANE), lambda i: (i, 0)), pl.BlockSpec((8, LANE), lambda i: (0, 0))],
        out_shape=[jax.ShapeDtypeStruct((n, LANE), F32), jax.ShapeDtypeStruct((8, LANE), F32)],
        scratch_shapes=[pltpu.VMEM((8, LANE), F32)],
        compiler_params=_cparams("arbitrary"),
        name="route",
    )(logits)


def _ffn_kernel(blk_e_ref, nv_ref, code_ref, u_hbm, wg_ref, wu_ref, wd_ref, o_hbm,
                xbuf, ybuf, wgb, wub, wdb, gsem, ssem, *, tb, xrows, yrows):
    b = pl.program_id(0)
    n_tok = u_hbm.shape[0] // xrows
    slot = b % 2
    prev = jnp.maximum(b - 1, 0)
    nv = nv_ref[b]
    nv_prev = jnp.where(b > 0, nv_ref[prev], 0)

    def slab(ref, srows, idx, count=1):
        return ref.at[pl.ds(pl.multiple_of(idx * srows, srows), count * srows)]

    def gather_row(blk, sl, r, lane):
        tok = code_ref[blk * tb + r] & (n_tok - 1)
        pltpu.make_async_copy(slab(u_hbm, xrows, tok), slab(xbuf, xrows, sl * tb + r),
                              gsem.at[sl]).start(priority=1)

    def scatter_row(blk, sl, r, dst, lane):
        pltpu.make_async_copy(slab(ybuf, yrows, sl * tb + r), slab(o_hbm, yrows, dst),
                              ssem.at[sl]).start(priority=lane % 2)

    def wait_gather(sl):
        pltpu.make_async_copy(slab(u_hbm, xrows, 0, tb), slab(xbuf, xrows, sl * tb, tb), gsem.at[sl]).wait()

    def wait_scatter(sl):
        pltpu.make_async_copy(slab(ybuf, yrows, sl * tb, tb), slab(o_hbm, yrows, 0, tb), ssem.at[sl]).wait()

    def row_loop(fn):
        def grp(i, c):
            for k in range(DMA_UNROLL):
                fn(i * DMA_UNROLL + k, k)
            return c
        lax.fori_loop(0, tb // DMA_UNROLL, grp, 0)

    @pl.when(b == 0)
    def _():
        ybuf[...] = jnp.zeros_like(ybuf)
        row_loop(lambda r, k: gather_row(0, 0, r, k))

    @pl.when((nv > 0) & ((b == 0) | (blk_e_ref[b] != blk_e_ref[prev])))
    def _():
        wgb[...] = wg_ref[...].astype(BF16)
        wub[...] = wu_ref[...].astype(BF16)
        wdb[...] = wd_ref[...].astype(BF16)

    @pl.when(nv > 0)
    def _():
        wait_gather(slot)
        words = _slabs_to_rows(xbuf, slot * tb * xrows, tb, xrows)
        x = _unpack_bf16_pairs(words).astype(BF16)
        has_prev = b > 0
        for r in range(tb):
            gather_row(b + 1, 1 - slot, r, r)
            dst = jnp.where(has_prev, code_ref[prev * tb + r], 2 * n_tok + r)
            scatter_row(prev, 1 - slot, r, dst, r)
        hdn = _silu(_dot(x, wgb[...])) * _dot(x, wub[...])
        y = _dot(hdn.astype(BF16), wdb[...])
        _rows_to_slabs(ybuf, slot * tb * yrows, _pack_bf16_pairs(y), yrows)
        wait_scatter(1 - slot)

    @pl.when((nv == 0) & (nv_prev > 0))
    def _():
        wait_gather(slot)
        row_loop(lambda r, k: scatter_row(prev, 1 - slot, r, code_ref[prev * tb + r], k))
        wait_scatter(1 - slot)


def _ffn(blk_e, nv, code, u2d, wg, wu, wd, layer):
    d, de = wg.shape[2], wg.shape[3]
    xrows = yrows = _slab_rows(d // 2)
    n = u2d.shape[0] // xrows
    assert n & (n - 1) == 0, "token index is taken from the slot code by masking"
    nblk = blk_e.shape[0]
    tb = MOE_BLOCK
    grid_spec = pltpu.PrefetchScalarGridSpec(
        num_scalar_prefetch=3,
        grid=(nblk,),
        in_specs=[pl.BlockSpec(memory_space=pl.ANY),
                  pl.BlockSpec((None, None, d, de), lambda b, be, nv, cd: (layer, be[b], 0, 0)),
                  pl.BlockSpec((None, None, d, de), lambda b, be, nv, cd: (layer, be[b], 0, 0)),
                  pl.BlockSpec((None, None, de, d), lambda b, be, nv, cd: (layer, be[b], 0, 0))],
        out_specs=pl.BlockSpec(memory_space=pl.ANY),
        scratch_shapes=[pltpu.VMEM((2 * tb * xrows, LANE), jnp.uint32),
                        pltpu.VMEM((2 * tb * yrows, LANE), jnp.uint32),
                        pltpu.VMEM((d, de), BF16), pltpu.VMEM((d, de), BF16), pltpu.VMEM((de, d), BF16),
                        pltpu.SemaphoreType.DMA((2,)), pltpu.SemaphoreType.DMA((2,))],
    )
    return pl.pallas_call(
        functools.partial(_ffn_kernel, tb=tb, xrows=xrows, yrows=yrows),
        grid_spec=grid_spec,
        out_shape=jax.ShapeDtypeStruct(((2 * n + tb) * yrows, LANE), jnp.uint32),
        compiler_params=_cparams("arbitrary"),
        name="moe_ffn",
    )(blk_e, nv, code, u2d, wg, wu, wd)


def _dispatch_tables(route, counts, n):
    tb = MOE_BLOCK
    e = route[:, 0:2].astype(jnp.int32)
    rank = route[:, 4:6].astype(jnp.int32)
    cnt = counts[0, N_GROUPS:N_GROUPS + N_EXPERTS].astype(jnp.int32)
    padded = ((cnt + tb - 1) // tb) * tb
    pends = jnp.cumsum(padded)
    pstarts = pends - padded
    start_of = jnp.sum(jnp.where(e[..., None] == jnp.arange(N_EXPERTS, dtype=jnp.int32), pstarts, 0), axis=-1)
    dest = (start_of + rank).reshape(-1)
    p_total = 2 * n + N_EXPERTS * tb
    nblk = p_total // tb
    slot_id = jnp.arange(2 * n, dtype=jnp.int32)
    code = (2 * n + jnp.arange(p_total, dtype=jnp.int32) % tb).at[dest].set(
        (slot_id & 1) * n + (slot_id >> 1), unique_indices=True, mode="promise_in_bounds")
    bstart = jnp.arange(nblk, dtype=jnp.int32) * tb
    blk_e = jnp.minimum(jnp.sum((bstart[:, None] >= pends[None, :]).astype(jnp.int32), axis=1), N_EXPERTS - 1)
    nv = jnp.clip(cnt[blk_e] - (bstart - pstarts[blk_e]), 0, tb)
    nv = jnp.where(bstart < pends[-1], nv, 0).astype(jnp.int32)
    return blk_e, nv, code


def _pad_lanes(v, width):
    return jnp.pad(v, ((0, 0), (0, width - v.shape[1])))


def kernel(x, attn_norm_w, w_in, ssd_conv_w, ssd_conv_b, ssd_dt_bias, ssd_a_log, ssd_d, ssd_norm_w, hgrn_lower_bounds, hgrn_norm_w, w_out, ffn_norm_w, router_group_w, router_group_b, router_expert_w, router_expert_b, expert_w_gate, expert_w_up, expert_w_down, final_norm_w):
    bsz, t, d = x.shape
    n = bsz * t
    depth = w_in.shape[0]

    half = HEAD_DIM // 2
    inv = 1.0 / (ROPE_THETA ** (jnp.arange(half, dtype=F32) / half))
    ang = jnp.arange(t, dtype=F32)[:, None] * inv[None, :]
    cos_t = jnp.concatenate([jnp.cos(ang), jnp.cos(ang)], axis=1)
    sin_t = jnp.concatenate([-jnp.sin(ang), jnp.sin(ang)], axis=1)

    def group_lanes(v):
        v = v.reshape(SSD_GROUPS, SSD_HEADS_PER_GROUP)
        return _pad_lanes(v, LANE).reshape(1, SSD_GROUPS * LANE)

    h = x.reshape(n, d)
    w_in_b = _wprep(jnp.swapaxes(w_in, 1, 2))
    moe = None
    for l in range(depth):
        if moe is None:
            (u,) = _norm_pass(h, attn_norm_w[l][None, :], None, BF16, False)
        else:
            u, h = _norm_pass(h, attn_norm_w[l][None, :], moe, BF16, True)
        p = _inproj(u, w_in_b, l)

        o_ret = _retention(p, cos_t, sin_t, bsz, t)
        o_ssd = _ssd(p, ssd_conv_w[l], ssd_conv_b[l][None, :], group_lanes(ssd_dt_bias[l]),
                     group_lanes(ssd_a_log[l]), jnp.repeat(ssd_d[l], SSD_HEADDIM)[None, :],
                     ssd_norm_w[l][None, :], bsz, t)
        o_hgrn = _hgrn(p, hgrn_lower_bounds, hgrn_norm_w[l][None, :], l, bsz, t)

        w_r = _pad_lanes(jnp.concatenate([router_group_w[l], router_expert_w[l]], axis=1), LANE)
        w_r_hi = w_r.astype(BF16)
        w_r = jnp.concatenate([w_r_hi, (w_r - w_r_hi.astype(F32)).astype(BF16)], axis=1)
        b_r = _pad_lanes(jnp.concatenate([router_group_b[l], router_expert_b[l]])[None, :], LANE)
        h, u2d, logits = _outproj(o_ret, o_ssd, o_hgrn, h, w_out[l].astype(BF16), ffn_norm_w[l][None, :], w_r, b_r)
        route, counts = _route(logits)
        blk_e, nv, code = _dispatch_tables(route, counts, n)
        o2d = _ffn(blk_e, nv, code, u2d, expert_w_gate, expert_w_up, expert_w_down, l)
        moe = (o2d, route)

    (out,) = _norm_pass(h, final_norm_w[None, :], moe, F32, False)
    return out.reshape(bsz, t, d)
```

```python
import functools
import math

import jax
import jax.numpy as jnp
from jax import lax
from jax.experimental import pallas as pl
from jax.experimental.pallas import tpu as pltpu

F32 = jnp.float32
BF16 = jnp.bfloat16
HIGHEST = lax.Precision.HIGHEST

V7X_VMEM_BYTES = 64 * 1024 * 1024
VMEM_LIMIT = V7X_VMEM_BYTES - 8 * 1024 * 1024
LANE = 128

EPS = 1e-6
LOG2E = 1.4426950408889634
LB_FLOOR = 1e-30
ROPE_THETA = 10000.0

RET_HEADS = 4
HEAD_DIM = 128
SSD_HEADDIM = 64
SSD_GROUPS = 2
SSD_HEADS_PER_GROUP = 8
SSD_CONV = 4
HGRN_HEADS = 4
CHUNK = 128
N_GROUPS = 4
EXPERTS_PER_GROUP = 8
N_EXPERTS = N_GROUPS * EXPERTS_PER_GROUP
MOE_BLOCK = 256
DMA_UNROLL = 8
ROUTE_TILE = 512
DT_LO = 4608
DT_HI = DT_LO + SSD_GROUPS * SSD_HEADS_PER_GROUP


def _cparams(*sem):
    return pltpu.CompilerParams(dimension_semantics=sem, vmem_limit_bytes=VMEM_LIMIT)


def _rms(x, w=None):
    y = x * lax.rsqrt(jnp.mean(x * x, axis=-1, keepdims=True) + EPS)
    return y if w is None else y * w


def _sigmoid(x):
    return 1.0 / (1.0 + jnp.exp(-x))


def _silu(x):
    return x * _sigmoid(x)


def _dot(a, b):
    return jnp.dot(a, b, preferred_element_type=F32)


def _dot_nt(a, b):
    return lax.dot_general(a, b, (((1,), (1,)), ((), ())), preferred_element_type=F32)


def _dot_tn(a, b):
    return lax.dot_general(a, b, (((0,), (0,)), ((), ())), preferred_element_type=F32)


def _tri_incl():
    ii = lax.broadcasted_iota(jnp.int32, (CHUNK, CHUNK), 0)
    jj = lax.broadcasted_iota(jnp.int32, (CHUNK, CHUNK), 1)
    return ii, jj


def _slab_rows(d):
    return d // LANE


def _pack_bf16_pairs(v):
    half = v.shape[1] // 2
    bits = lax.bitcast_convert_type(v.astype(BF16).astype(F32), jnp.uint32)
    return (bits[:, :half] >> 16) | (bits[:, half:] & jnp.uint32(0xFFFF0000))


def _unpack_bf16_pairs(words):
    return jnp.concatenate([lax.bitcast_convert_type(words << 16, F32),
                            lax.bitcast_convert_type(words & jnp.uint32(0xFFFF0000), F32)], axis=1)


def _slabs_to_rows(ref2d, base, rows, srows):
    return jnp.concatenate([ref2d[pl.ds(base + c, rows, stride=srows), :] for c in range(srows)], axis=1)


def _rows_to_slabs(ref2d, base, val, srows):
    rows = val.shape[0]
    for c in range(srows):
        ref2d[pl.ds(base + c, rows, stride=srows), :] = val[:, c * LANE:(c + 1) * LANE]


WPREP_ROWS = 256


def _wprep_kernel(wt_hbm, out_ref, buf, sem):
    l, j = pl.program_id(0), pl.program_id(1)
    nj = pl.num_programs(1)
    step = l * nj + j
    slot = step % 2
    n_lo = DT_LO // WPREP_ROWS
    n_main = n_lo + (wt_hbm.shape[1] - DT_HI) // WPREP_ROWS

    def block_copy(st, sl):
        jj = st % nj
        src = jnp.where(jj < n_lo, jj * WPREP_ROWS,
                        jnp.where(jj < n_main, DT_HI + (jj - n_lo) * WPREP_ROWS, DT_LO))
        return pltpu.make_async_copy(wt_hbm.at[st // nj, pl.ds(pl.multiple_of(src, 8), WPREP_ROWS), :],
                                     buf.at[sl], sem.at[sl])

    @pl.when(step == 0)
    def _():
        block_copy(0, 0).start()

    @pl.when(step + 1 < pl.num_programs(0) * nj)
    def _():
        block_copy(step + 1, 1 - slot).start()

    block_copy(step, slot).wait()

    @pl.when(j < n_main)
    def _():
        out_ref[...] = buf[slot].astype(BF16)

    @pl.when(j >= n_main)
    def _():
        pad = jnp.zeros((LANE - SSD_HEADS_PER_GROUP, out_ref.shape[1]), F32)
        parts = []
        for g in range(SSD_GROUPS):
            parts += [buf[slot, g * SSD_HEADS_PER_GROUP:(g + 1) * SSD_HEADS_PER_GROUP, :], pad]
        out_ref[...] = jnp.concatenate(parts, axis=0).astype(BF16)


def _wprep(wt):
    depth, nin, d = wt.shape
    nout = nin - (DT_HI - DT_LO) + SSD_GROUPS * LANE
    assert DT_LO % WPREP_ROWS == 0 and (nin - DT_HI) % WPREP_ROWS == 0 and SSD_GROUPS * LANE == WPREP_ROWS
    return pl.pallas_call(
        _wprep_kernel,
        grid=(depth, nout // WPREP_ROWS),
        in_specs=[pl.BlockSpec(memory_space=pl.ANY)],
        out_specs=pl.BlockSpec((None, WPREP_ROWS, d), lambda l, j: (l, j, 0)),
        out_shape=jax.ShapeDtypeStruct((depth, nout, d), BF16),
        scratch_shapes=[pltpu.VMEM((2, WPREP_ROWS, d), F32), pltpu.SemaphoreType.DMA((2,))],
        compiler_params=_cparams("arbitrary", "arbitrary"),
        name="wprep",
    )(wt)


def _norm_kernel(*refs, combine, write_h):
    if combine:
        h_ref, o0_ref, o1_ref, rt_ref, nw_ref = refs[:5]
        outs = refs[5:]
        rows = h_ref.shape[0]
        srows = _slab_rows(h_ref.shape[1] // 2)
        rt = rt_ref[...]
        h = (h_ref[...] + rt[:, 2:3] * _unpack_bf16_pairs(_slabs_to_rows(o0_ref, 0, rows, srows))
             + rt[:, 3:4] * _unpack_bf16_pairs(_slabs_to_rows(o1_ref, 0, rows, srows)))
    else:
        h_ref, nw_ref = refs[:2]
        outs = refs[2:]
        h = h_ref[...]
    u_ref = outs[0]
    u_ref[...] = _rms(h, nw_ref[...]).astype(u_ref.dtype)
    if write_h:
        outs[1][...] = h


def _norm_pass(h, nw, moe, out_dtype, write_h):
    n, d = h.shape
    combine = moe is not None
    tm = min(512, n)
    row = lambda i: (i, 0)
    in_specs = [pl.BlockSpec((tm, d), row)]
    args = [h]
    if combine:
        o2d, rt = moe
        srows = _slab_rows(d // 2)
        nt = n // tm
        in_specs += [pl.BlockSpec((tm * srows, LANE), row),
                     pl.BlockSpec((tm * srows, LANE), lambda i: (nt + i, 0)),
                     pl.BlockSpec((tm, LANE), row)]
        args += [o2d, o2d, rt]
    in_specs.append(pl.BlockSpec((1, d), lambda i: (0, 0)))
    args.append(nw)
    out_shape = [jax.ShapeDtypeStruct((n, d), out_dtype)]
    out_specs = [pl.BlockSpec((tm, d), row)]
    if write_h:
        out_shape.append(jax.ShapeDtypeStruct((n, d), F32))
        out_specs.append(pl.BlockSpec((tm, d), row))
    return pl.pallas_call(
        functools.partial(_norm_kernel, combine=combine, write_h=write_h),
        grid=(n // tm,),
        in_specs=in_specs, out_specs=out_specs, out_shape=out_shape,
        compiler_params=_cparams("arbitrary"),
        name="norm_pass",
    )(*args)


def _inproj_kernel(u_ref, wt_ref, p_ref):
    p_ref[...] = _dot_nt(u_ref[...], wt_ref[...])


def _inproj(u, wt, layer):
    n, d = u.shape
    np_ = wt.shape[1]
    tm = min(1024, n)
    tn = np_ // 3
    return pl.pallas_call(
        _inproj_kernel,
        grid=(np_ // tn, n // tm),
        in_specs=[pl.BlockSpec((tm, d), lambda j, i: (i, 0)),
                  pl.BlockSpec((None, tn, d), lambda j, i: (layer, j, 0))],
        out_specs=pl.BlockSpec((tm, tn), lambda j, i: (i, j)),
        out_shape=jax.ShapeDtypeStruct((n, np_), F32),
        compiler_params=_cparams("arbitrary", "arbitrary"),
        name="inproj",
    )(u, wt)


def _ret_kernel(q_ref, k_ref, v_ref, g_ref, cos_ref, sin_ref, o_ref, s_scr):
    t = q_ref.shape[0]
    ii, jj = _tri_incl()
    causal = ii >= jj
    dist = (ii - jj).astype(F32)
    iif = ii.astype(F32)
    consts = []
    for hd in range(RET_HEADS):
        lg = math.log(1.0 - 2.0 ** (-5.0 - hd))
        dmat = jnp.where(causal, jnp.exp(jnp.where(causal, dist * lg, 0.0)), 0.0)
        ecum = jnp.exp((iif + 1.0) * lg)
        wk = jnp.exp((CHUNK - 1.0 - iif) * lg)
        consts.append((dmat, ecum, wk, math.exp(CHUNK * lg)))
    scale = HEAD_DIM ** -0.5
    s_scr[...] = jnp.zeros_like(s_scr)

    def step(c, carry):
        r = pl.ds(pl.multiple_of(c * CHUNK, CHUNK), CHUNK)
        cs, sn = cos_ref[r, :], sin_ref[r, :]
        for hd in range(RET_HEADS):
            dmat, ecum, wk, elast = consts[hd]
            cols = slice(hd * HEAD_DIM, (hd + 1) * HEAD_DIM)
            q, k = q_ref[r, cols], k_ref[r, cols]
            qr = q * cs + pltpu.roll(q, HEAD_DIM // 2, 1) * sn
            kr = (k * cs + pltpu.roll(k, HEAD_DIM // 2, 1) * sn) * scale
            vb = v_ref[r, cols].astype(BF16)
            s = s_scr[hd]
            scores = _dot_nt(qr.astype(BF16), kr.astype(BF16)) * dmat
            out = _dot(scores.astype(BF16), vb) + _dot((qr * ecum).astype(BF16), s.astype(BF16))
            s_scr[hd] = elast * s + _dot_tn((kr * wk).astype(BF16), vb)
            o_ref[r, cols] = (_silu(g_ref[r, cols]) * _rms(out)).astype(o_ref.dtype)
        return carry

    lax.fori_loop(0, t // CHUNK, step, 0, unroll=2)


def _retention(p, cos_t, sin_t, bsz, t):
    n = bsz * t
    width = RET_HEADS * HEAD_DIM
    blk = lambda off: pl.BlockSpec((t, width), lambda b, off=off: (b, off))
    tab = pl.BlockSpec((t, HEAD_DIM), lambda b: (0, 0))
    return pl.pallas_call(
        _ret_kernel,
        grid=(bsz,),
        in_specs=[blk(0), blk(1), blk(2), blk(3), tab, tab],
        out_specs=pl.BlockSpec((t, width), lambda b: (b, 0)),
        out_shape=jax.ShapeDtypeStruct((n, width), BF16),
        scratch_shapes=[pltpu.VMEM((RET_HEADS, HEAD_DIM, HEAD_DIM), F32)],
        compiler_params=_cparams("arbitrary"),
        name="retention",
    )(p, p, p, p, cos_t, sin_t)


def _ssd_kernel(z_ref, x_ref, b_ref, c_ref, dt_ref, cwx_ref, cwb_ref, cwc_ref, cbx_ref, cbb_ref, cbc_ref,
                dtb_ref, alog_ref, dsk_ref, nw_ref, o_ref, s_scr):
    t = z_ref.shape[0]
    ii, jj = _tri_incl()
    causal = ii >= jj
    tri = jnp.where(causal, 1.0, 0.0).astype(F32)
    lane_lo = lax.broadcasted_iota(jnp.int32, (CHUNK, LANE), 1) < SSD_HEADDIM
    neg_a = -jnp.exp(alog_ref[...])
    s_scr[...] = jnp.zeros_like(s_scr)

    def conv(ref, w_ref, bias_ref, c):
        r = pl.ds(pl.multiple_of(c * CHUNK, CHUNK), CHUNK)
        rp = pl.ds(pl.multiple_of(jnp.maximum(c * CHUNK - 8, 0), 8), 8)
        cur = ref[r, :]
        ext = jnp.concatenate([jnp.where(c > 0, ref[rp, :], 0.0), cur], axis=0)
        w = w_ref[...]
        acc = bias_ref[...] + w[SSD_CONV - 1:SSD_CONV, :] * cur
        for lag in range(1, SSD_CONV):
            shifted = pltpu.roll(ext, lag, 0)[8:8 + CHUNK, :]
            acc = acc + w[SSD_CONV - 1 - lag:SSD_CONV - lag, :] * shifted
        return _silu(acc)

    def step(c, carry):
        r = pl.ds(pl.multiple_of(c * CHUNK, CHUNK), CHUNK)
        xs = conv(x_ref, cwx_ref, cbx_ref, c)
        bm = conv(b_ref, cwb_ref, cbb_ref, c).astype(BF16)
        cm = conv(c_ref, cwc_ref, cbc_ref, c).astype(BF16)
        xr = dt_ref[r, :] + dtb_ref[...]
        dt = jnp.maximum(xr, 0.0) + jnp.log1p(jnp.exp(-jnp.abs(xr)))
        la = dt * neg_a
        cum = jnp.dot(tri, la, precision=HIGHEST, preferred_element_type=F32) * LOG2E
        last = cum[CHUNK - 1:CHUNK, :]
        wj = dt * jnp.exp2(last - cum)
        ecum = jnp.exp2(cum)
        cum_t = cum.T
        dt_t = dt.T
        gm = jnp.where(causal, _dot_nt(cm, bm), 0.0)
        s = s_scr[...]
        cs = _dot(cm, s.astype(BF16))
        ys, xws, els = [], [], []
        for pr in range(SSD_HEADS_PER_GROUP // 2):
            xp = xs[:, pr * LANE:(pr + 1) * LANE]
            acc = None
            ecp, wjp, elp = None, None, None
            for half in range(2):
                hd = 2 * pr + half
                ci = jnp.broadcast_to(cum[:, hd:hd + 1], (CHUNK, CHUNK))
                dec = jnp.exp2(jnp.minimum(ci - cum_t[hd:hd + 1, :], 0.0))
                m = gm * dec * dt_t[hd:hd + 1, :]
                sel = lane_lo if half == 0 else jnp.logical_not(lane_lo)
                y = _dot(m.astype(BF16), jnp.where(sel, xp, 0.0).astype(BF16))
                acc = y if acc is None else acc + y
                eb = jnp.broadcast_to(ecum[:, hd:hd + 1], (CHUNK, LANE))
                wb = jnp.broadcast_to(wj[:, hd:hd + 1], (CHUNK, LANE))
                lb = jnp.broadcast_to(ecum[CHUNK - 1:CHUNK, hd:hd + 1], (1, LANE))
                ecp = eb if half == 0 else jnp.where(lane_lo, ecp, eb)
                wjp = wb if half == 0 else jnp.where(lane_lo, wjp, wb)
                elp = lb if half == 0 else jnp.where(lane_lo[0:1, :], elp, lb)
            ys.append(acc + ecp * cs[:, pr * LANE:(pr + 1) * LANE])
            xws.append((wjp * xp).astype(BF16))
            els.append(elp)
        y = jnp.concatenate(ys, axis=1)
        s_scr[...] = jnp.concatenate(els, axis=1) * s + _dot_tn(bm, jnp.concatenate(xws, axis=1))
        y = (y + dsk_ref[...] * xs) * _silu(z_ref[r, :])
        o_ref[r, :] = (_rms(y) * nw_ref[...]).astype(o_ref.dtype)
        return carry

    lax.fori_loop(0, t // CHUNK, step, 0, unroll=2)


def _ssd(p, conv_w, conv_b, dtb, alog, dsk, nw, bsz, t):
    n = bsz * t
    gw = SSD_HEADS_PER_GROUP * SSD_HEADDIM
    st = HEAD_DIM
    z_off, x_off = 2048 // gw, 3072 // gw
    b_off, c_off = 4096 // st, 4352 // st
    dt_off = (p.shape[1] - SSD_GROUPS * LANE) // LANE
    xw = SSD_GROUPS * gw
    par = lambda shape, f: pl.BlockSpec(shape, f)
    in_specs = [
        par((t, gw), lambda b, g: (b, z_off + g)),
        par((t, gw), lambda b, g: (b, x_off + g)),
        par((t, st), lambda b, g: (b, b_off + g)),
        par((t, st), lambda b, g: (b, c_off + g)),
        par((t, LANE), lambda b, g: (b, dt_off + g)),
        par((SSD_CONV, gw), lambda b, g: (0, g)),
        par((SSD_CONV, st), lambda b, g: (0, xw // st + g)),
        par((SSD_CONV, st), lambda b, g: (0, xw // st + SSD_GROUPS + g)),
        par((1, gw), lambda b, g: (0, g)),
        par((1, st), lambda b, g: (0, xw // st + g)),
        par((1, st), lambda b, g: (0, xw // st + SSD_GROUPS + g)),
        par((1, LANE), lambda b, g: (0, g)),
        par((1, LANE), lambda b, g: (0, g)),
        par((1, gw), lambda b, g: (0, g)),
        par((1, gw), lambda b, g: (0, g)),
    ]
    return pl.pallas_call(
        _ssd_kernel,
        grid=(bsz, SSD_GROUPS),
        in_specs=in_specs,
        out_specs=pl.BlockSpec((t, gw), lambda b, g: (b, g)),
        out_shape=jax.ShapeDtypeStruct((n, SSD_GROUPS * gw), BF16),
        scratch_shapes=[pltpu.VMEM((st, gw), F32)],
        compiler_params=_cparams("arbitrary", "arbitrary"),
        name="ssd",
    )(p, p, p, p, p, conv_w, conv_w, conv_w, conv_b, conv_b, conv_b, dtb, alog, dsk, nw)


def _hgrn_kernel(q_ref, f_ref, i_ref, g_ref, lb_ref, nw_ref, o_ref, st_scr, cum_scr, *, layer):
    t = q_ref.shape[0]
    lbm = lb_ref[...]
    depth = lbm.shape[0]
    mx = lbm[0:1, :]
    for i in range(1, depth):
        mx = jnp.maximum(mx, lbm[i:i + 1, :])
    ex = [jnp.exp(lbm[i:i + 1, :] - mx) for i in range(depth)]
    den = ex[0]
    for i in range(1, depth):
        den = den + ex[i]
    sm = [e / den for e in ex]
    csum = sm[0]
    for i in range(1, layer + 1):
        csum = csum + sm[i]
    lb = jnp.maximum(csum - sm[0], 0.0)
    log_lb = jnp.log(jnp.maximum(lb, LB_FLOOR))
    l1m = jnp.log1p(-lb)
    oml = 1.0 - lb

    ii, jj = _tri_incl()
    tri = jnp.where(ii >= jj, 1.0, 0.0).astype(F32)
    lvl = jnp.where(ii > jj, 31 - lax.clz(ii ^ jj), -1)
    eye = ii == jj
    width = q_ref.shape[1]
    hcols = [slice(c0, c0 + HEAD_DIM) for c0 in range(0, width, HEAD_DIM)]
    row = lax.broadcasted_iota(jnp.int32, (CHUNK, width), 0)
    scale = HEAD_DIM ** -0.5
    nlev = int(math.log2(CHUNK))
    st_scr[...] = jnp.zeros_like(st_scr)

    def step(c, carry):
        r = pl.ds(pl.multiple_of(c * CHUNK, CHUNK), CHUNK)
        q = _silu(q_ref[r, :]) * scale
        f = f_ref[r, :]
        ls = jnp.minimum(f, 0.0) - jnp.log1p(jnp.exp(-jnp.abs(f)))
        a, b = log_lb, l1m + ls
        lf = jnp.maximum(a, b) + jnp.log1p(jnp.exp(-jnp.abs(a - b)))
        kk = oml / (1.0 + jnp.exp(f))
        vb = i_ref[r, :].astype(BF16)
        cum = jnp.dot(tri, lf, precision=HIGHEST, preferred_element_type=F32) * LOG2E
        cum_scr[...] = cum
        qb, kb = q.astype(BF16), kk.astype(BF16)
        scores = [jnp.where(eye, _dot_nt(qb[:, cs], kb[:, cs]), 0.0) for cs in hcols]
        for lv in range(nlev):
            s = 1 << lv
            if 2 * s >= 8:
                ref = jnp.concatenate(
                    [jnp.broadcast_to(cum_scr[g0 * 2 * s + s - 1:g0 * 2 * s + s, :], (2 * s, width))
                     for g0 in range(CHUNK // (2 * s))], axis=0)
            elif s == 2:
                m4 = row & 3
                ref = jnp.where(m4 == 0, pltpu.roll(cum, CHUNK - 1, 0),
                                jnp.where(m4 == 1, cum,
                                          jnp.where(m4 == 2, pltpu.roll(cum, 1, 0), pltpu.roll(cum, 2, 0))))
            else:
                ref = jnp.where((row & 1) == 1, pltpu.roll(cum, 1, 0), cum)
            dlt = cum - ref
            e = jnp.exp2(jnp.minimum(dlt, -dlt))
            qs = (q * e).astype(BF16)
            ks = (kk * e).astype(BF16)
            scores = [jnp.where(lvl == lv, _dot_nt(qs[:, cs], ks[:, cs]), sc) for cs, sc in zip(hcols, scores)]
        last = cum[CHUNK - 1:CHUNK, :]
        qe = (q * jnp.exp2(cum)).astype(BF16)
        kw = (kk * jnp.exp2(last - cum)).astype(BF16)
        elast = jnp.exp2(last)
        outs = []
        for hd, cs in enumerate(hcols):
            st = st_scr[hd]
            out = _dot(scores[hd].astype(BF16), vb[:, cs]) + _dot_nt(qe[:, cs], st.astype(BF16))
            st_scr[hd] = st * elast[:, cs] + _dot_tn(vb[:, cs], kw[:, cs])
            outs.append(_rms(out))
        o = jnp.concatenate(outs, axis=1) * nw_ref[...]
        o_ref[r, :] = (_silu(g_ref[r, :]) * o).astype(o_ref.dtype)
        return carry

    lax.fori_loop(0, t // CHUNK, step, 0, unroll=2)


def _hgrn(p, lbounds, nw, layer, bsz, t):
    n = bsz * t
    width = HGRN_HEADS * HEAD_DIM
    base = DT_LO // width
    blk = lambda off: pl.BlockSpec((t, width), lambda b, off=off: (b, base + off))
    return pl.pallas_call(
        functools.partial(_hgrn_kernel, layer=layer),
        grid=(bsz,),
        in_specs=[blk(0), blk(1), blk(2), blk(3),
                  pl.BlockSpec((lbounds.shape[0], width), lambda b: (0, 0)),
                  pl.BlockSpec((1, width), lambda b: (0, 0))],
        out_specs=pl.BlockSpec((t, width), lambda b: (b, 0)),
        out_shape=jax.ShapeDtypeStruct((n, width), BF16),
        scratch_shapes=[pltpu.VMEM((HGRN_HEADS, HEAD_DIM, HEAD_DIM), F32), pltpu.VMEM((CHUNK, width), F32)],
        compiler_params=_cparams("arbitrary"),
        name="hgrn2",
    )(p, p, p, p, lbounds, nw)


def _outproj_kernel(a_ref, b_ref, c_ref, h_ref, w_ref, nw_ref, wr_ref, br_ref, hout_ref, u_ref, lg_ref):
    xrows = _slab_rows(h_ref.shape[1] // 2)
    half = h_ref.shape[0] // 2
    accs = []
    for r0 in (0, half):
        rs = slice(r0, r0 + half)
        mix = jnp.concatenate([a_ref[rs, :], b_ref[rs, :], c_ref[rs, :]], axis=1)
        accs.append(_dot(mix, w_ref[...]))
    for r0, acc in zip((0, half), accs):
        rs = slice(r0, r0 + half)
        h = h_ref[rs, :] + acc
        hout_ref[rs, :] = h
        u = _rms(h, nw_ref[...])
        u_hi = u.astype(BF16)
        u_lo = (u - u_hi.astype(F32)).astype(BF16)
        t2 = _dot(u_hi, wr_ref[...])
        lg_ref[rs, :] = t2[:, :LANE] + t2[:, LANE:] + _dot(u_lo, wr_ref[:, :LANE]) + br_ref[...]
        _rows_to_slabs(u_ref, r0 * xrows, _pack_bf16_pairs(u), xrows)


def _outproj(o_a, o_b, o_c, h, w_out, nw, w_r, b_r):
    n, d = h.shape
    tm = min(512, n)
    srows = _slab_rows(d // 2)
    row = lambda i: (i, 0)
    fix = lambda i: (0, 0)
    return pl.pallas_call(
        _outproj_kernel,
        grid=(n // tm,),
        in_specs=[pl.BlockSpec((tm, o_a.shape[1]), row), pl.BlockSpec((tm, o_b.shape[1]), row),
                  pl.BlockSpec((tm, o_c.shape[1]), row), pl.BlockSpec((tm, d), row),
                  pl.BlockSpec(w_out.shape, fix), pl.BlockSpec((1, d), fix),
                  pl.BlockSpec(w_r.shape, fix), pl.BlockSpec((1, LANE), fix)],
        out_specs=[pl.BlockSpec((tm, d), row), pl.BlockSpec((tm * srows, LANE), row),
                   pl.BlockSpec((tm, LANE), row)],
        out_shape=[jax.ShapeDtypeStruct((n, d), F32), jax.ShapeDtypeStruct((n * srows, LANE), jnp.uint32),
                   jax.ShapeDtypeStruct((n, LANE), F32)],
        compiler_params=_cparams("arbitrary"),
        name="outproj",
    )(o_a, o_b, o_c, h, w_out, nw, w_r, b_r)


def _route_kernel(lg_ref, rt_ref, rtt_ref, cnt_ref, carry_scr):
    tr = lg_ref.shape[0]

    @pl.when(pl.program_id(0) == 0)
    def _():
        carry_scr[...] = jnp.zeros_like(carry_scr)

    lg = lg_ref[...]
    lane = lax.broadcasted_iota(jnp.int32, (tr, LANE), 1)
    neg = -jnp.inf
    big = jnp.int32(LANE)

    def first_max(vals):
        m = jnp.max(vals, axis=-1, keepdims=True)
        idx = jnp.min(jnp.where(vals == m, lane, big), axis=-1, keepdims=True)
        return m, idx

    gl = jnp.where(lane < N_GROUPS, lg, neg)
    gmax, gidx = first_max(gl)
    gate = 1.0 / jnp.sum(jnp.exp(gl - gmax), axis=-1, keepdims=True)
    lo = N_GROUPS + EXPERTS_PER_GROUP * gidx
    el = jnp.where((lane >= lo) & (lane < lo + EXPERTS_PER_GROUP), lg, neg)
    m1, i1 = first_max(el)
    m2, i2 = first_max(jnp.where(lane == i1, neg, el))
    e21 = jnp.exp(m2 - m1)
    w1 = gate * (1.0 / (1.0 + e21))
    w2 = gate * (e21 / (1.0 + e21))

    oh1 = lane == i1
    oh2 = lane == i2
    m = jnp.where(oh1 | oh2, 1.0, 0.0)
    ti = lax.broadcasted_iota(jnp.int32, (tr, tr), 0)
    tj = lax.broadcasted_iota(jnp.int32, (tr, tr), 1)
    before = _dot(jnp.where(ti > tj, 1.0, 0.0).astype(BF16), m.astype(BF16)) + carry_scr[0:1, :]
    r1 = jnp.sum(jnp.where(oh1, before, 0.0), axis=-1, keepdims=True)
    r2 = jnp.sum(jnp.where(oh2, before, 0.0), axis=-1, keepdims=True)
    total = carry_scr[0:1, :] + jnp.sum(m, axis=0, keepdims=True)
    carry_scr[...] = jnp.broadcast_to(total, carry_scr.shape)
    cnt_ref[...] = jnp.broadcast_to(total, cnt_ref.shape)

    e1 = (i1 - N_GROUPS).astype(F32)
    e2 = (i2 - N_GROUPS).astype(F32)
    out = jnp.zeros((tr, LANE), F32)
    for pos, val in enumerate((e1, e2, w1, w2, r1, r2)):
        out = jnp.where(lane == pos, val, out)
    rt_ref[...] = out
    rtt_ref[...] = out.T[0:rtt_ref.shape[0], :]


def _route(logits):
    n = logits.shape[0]
    tr = min(ROUTE_TILE, n)
    return pl.pallas_call(
        _route_kernel,
        grid=(n // tr,),
        in_specs=[pl.BlockSpec((tr, LANE), lambda i: (i, 0))],
        out_specs=[pl.BlockSpec((tr, LANE), lambda i: (i, 0)), pl.BlockSpec((8, tr), lambda i: (0, i)),
                   pl.BlockSpec((8, LANE), lambda i: (0, 0))],
        out_shape=[jax.ShapeDtypeStruct((n, LANE), F32), jax.ShapeDtypeStruct((8, n), F32),
                   jax.ShapeDtypeStruct((8, LANE), F32)],
        scratch_shapes=[pltpu.VMEM((8, LANE), F32)],
        compiler_params=_cparams("arbitrary"),
        name="route",
    )(logits)


def _ffn_kernel(blk_e_ref, nv_ref, code_ref, u_hbm, wg_ref, wu_ref, wd_ref, o_hbm,
                xbuf, ybuf, wgb, wub, wdb, gsem, ssem, *, tb, xrows, yrows):
    b = pl.program_id(0)
    n_tok = u_hbm.shape[0] // xrows
    slot = b % 2
    prev = jnp.maximum(b - 1, 0)
    nv = nv_ref[b]
    nv_prev = jnp.where(b > 0, nv_ref[prev], 0)

    def slab(ref, srows, idx, count=1):
        return ref.at[pl.ds(pl.multiple_of(idx * srows, srows), count * srows)]

    def gather_row(blk, sl, r, lane):
        tok = code_ref[blk * tb + r] & (n_tok - 1)
        pltpu.make_async_copy(slab(u_hbm, xrows, tok), slab(xbuf, xrows, sl * tb + r),
                              gsem.at[sl]).start(priority=lane % 2)

    def scatter_row(blk, sl, r, dst, lane):
        pltpu.make_async_copy(slab(ybuf, yrows, sl * tb + r), slab(o_hbm, yrows, dst),
                              ssem.at[sl]).start(priority=lane % 2)

    def wait_gather(sl):
        pltpu.make_async_copy(slab(u_hbm, xrows, 0, tb), slab(xbuf, xrows, sl * tb, tb), gsem.at[sl]).wait()

    def wait_scatter(sl):
        pltpu.make_async_copy(slab(ybuf, yrows, sl * tb, tb), slab(o_hbm, yrows, 0, tb), ssem.at[sl]).wait()

    def row_loop(fn):
        def grp(i, c):
            for k in range(DMA_UNROLL):
                fn(i * DMA_UNROLL + k, k)
            return c
        lax.fori_loop(0, tb // DMA_UNROLL, grp, 0)

    @pl.when(b == 0)
    def _():
        ybuf[...] = jnp.zeros_like(ybuf)
        row_loop(lambda r, k: gather_row(0, 0, r, k))

    @pl.when((nv > 0) & ((b == 0) | (blk_e_ref[b] != blk_e_ref[prev])))
    def _():
        wgb[...] = wg_ref[...].astype(BF16)
        wub[...] = wu_ref[...].astype(BF16)
        wdb[...] = wd_ref[...].astype(BF16)

    @pl.when(nv > 0)
    def _():
        wait_gather(slot)
        words = _slabs_to_rows(xbuf, slot * tb * xrows, tb, xrows)
        x = _unpack_bf16_pairs(words).astype(BF16)
        has_prev = b > 0
        for r in range(tb):
            gather_row(b + 1, 1 - slot, r, r)
            dst = jnp.where(has_prev, code_ref[prev * tb + r], 2 * n_tok + r)
            scatter_row(prev, 1 - slot, r, dst, r)
        hdn = _silu(_dot(x, wgb[...])) * _dot(x, wub[...])
        y = _dot(hdn.astype(BF16), wdb[...])
        _rows_to_slabs(ybuf, slot * tb * yrows, _pack_bf16_pairs(y), yrows)
        wait_scatter(1 - slot)

    @pl.when((nv == 0) & (nv_prev > 0))
    def _():
        wait_gather(slot)
        row_loop(lambda r, k: scatter_row(prev, 1 - slot, r, code_ref[prev * tb + r], k))
        wait_scatter(1 - slot)


def _ffn(blk_e, nv, code, u2d, wg, wu, wd, layer):
    d, de = wg.shape[2], wg.shape[3]
    xrows = yrows = _slab_rows(d // 2)
    n = u2d.shape[0] // xrows
    assert n & (n - 1) == 0, "token index is taken from the slot code by masking"
    nblk = blk_e.shape[0]
    tb = MOE_BLOCK
    grid_spec = pltpu.PrefetchScalarGridSpec(
        num_scalar_prefetch=3,
        grid=(nblk,),
        in_specs=[pl.BlockSpec(memory_space=pl.ANY),
                  pl.BlockSpec((None, None, d, de), lambda b, be, nv, cd: (layer, be[b], 0, 0)),
                  pl.BlockSpec((None, None, d, de), lambda b, be, nv, cd: (layer, be[b], 0, 0)),
                  pl.BlockSpec((None, None, de, d), lambda b, be, nv, cd: (layer, be[b], 0, 0))],
        out_specs=pl.BlockSpec(memory_space=pl.ANY),
        scratch_shapes=[pltpu.VMEM((2 * tb * xrows, LANE), jnp.uint32),
                        pltpu.VMEM((2 * tb * yrows, LANE), jnp.uint32),
                        pltpu.VMEM((d, de), BF16), pltpu.VMEM((d, de), BF16), pltpu.VMEM((de, d), BF16),
                        pltpu.SemaphoreType.DMA((2,)), pltpu.SemaphoreType.DMA((2,))],
    )
    return pl.pallas_call(
        functools.partial(_ffn_kernel, tb=tb, xrows=xrows, yrows=yrows),
        grid_spec=grid_spec,
        out_shape=jax.ShapeDtypeStruct(((2 * n + tb) * yrows, LANE), jnp.uint32),
        compiler_params=_cparams("arbitrary"),
        name="moe_ffn",
    )(blk_e, nv, code, u2d, wg, wu, wd)


def _invert_kernel(dest_ref, code_ref, *, tb):
    n_slots, p_total = dest_ref.shape[0], code_ref.shape[0]

    def init(i, c):
        base = n_slots + ((i * DMA_UNROLL) & (tb - 1))
        for k in range(DMA_UNROLL):
            code_ref[i * DMA_UNROLL + k] = base + k
        return c
    lax.fori_loop(0, p_total // DMA_UNROLL, init, 0)

    def put(i, c):
        for k in range(DMA_UNROLL):
            s = i * DMA_UNROLL + k
            code_ref[dest_ref[s]] = s
        return c
    lax.fori_loop(0, n_slots // DMA_UNROLL, put, 0)


def _invert(dest, p_total, tb):
    assert tb & (tb - 1) == 0
    return pl.pallas_call(
        functools.partial(_invert_kernel, tb=tb),
        in_specs=[pl.BlockSpec(memory_space=pltpu.SMEM)],
        out_specs=pl.BlockSpec(memory_space=pltpu.SMEM),
        out_shape=jax.ShapeDtypeStruct((p_total,), jnp.int32),
        name="invert_slots",
    )(dest)


def _dispatch_tables(route_t, counts, n):
    tb = MOE_BLOCK
    e = route_t[0:2, :].astype(jnp.int32)
    rank = route_t[4:6, :].astype(jnp.int32)
    cnt = counts[0, N_GROUPS:N_GROUPS + N_EXPERTS].astype(jnp.int32)
    padded = ((cnt + tb - 1) // tb) * tb
    pends = jnp.cumsum(padded)
    pstarts = pends - padded
    ids = jnp.arange(N_EXPERTS, dtype=jnp.int32)[:, None, None]
    start_of = jnp.sum(jnp.where(ids == e[None], pstarts[:, None, None], 0), axis=0)
    dest = (start_of + rank).reshape(-1)
    p_total = 2 * n + N_EXPERTS * tb
    nblk = p_total // tb
    code = _invert(dest, p_total, tb)
    bstart = jnp.arange(nblk, dtype=jnp.int32) * tb
    blk_e = jnp.minimum(jnp.sum((bstart[:, None] >= pends[None, :]).astype(jnp.int32), axis=1), N_EXPERTS - 1)
    nv = jnp.clip(cnt[blk_e] - (bstart - pstarts[blk_e]), 0, tb)
    nv = jnp.where(bstart < pends[-1], nv, 0).astype(jnp.int32)
    return blk_e, nv, code


def _pad_lanes(v, width):
    return jnp.pad(v, ((0, 0), (0, width - v.shape[1])))


def kernel(x, attn_norm_w, w_in, ssd_conv_w, ssd_conv_b, ssd_dt_bias, ssd_a_log, ssd_d, ssd_norm_w, hgrn_lower_bounds, hgrn_norm_w, w_out, ffn_norm_w, router_group_w, router_group_b, router_expert_w, router_expert_b, expert_w_gate, expert_w_up, expert_w_down, final_norm_w):
    bsz, t, d = x.shape
    n = bsz * t
    depth = w_in.shape[0]

    half = HEAD_DIM // 2
    inv = 1.0 / (ROPE_THETA ** (jnp.arange(half, dtype=F32) / half))
    ang = jnp.arange(t, dtype=F32)[:, None] * inv[None, :]
    cos_t = jnp.concatenate([jnp.cos(ang), jnp.cos(ang)], axis=1)
    sin_t = jnp.concatenate([-jnp.sin(ang), jnp.sin(ang)], axis=1)

    def group_lanes(v):
        v = v.reshape(SSD_GROUPS, SSD_HEADS_PER_GROUP)
        return _pad_lanes(v, LANE).reshape(1, SSD_GROUPS * LANE)

    h = x.reshape(n, d)
    w_in_b = _wprep(jnp.swapaxes(w_in, 1, 2))
    moe = None
    for l in range(depth):
        if moe is None:
            (u,) = _norm_pass(h, attn_norm_w[l][None, :], None, BF16, False)
        else:
            u, h = _norm_pass(h, attn_norm_w[l][None, :], moe, BF16, True)
        p = _inproj(u, w_in_b, l)

        o_ret = _retention(p, cos_t, sin_t, bsz, t)
        o_ssd = _ssd(p, ssd_conv_w[l], ssd_conv_b[l][None, :], group_lanes(ssd_dt_bias[l]),
                     group_lanes(ssd_a_log[l]), jnp.repeat(ssd_d[l], SSD_HEADDIM)[None, :],
                     ssd_norm_w[l][None, :], bsz, t)
        o_hgrn = _hgrn(p, hgrn_lower_bounds, hgrn_norm_w[l][None, :], l, bsz, t)

        w_r = _pad_lanes(jnp.concatenate([router_group_w[l], router_expert_w[l]], axis=1), LANE)
        w_r_hi = w_r.astype(BF16)
        w_r = jnp.concatenate([w_r_hi, (w_r - w_r_hi.astype(F32)).astype(BF16)], axis=1)
        b_r = _pad_lanes(jnp.concatenate([router_group_b[l], router_expert_b[l]])[None, :], LANE)
        h, u2d, logits = _outproj(o_ret, o_ssd, o_hgrn, h, w_out[l].astype(BF16), ffn_norm_w[l][None, :], w_r, b_r)
        route, route_t, counts = _route(logits)
        blk_e, nv, code = _dispatch_tables(route_t, counts, n)
        o2d = _ffn(blk_e, nv, code, u2d, expert_w_gate, expert_w_up, expert_w_down, l)
        moe = (o2d, route)

    (out,) = _norm_pass(h, final_norm_w[None, :], moe, F32, False)
    return out.reshape(bsz, t, d)
```

```python
import functools
import math

import jax
import jax.numpy as jnp
from jax import lax
from jax.experimental import pallas as pl
from jax.experimental.pallas import tpu as pltpu

F32 = jnp.float32
BF16 = jnp.bfloat16
HIGHEST = lax.Precision.HIGHEST

V7X_VMEM_BYTES = 64 * 1024 * 1024
VMEM_LIMIT = V7X_VMEM_BYTES - 8 * 1024 * 1024
LANE = 128

EPS = 1e-6
LOG2E = 1.4426950408889634
LB_FLOOR = 1e-30
ROPE_THETA = 10000.0

RET_HEADS = 4
HEAD_DIM = 128
SSD_HEADDIM = 64
SSD_GROUPS = 2
SSD_HEADS_PER_GROUP = 8
SSD_CONV = 4
HGRN_HEADS = 4
CHUNK = 128
N_GROUPS = 4
EXPERTS_PER_GROUP = 8
N_EXPERTS = N_GROUPS * EXPERTS_PER_GROUP
MOE_BLOCK = 256
DMA_UNROLL = 8
ROUTE_TILE = 512
DT_LO = 4608
DT_HI = DT_LO + SSD_GROUPS * SSD_HEADS_PER_GROUP


def _cparams(*sem):
    return pltpu.CompilerParams(dimension_semantics=sem, vmem_limit_bytes=VMEM_LIMIT)


def _rms(x, w=None):
    y = x * lax.rsqrt(jnp.mean(x * x, axis=-1, keepdims=True) + EPS)
    return y if w is None else y * w


def _sigmoid(x):
    return 1.0 / (1.0 + jnp.exp(-x))


def _silu(x):
    return x * _sigmoid(x)


def _dot(a, b):
    return jnp.dot(a, b, preferred_element_type=F32)


def _dot_nt(a, b):
    return lax.dot_general(a, b, (((1,), (1,)), ((), ())), preferred_element_type=F32)


def _dot_tn(a, b):
    return lax.dot_general(a, b, (((0,), (0,)), ((), ())), preferred_element_type=F32)


def _tri_incl():
    ii = lax.broadcasted_iota(jnp.int32, (CHUNK, CHUNK), 0)
    jj = lax.broadcasted_iota(jnp.int32, (CHUNK, CHUNK), 1)
    return ii, jj


def _lane_spread_matrix(width, per_head):
    hh = lax.broadcasted_iota(jnp.int32, (LANE, width), 0)
    cc = lax.broadcasted_iota(jnp.int32, (LANE, width), 1)
    return jnp.where((cc >= hh * per_head) & (cc < (hh + 1) * per_head), 1.0, 0.0).astype(BF16)


def _spread(v, e, terms):
    out = None
    rest = v
    for i in range(terms):
        piece = rest.astype(BF16)
        part = _dot(piece, e)
        out = part if out is None else out + part
        if i + 1 < terms:
            rest = rest - piece.astype(F32)
    return out


def _slab_rows(d):
    return d // LANE


def _pack_bf16_pairs(v):
    half = v.shape[1] // 2
    bits = lax.bitcast_convert_type(v.astype(BF16).astype(F32), jnp.uint32)
    return (bits[:, :half] >> 16) | (bits[:, half:] & jnp.uint32(0xFFFF0000))


def _unpack_bf16_pairs(words):
    return jnp.concatenate([lax.bitcast_convert_type(words << 16, F32),
                            lax.bitcast_convert_type(words & jnp.uint32(0xFFFF0000), F32)], axis=1)


def _slabs_to_rows(ref2d, base, rows, srows):
    return jnp.concatenate([ref2d[pl.ds(base + c, rows, stride=srows), :] for c in range(srows)], axis=1)


def _rows_to_slabs(ref2d, base, val, srows):
    rows = val.shape[0]
    for c in range(srows):
        ref2d[pl.ds(base + c, rows, stride=srows), :] = val[:, c * LANE:(c + 1) * LANE]


WPREP_ROWS = 256


def _wprep_kernel(wt_hbm, out_ref, buf, sem):
    l, j = pl.program_id(0), pl.program_id(1)
    nj = pl.num_programs(1)
    step = l * nj + j
    slot = step % 2
    n_lo = DT_LO // WPREP_ROWS
    n_main = n_lo + (wt_hbm.shape[1] - DT_HI) // WPREP_ROWS

    def block_copy(st, sl):
        jj = st % nj
        src = jnp.where(jj < n_lo, jj * WPREP_ROWS,
                        jnp.where(jj < n_main, DT_HI + (jj - n_lo) * WPREP_ROWS, DT_LO))
        return pltpu.make_async_copy(wt_hbm.at[st // nj, pl.ds(pl.multiple_of(src, 8), WPREP_ROWS), :],
                                     buf.at[sl], sem.at[sl])

    @pl.when(step == 0)
    def _():
        block_copy(0, 0).start()

    @pl.when(step + 1 < pl.num_programs(0) * nj)
    def _():
        block_copy(step + 1, 1 - slot).start()

    block_copy(step, slot).wait()

    @pl.when(j < n_main)
    def _():
        out_ref[...] = buf[slot].astype(BF16)

    @pl.when(j >= n_main)
    def _():
        pad = jnp.zeros((LANE - SSD_HEADS_PER_GROUP, out_ref.shape[1]), F32)
        parts = []
        for g in range(SSD_GROUPS):
            parts += [buf[slot, g * SSD_HEADS_PER_GROUP:(g + 1) * SSD_HEADS_PER_GROUP, :], pad]
        out_ref[...] = jnp.concatenate(parts, axis=0).astype(BF16)


def _wprep(wt):
    depth, nin, d = wt.shape
    nout = nin - (DT_HI - DT_LO) + SSD_GROUPS * LANE
    assert DT_LO % WPREP_ROWS == 0 and (nin - DT_HI) % WPREP_ROWS == 0 and SSD_GROUPS * LANE == WPREP_ROWS
    return pl.pallas_call(
        _wprep_kernel,
        grid=(depth, nout // WPREP_ROWS),
        in_specs=[pl.BlockSpec(memory_space=pl.ANY)],
        out_specs=pl.BlockSpec((None, WPREP_ROWS, d), lambda l, j: (l, j, 0)),
        out_shape=jax.ShapeDtypeStruct((depth, nout, d), BF16),
        scratch_shapes=[pltpu.VMEM((2, WPREP_ROWS, d), F32), pltpu.SemaphoreType.DMA((2,))],
        compiler_params=_cparams("arbitrary", "arbitrary"),
        name="wprep",
    )(wt)


def _norm_kernel(*refs, combine, write_h):
    if combine:
        h_ref, o0_ref, o1_ref, rt_ref, nw_ref = refs[:5]
        outs = refs[5:]
        rows = h_ref.shape[0]
        srows = _slab_rows(h_ref.shape[1] // 2)
        rt = rt_ref[...]
        h = (h_ref[...] + rt[:, 2:3] * _unpack_bf16_pairs(_slabs_to_rows(o0_ref, 0, rows, srows))
             + rt[:, 3:4] * _unpack_bf16_pairs(_slabs_to_rows(o1_ref, 0, rows, srows)))
    else:
        h_ref, nw_ref = refs[:2]
        outs = refs[2:]
        h = h_ref[...]
    u_ref = outs[0]
    u_ref[...] = _rms(h, nw_ref[...]).astype(u_ref.dtype)
    if write_h:
        outs[1][...] = h


def _norm_pass(h, nw, moe, out_dtype, write_h):
    n, d = h.shape
    combine = moe is not None
    tm = min(512, n)
    row = lambda i: (i, 0)
    in_specs = [pl.BlockSpec((tm, d), row)]
    args = [h]
    if combine:
        o2d, rt = moe
        srows = _slab_rows(d // 2)
        nt = n // tm
        in_specs += [pl.BlockSpec((tm * srows, LANE), row),
                     pl.BlockSpec((tm * srows, LANE), lambda i: (nt + i, 0)),
                     pl.BlockSpec((tm, LANE), row)]
        args += [o2d, o2d, rt]
    in_specs.append(pl.BlockSpec((1, d), lambda i: (0, 0)))
    args.append(nw)
    out_shape = [jax.ShapeDtypeStruct((n, d), out_dtype)]
    out_specs = [pl.BlockSpec((tm, d), row)]
    if write_h:
        out_shape.append(jax.ShapeDtypeStruct((n, d), F32))
        out_specs.append(pl.BlockSpec((tm, d), row))
    return pl.pallas_call(
        functools.partial(_norm_kernel, combine=combine, write_h=write_h),
        grid=(n // tm,),
        in_specs=in_specs, out_specs=out_specs, out_shape=out_shape,
        compiler_params=_cparams("arbitrary"),
        name="norm_pass",
    )(*args)


def _inproj_kernel(u_ref, wt_ref, p_ref):
    p_ref[...] = _dot_nt(u_ref[...], wt_ref[...])


def _inproj(u, wt, layer):
    n, d = u.shape
    np_ = wt.shape[1]
    tm = min(1024, n)
    tn = np_ // 3
    return pl.pallas_call(
        _inproj_kernel,
        grid=(np_ // tn, n // tm),
        in_specs=[pl.BlockSpec((tm, d), lambda j, i: (i, 0)),
                  pl.BlockSpec((None, tn, d), lambda j, i: (layer, j, 0))],
        out_specs=pl.BlockSpec((tm, tn), lambda j, i: (i, j)),
        out_shape=jax.ShapeDtypeStruct((n, np_), F32),
        compiler_params=_cparams("arbitrary", "arbitrary"),
        name="inproj",
    )(u, wt)


def _ret_kernel(q_ref, k_ref, v_ref, g_ref, cos_ref, sin_ref, o_ref, s_scr):
    t = q_ref.shape[0]
    ii, jj = _tri_incl()
    causal = ii >= jj
    dist = (ii - jj).astype(F32)
    iif = ii.astype(F32)
    consts = []
    for hd in range(RET_HEADS):
        lg = math.log(1.0 - 2.0 ** (-5.0 - hd))
        dmat = jnp.where(causal, jnp.exp(jnp.where(causal, dist * lg, 0.0)), 0.0)
        ecum = jnp.exp((iif + 1.0) * lg)
        wk = jnp.exp((CHUNK - 1.0 - iif) * lg)
        consts.append((dmat, ecum, wk, math.exp(CHUNK * lg)))
    scale = HEAD_DIM ** -0.5
    s_scr[...] = jnp.zeros_like(s_scr)

    def step(c, carry):
        r = pl.ds(pl.multiple_of(c * CHUNK, CHUNK), CHUNK)
        cs, sn = cos_ref[r, :], sin_ref[r, :]
        for hd in range(RET_HEADS):
            dmat, ecum, wk, elast = consts[hd]
            cols = slice(hd * HEAD_DIM, (hd + 1) * HEAD_DIM)
            q, k = q_ref[r, cols], k_ref[r, cols]
            qr = q * cs + pltpu.roll(q, HEAD_DIM // 2, 1) * sn
            kr = (k * cs + pltpu.roll(k, HEAD_DIM // 2, 1) * sn) * scale
            vb = v_ref[r, cols].astype(BF16)
            s = s_scr[hd]
            scores = _dot_nt(qr.astype(BF16), kr.astype(BF16)) * dmat
            out = _dot(scores.astype(BF16), vb) + _dot((qr * ecum).astype(BF16), s.astype(BF16))
            s_scr[hd] = elast * s + _dot_tn((kr * wk).astype(BF16), vb)
            o_ref[r, cols] = (_silu(g_ref[r, cols]) * _rms(out)).astype(o_ref.dtype)
        return carry

    lax.fori_loop(0, t // CHUNK, step, 0, unroll=2)


def _retention(p, cos_t, sin_t, bsz, t):
    n = bsz * t
    width = RET_HEADS * HEAD_DIM
    blk = lambda off: pl.BlockSpec((t, width), lambda b, off=off: (b, off))
    tab = pl.BlockSpec((t, HEAD_DIM), lambda b: (0, 0))
    return pl.pallas_call(
        _ret_kernel,
        grid=(bsz,),
        in_specs=[blk(0), blk(1), blk(2), blk(3), tab, tab],
        out_specs=pl.BlockSpec((t, width), lambda b: (b, 0)),
        out_shape=jax.ShapeDtypeStruct((n, width), BF16),
        scratch_shapes=[pltpu.VMEM((RET_HEADS, HEAD_DIM, HEAD_DIM), F32)],
        compiler_params=_cparams("arbitrary"),
        name="retention",
    )(p, p, p, p, cos_t, sin_t)


def _ssd_kernel(z_ref, x_ref, b_ref, c_ref, dt_ref, cwx_ref, cwb_ref, cwc_ref, cbx_ref, cbb_ref, cbc_ref,
                dtb_ref, alog_ref, dsk_ref, nw_ref, o_ref, s_scr):
    t = z_ref.shape[0]
    ii, jj = _tri_incl()
    causal = ii >= jj
    tri = jnp.where(causal, 1.0, 0.0).astype(F32)
    lane_lo = lax.broadcasted_iota(jnp.int32, (CHUNK, LANE), 1) < SSD_HEADDIM
    neg_a = -jnp.exp(alog_ref[...])
    gw = x_ref.shape[1]
    e_pairs = _lane_spread_matrix(SSD_HEADS_PER_GROUP * CHUNK, CHUNK)
    e_heads = _lane_spread_matrix(gw, SSD_HEADDIM)
    s_scr[...] = jnp.zeros_like(s_scr)

    def conv(ref, w_ref, bias_ref, c):
        r = pl.ds(pl.multiple_of(c * CHUNK, CHUNK), CHUNK)
        rp = pl.ds(pl.multiple_of(jnp.maximum(c * CHUNK - 8, 0), 8), 8)
        cur = ref[r, :]
        ext = jnp.concatenate([jnp.where(c > 0, ref[rp, :], 0.0), cur], axis=0)
        w = w_ref[...]
        acc = bias_ref[...] + w[SSD_CONV - 1:SSD_CONV, :] * cur
        for lag in range(1, SSD_CONV):
            shifted = pltpu.roll(ext, lag, 0)[8:8 + CHUNK, :]
            acc = acc + w[SSD_CONV - 1 - lag:SSD_CONV - lag, :] * shifted
        return _silu(acc)

    def step(c, carry):
        r = pl.ds(pl.multiple_of(c * CHUNK, CHUNK), CHUNK)
        xs = conv(x_ref, cwx_ref, cbx_ref, c)
        bm = conv(b_ref, cwb_ref, cbb_ref, c).astype(BF16)
        cm = conv(c_ref, cwc_ref, cbc_ref, c).astype(BF16)
        xr = dt_ref[r, :] + dtb_ref[...]
        dt = jnp.maximum(xr, 0.0) + jnp.log1p(jnp.exp(-jnp.abs(xr)))
        la = dt * neg_a
        cum = jnp.dot(tri, la, precision=HIGHEST, preferred_element_type=F32) * LOG2E
        last = cum[CHUNK - 1:CHUNK, :]
        wj = dt * jnp.exp2(last - cum)
        ecum = jnp.exp2(cum)
        cum_t = cum.T
        dt_t = dt.T
        gm = jnp.where(causal, _dot_nt(cm, bm), 0.0)
        s = s_scr[...]
        cs = _dot(cm, s.astype(BF16))
        cix = _spread(cum, e_pairs, 3)
        ecx = _spread(ecum, e_heads, 2)
        wjx = _spread(wj, e_heads, 2)
        ys = []
        for pr in range(SSD_HEADS_PER_GROUP // 2):
            xp = xs[:, pr * LANE:(pr + 1) * LANE]
            acc = None
            for half in range(2):
                hd = 2 * pr + half
                ci = cix[:, hd * CHUNK:(hd + 1) * CHUNK]
                dec = jnp.exp2(jnp.minimum(ci - cum_t[hd:hd + 1, :], 0.0))
                m = gm * dec * dt_t[hd:hd + 1, :]
                sel = lane_lo if half == 0 else jnp.logical_not(lane_lo)
                y = _dot(m.astype(BF16), jnp.where(sel, xp, 0.0).astype(BF16))
                acc = y if acc is None else acc + y
            ys.append(acc)
        y = jnp.concatenate(ys, axis=1) + ecx * cs
        s_scr[...] = ecx[CHUNK - 1:CHUNK, :] * s + _dot_tn(bm, (wjx * xs).astype(BF16))
        y = (y + dsk_ref[...] * xs) * _silu(z_ref[r, :])
        o_ref[r, :] = (_rms(y) * nw_ref[...]).astype(o_ref.dtype)
        return carry

    lax.fori_loop(0, t // CHUNK, step, 0, unroll=2)


def _ssd(p, conv_w, conv_b, dtb, alog, dsk, nw, bsz, t):
    n = bsz * t
    gw = SSD_HEADS_PER_GROUP * SSD_HEADDIM
    st = HEAD_DIM
    z_off, x_off = 2048 // gw, 3072 // gw
    b_off, c_off = 4096 // st, 4352 // st
    dt_off = (p.shape[1] - SSD_GROUPS * LANE) // LANE
    xw = SSD_GROUPS * gw
    par = lambda shape, f: pl.BlockSpec(shape, f)
    in_specs = [
        par((t, gw), lambda b, g: (b, z_off + g)),
        par((t, gw), lambda b, g: (b, x_off + g)),
        par((t, st), lambda b, g: (b, b_off + g)),
        par((t, st), lambda b, g: (b, c_off + g)),
        par((t, LANE), lambda b, g: (b, dt_off + g)),
        par((SSD_CONV, gw), lambda b, g: (0, g)),
        par((SSD_CONV, st), lambda b, g: (0, xw // st + g)),
        par((SSD_CONV, st), lambda b, g: (0, xw // st + SSD_GROUPS + g)),
        par((1, gw), lambda b, g: (0, g)),
        par((1, st), lambda b, g: (0, xw // st + g)),
        par((1, st), lambda b, g: (0, xw // st + SSD_GROUPS + g)),
        par((1, LANE), lambda b, g: (0, g)),
        par((1, LANE), lambda b, g: (0, g)),
        par((1, gw), lambda b, g: (0, g)),
        par((1, gw), lambda b, g: (0, g)),
    ]
    return pl.pallas_call(
        _ssd_kernel,
        grid=(bsz, SSD_GROUPS),
        in_specs=in_specs,
        out_specs=pl.BlockSpec((t, gw), lambda b, g: (b, g)),
        out_shape=jax.ShapeDtypeStruct((n, SSD_GROUPS * gw), BF16),
        scratch_shapes=[pltpu.VMEM((st, gw), F32)],
        compiler_params=_cparams("arbitrary", "arbitrary"),
        name="ssd",
    )(p, p, p, p, p, conv_w, conv_w, conv_w, conv_b, conv_b, conv_b, dtb, alog, dsk, nw)


def _hgrn_kernel(q_ref, f_ref, i_ref, g_ref, lb_ref, nw_ref, o_ref, st_scr, cum_scr, *, layer):
    t = q_ref.shape[0]
    lbm = lb_ref[...]
    depth = lbm.shape[0]
    mx = lbm[0:1, :]
    for i in range(1, depth):
        mx = jnp.maximum(mx, lbm[i:i + 1, :])
    ex = [jnp.exp(lbm[i:i + 1, :] - mx) for i in range(depth)]
    den = ex[0]
    for i in range(1, depth):
        den = den + ex[i]
    sm = [e / den for e in ex]
    csum = sm[0]
    for i in range(1, layer + 1):
        csum = csum + sm[i]
    lb = jnp.maximum(csum - sm[0], 0.0)
    log_lb = jnp.log(jnp.maximum(lb, LB_FLOOR))
    l1m = jnp.log1p(-lb)
    oml = 1.0 - lb

    ii, jj = _tri_incl()
    tri = jnp.where(ii >= jj, 1.0, 0.0).astype(F32)
    lvl = jnp.where(ii > jj, 31 - lax.clz(ii ^ jj), -1)
    eye = ii == jj
    width = q_ref.shape[1]
    hcols = [slice(c0, c0 + HEAD_DIM) for c0 in range(0, width, HEAD_DIM)]
    row = lax.broadcasted_iota(jnp.int32, (CHUNK, width), 0)
    scale = HEAD_DIM ** -0.5
    nlev = int(math.log2(CHUNK))
    st_scr[...] = jnp.zeros_like(st_scr)

    def step(c, carry):
        r = pl.ds(pl.multiple_of(c * CHUNK, CHUNK), CHUNK)
        q = _silu(q_ref[r, :]) * scale
        f = f_ref[r, :]
        ls = jnp.minimum(f, 0.0) - jnp.log1p(jnp.exp(-jnp.abs(f)))
        a, b = log_lb, l1m + ls
        lf = jnp.maximum(a, b) + jnp.log1p(jnp.exp(-jnp.abs(a - b)))
        kk = oml / (1.0 + jnp.exp(f))
        vb = i_ref[r, :].astype(BF16)
        cum = jnp.dot(tri, lf, precision=HIGHEST, preferred_element_type=F32) * LOG2E
        cum_scr[...] = cum
        qb, kb = q.astype(BF16), kk.astype(BF16)
        scores = [jnp.where(eye, _dot_nt(qb[:, cs], kb[:, cs]), 0.0) for cs in hcols]
        for lv in range(nlev):
            s = 1 << lv
            if 2 * s >= 8:
                ref = jnp.concatenate(
                    [jnp.broadcast_to(cum_scr[g0 * 2 * s + s - 1:g0 * 2 * s + s, :], (2 * s, width))
                     for g0 in range(CHUNK // (2 * s))], axis=0)
            elif s == 2:
                m4 = row & 3
                ref = jnp.where(m4 == 0, pltpu.roll(cum, CHUNK - 1, 0),
                                jnp.where(m4 == 1, cum,
                                          jnp.where(m4 == 2, pltpu.roll(cum, 1, 0), pltpu.roll(cum, 2, 0))))
            else:
                ref = jnp.where((row & 1) == 1, pltpu.roll(cum, 1, 0), cum)
            dlt = cum - ref
            e = jnp.exp2(jnp.minimum(dlt, -dlt))
            qs = (q * e).astype(BF16)
            ks = (kk * e).astype(BF16)
            scores = [jnp.where(lvl == lv, _dot_nt(qs[:, cs], ks[:, cs]), sc) for cs, sc in zip(hcols, scores)]
        last = cum[CHUNK - 1:CHUNK, :]
        qe = (q * jnp.exp2(cum)).astype(BF16)
        kw = (kk * jnp.exp2(last - cum)).astype(BF16)
        elast = jnp.exp2(last)
        outs = []
        for hd, cs in enumerate(hcols):
            st = st_scr[hd]
            out = _dot(scores[hd].astype(BF16), vb[:, cs]) + _dot_nt(qe[:, cs], st.astype(BF16))
            st_scr[hd] = st * elast[:, cs] + _dot_tn(vb[:, cs], kw[:, cs])
            outs.append(_rms(out))
        o = jnp.concatenate(outs, axis=1) * nw_ref[...]
        o_ref[r, :] = (_silu(g_ref[r, :]) * o).astype(o_ref.dtype)
        return carry

    lax.fori_loop(0, t // CHUNK, step, 0, unroll=2)


def _hgrn(p, lbounds, nw, layer, bsz, t):
    n = bsz * t
    width = HGRN_HEADS * HEAD_DIM
    base = DT_LO // width
    blk = lambda off: pl.BlockSpec((t, width), lambda b, off=off: (b, base + off))
    return pl.pallas_call(
        functools.partial(_hgrn_kernel, layer=layer),
        grid=(bsz,),
        in_specs=[blk(0), blk(1), blk(2), blk(3),
                  pl.BlockSpec((lbounds.shape[0], width), lambda b: (0, 0)),
                  pl.BlockSpec((1, width), lambda b: (0, 0))],
        out_specs=pl.BlockSpec((t, width), lambda b: (b, 0)),
        out_shape=jax.ShapeDtypeStruct((n, width), BF16),
        scratch_shapes=[pltpu.VMEM((HGRN_HEADS, HEAD_DIM, HEAD_DIM), F32), pltpu.VMEM((CHUNK, width), F32)],
        compiler_params=_cparams("arbitrary"),
        name="hgrn2",
    )(p, p, p, p, lbounds, nw)


def _outproj_kernel(a_ref, b_ref, c_ref, h_ref, w_ref, nw_ref, wr_ref, br_ref, hout_ref, u_ref, lg_ref):
    xrows = _slab_rows(h_ref.shape[1] // 2)
    half = h_ref.shape[0] // 2
    accs = []
    for r0 in (0, half):
        rs = slice(r0, r0 + half)
        mix = jnp.concatenate([a_ref[rs, :], b_ref[rs, :], c_ref[rs, :]], axis=1)
        accs.append(_dot(mix, w_ref[...]))
    for r0, acc in zip((0, half), accs):
        rs = slice(r0, r0 + half)
        h = h_ref[rs, :] + acc
        hout_ref[rs, :] = h
        u = _rms(h, nw_ref[...])
        u_hi = u.astype(BF16)
        u_lo = (u - u_hi.astype(F32)).astype(BF16)
        t2 = _dot(u_hi, wr_ref[...])
        lg_ref[rs, :] = t2[:, :LANE] + t2[:, LANE:] + _dot(u_lo, wr_ref[:, :LANE]) + br_ref[...]
        _rows_to_slabs(u_ref, r0 * xrows, _pack_bf16_pairs(u), xrows)


def _outproj(o_a, o_b, o_c, h, w_out, nw, w_r, b_r):
    n, d = h.shape
    tm = min(512, n)
    srows = _slab_rows(d // 2)
    row = lambda i: (i, 0)
    fix = lambda i: (0, 0)
    return pl.pallas_call(
        _outproj_kernel,
        grid=(n // tm,),
        in_specs=[pl.BlockSpec((tm, o_a.shape[1]), row), pl.BlockSpec((tm, o_b.shape[1]), row),
                  pl.BlockSpec((tm, o_c.shape[1]), row), pl.BlockSpec((tm, d), row),
                  pl.BlockSpec(w_out.shape, fix), pl.BlockSpec((1, d), fix),
                  pl.BlockSpec(w_r.shape, fix), pl.BlockSpec((1, LANE), fix)],
        out_specs=[pl.BlockSpec((tm, d), row), pl.BlockSpec((tm * srows, LANE), row),
                   pl.BlockSpec((tm, LANE), row)],
        out_shape=[jax.ShapeDtypeStruct((n, d), F32), jax.ShapeDtypeStruct((n * srows, LANE), jnp.uint32),
                   jax.ShapeDtypeStruct((n, LANE), F32)],
        compiler_params=_cparams("arbitrary"),
        name="outproj",
    )(o_a, o_b, o_c, h, w_out, nw, w_r, b_r)


def _route_kernel(lg_ref, rt_ref, rtt_ref, cnt_ref, carry_scr):
    tr = lg_ref.shape[0]

    @pl.when(pl.program_id(0) == 0)
    def _():
        carry_scr[...] = jnp.zeros_like(carry_scr)

    lg = lg_ref[...]
    lane = lax.broadcasted_iota(jnp.int32, (tr, LANE), 1)
    neg = -jnp.inf
    big = jnp.int32(LANE)

    def first_max(vals):
        m = jnp.max(vals, axis=-1, keepdims=True)
        idx = jnp.min(jnp.where(vals == m, lane, big), axis=-1, keepdims=True)
        return m, idx

    gl = jnp.where(lane < N_GROUPS, lg, neg)
    gmax, gidx = first_max(gl)
    gate = 1.0 / jnp.sum(jnp.exp(gl - gmax), axis=-1, keepdims=True)
    lo = N_GROUPS + EXPERTS_PER_GROUP * gidx
    el = jnp.where((lane >= lo) & (lane < lo + EXPERTS_PER_GROUP), lg, neg)
    m1, i1 = first_max(el)
    m2, i2 = first_max(jnp.where(lane == i1, neg, el))
    e21 = jnp.exp(m2 - m1)
    w1 = gate * (1.0 / (1.0 + e21))
    w2 = gate * (e21 / (1.0 + e21))

    oh1 = lane == i1
    oh2 = lane == i2
    m = jnp.where(oh1 | oh2, 1.0, 0.0)
    ti = lax.broadcasted_iota(jnp.int32, (tr, tr), 0)
    tj = lax.broadcasted_iota(jnp.int32, (tr, tr), 1)
    before = _dot(jnp.where(ti > tj, 1.0, 0.0).astype(BF16), m.astype(BF16)) + carry_scr[0:1, :]
    r1 = jnp.sum(jnp.where(oh1, before, 0.0), axis=-1, keepdims=True)
    r2 = jnp.sum(jnp.where(oh2, before, 0.0), axis=-1, keepdims=True)
    total = carry_scr[0:1, :] + jnp.sum(m, axis=0, keepdims=True)
    carry_scr[...] = jnp.broadcast_to(total, carry_scr.shape)
    cnt_ref[...] = jnp.broadcast_to(total, cnt_ref.shape)

    e1 = (i1 - N_GROUPS).astype(F32)
    e2 = (i2 - N_GROUPS).astype(F32)
    out = jnp.zeros((tr, LANE), F32)
    for pos, val in enumerate((e1, e2, w1, w2, r1, r2)):
        out = jnp.where(lane == pos, val, out)
    rt_ref[...] = out
    rtt_ref[...] = out.T[0:rtt_ref.shape[0], :]


def _route(logits):
    n = logits.shape[0]
    tr = min(ROUTE_TILE, n)
    return pl.pallas_call(
        _route_kernel,
        grid=(n // tr,),
        in_specs=[pl.BlockSpec((tr, LANE), lambda i: (i, 0))],
        out_specs=[pl.BlockSpec((tr, LANE), lambda i: (i, 0)), pl.BlockSpec((8, tr), lambda i: (0, i)),
                   pl.BlockSpec((8, LANE), lambda i: (0, 0))],
        out_shape=[jax.ShapeDtypeStruct((n, LANE), F32), jax.ShapeDtypeStruct((8, n), F32),
                   jax.ShapeDtypeStruct((8, LANE), F32)],
        scratch_shapes=[pltpu.VMEM((8, LANE), F32)],
        compiler_params=_cparams("arbitrary"),
        name="route",
    )(logits)


def _ffn_kernel(blk_e_ref, nv_ref, code_ref, u_hbm, wg_ref, wu_ref, wd_ref, o_hbm,
                xbuf, ybuf, wgb, wub, wdb, gsem, ssem, *, tb, xrows, yrows):
    b = pl.program_id(0)
    n_tok = u_hbm.shape[0] // xrows
    slot = b % 2
    prev = jnp.maximum(b - 1, 0)
    nv = nv_ref[b]
    nv_prev = jnp.where(b > 0, nv_ref[prev], 0)

    def slab(ref, srows, idx, count=1):
        return ref.at[pl.ds(pl.multiple_of(idx * srows, srows), count * srows)]

    def gather_row(blk, sl, r, lane):
        tok = code_ref[blk * tb + r] & (n_tok - 1)
        pltpu.make_async_copy(slab(u_hbm, xrows, tok), slab(xbuf, xrows, sl * tb + r),
                              gsem.at[sl]).start(priority=lane % 2)

    def scatter_row(blk, sl, r, dst, lane):
        pltpu.make_async_copy(slab(ybuf, yrows, sl * tb + r), slab(o_hbm, yrows, dst),
                              ssem.at[sl]).start(priority=lane % 2)

    def wait_gather(sl):
        pltpu.make_async_copy(slab(u_hbm, xrows, 0, tb), slab(xbuf, xrows, sl * tb, tb), gsem.at[sl]).wait()

    def wait_scatter(sl):
        pltpu.make_async_copy(slab(ybuf, yrows, sl * tb, tb), slab(o_hbm, yrows, 0, tb), ssem.at[sl]).wait()

    def row_loop(fn):
        def grp(i, c):
            for k in range(DMA_UNROLL):
                fn(i * DMA_UNROLL + k, k)
            return c
        lax.fori_loop(0, tb // DMA_UNROLL, grp, 0)

    @pl.when(b == 0)
    def _():
        ybuf[...] = jnp.zeros_like(ybuf)
        row_loop(lambda r, k: gather_row(0, 0, r, k))

    @pl.when((nv > 0) & ((b == 0) | (blk_e_ref[b] != blk_e_ref[prev])))
    def _():
        wgb[...] = wg_ref[...].astype(BF16)
        wub[...] = wu_ref[...].astype(BF16)
        wdb[...] = wd_ref[...].astype(BF16)

    @pl.when(nv > 0)
    def _():
        wait_gather(slot)
        words = _slabs_to_rows(xbuf, slot * tb * xrows, tb, xrows)
        x = _unpack_bf16_pairs(words).astype(BF16)
        has_prev = b > 0
        for r in range(tb):
            gather_row(b + 1, 1 - slot, r, r)
            dst = jnp.where(has_prev, code_ref[prev * tb + r], 2 * n_tok + r)
            scatter_row(prev, 1 - slot, r, dst, r)
        hdn = _silu(_dot(x, wgb[...])) * _dot(x, wub[...])
        y = _dot(hdn.astype(BF16), wdb[...])
        _rows_to_slabs(ybuf, slot * tb * yrows, _pack_bf16_pairs(y), yrows)
        wait_scatter(1 - slot)

    @pl.when((nv == 0) & (nv_prev > 0))
    def _():
        wait_gather(slot)
        row_loop(lambda r, k: scatter_row(prev, 1 - slot, r, code_ref[prev * tb + r], k))
        wait_scatter(1 - slot)


def _ffn(blk_e, nv, code, u2d, wg, wu, wd, layer):
    d, de = wg.shape[2], wg.shape[3]
    xrows = yrows = _slab_rows(d // 2)
    n = u2d.shape[0] // xrows
    assert n & (n - 1) == 0, "token index is taken from the slot code by masking"
    nblk = blk_e.shape[0]
    tb = MOE_BLOCK
    grid_spec = pltpu.PrefetchScalarGridSpec(
        num_scalar_prefetch=3,
        grid=(nblk,),
        in_specs=[pl.BlockSpec(memory_space=pl.ANY),
                  pl.BlockSpec((None, None, d, de), lambda b, be, nv, cd: (layer, be[b], 0, 0)),
                  pl.BlockSpec((None, None, d, de), lambda b, be, nv, cd: (layer, be[b], 0, 0)),
                  pl.BlockSpec((None, None, de, d), lambda b, be, nv, cd: (layer, be[b], 0, 0))],
        out_specs=pl.BlockSpec(memory_space=pl.ANY),
        scratch_shapes=[pltpu.VMEM((2 * tb * xrows, LANE), jnp.uint32),
                        pltpu.VMEM((2 * tb * yrows, LANE), jnp.uint32),
                        pltpu.VMEM((d, de), BF16), pltpu.VMEM((d, de), BF16), pltpu.VMEM((de, d), BF16),
                        pltpu.SemaphoreType.DMA((2,)), pltpu.SemaphoreType.DMA((2,))],
    )
    return pl.pallas_call(
        functools.partial(_ffn_kernel, tb=tb, xrows=xrows, yrows=yrows),
        grid_spec=grid_spec,
        out_shape=jax.ShapeDtypeStruct(((2 * n + tb) * yrows, LANE), jnp.uint32),
        compiler_params=_cparams("arbitrary"),
        name="moe_ffn",
    )(blk_e, nv, code, u2d, wg, wu, wd)


def _invert_kernel(dest_ref, code_ref, *, tb):
    n_slots, p_total = dest_ref.shape[0], code_ref.shape[0]

    def init(i, c):
        base = n_slots + ((i * DMA_UNROLL) & (tb - 1))
        for k in range(DMA_UNROLL):
            code_ref[i * DMA_UNROLL + k] = base + k
        return c
    lax.fori_loop(0, p_total // DMA_UNROLL, init, 0)

    def put(i, c):
        for k in range(DMA_UNROLL):
            s = i * DMA_UNROLL + k
            code_ref[dest_ref[s]] = s
        return c
    lax.fori_loop(0, n_slots // DMA_UNROLL, put, 0)


def _invert(dest, p_total, tb):
    assert tb & (tb - 1) == 0
    return pl.pallas_call(
        functools.partial(_invert_kernel, tb=tb),
        in_specs=[pl.BlockSpec(memory_space=pltpu.SMEM)],
        out_specs=pl.BlockSpec(memory_space=pltpu.SMEM),
        out_shape=jax.ShapeDtypeStruct((p_total,), jnp.int32),
        name="invert_slots",
    )(dest)


def _dispatch_tables(route_t, counts, n):
    tb = MOE_BLOCK
    e = route_t[0:2, :].astype(jnp.int32)
    rank = route_t[4:6, :].astype(jnp.int32)
    cnt = counts[0, N_GROUPS:N_GROUPS + N_EXPERTS].astype(jnp.int32)
    padded = ((cnt + tb - 1) // tb) * tb
    pends = jnp.cumsum(padded)
    pstarts = pends - padded
    ids = jnp.arange(N_EXPERTS, dtype=jnp.int32)[:, None, None]
    start_of = jnp.sum(jnp.where(ids == e[None], pstarts[:, None, None], 0), axis=0)
    dest = (start_of + rank).reshape(-1)
    p_total = 2 * n + N_EXPERTS * tb
    nblk = p_total // tb
    code = _invert(dest, p_total, tb)
    bstart = jnp.arange(nblk, dtype=jnp.int32) * tb
    blk_e = jnp.minimum(jnp.sum((bstart[:, None] >= pends[None, :]).astype(jnp.int32), axis=1), N_EXPERTS - 1)
    of_blk = blk_e[:, None] == jnp.arange(N_EXPERTS, dtype=jnp.int32)[None, :]
    cnt_b = jnp.sum(jnp.where(of_blk, cnt[None, :], 0), axis=1)
    pstart_b = jnp.sum(jnp.where(of_blk, pstarts[None, :], 0), axis=1)
    nv = jnp.clip(cnt_b - (bstart - pstart_b), 0, tb)
    nv = jnp.where(bstart < pends[-1], nv, 0).astype(jnp.int32)
    return blk_e, nv, code


def _pad_lanes(v, width):
    return jnp.pad(v, ((0, 0), (0, width - v.shape[1])))


def kernel(x, attn_norm_w, w_in, ssd_conv_w, ssd_conv_b, ssd_dt_bias, ssd_a_log, ssd_d, ssd_norm_w, hgrn_lower_bounds, hgrn_norm_w, w_out, ffn_norm_w, router_group_w, router_group_b, router_expert_w, router_expert_b, expert_w_gate, expert_w_up, expert_w_down, final_norm_w):
    bsz, t, d = x.shape
    n = bsz * t
    depth = w_in.shape[0]

    half = HEAD_DIM // 2
    inv = 1.0 / (ROPE_THETA ** (jnp.arange(half, dtype=F32) / half))
    ang = jnp.arange(t, dtype=F32)[:, None] * inv[None, :]
    cos_t = jnp.concatenate([jnp.cos(ang), jnp.cos(ang)], axis=1)
    sin_t = jnp.concatenate([-jnp.sin(ang), jnp.sin(ang)], axis=1)

    def group_lanes(v):
        v = v.reshape(SSD_GROUPS, SSD_HEADS_PER_GROUP)
        return _pad_lanes(v, LANE).reshape(1, SSD_GROUPS * LANE)

    h = x.reshape(n, d)
    w_in_b = _wprep(jnp.swapaxes(w_in, 1, 2))
    moe = None
    for l in range(depth):
        if moe is None:
            (u,) = _norm_pass(h, attn_norm_w[l][None, :], None, BF16, False)
        else:
            u, h = _norm_pass(h, attn_norm_w[l][None, :], moe, BF16, True)
        p = _inproj(u, w_in_b, l)

        o_ret = _retention(p, cos_t, sin_t, bsz, t)
        o_ssd = _ssd(p, ssd_conv_w[l], ssd_conv_b[l][None, :], group_lanes(ssd_dt_bias[l]),
                     group_lanes(ssd_a_log[l]), jnp.repeat(ssd_d[l], SSD_HEADDIM)[None, :],
                     ssd_norm_w[l][None, :], bsz, t)
        o_hgrn = _hgrn(p, hgrn_lower_bounds, hgrn_norm_w[l][None, :], l, bsz, t)

        w_r = _pad_lanes(jnp.concatenate([router_group_w[l], router_expert_w[l]], axis=1), LANE)
        w_r_hi = w_r.astype(BF16)
        w_r = jnp.concatenate([w_r_hi, (w_r - w_r_hi.astype(F32)).astype(BF16)], axis=1)
        b_r = _pad_lanes(jnp.concatenate([router_group_b[l], router_expert_b[l]])[None, :], LANE)
        h, u2d, logits = _outproj(o_ret, o_ssd, o_hgrn, h, w_out[l].astype(BF16), ffn_norm_w[l][None, :], w_r, b_r)
        route, route_t, counts = _route(logits)
        blk_e, nv, code = _dispatch_tables(route_t, counts, n)
        o2d = _ffn(blk_e, nv, code, u2d, expert_w_gate, expert_w_up, expert_w_down, l)
        moe = (o2d, route)

    (out,) = _norm_pass(h, final_norm_w[None, :], moe, F32, False)
    return out.reshape(bsz, t, d)
```

```python
import functools
import math

import jax
import jax.numpy as jnp
from jax import lax
from jax.experimental import pallas as pl
from jax.experimental.pallas import tpu as pltpu

F32 = jnp.float32
BF16 = jnp.bfloat16
HIGHEST = lax.Precision.HIGHEST

V7X_VMEM_BYTES = 64 * 1024 * 1024
VMEM_LIMIT = V7X_VMEM_BYTES - 8 * 1024 * 1024
LANE = 128

EPS = 1e-6
LOG2E = 1.4426950408889634
LB_FLOOR = 1e-30
ROPE_THETA = 10000.0

RET_HEADS = 4
HEAD_DIM = 128
SSD_HEADDIM = 64
SSD_GROUPS = 2
SSD_HEADS_PER_GROUP = 8
SSD_CONV = 4
HGRN_HEADS = 4
CHUNK = 128
N_GROUPS = 4
EXPERTS_PER_GROUP = 8
N_EXPERTS = N_GROUPS * EXPERTS_PER_GROUP
MOE_BLOCK = 256
DMA_UNROLL = 8
ROUTE_TILE = 512
DT_LO = 4608
DT_HI = DT_LO + SSD_GROUPS * SSD_HEADS_PER_GROUP


def _cparams(*sem):
    return pltpu.CompilerParams(dimension_semantics=sem, vmem_limit_bytes=VMEM_LIMIT)


def _rms(x, w=None):
    y = x * lax.rsqrt(jnp.mean(x * x, axis=-1, keepdims=True) + EPS)
    return y if w is None else y * w


def _sigmoid(x):
    return 1.0 / (1.0 + jnp.exp(-x))


def _silu(x):
    return x * _sigmoid(x)


def _dot(a, b):
    return jnp.dot(a, b, preferred_element_type=F32)


def _dot_nt(a, b):
    return lax.dot_general(a, b, (((1,), (1,)), ((), ())), preferred_element_type=F32)


def _dot_tn(a, b):
    return lax.dot_general(a, b, (((0,), (0,)), ((), ())), preferred_element_type=F32)


def _tri_incl():
    ii = lax.broadcasted_iota(jnp.int32, (CHUNK, CHUNK), 0)
    jj = lax.broadcasted_iota(jnp.int32, (CHUNK, CHUNK), 1)
    return ii, jj


def _lane_spread_matrix(width, per_head):
    hh = lax.broadcasted_iota(jnp.int32, (LANE, width), 0)
    cc = lax.broadcasted_iota(jnp.int32, (LANE, width), 1)
    return jnp.where((cc >= hh * per_head) & (cc < (hh + 1) * per_head), 1.0, 0.0).astype(BF16)


def _spread(v, e, terms):
    out = None
    rest = v
    for i in range(terms):
        piece = rest.astype(BF16)
        part = _dot(piece, e)
        out = part if out is None else out + part
        if i + 1 < terms:
            rest = rest - piece.astype(F32)
    return out


def _slab_rows(d):
    return d // LANE


def _pack_bf16_pairs(v):
    half = v.shape[1] // 2
    bits = lax.bitcast_convert_type(v.astype(BF16).astype(F32), jnp.uint32)
    return (bits[:, :half] >> 16) | (bits[:, half:] & jnp.uint32(0xFFFF0000))


def _unpack_bf16_pairs(words):
    return jnp.concatenate([lax.bitcast_convert_type(words << 16, F32),
                            lax.bitcast_convert_type(words & jnp.uint32(0xFFFF0000), F32)], axis=1)


def _slabs_to_rows(ref2d, base, rows, srows):
    return jnp.concatenate([ref2d[pl.ds(base + c, rows, stride=srows), :] for c in range(srows)], axis=1)


def _rows_to_slabs(ref2d, base, val, srows):
    rows = val.shape[0]
    for c in range(srows):
        ref2d[pl.ds(base + c, rows, stride=srows), :] = val[:, c * LANE:(c + 1) * LANE]


WPREP_ROWS = 256


def _wprep_kernel(wt_hbm, out_ref, buf, sem):
    l, j = pl.program_id(0), pl.program_id(1)
    nj = pl.num_programs(1)
    step = l * nj + j
    slot = step % 2
    n_lo = DT_LO // WPREP_ROWS
    n_main = n_lo + (wt_hbm.shape[1] - DT_HI) // WPREP_ROWS

    def block_copy(st, sl):
        jj = st % nj
        src = jnp.where(jj < n_lo, jj * WPREP_ROWS,
                        jnp.where(jj < n_main, DT_HI + (jj - n_lo) * WPREP_ROWS, DT_LO))
        return pltpu.make_async_copy(wt_hbm.at[st // nj, pl.ds(pl.multiple_of(src, 8), WPREP_ROWS), :],
                                     buf.at[sl], sem.at[sl])

    @pl.when(step == 0)
    def _():
        block_copy(0, 0).start()

    @pl.when(step + 1 < pl.num_programs(0) * nj)
    def _():
        block_copy(step + 1, 1 - slot).start()

    block_copy(step, slot).wait()

    @pl.when(j < n_main)
    def _():
        out_ref[...] = buf[slot].astype(BF16)

    @pl.when(j >= n_main)
    def _():
        pad = jnp.zeros((LANE - SSD_HEADS_PER_GROUP, out_ref.shape[1]), F32)
        parts = []
        for g in range(SSD_GROUPS):
            parts += [buf[slot, g * SSD_HEADS_PER_GROUP:(g + 1) * SSD_HEADS_PER_GROUP, :], pad]
        out_ref[...] = jnp.concatenate(parts, axis=0).astype(BF16)


def _wprep(wt):
    depth, nin, d = wt.shape
    nout = nin - (DT_HI - DT_LO) + SSD_GROUPS * LANE
    assert DT_LO % WPREP_ROWS == 0 and (nin - DT_HI) % WPREP_ROWS == 0 and SSD_GROUPS * LANE == WPREP_ROWS
    return pl.pallas_call(
        _wprep_kernel,
        grid=(depth, nout // WPREP_ROWS),
        in_specs=[pl.BlockSpec(memory_space=pl.ANY)],
        out_specs=pl.BlockSpec((None, WPREP_ROWS, d), lambda l, j: (l, j, 0)),
        out_shape=jax.ShapeDtypeStruct((depth, nout, d), BF16),
        scratch_shapes=[pltpu.VMEM((2, WPREP_ROWS, d), F32), pltpu.SemaphoreType.DMA((2,))],
        compiler_params=_cparams("arbitrary", "arbitrary"),
        name="wprep",
    )(wt)


def _norm_kernel(*refs, combine, write_h):
    if combine:
        h_ref, o0_ref, o1_ref, rt_ref, nw_ref = refs[:5]
        outs = refs[5:]
        rows = h_ref.shape[0]
        srows = _slab_rows(h_ref.shape[1] // 2)
        rt = rt_ref[...]
        h = (h_ref[...] + rt[:, 2:3] * _unpack_bf16_pairs(_slabs_to_rows(o0_ref, 0, rows, srows))
             + rt[:, 3:4] * _unpack_bf16_pairs(_slabs_to_rows(o1_ref, 0, rows, srows)))
    else:
        h_ref, nw_ref = refs[:2]
        outs = refs[2:]
        h = h_ref[...]
    u_ref = outs[0]
    u_ref[...] = _rms(h, nw_ref[...]).astype(u_ref.dtype)
    if write_h:
        outs[1][...] = h


def _norm_pass(h, nw, moe, out_dtype, write_h):
    n, d = h.shape
    combine = moe is not None
    tm = min(512, n)
    row = lambda i: (i, 0)
    in_specs = [pl.BlockSpec((tm, d), row)]
    args = [h]
    if combine:
        o2d, rt = moe
        srows = _slab_rows(d // 2)
        nt = n // tm
        in_specs += [pl.BlockSpec((tm * srows, LANE), row),
                     pl.BlockSpec((tm * srows, LANE), lambda i: (nt + i, 0)),
                     pl.BlockSpec((tm, LANE), row)]
        args += [o2d, o2d, rt]
    in_specs.append(pl.BlockSpec((1, d), lambda i: (0, 0)))
    args.append(nw)
    out_shape = [jax.ShapeDtypeStruct((n, d), out_dtype)]
    out_specs = [pl.BlockSpec((tm, d), row)]
    if write_h:
        out_shape.append(jax.ShapeDtypeStruct((n, d), F32))
        out_specs.append(pl.BlockSpec((tm, d), row))
    return pl.pallas_call(
        functools.partial(_norm_kernel, combine=combine, write_h=write_h),
        grid=(n // tm,),
        in_specs=in_specs, out_specs=out_specs, out_shape=out_shape,
        compiler_params=_cparams("arbitrary"),
        name="norm_pass",
    )(*args)


def _inproj_kernel(u_ref, wt_ref, p_ref):
    p_ref[...] = _dot_nt(u_ref[...], wt_ref[...])


def _inproj(u, wt, layer):
    n, d = u.shape
    np_ = wt.shape[1]
    tm = min(1024, n)
    tn = np_ // 3
    return pl.pallas_call(
        _inproj_kernel,
        grid=(np_ // tn, n // tm),
        in_specs=[pl.BlockSpec((tm, d), lambda j, i: (i, 0)),
                  pl.BlockSpec((None, tn, d), lambda j, i: (layer, j, 0))],
        out_specs=pl.BlockSpec((tm, tn), lambda j, i: (i, j)),
        out_shape=jax.ShapeDtypeStruct((n, np_), F32),
        compiler_params=_cparams("arbitrary", "arbitrary"),
        name="inproj",
    )(u, wt)


def _ret_kernel(q_ref, k_ref, v_ref, g_ref, cos_ref, sin_ref, o_ref, s_scr):
    t = q_ref.shape[0]
    ii, jj = _tri_incl()
    causal = ii >= jj
    dist = (ii - jj).astype(F32)
    iif = ii.astype(F32)
    consts = []
    for hd in range(RET_HEADS):
        lg = math.log(1.0 - 2.0 ** (-5.0 - hd))
        dmat = jnp.where(causal, jnp.exp(jnp.where(causal, dist * lg, 0.0)), 0.0)
        ecum = jnp.exp((iif + 1.0) * lg)
        wk = jnp.exp((CHUNK - 1.0 - iif) * lg)
        consts.append((dmat, ecum, wk, math.exp(CHUNK * lg)))
    scale = HEAD_DIM ** -0.5
    s_scr[...] = jnp.zeros_like(s_scr)

    def step(c, carry):
        r = pl.ds(pl.multiple_of(c * CHUNK, CHUNK), CHUNK)
        cs, sn = cos_ref[r, :], sin_ref[r, :]
        for hd in range(RET_HEADS):
            dmat, ecum, wk, elast = consts[hd]
            cols = slice(hd * HEAD_DIM, (hd + 1) * HEAD_DIM)
            q, k = q_ref[r, cols], k_ref[r, cols]
            qr = q * cs + pltpu.roll(q, HEAD_DIM // 2, 1) * sn
            kr = (k * cs + pltpu.roll(k, HEAD_DIM // 2, 1) * sn) * scale
            vb = v_ref[r, cols].astype(BF16)
            s = s_scr[hd]
            scores = _dot_nt(qr.astype(BF16), kr.astype(BF16)) * dmat
            out = _dot(scores.astype(BF16), vb) + _dot((qr * ecum).astype(BF16), s.astype(BF16))
            s_scr[hd] = elast * s + _dot_tn((kr * wk).astype(BF16), vb)
            o_ref[r, cols] = (_silu(g_ref[r, cols]) * _rms(out)).astype(o_ref.dtype)
        return carry

    lax.fori_loop(0, t // CHUNK, step, 0, unroll=2)


def _retention(p, cos_t, sin_t, bsz, t):
    n = bsz * t
    width = RET_HEADS * HEAD_DIM
    blk = lambda off: pl.BlockSpec((t, width), lambda b, off=off: (b, off))
    tab = pl.BlockSpec((t, HEAD_DIM), lambda b: (0, 0))
    return pl.pallas_call(
        _ret_kernel,
        grid=(bsz,),
        in_specs=[blk(0), blk(1), blk(2), blk(3), tab, tab],
        out_specs=pl.BlockSpec((t, width), lambda b: (b, 0)),
        out_shape=jax.ShapeDtypeStruct((n, width), BF16),
        scratch_shapes=[pltpu.VMEM((RET_HEADS, HEAD_DIM, HEAD_DIM), F32)],
        compiler_params=_cparams("arbitrary"),
        name="retention",
    )(p, p, p, p, cos_t, sin_t)


def _ssd_kernel(z_ref, x_ref, b_ref, c_ref, dt_ref, cwx_ref, cwb_ref, cwc_ref, cbx_ref, cbb_ref, cbc_ref,
                dtb_ref, alog_ref, dsk_ref, nw_ref, o_ref, s_scr):
    t = z_ref.shape[0]
    ii, jj = _tri_incl()
    causal = ii >= jj
    tri = jnp.where(causal, 1.0, 0.0).astype(F32)
    lane_lo = lax.broadcasted_iota(jnp.int32, (CHUNK, LANE), 1) < SSD_HEADDIM
    neg_a = -jnp.exp(alog_ref[...])
    gw = x_ref.shape[1]
    e_pairs = _lane_spread_matrix(SSD_HEADS_PER_GROUP * CHUNK, CHUNK)
    e_heads = _lane_spread_matrix(gw, SSD_HEADDIM)
    s_scr[...] = jnp.zeros_like(s_scr)

    def conv(ref, w_ref, bias_ref, c):
        r = pl.ds(pl.multiple_of(c * CHUNK, CHUNK), CHUNK)
        rp = pl.ds(pl.multiple_of(jnp.maximum(c * CHUNK - 8, 0), 8), 8)
        cur = ref[r, :]
        ext = jnp.concatenate([jnp.where(c > 0, ref[rp, :], 0.0), cur], axis=0)
        w = w_ref[...]
        acc = bias_ref[...] + w[SSD_CONV - 1:SSD_CONV, :] * cur
        for lag in range(1, SSD_CONV):
            shifted = pltpu.roll(ext, lag, 0)[8:8 + CHUNK, :]
            acc = acc + w[SSD_CONV - 1 - lag:SSD_CONV - lag, :] * shifted
        return _silu(acc)

    def step(c, carry):
        r = pl.ds(pl.multiple_of(c * CHUNK, CHUNK), CHUNK)
        xs = conv(x_ref, cwx_ref, cbx_ref, c)
        bm = conv(b_ref, cwb_ref, cbb_ref, c).astype(BF16)
        cm = conv(c_ref, cwc_ref, cbc_ref, c).astype(BF16)
        xr = dt_ref[r, :] + dtb_ref[...]
        dt = jnp.maximum(xr, 0.0) + jnp.log1p(jnp.exp(-jnp.abs(xr)))
        la = dt * neg_a
        cum = jnp.dot(tri, la, precision=HIGHEST, preferred_element_type=F32) * LOG2E
        last = cum[CHUNK - 1:CHUNK, :]
        wj = dt * jnp.exp2(last - cum)
        ecum = jnp.exp2(cum)
        cum_t = cum.T
        dt_t = dt.T
        gm = jnp.where(causal, _dot_nt(cm, bm), 0.0)
        s = s_scr[...]
        cs = _dot(cm, s.astype(BF16))
        cix = _spread(cum, e_pairs, 3)
        ecx = _spread(ecum, e_heads, 2)
        wjx = _spread(wj, e_heads, 2)
        ys = []
        for pr in range(SSD_HEADS_PER_GROUP // 2):
            xp = xs[:, pr * LANE:(pr + 1) * LANE]
            acc = None
            for half in range(2):
                hd = 2 * pr + half
                ci = cix[:, hd * CHUNK:(hd + 1) * CHUNK]
                dec = jnp.exp2(jnp.minimum(ci - cum_t[hd:hd + 1, :], 0.0))
                m = gm * dec * dt_t[hd:hd + 1, :]
                sel = lane_lo if half == 0 else jnp.logical_not(lane_lo)
                y = _dot(m.astype(BF16), jnp.where(sel, xp, 0.0).astype(BF16))
                acc = y if acc is None else acc + y
            ys.append(acc)
        y = jnp.concatenate(ys, axis=1) + ecx * cs
        s_scr[...] = ecx[CHUNK - 1:CHUNK, :] * s + _dot_tn(bm, (wjx * xs).astype(BF16))
        y = (y + dsk_ref[...] * xs) * _silu(z_ref[r, :])
        o_ref[r, :] = (_rms(y) * nw_ref[...]).astype(o_ref.dtype)
        return carry

    lax.fori_loop(0, t // CHUNK, step, 0, unroll=2)


def _ssd(p, conv_w, conv_b, dtb, alog, dsk, nw, bsz, t):
    n = bsz * t
    gw = SSD_HEADS_PER_GROUP * SSD_HEADDIM
    st = HEAD_DIM
    z_off, x_off = 2048 // gw, 3072 // gw
    b_off, c_off = 4096 // st, 4352 // st
    dt_off = (p.shape[1] - SSD_GROUPS * LANE) // LANE
    xw = SSD_GROUPS * gw
    par = lambda shape, f: pl.BlockSpec(shape, f)
    in_specs = [
        par((t, gw), lambda b, g: (b, z_off + g)),
        par((t, gw), lambda b, g: (b, x_off + g)),
        par((t, st), lambda b, g: (b, b_off + g)),
        par((t, st), lambda b, g: (b, c_off + g)),
        par((t, LANE), lambda b, g: (b, dt_off + g)),
        par((SSD_CONV, gw), lambda b, g: (0, g)),
        par((SSD_CONV, st), lambda b, g: (0, xw // st + g)),
        par((SSD_CONV, st), lambda b, g: (0, xw // st + SSD_GROUPS + g)),
        par((1, gw), lambda b, g: (0, g)),
        par((1, st), lambda b, g: (0, xw // st + g)),
        par((1, st), lambda b, g: (0, xw // st + SSD_GROUPS + g)),
        par((1, LANE), lambda b, g: (0, g)),
        par((1, LANE), lambda b, g: (0, g)),
        par((1, gw), lambda b, g: (0, g)),
        par((1, gw), lambda b, g: (0, g)),
    ]
    return pl.pallas_call(
        _ssd_kernel,
        grid=(bsz, SSD_GROUPS),
        in_specs=in_specs,
        out_specs=pl.BlockSpec((t, gw), lambda b, g: (b, g)),
        out_shape=jax.ShapeDtypeStruct((n, SSD_GROUPS * gw), BF16),
        scratch_shapes=[pltpu.VMEM((st, gw), F32)],
        compiler_params=_cparams("arbitrary", "arbitrary"),
        name="ssd",
    )(p, p, p, p, p, conv_w, conv_w, conv_w, conv_b, conv_b, conv_b, dtb, alog, dsk, nw)


def _hgrn_kernel(q_ref, f_ref, i_ref, g_ref, lb_ref, nw_ref, o_ref, st_scr, cum_scr, *, layer):
    t = q_ref.shape[0]
    lbm = lb_ref[...]
    depth = lbm.shape[0]
    mx = lbm[0:1, :]
    for i in range(1, depth):
        mx = jnp.maximum(mx, lbm[i:i + 1, :])
    ex = [jnp.exp(lbm[i:i + 1, :] - mx) for i in range(depth)]
    den = ex[0]
    for i in range(1, depth):
        den = den + ex[i]
    sm = [e / den for e in ex]
    csum = sm[0]
    for i in range(1, layer + 1):
        csum = csum + sm[i]
    lb = jnp.maximum(csum - sm[0], 0.0)
    lb_floor = jnp.maximum(lb, LB_FLOOR)
    oml = 1.0 - lb

    ii, jj = _tri_incl()
    tri = jnp.where(ii >= jj, 1.0, 0.0).astype(F32)
    lvl = jnp.where(ii > jj, 31 - lax.clz(ii ^ jj), -1)
    eye = ii == jj
    width = q_ref.shape[1]
    hcols = [slice(c0, c0 + HEAD_DIM) for c0 in range(0, width, HEAD_DIM)]
    row = lax.broadcasted_iota(jnp.int32, (CHUNK, width), 0)
    scale = HEAD_DIM ** -0.5
    nlev = int(math.log2(CHUNK))
    st_scr[...] = jnp.zeros_like(st_scr)

    def step(c, carry):
        r = pl.ds(pl.multiple_of(c * CHUNK, CHUNK), CHUNK)
        q = _silu(q_ref[r, :]) * scale
        f = f_ref[r, :]
        ef = jnp.exp(-jnp.abs(f))
        big = 1.0 / (1.0 + ef)
        small = ef * big
        sig = jnp.where(f >= 0.0, big, small)
        sig_m = jnp.where(f >= 0.0, small, big)
        lf = jnp.log(lb_floor + oml * sig)
        kk = oml * sig_m
        vb = i_ref[r, :].astype(BF16)
        cum = jnp.dot(tri, lf, precision=HIGHEST, preferred_element_type=F32) * LOG2E
        cum_scr[...] = cum
        qb, kb = q.astype(BF16), kk.astype(BF16)
        scores = [jnp.where(eye, _dot_nt(qb[:, cs], kb[:, cs]), 0.0) for cs in hcols]
        for lv in range(nlev):
            s = 1 << lv
            if 2 * s >= 8:
                ref = jnp.concatenate(
                    [jnp.broadcast_to(cum_scr[g0 * 2 * s + s - 1:g0 * 2 * s + s, :], (2 * s, width))
                     for g0 in range(CHUNK // (2 * s))], axis=0)
            elif s == 2:
                m4 = row & 3
                ref = jnp.where(m4 == 0, pltpu.roll(cum, CHUNK - 1, 0),
                                jnp.where(m4 == 1, cum,
                                          jnp.where(m4 == 2, pltpu.roll(cum, 1, 0), pltpu.roll(cum, 2, 0))))
            else:
                ref = jnp.where((row & 1) == 1, pltpu.roll(cum, 1, 0), cum)
            dlt = cum - ref
            e = jnp.exp2(jnp.minimum(dlt, -dlt))
            qs = (q * e).astype(BF16)
            ks = (kk * e).astype(BF16)
            scores = [jnp.where(lvl == lv, _dot_nt(qs[:, cs], ks[:, cs]), sc) for cs, sc in zip(hcols, scores)]
        last = cum[CHUNK - 1:CHUNK, :]
        qe = (q * jnp.exp2(cum)).astype(BF16)
        kw = (kk * jnp.exp2(last - cum)).astype(BF16)
        elast = jnp.exp2(last)
        outs = []
        for hd, cs in enumerate(hcols):
            st = st_scr[hd]
            out = _dot(scores[hd].astype(BF16), vb[:, cs]) + _dot_nt(qe[:, cs], st.astype(BF16))
            st_scr[hd] = st * elast[:, cs] + _dot_tn(vb[:, cs], kw[:, cs])
            outs.append(_rms(out))
        o = jnp.concatenate(outs, axis=1) * nw_ref[...]
        o_ref[r, :] = (_silu(g_ref[r, :]) * o).astype(o_ref.dtype)
        return carry

    lax.fori_loop(0, t // CHUNK, step, 0, unroll=2)


def _hgrn(p, lbounds, nw, layer, bsz, t):
    n = bsz * t
    width = HGRN_HEADS * HEAD_DIM
    base = DT_LO // width
    blk = lambda off: pl.BlockSpec((t, width), lambda b, off=off: (b, base + off))
    return pl.pallas_call(
        functools.partial(_hgrn_kernel, layer=layer),
        grid=(bsz,),
        in_specs=[blk(0), blk(1), blk(2), blk(3),
                  pl.BlockSpec((lbounds.shape[0], width), lambda b: (0, 0)),
                  pl.BlockSpec((1, width), lambda b: (0, 0))],
        out_specs=pl.BlockSpec((t, width), lambda b: (b, 0)),
        out_shape=jax.ShapeDtypeStruct((n, width), BF16),
        scratch_shapes=[pltpu.VMEM((HGRN_HEADS, HEAD_DIM, HEAD_DIM), F32), pltpu.VMEM((CHUNK, width), F32)],
        compiler_params=_cparams("arbitrary"),
        name="hgrn2",
    )(p, p, p, p, lbounds, nw)


def _outproj_kernel(a_ref, b_ref, c_ref, h_ref, w_ref, nw_ref, wr_ref, br_ref, hout_ref, u_ref, lg_ref):
    xrows = _slab_rows(h_ref.shape[1] // 2)
    half = h_ref.shape[0] // 2
    accs = []
    for r0 in (0, half):
        rs = slice(r0, r0 + half)
        mix = jnp.concatenate([a_ref[rs, :], b_ref[rs, :], c_ref[rs, :]], axis=1)
        accs.append(_dot(mix, w_ref[...]))
    for r0, acc in zip((0, half), accs):
        rs = slice(r0, r0 + half)
        h = h_ref[rs, :] + acc
        hout_ref[rs, :] = h
        u = _rms(h, nw_ref[...])
        u_hi = u.astype(BF16)
        u_lo = (u - u_hi.astype(F32)).astype(BF16)
        t2 = _dot(u_hi, wr_ref[...])
        lg_ref[rs, :] = t2[:, :LANE] + t2[:, LANE:] + _dot(u_lo, wr_ref[:, :LANE]) + br_ref[...]
        _rows_to_slabs(u_ref, r0 * xrows, _pack_bf16_pairs(u), xrows)


def _outproj(o_a, o_b, o_c, h, w_out, nw, w_r, b_r):
    n, d = h.shape
    tm = min(512, n)
    srows = _slab_rows(d // 2)
    row = lambda i: (i, 0)
    fix = lambda i: (0, 0)
    return pl.pallas_call(
        _outproj_kernel,
        grid=(n // tm,),
        in_specs=[pl.BlockSpec((tm, o_a.shape[1]), row), pl.BlockSpec((tm, o_b.shape[1]), row),
                  pl.BlockSpec((tm, o_c.shape[1]), row), pl.BlockSpec((tm, d), row),
                  pl.BlockSpec(w_out.shape, fix), pl.BlockSpec((1, d), fix),
                  pl.BlockSpec(w_r.shape, fix), pl.BlockSpec((1, LANE), fix)],
        out_specs=[pl.BlockSpec((tm, d), row), pl.BlockSpec((tm * srows, LANE), row),
                   pl.BlockSpec((tm, LANE), row)],
        out_shape=[jax.ShapeDtypeStruct((n, d), F32), jax.ShapeDtypeStruct((n * srows, LANE), jnp.uint32),
                   jax.ShapeDtypeStruct((n, LANE), F32)],
        compiler_params=_cparams("arbitrary"),
        name="outproj",
    )(o_a, o_b, o_c, h, w_out, nw, w_r, b_r)


def _route_kernel(lg_ref, rt_ref, rtt_ref, cnt_ref, carry_scr):
    tr = lg_ref.shape[0]

    @pl.when(pl.program_id(0) == 0)
    def _():
        carry_scr[...] = jnp.zeros_like(carry_scr)

    lg = lg_ref[...]
    lane = lax.broadcasted_iota(jnp.int32, (tr, LANE), 1)
    neg = -jnp.inf
    big = jnp.int32(LANE)

    def first_max(vals):
        m = jnp.max(vals, axis=-1, keepdims=True)
        idx = jnp.min(jnp.where(vals == m, lane, big), axis=-1, keepdims=True)
        return m, idx

    gl = jnp.where(lane < N_GROUPS, lg, neg)
    gmax, gidx = first_max(gl)
    gate = 1.0 / jnp.sum(jnp.exp(gl - gmax), axis=-1, keepdims=True)
    lo = N_GROUPS + EXPERTS_PER_GROUP * gidx
    el = jnp.where((lane >= lo) & (lane < lo + EXPERTS_PER_GROUP), lg, neg)
    m1, i1 = first_max(el)
    m2, i2 = first_max(jnp.where(lane == i1, neg, el))
    e21 = jnp.exp(m2 - m1)
    w1 = gate * (1.0 / (1.0 + e21))
    w2 = gate * (e21 / (1.0 + e21))

    oh1 = lane == i1
    oh2 = lane == i2
    m = jnp.where(oh1 | oh2, 1.0, 0.0)
    ti = lax.broadcasted_iota(jnp.int32, (tr, tr), 0)
    tj = lax.broadcasted_iota(jnp.int32, (tr, tr), 1)
    before = _dot(jnp.where(ti > tj, 1.0, 0.0).astype(BF16), m.astype(BF16)) + carry_scr[0:1, :]
    r1 = jnp.sum(jnp.where(oh1, before, 0.0), axis=-1, keepdims=True)
    r2 = jnp.sum(jnp.where(oh2, before, 0.0), axis=-1, keepdims=True)
    total = carry_scr[0:1, :] + jnp.sum(m, axis=0, keepdims=True)
    carry_scr[...] = jnp.broadcast_to(total, carry_scr.shape)
    cnt_ref[...] = jnp.broadcast_to(total, cnt_ref.shape)

    e1 = (i1 - N_GROUPS).astype(F32)
    e2 = (i2 - N_GROUPS).astype(F32)
    out = jnp.zeros((tr, LANE), F32)
    for pos, val in enumerate((e1, e2, w1, w2, r1, r2)):
        out = jnp.where(lane == pos, val, out)
    rt_ref[...] = out
    rtt_ref[...] = out.T[0:rtt_ref.shape[0], :]


def _route(logits):
    n = logits.shape[0]
    tr = min(ROUTE_TILE, n)
    return pl.pallas_call(
        _route_kernel,
        grid=(n // tr,),
        in_specs=[pl.BlockSpec((tr, LANE), lambda i: (i, 0))],
        out_specs=[pl.BlockSpec((tr, LANE), lambda i: (i, 0)), pl.BlockSpec((8, tr), lambda i: (0, i)),
                   pl.BlockSpec((8, LANE), lambda i: (0, 0))],
        out_shape=[jax.ShapeDtypeStruct((n, LANE), F32), jax.ShapeDtypeStruct((8, n), F32),
                   jax.ShapeDtypeStruct((8, LANE), F32)],
        scratch_shapes=[pltpu.VMEM((8, LANE), F32)],
        compiler_params=_cparams("arbitrary"),
        name="route",
    )(logits)


def _ffn_kernel(blk_e_ref, nv_ref, code_ref, u_hbm, wg_ref, wu_ref, wd_ref, o_hbm,
                xbuf, ybuf, wgb, wub, wdb, gsem, ssem, *, tb, xrows, yrows):
    b = pl.program_id(0)
    n_tok = u_hbm.shape[0] // xrows
    slot = b % 2
    prev = jnp.maximum(b - 1, 0)
    nv = nv_ref[b]
    nv_prev = jnp.where(b > 0, nv_ref[prev], 0)

    def slab(ref, srows, idx, count=1):
        return ref.at[pl.ds(pl.multiple_of(idx * srows, srows), count * srows)]

    def gather_row(blk, sl, r, lane):
        tok = code_ref[blk * tb + r] & (n_tok - 1)
        pltpu.make_async_copy(slab(u_hbm, xrows, tok), slab(xbuf, xrows, sl * tb + r),
                              gsem.at[sl]).start(priority=lane % 2)

    def scatter_row(blk, sl, r, dst, lane):
        pltpu.make_async_copy(slab(ybuf, yrows, sl * tb + r), slab(o_hbm, yrows, dst),
                              ssem.at[sl]).start(priority=lane % 2)

    def wait_gather(sl):
        pltpu.make_async_copy(slab(u_hbm, xrows, 0, tb), slab(xbuf, xrows, sl * tb, tb), gsem.at[sl]).wait()

    def wait_scatter(sl):
        pltpu.make_async_copy(slab(ybuf, yrows, sl * tb, tb), slab(o_hbm, yrows, 0, tb), ssem.at[sl]).wait()

    def row_loop(fn):
        def grp(i, c):
            for k in range(DMA_UNROLL):
                fn(i * DMA_UNROLL + k, k)
            return c
        lax.fori_loop(0, tb // DMA_UNROLL, grp, 0)

    @pl.when(b == 0)
    def _():
        ybuf[...] = jnp.zeros_like(ybuf)
        row_loop(lambda r, k: gather_row(0, 0, r, k))

    @pl.when((nv > 0) & ((b == 0) | (blk_e_ref[b] != blk_e_ref[prev])))
    def _():
        wgb[...] = wg_ref[...].astype(BF16)
        wub[...] = wu_ref[...].astype(BF16)
        wdb[...] = wd_ref[...].astype(BF16)

    @pl.when(nv > 0)
    def _():
        wait_gather(slot)
        words = _slabs_to_rows(xbuf, slot * tb * xrows, tb, xrows)
        x = _unpack_bf16_pairs(words).astype(BF16)
        has_prev = b > 0
        for r in range(tb):
            gather_row(b + 1, 1 - slot, r, r)
            dst = jnp.where(has_prev, code_ref[prev * tb + r], 2 * n_tok + r)
            scatter_row(prev, 1 - slot, r, dst, r)
        hdn = _silu(_dot(x, wgb[...])) * _dot(x, wub[...])
        y = _dot(hdn.astype(BF16), wdb[...])
        _rows_to_slabs(ybuf, slot * tb * yrows, _pack_bf16_pairs(y), yrows)
        wait_scatter(1 - slot)

    @pl.when((nv == 0) & (nv_prev > 0))
    def _():
        wait_gather(slot)
        row_loop(lambda r, k: scatter_row(prev, 1 - slot, r, code_ref[prev * tb + r], k))
        wait_scatter(1 - slot)


def _ffn(blk_e, nv, code, u2d, wg, wu, wd, layer):
    d, de = wg.shape[2], wg.shape[3]
    xrows = yrows = _slab_rows(d // 2)
    n = u2d.shape[0] // xrows
    assert n & (n - 1) == 0, "token index is taken from the slot code by masking"
    nblk = blk_e.shape[0]
    tb = MOE_BLOCK
    grid_spec = pltpu.PrefetchScalarGridSpec(
        num_scalar_prefetch=3,
        grid=(nblk,),
        in_specs=[pl.BlockSpec(memory_space=pl.ANY),
                  pl.BlockSpec((None, None, d, de), lambda b, be, nv, cd: (layer, be[b], 0, 0)),
                  pl.BlockSpec((None, None, d, de), lambda b, be, nv, cd: (layer, be[b], 0, 0)),
                  pl.BlockSpec((None, None, de, d), lambda b, be, nv, cd: (layer, be[b], 0, 0))],
        out_specs=pl.BlockSpec(memory_space=pl.ANY),
        scratch_shapes=[pltpu.VMEM((2 * tb * xrows, LANE), jnp.uint32),
                        pltpu.VMEM((2 * tb * yrows, LANE), jnp.uint32),
                        pltpu.VMEM((d, de), BF16), pltpu.VMEM((d, de), BF16), pltpu.VMEM((de, d), BF16),
                        pltpu.SemaphoreType.DMA((2,)), pltpu.SemaphoreType.DMA((2,))],
    )
    return pl.pallas_call(
        functools.partial(_ffn_kernel, tb=tb, xrows=xrows, yrows=yrows),
        grid_spec=grid_spec,
        out_shape=jax.ShapeDtypeStruct(((2 * n + tb) * yrows, LANE), jnp.uint32),
        compiler_params=_cparams("arbitrary"),
        name="moe_ffn",
    )(blk_e, nv, code, u2d, wg, wu, wd)


def _invert_kernel(dest_ref, code_ref, *, tb):
    n_slots, p_total = dest_ref.shape[0], code_ref.shape[0]

    def init(i, c):
        base = n_slots + ((i * DMA_UNROLL) & (tb - 1))
        for k in range(DMA_UNROLL):
            code_ref[i * DMA_UNROLL + k] = base + k
        return c
    lax.fori_loop(0, p_total // DMA_UNROLL, init, 0)

    def put(i, c):
        for k in range(DMA_UNROLL):
            s = i * DMA_UNROLL + k
            code_ref[dest_ref[s]] = s
        return c
    lax.fori_loop(0, n_slots // DMA_UNROLL, put, 0)


def _invert(dest, p_total, tb):
    assert tb & (tb - 1) == 0
    return pl.pallas_call(
        functools.partial(_invert_kernel, tb=tb),
        in_specs=[pl.BlockSpec(memory_space=pltpu.SMEM)],
        out_specs=pl.BlockSpec(memory_space=pltpu.SMEM),
        out_shape=jax.ShapeDtypeStruct((p_total,), jnp.int32),
        name="invert_slots",
    )(dest)


def _dispatch_tables(route_t, counts, n):
    tb = MOE_BLOCK
    e = route_t[0:2, :].astype(jnp.int32)
    rank = route_t[4:6, :].astype(jnp.int32)
    cnt = counts[0, N_GROUPS:N_GROUPS + N_EXPERTS].astype(jnp.int32)
    padded = ((cnt + tb - 1) // tb) * tb
    pends = jnp.cumsum(padded)
    pstarts = pends - padded
    ids = jnp.arange(N_EXPERTS, dtype=jnp.int32)[:, None, None]
    start_of = jnp.sum(jnp.where(ids == e[None], pstarts[:, None, None], 0), axis=0)
    dest = (start_of + rank).reshape(-1)
    p_total = 2 * n + N_EXPERTS * tb
    nblk = p_total // tb
    code = _invert(dest, p_total, tb)
    bstart = jnp.arange(nblk, dtype=jnp.int32) * tb
    blk_e = jnp.minimum(jnp.sum((bstart[:, None] >= pends[None, :]).astype(jnp.int32), axis=1), N_EXPERTS - 1)
    of_blk = blk_e[:, None] == jnp.arange(N_EXPERTS, dtype=jnp.int32)[None, :]
    cnt_b = jnp.sum(jnp.where(of_blk, cnt[None, :], 0), axis=1)
    pstart_b = jnp.sum(jnp.where(of_blk, pstarts[None, :], 0), axis=1)
    nv = jnp.clip(cnt_b - (bstart - pstart_b), 0, tb)
    nv = jnp.where(bstart < pends[-1], nv, 0).astype(jnp.int32)
    return blk_e, nv, code


def _pad_lanes(v, width):
    return jnp.pad(v, ((0, 0), (0, width - v.shape[1])))


def kernel(x, attn_norm_w, w_in, ssd_conv_w, ssd_conv_b, ssd_dt_bias, ssd_a_log, ssd_d, ssd_norm_w, hgrn_lower_bounds, hgrn_norm_w, w_out, ffn_norm_w, router_group_w, router_group_b, router_expert_w, router_expert_b, expert_w_gate, expert_w_up, expert_w_down, final_norm_w):
    bsz, t, d = x.shape
    n = bsz * t
    depth = w_in.shape[0]

    half = HEAD_DIM // 2
    inv = 1.0 / (ROPE_THETA ** (jnp.arange(half, dtype=F32) / half))
    ang = jnp.arange(t, dtype=F32)[:, None] * inv[None, :]
    cos_t = jnp.concatenate([jnp.cos(ang), jnp.cos(ang)], axis=1)
    sin_t = jnp.concatenate([-jnp.sin(ang), jnp.sin(ang)], axis=1)

    def group_lanes(v):
        v = v.reshape(SSD_GROUPS, SSD_HEADS_PER_GROUP)
        return _pad_lanes(v, LANE).reshape(1, SSD_GROUPS * LANE)

    h = x.reshape(n, d)
    w_in_b = _wprep(jnp.swapaxes(w_in, 1, 2))
    moe = None
    for l in range(depth):
        if moe is None:
            (u,) = _norm_pass(h, attn_norm_w[l][None, :], None, BF16, False)
        else:
            u, h = _norm_pass(h, attn_norm_w[l][None, :], moe, BF16, True)
        p = _inproj(u, w_in_b, l)

        o_ret = _retention(p, cos_t, sin_t, bsz, t)
        o_ssd = _ssd(p, ssd_conv_w[l], ssd_conv_b[l][None, :], group_lanes(ssd_dt_bias[l]),
                     group_lanes(ssd_a_log[l]), jnp.repeat(ssd_d[l], SSD_HEADDIM)[None, :],
                     ssd_norm_w[l][None, :], bsz, t)
        o_hgrn = _hgrn(p, hgrn_lower_bounds, hgrn_norm_w[l][None, :], l, bsz, t)

        w_r = _pad_lanes(jnp.concatenate([router_group_w[l], router_expert_w[l]], axis=1), LANE)
        w_r_hi = w_r.astype(BF16)
        w_r = jnp.concatenate([w_r_hi, (w_r - w_r_hi.astype(F32)).astype(BF16)], axis=1)
        b_r = _pad_lanes(jnp.concatenate([router_group_b[l], router_expert_b[l]])[None, :], LANE)
        h, u2d, logits = _outproj(o_ret, o_ssd, o_hgrn, h, w_out[l].astype(BF16), ffn_norm_w[l][None, :], w_r, b_r)
        route, route_t, counts = _route(logits)
        blk_e, nv, code = _dispatch_tables(route_t, counts, n)
        o2d = _ffn(blk_e, nv, code, u2d, expert_w_gate, expert_w_up, expert_w_down, l)
        moe = (o2d, route)

    (out,) = _norm_pass(h, final_norm_w[None, :], moe, F32, False)
    return out.reshape(bsz, t, d)
```

```python
import functools
import math

import jax
import jax.numpy as jnp
from jax import lax
from jax.experimental import pallas as pl
from jax.experimental.pallas import tpu as pltpu

F32 = jnp.float32
BF16 = jnp.bfloat16
HIGHEST = lax.Precision.HIGHEST

V7X_VMEM_BYTES = 64 * 1024 * 1024
VMEM_LIMIT = V7X_VMEM_BYTES - 8 * 1024 * 1024
LANE = 128

EPS = 1e-6
LOG2E = 1.4426950408889634
LB_FLOOR = 1e-30
ROPE_THETA = 10000.0

RET_HEADS = 4
HEAD_DIM = 128
SSD_HEADDIM = 64
SSD_GROUPS = 2
SSD_HEADS_PER_GROUP = 8
SSD_CONV = 4
HGRN_HEADS = 4
CHUNK = 128
N_GROUPS = 4
EXPERTS_PER_GROUP = 8
N_EXPERTS = N_GROUPS * EXPERTS_PER_GROUP
MOE_BLOCK = 256
DMA_UNROLL = 8
ROUTE_TILE = 512
DT_LO = 4608
DT_HI = DT_LO + SSD_GROUPS * SSD_HEADS_PER_GROUP


def _cparams(*sem):
    return pltpu.CompilerParams(dimension_semantics=sem, vmem_limit_bytes=VMEM_LIMIT)


def _rms(x, w=None):
    y = x * lax.rsqrt(jnp.mean(x * x, axis=-1, keepdims=True) + EPS)
    return y if w is None else y * w


def _sigmoid(x):
    return 1.0 / (1.0 + jnp.exp(-x))


def _silu(x):
    return x * _sigmoid(x)


def _dot(a, b):
    return jnp.dot(a, b, preferred_element_type=F32)


def _dot_nt(a, b):
    return lax.dot_general(a, b, (((1,), (1,)), ((), ())), preferred_element_type=F32)


def _dot_tn(a, b):
    return lax.dot_general(a, b, (((0,), (0,)), ((), ())), preferred_element_type=F32)


def _tri_incl():
    ii = lax.broadcasted_iota(jnp.int32, (CHUNK, CHUNK), 0)
    jj = lax.broadcasted_iota(jnp.int32, (CHUNK, CHUNK), 1)
    return ii, jj


def _lane_spread_matrix(width, per_head):
    hh = lax.broadcasted_iota(jnp.int32, (LANE, width), 0)
    cc = lax.broadcasted_iota(jnp.int32, (LANE, width), 1)
    return jnp.where((cc >= hh * per_head) & (cc < (hh + 1) * per_head), 1.0, 0.0).astype(BF16)


def _spread(v, e, terms):
    out = None
    rest = v
    for i in range(terms):
        piece = rest.astype(BF16)
        part = _dot(piece, e)
        out = part if out is None else out + part
        if i + 1 < terms:
            rest = rest - piece.astype(F32)
    return out


def _slab_rows(d):
    return d // LANE


def _pack_bf16_pairs(v):
    half = v.shape[1] // 2
    bits = lax.bitcast_convert_type(v.astype(BF16).astype(F32), jnp.uint32)
    return (bits[:, :half] >> 16) | (bits[:, half:] & jnp.uint32(0xFFFF0000))


def _unpack_bf16_pairs(words):
    return jnp.concatenate([lax.bitcast_convert_type(words << 16, F32),
                            lax.bitcast_convert_type(words & jnp.uint32(0xFFFF0000), F32)], axis=1)


def _slabs_to_rows(ref2d, base, rows, srows):
    return jnp.concatenate([ref2d[pl.ds(base + c, rows, stride=srows), :] for c in range(srows)], axis=1)


def _rows_to_slabs(ref2d, base, val, srows):
    rows = val.shape[0]
    for c in range(srows):
        ref2d[pl.ds(base + c, rows, stride=srows), :] = val[:, c * LANE:(c + 1) * LANE]


WPREP_ROWS = 256


def _wprep_kernel(wt_hbm, out_ref, buf, sem):
    l, j = pl.program_id(0), pl.program_id(1)
    nj = pl.num_programs(1)
    step = l * nj + j
    slot = step % 2
    n_lo = DT_LO // WPREP_ROWS
    n_main = n_lo + (wt_hbm.shape[1] - DT_HI) // WPREP_ROWS

    def block_copy(st, sl):
        jj = st % nj
        src = jnp.where(jj < n_lo, jj * WPREP_ROWS,
                        jnp.where(jj < n_main, DT_HI + (jj - n_lo) * WPREP_ROWS, DT_LO))
        return pltpu.make_async_copy(wt_hbm.at[st // nj, pl.ds(pl.multiple_of(src, 8), WPREP_ROWS), :],
                                     buf.at[sl], sem.at[sl])

    @pl.when(step == 0)
    def _():
        block_copy(0, 0).start()

    @pl.when(step + 1 < pl.num_programs(0) * nj)
    def _():
        block_copy(step + 1, 1 - slot).start()

    block_copy(step, slot).wait()

    @pl.when(j < n_main)
    def _():
        out_ref[...] = buf[slot].astype(BF16)

    @pl.when(j >= n_main)
    def _():
        pad = jnp.zeros((LANE - SSD_HEADS_PER_GROUP, out_ref.shape[1]), F32)
        parts = []
        for g in range(SSD_GROUPS):
            parts += [buf[slot, g * SSD_HEADS_PER_GROUP:(g + 1) * SSD_HEADS_PER_GROUP, :], pad]
        out_ref[...] = jnp.concatenate(parts, axis=0).astype(BF16)


def _wprep(wt):
    depth, nin, d = wt.shape
    nout = nin - (DT_HI - DT_LO) + SSD_GROUPS * LANE
    assert DT_LO % WPREP_ROWS == 0 and (nin - DT_HI) % WPREP_ROWS == 0 and SSD_GROUPS * LANE == WPREP_ROWS
    return pl.pallas_call(
        _wprep_kernel,
        grid=(depth, nout // WPREP_ROWS),
        in_specs=[pl.BlockSpec(memory_space=pl.ANY)],
        out_specs=pl.BlockSpec((None, WPREP_ROWS, d), lambda l, j: (l, j, 0)),
        out_shape=jax.ShapeDtypeStruct((depth, nout, d), BF16),
        scratch_shapes=[pltpu.VMEM((2, WPREP_ROWS, d), F32), pltpu.SemaphoreType.DMA((2,))],
        compiler_params=_cparams("arbitrary", "arbitrary"),
        name="wprep",
    )(wt)


def _norm_kernel(h_ref, nw_ref, u_ref):
    u_ref[...] = _rms(h_ref[...], nw_ref[...]).astype(u_ref.dtype)


def _norm_pass(h, nw, out_dtype):
    n, d = h.shape
    tm = min(512, n)
    row = lambda i: (i, 0)
    return pl.pallas_call(
        _norm_kernel,
        grid=(n // tm,),
        in_specs=[pl.BlockSpec((tm, d), row), pl.BlockSpec((1, d), lambda i: (0, 0))],
        out_specs=pl.BlockSpec((tm, d), row),
        out_shape=jax.ShapeDtypeStruct((n, d), out_dtype),
        compiler_params=_cparams("arbitrary"),
        name="norm_pass",
    )(h, nw)


def _inproj_kernel(u_ref, wt_ref, p_ref):
    p_ref[...] = _dot_nt(u_ref[...], wt_ref[...])


def _inproj(u, wt, layer):
    n, d = u.shape
    np_ = wt.shape[1]
    tm = min(1024, n)
    tn = np_ // 3
    return pl.pallas_call(
        _inproj_kernel,
        grid=(np_ // tn, n // tm),
        in_specs=[pl.BlockSpec((tm, d), lambda j, i: (i, 0)),
                  pl.BlockSpec((None, tn, d), lambda j, i: (layer, j, 0))],
        out_specs=pl.BlockSpec((tm, tn), lambda j, i: (i, j)),
        out_shape=jax.ShapeDtypeStruct((n, np_), F32),
        compiler_params=_cparams("arbitrary", "arbitrary"),
        name="inproj",
    )(u, wt)


def _ret_kernel(q_ref, k_ref, v_ref, g_ref, cos_ref, sin_ref, o_ref, s_scr):
    t = q_ref.shape[0]
    ii, jj = _tri_incl()
    causal = ii >= jj
    dist = (ii - jj).astype(F32)
    iif = ii.astype(F32)
    consts = []
    for hd in range(RET_HEADS):
        lg = math.log(1.0 - 2.0 ** (-5.0 - hd))
        dmat = jnp.where(causal, jnp.exp(jnp.where(causal, dist * lg, 0.0)), 0.0)
        ecum = jnp.exp((iif + 1.0) * lg)
        wk = jnp.exp((CHUNK - 1.0 - iif) * lg)
        consts.append((dmat, ecum, wk, math.exp(CHUNK * lg)))
    scale = HEAD_DIM ** -0.5
    s_scr[...] = jnp.zeros_like(s_scr)

    def step(c, carry):
        r = pl.ds(pl.multiple_of(c * CHUNK, CHUNK), CHUNK)
        cs, sn = cos_ref[r, :], sin_ref[r, :]
        for hd in range(RET_HEADS):
            dmat, ecum, wk, elast = consts[hd]
            cols = slice(hd * HEAD_DIM, (hd + 1) * HEAD_DIM)
            q, k = q_ref[r, cols], k_ref[r, cols]
            qr = q * cs + pltpu.roll(q, HEAD_DIM // 2, 1) * sn
            kr = (k * cs + pltpu.roll(k, HEAD_DIM // 2, 1) * sn) * scale
            vb = v_ref[r, cols].astype(BF16)
            s = s_scr[hd]
            scores = _dot_nt(qr.astype(BF16), kr.astype(BF16)) * dmat
            out = _dot(scores.astype(BF16), vb) + _dot((qr * ecum).astype(BF16), s.astype(BF16))
            s_scr[hd] = elast * s + _dot_tn((kr * wk).astype(BF16), vb)
            o_ref[r, cols] = (_silu(g_ref[r, cols]) * _rms(out)).astype(o_ref.dtype)
        return carry

    lax.fori_loop(0, t // CHUNK, step, 0, unroll=2)


def _retention(p, cos_t, sin_t, bsz, t):
    n = bsz * t
    width = RET_HEADS * HEAD_DIM
    blk = lambda off: pl.BlockSpec((t, width), lambda b, off=off: (b, off))
    tab = pl.BlockSpec((t, HEAD_DIM), lambda b: (0, 0))
    return pl.pallas_call(
        _ret_kernel,
        grid=(bsz,),
        in_specs=[blk(0), blk(1), blk(2), blk(3), tab, tab],
        out_specs=pl.BlockSpec((t, width), lambda b: (b, 0)),
        out_shape=jax.ShapeDtypeStruct((n, width), BF16),
        scratch_shapes=[pltpu.VMEM((RET_HEADS, HEAD_DIM, HEAD_DIM), F32)],
        compiler_params=_cparams("arbitrary"),
        name="retention",
    )(p, p, p, p, cos_t, sin_t)


def _ssd_kernel(z_ref, x_ref, b_ref, c_ref, dt_ref, cwx_ref, cwb_ref, cwc_ref, cbx_ref, cbb_ref, cbc_ref,
                dtb_ref, alog_ref, dsk_ref, nw_ref, o_ref, s_scr):
    t = z_ref.shape[0]
    ii, jj = _tri_incl()
    causal = ii >= jj
    tri = jnp.where(causal, 1.0, 0.0).astype(F32)
    lane_lo = lax.broadcasted_iota(jnp.int32, (CHUNK, LANE), 1) < SSD_HEADDIM
    neg_a = -jnp.exp(alog_ref[...])
    gw = x_ref.shape[1]
    e_pairs = _lane_spread_matrix(SSD_HEADS_PER_GROUP * CHUNK, CHUNK)
    e_heads = _lane_spread_matrix(gw, SSD_HEADDIM)
    s_scr[...] = jnp.zeros_like(s_scr)

    def conv(ref, w_ref, bias_ref, c):
        r = pl.ds(pl.multiple_of(c * CHUNK, CHUNK), CHUNK)
        rp = pl.ds(pl.multiple_of(jnp.maximum(c * CHUNK - 8, 0), 8), 8)
        cur = ref[r, :]
        ext = jnp.concatenate([jnp.where(c > 0, ref[rp, :], 0.0), cur], axis=0)
        w = w_ref[...]
        acc = bias_ref[...] + w[SSD_CONV - 1:SSD_CONV, :] * cur
        for lag in range(1, SSD_CONV):
            shifted = pltpu.roll(ext, lag, 0)[8:8 + CHUNK, :]
            acc = acc + w[SSD_CONV - 1 - lag:SSD_CONV - lag, :] * shifted
        return _silu(acc)

    def step(c, carry):
        r = pl.ds(pl.multiple_of(c * CHUNK, CHUNK), CHUNK)
        xs = conv(x_ref, cwx_ref, cbx_ref, c)
        bm = conv(b_ref, cwb_ref, cbb_ref, c).astype(BF16)
        cm = conv(c_ref, cwc_ref, cbc_ref, c).astype(BF16)
        xr = dt_ref[r, :] + dtb_ref[...]
        dt = jnp.maximum(xr, 0.0) + jnp.log1p(jnp.exp(-jnp.abs(xr)))
        la = dt * neg_a
        cum = jnp.dot(tri, la, precision=HIGHEST, preferred_element_type=F32) * LOG2E
        last = cum[CHUNK - 1:CHUNK, :]
        wj = dt * jnp.exp2(last - cum)
        ecum = jnp.exp2(cum)
        cum_t = cum.T
        dt_t = dt.T
        gm = jnp.where(causal, _dot_nt(cm, bm), 0.0)
        s = s_scr[...]
        cs = _dot(cm, s.astype(BF16))
        cix = _spread(cum, e_pairs, 3)
        ecx = _spread(ecum, e_heads, 2)
        wjx = _spread(wj, e_heads, 2)
        ys = []
        for pr in range(SSD_HEADS_PER_GROUP // 2):
            xp = xs[:, pr * LANE:(pr + 1) * LANE]
            acc = None
            for half in range(2):
                hd = 2 * pr + half
                ci = cix[:, hd * CHUNK:(hd + 1) * CHUNK]
                dec = jnp.exp2(jnp.minimum(ci - cum_t[hd:hd + 1, :], 0.0))
                m = gm * dec * dt_t[hd:hd + 1, :]
                sel = lane_lo if half == 0 else jnp.logical_not(lane_lo)
                y = _dot(m.astype(BF16), jnp.where(sel, xp, 0.0).astype(BF16))
                acc = y if acc is None else acc + y
            ys.append(acc)
        y = jnp.concatenate(ys, axis=1) + ecx * cs
        s_scr[...] = ecx[CHUNK - 1:CHUNK, :] * s + _dot_tn(bm, (wjx * xs).astype(BF16))
        y = (y + dsk_ref[...] * xs) * _silu(z_ref[r, :])
        o_ref[r, :] = (_rms(y) * nw_ref[...]).astype(o_ref.dtype)
        return carry

    lax.fori_loop(0, t // CHUNK, step, 0, unroll=2)


def _ssd(p, conv_w, conv_b, dtb, alog, dsk, nw, bsz, t):
    n = bsz * t
    gw = SSD_HEADS_PER_GROUP * SSD_HEADDIM
    st = HEAD_DIM
    z_off, x_off = 2048 // gw, 3072 // gw
    b_off, c_off = 4096 // st, 4352 // st
    dt_off = (p.shape[1] - SSD_GROUPS * LANE) // LANE
    xw = SSD_GROUPS * gw
    par = lambda shape, f: pl.BlockSpec(shape, f)
    in_specs = [
        par((t, gw), lambda b, g: (b, z_off + g)),
        par((t, gw), lambda b, g: (b, x_off + g)),
        par((t, st), lambda b, g: (b, b_off + g)),
        par((t, st), lambda b, g: (b, c_off + g)),
        par((t, LANE), lambda b, g: (b, dt_off + g)),
        par((SSD_CONV, gw), lambda b, g: (0, g)),
        par((SSD_CONV, st), lambda b, g: (0, xw // st + g)),
        par((SSD_CONV, st), lambda b, g: (0, xw // st + SSD_GROUPS + g)),
        par((1, gw), lambda b, g: (0, g)),
        par((1, st), lambda b, g: (0, xw // st + g)),
        par((1, st), lambda b, g: (0, xw // st + SSD_GROUPS + g)),
        par((1, LANE), lambda b, g: (0, g)),
        par((1, LANE), lambda b, g: (0, g)),
        par((1, gw), lambda b, g: (0, g)),
        par((1, gw), lambda b, g: (0, g)),
    ]
    return pl.pallas_call(
        _ssd_kernel,
        grid=(bsz, SSD_GROUPS),
        in_specs=in_specs,
        out_specs=pl.BlockSpec((t, gw), lambda b, g: (b, g)),
        out_shape=jax.ShapeDtypeStruct((n, SSD_GROUPS * gw), BF16),
        scratch_shapes=[pltpu.VMEM((st, gw), F32)],
        compiler_params=_cparams("arbitrary", "arbitrary"),
        name="ssd",
    )(p, p, p, p, p, conv_w, conv_w, conv_w, conv_b, conv_b, conv_b, dtb, alog, dsk, nw)


def _hgrn_kernel(q_ref, f_ref, i_ref, g_ref, lb_ref, nw_ref, o_ref, st_scr, cum_scr, *, layer):
    t = q_ref.shape[0]
    lbm = lb_ref[...]
    depth = lbm.shape[0]
    mx = lbm[0:1, :]
    for i in range(1, depth):
        mx = jnp.maximum(mx, lbm[i:i + 1, :])
    ex = [jnp.exp(lbm[i:i + 1, :] - mx) for i in range(depth)]
    den = ex[0]
    for i in range(1, depth):
        den = den + ex[i]
    sm = [e / den for e in ex]
    csum = sm[0]
    for i in range(1, layer + 1):
        csum = csum + sm[i]
    lb = jnp.maximum(csum - sm[0], 0.0)
    lb_floor = jnp.maximum(lb, LB_FLOOR)
    oml = 1.0 - lb

    ii, jj = _tri_incl()
    tri = jnp.where(ii >= jj, 1.0, 0.0).astype(F32)
    lvl = jnp.where(ii > jj, 31 - lax.clz(ii ^ jj), -1)
    eye = ii == jj
    width = q_ref.shape[1]
    hcols = [slice(c0, c0 + HEAD_DIM) for c0 in range(0, width, HEAD_DIM)]
    row = lax.broadcasted_iota(jnp.int32, (CHUNK, width), 0)
    scale = HEAD_DIM ** -0.5
    nlev = int(math.log2(CHUNK))
    st_scr[...] = jnp.zeros_like(st_scr)

    def step(c, carry):
        r = pl.ds(pl.multiple_of(c * CHUNK, CHUNK), CHUNK)
        q = _silu(q_ref[r, :]) * scale
        f = f_ref[r, :]
        ef = jnp.exp(-jnp.abs(f))
        big = 1.0 / (1.0 + ef)
        small = ef * big
        sig = jnp.where(f >= 0.0, big, small)
        sig_m = jnp.where(f >= 0.0, small, big)
        lf = jnp.log(lb_floor + oml * sig)
        kk = oml * sig_m
        vb = i_ref[r, :].astype(BF16)
        cum = jnp.dot(tri, lf, precision=HIGHEST, preferred_element_type=F32) * LOG2E
        cum_scr[...] = cum
        qb, kb = q.astype(BF16), kk.astype(BF16)
        scores = [jnp.where(eye, _dot_nt(qb[:, cs], kb[:, cs]), 0.0) for cs in hcols]
        for lv in range(nlev):
            s = 1 << lv
            if 2 * s >= 8:
                ref = jnp.concatenate(
                    [jnp.broadcast_to(cum_scr[g0 * 2 * s + s - 1:g0 * 2 * s + s, :], (2 * s, width))
                     for g0 in range(CHUNK // (2 * s))], axis=0)
            elif s == 2:
                m4 = row & 3
                ref = jnp.where(m4 == 0, pltpu.roll(cum, CHUNK - 1, 0),
                                jnp.where(m4 == 1, cum,
                                          jnp.where(m4 == 2, pltpu.roll(cum, 1, 0), pltpu.roll(cum, 2, 0))))
            else:
                ref = jnp.where((row & 1) == 1, pltpu.roll(cum, 1, 0), cum)
            dlt = cum - ref
            e = jnp.exp2(jnp.minimum(dlt, -dlt))
            qs = (q * e).astype(BF16)
            ks = (kk * e).astype(BF16)
            scores = [jnp.where(lvl == lv, _dot_nt(qs[:, cs], ks[:, cs]), sc) for cs, sc in zip(hcols, scores)]
        last = cum[CHUNK - 1:CHUNK, :]
        qe = (q * jnp.exp2(cum)).astype(BF16)
        kw = (kk * jnp.exp2(last - cum)).astype(BF16)
        elast = jnp.exp2(last)
        outs = []
        for hd, cs in enumerate(hcols):
            st = st_scr[hd]
            out = _dot(scores[hd].astype(BF16), vb[:, cs]) + _dot_nt(qe[:, cs], st.astype(BF16))
            st_scr[hd] = st * elast[:, cs] + _dot_tn(vb[:, cs], kw[:, cs])
            outs.append(_rms(out))
        o = jnp.concatenate(outs, axis=1) * nw_ref[...]
        o_ref[r, :] = (_silu(g_ref[r, :]) * o).astype(o_ref.dtype)
        return carry

    lax.fori_loop(0, t // CHUNK, step, 0, unroll=2)


def _hgrn(p, lbounds, nw, layer, bsz, t):
    n = bsz * t
    width = HGRN_HEADS * HEAD_DIM
    base = DT_LO // width
    blk = lambda off: pl.BlockSpec((t, width), lambda b, off=off: (b, base + off))
    return pl.pallas_call(
        functools.partial(_hgrn_kernel, layer=layer),
        grid=(bsz,),
        in_specs=[blk(0), blk(1), blk(2), blk(3),
                  pl.BlockSpec((lbounds.shape[0], width), lambda b: (0, 0)),
                  pl.BlockSpec((1, width), lambda b: (0, 0))],
        out_specs=pl.BlockSpec((t, width), lambda b: (b, 0)),
        out_shape=jax.ShapeDtypeStruct((n, width), BF16),
        scratch_shapes=[pltpu.VMEM((HGRN_HEADS, HEAD_DIM, HEAD_DIM), F32), pltpu.VMEM((CHUNK, width), F32)],
        compiler_params=_cparams("arbitrary"),
        name="hgrn2",
    )(p, p, p, p, lbounds, nw)


def _outproj_kernel(a_ref, b_ref, c_ref, h_ref, w_ref, nw_ref, wr_ref, br_ref, hout_ref, u_ref, lg_ref):
    xrows = _slab_rows(h_ref.shape[1] // 2)
    half = h_ref.shape[0] // 2
    accs = []
    for r0 in (0, half):
        rs = slice(r0, r0 + half)
        mix = jnp.concatenate([a_ref[rs, :], b_ref[rs, :], c_ref[rs, :]], axis=1)
        accs.append(_dot(mix, w_ref[...]))
    for r0, acc in zip((0, half), accs):
        rs = slice(r0, r0 + half)
        h = h_ref[rs, :] + acc
        hout_ref[rs, :] = h
        u = _rms(h, nw_ref[...])
        u_hi = u.astype(BF16)
        u_lo = (u - u_hi.astype(F32)).astype(BF16)
        t2 = _dot(u_hi, wr_ref[...])
        lg_ref[rs, :] = t2[:, :LANE] + t2[:, LANE:] + _dot(u_lo, wr_ref[:, :LANE]) + br_ref[...]
        _rows_to_slabs(u_ref, r0 * xrows, _pack_bf16_pairs(u), xrows)


def _outproj(o_a, o_b, o_c, h, w_out, nw, w_r, b_r):
    n, d = h.shape
    tm = min(512, n)
    srows = _slab_rows(d // 2)
    row = lambda i: (i, 0)
    fix = lambda i: (0, 0)
    return pl.pallas_call(
        _outproj_kernel,
        grid=(n // tm,),
        in_specs=[pl.BlockSpec((tm, o_a.shape[1]), row), pl.BlockSpec((tm, o_b.shape[1]), row),
                  pl.BlockSpec((tm, o_c.shape[1]), row), pl.BlockSpec((tm, d), row),
                  pl.BlockSpec(w_out.shape, fix), pl.BlockSpec((1, d), fix),
                  pl.BlockSpec(w_r.shape, fix), pl.BlockSpec((1, LANE), fix)],
        out_specs=[pl.BlockSpec((tm, d), row), pl.BlockSpec((tm * srows, LANE), row),
                   pl.BlockSpec((tm, LANE), row)],
        out_shape=[jax.ShapeDtypeStruct((n, d), F32), jax.ShapeDtypeStruct((n * srows, LANE), jnp.uint32),
                   jax.ShapeDtypeStruct((n, LANE), F32)],
        compiler_params=_cparams("arbitrary"),
        name="outproj",
    )(o_a, o_b, o_c, h, w_out, nw, w_r, b_r)


def _route_kernel(lg_ref, rt_ref, rtt_ref, cnt_ref, carry_scr):
    tr = lg_ref.shape[0]

    @pl.when(pl.program_id(0) == 0)
    def _():
        carry_scr[...] = jnp.zeros_like(carry_scr)

    lg = lg_ref[...]
    lane = lax.broadcasted_iota(jnp.int32, (tr, LANE), 1)
    neg = -jnp.inf
    big = jnp.int32(LANE)

    def first_max(vals):
        m = jnp.max(vals, axis=-1, keepdims=True)
        idx = jnp.min(jnp.where(vals == m, lane, big), axis=-1, keepdims=True)
        return m, idx

    gl = jnp.where(lane < N_GROUPS, lg, neg)
    gmax, gidx = first_max(gl)
    gate = 1.0 / jnp.sum(jnp.exp(gl - gmax), axis=-1, keepdims=True)
    lo = N_GROUPS + EXPERTS_PER_GROUP * gidx
    el = jnp.where((lane >= lo) & (lane < lo + EXPERTS_PER_GROUP), lg, neg)
    m1, i1 = first_max(el)
    m2, i2 = first_max(jnp.where(lane == i1, neg, el))
    e21 = jnp.exp(m2 - m1)
    w1 = gate * (1.0 / (1.0 + e21))
    w2 = gate * (e21 / (1.0 + e21))

    oh1 = lane == i1
    oh2 = lane == i2
    m = jnp.where(oh1 | oh2, 1.0, 0.0)
    ti = lax.broadcasted_iota(jnp.int32, (tr, tr), 0)
    tj = lax.broadcasted_iota(jnp.int32, (tr, tr), 1)
    before = _dot(jnp.where(ti > tj, 1.0, 0.0).astype(BF16), m.astype(BF16)) + carry_scr[0:1, :]
    r1 = jnp.sum(jnp.where(oh1, before, 0.0), axis=-1, keepdims=True)
    r2 = jnp.sum(jnp.where(oh2, before, 0.0), axis=-1, keepdims=True)
    total = carry_scr[0:1, :] + jnp.sum(m, axis=0, keepdims=True)
    carry_scr[...] = jnp.broadcast_to(total, carry_scr.shape)
    cnt_ref[...] = jnp.broadcast_to(total, cnt_ref.shape)

    e1 = (i1 - N_GROUPS).astype(F32)
    e2 = (i2 - N_GROUPS).astype(F32)
    out = jnp.zeros((tr, LANE), F32)
    for pos, val in enumerate((e1, e2, w1, w2, r1, r2)):
        out = jnp.where(lane == pos, val, out)
    rt_ref[...] = out
    rtt_ref[...] = out.T[0:rtt_ref.shape[0], :]


def _route(logits):
    n = logits.shape[0]
    tr = min(ROUTE_TILE, n)
    return pl.pallas_call(
        _route_kernel,
        grid=(n // tr,),
        in_specs=[pl.BlockSpec((tr, LANE), lambda i: (i, 0))],
        out_specs=[pl.BlockSpec((tr, LANE), lambda i: (i, 0)), pl.BlockSpec((8, tr), lambda i: (0, i)),
                   pl.BlockSpec((8, LANE), lambda i: (0, 0))],
        out_shape=[jax.ShapeDtypeStruct((n, LANE), F32), jax.ShapeDtypeStruct((8, n), F32),
                   jax.ShapeDtypeStruct((8, LANE), F32)],
        scratch_shapes=[pltpu.VMEM((8, LANE), F32)],
        compiler_params=_cparams("arbitrary"),
        name="route",
    )(logits)


def _ffn_kernel(blk_e_ref, nv_ref, code_ref, u_hbm, wg_ref, wu_ref, wd_ref, y_ref,
                xbuf, wgb, wub, wdb, gsem, *, tb, xrows):
    b = pl.program_id(0)
    n_tok = u_hbm.shape[0] // xrows
    slot = b % 2
    prev = jnp.maximum(b - 1, 0)
    nv = nv_ref[b]
    nv_prev = jnp.where(b > 0, nv_ref[prev], 0)

    def slab(ref, idx, count=1):
        return ref.at[pl.ds(pl.multiple_of(idx * xrows, xrows), count * xrows)]

    def gather_row(blk, sl, r, lane):
        tok = code_ref[blk * tb + r] & (n_tok - 1)
        pltpu.make_async_copy(slab(u_hbm, tok), slab(xbuf, sl * tb + r), gsem.at[sl]).start(priority=lane % 2)

    def wait_gather(sl):
        pltpu.make_async_copy(slab(u_hbm, 0, tb), slab(xbuf, sl * tb, tb), gsem.at[sl]).wait()

    @pl.when(b == 0)
    def _():
        def grp(i, c):
            for k in range(DMA_UNROLL):
                gather_row(0, 0, i * DMA_UNROLL + k, k)
            return c
        lax.fori_loop(0, tb // DMA_UNROLL, grp, 0)

    @pl.when((nv > 0) & ((b == 0) | (blk_e_ref[b] != blk_e_ref[prev])))
    def _():
        wgb[...] = wg_ref[...].astype(BF16)
        wub[...] = wu_ref[...].astype(BF16)
        wdb[...] = wd_ref[...].astype(BF16)

    @pl.when(nv > 0)
    def _():
        wait_gather(slot)
        words = _slabs_to_rows(xbuf, slot * tb * xrows, tb, xrows)
        x = _unpack_bf16_pairs(words).astype(BF16)
        for r in range(tb):
            gather_row(b + 1, 1 - slot, r, r)
        hdn = _silu(_dot(x, wgb[...])) * _dot(x, wub[...])
        y = _dot(hdn.astype(BF16), wdb[...])
        _rows_to_slabs(y_ref, 0, _pack_bf16_pairs(y), xrows)

    @pl.when(nv == 0)
    def _():
        y_ref[...] = jnp.zeros_like(y_ref)

        @pl.when(nv_prev > 0)
        def _():
            wait_gather(slot)


def _ffn(blk_e, nv, code, u2d, wg, wu, wd, layer):
    d, de = wg.shape[2], wg.shape[3]
    xrows = _slab_rows(d // 2)
    n = u2d.shape[0] // xrows
    assert n & (n - 1) == 0, "token index is taken from the slot code by masking"
    nblk = blk_e.shape[0]
    tb = MOE_BLOCK
    grid_spec = pltpu.PrefetchScalarGridSpec(
        num_scalar_prefetch=3,
        grid=(nblk,),
        in_specs=[pl.BlockSpec(memory_space=pl.ANY),
                  pl.BlockSpec((None, None, d, de), lambda b, be, nv, cd: (layer, be[b], 0, 0)),
                  pl.BlockSpec((None, None, d, de), lambda b, be, nv, cd: (layer, be[b], 0, 0)),
                  pl.BlockSpec((None, None, de, d), lambda b, be, nv, cd: (layer, be[b], 0, 0))],
        out_specs=pl.BlockSpec((tb * xrows, LANE), lambda b, be, nv, cd: (b, 0)),
        scratch_shapes=[pltpu.VMEM((2 * tb * xrows, LANE), jnp.uint32),
                        pltpu.VMEM((d, de), BF16), pltpu.VMEM((d, de), BF16), pltpu.VMEM((de, d), BF16),
                        pltpu.SemaphoreType.DMA((2,))],
    )
    return pl.pallas_call(
        functools.partial(_ffn_kernel, tb=tb, xrows=xrows),
        grid_spec=grid_spec,
        out_shape=jax.ShapeDtypeStruct((nblk * tb * xrows, LANE), jnp.uint32),
        compiler_params=_cparams("arbitrary"),
        name="moe_ffn",
    )(blk_e, nv, code, u2d, wg, wu, wd)


def _combine_kernel(dest_ref, h_ref, y_hbm, rt_ref, nw_ref, *rest, write_h):
    outs, (gbuf, gsem) = rest[:-2], rest[-2:]
    i = pl.program_id(0)
    nt = pl.num_programs(0)
    tm, d = h_ref.shape
    n_tok = nt * tm
    srows = _slab_rows(d // 2)
    slot = i % 2

    def slab(ref, idx, count=1):
        return ref.at[pl.ds(pl.multiple_of(idx * srows, srows), count * srows)]

    def issue(tile, sl):
        def grp(g, c):
            for j in range(DMA_UNROLL):
                t = g * DMA_UNROLL + j
                for k in range(2):
                    src = dest_ref[k * n_tok + tile * tm + t]
                    pltpu.make_async_copy(slab(y_hbm, src), slab(gbuf, (sl * 2 + k) * tm + t),
                                          gsem.at[sl]).start(priority=(j + k) % 2)
            return c
        lax.fori_loop(0, tm // DMA_UNROLL, grp, 0)

    @pl.when(i == 0)
    def _():
        issue(0, 0)

    @pl.when(i + 1 < nt)
    def _():
        issue(i + 1, 1 - slot)

    pltpu.make_async_copy(slab(y_hbm, 0, 2 * tm), slab(gbuf, slot * 2 * tm, 2 * tm), gsem.at[slot]).wait()
    rt = rt_ref[...]
    y0 = _unpack_bf16_pairs(_slabs_to_rows(gbuf, (slot * 2) * tm * srows, tm, srows))
    y1 = _unpack_bf16_pairs(_slabs_to_rows(gbuf, (slot * 2 + 1) * tm * srows, tm, srows))
    h = h_ref[...] + rt[:, 2:3] * y0 + rt[:, 3:4] * y1
    outs[0][...] = _rms(h, nw_ref[...]).astype(outs[0].dtype)
    if write_h:
        outs[1][...] = h


def _combine_pass(h, nw, y2d, rt, dest, out_dtype, write_h):
    n, d = h.shape
    tm = min(512, n)
    srows = _slab_rows(d // 2)
    row = lambda i, dst: (i, 0)
    out_shape = [jax.ShapeDtypeStruct((n, d), out_dtype)]
    out_specs = [pl.BlockSpec((tm, d), row)]
    if write_h:
        out_shape.append(jax.ShapeDtypeStruct((n, d), F32))
        out_specs.append(pl.BlockSpec((tm, d), row))
    grid_spec = pltpu.PrefetchScalarGridSpec(
        num_scalar_prefetch=1,
        grid=(n // tm,),
        in_specs=[pl.BlockSpec((tm, d), row), pl.BlockSpec(memory_space=pl.ANY),
                  pl.BlockSpec((tm, LANE), row), pl.BlockSpec((1, d), lambda i, dst: (0, 0))],
        out_specs=out_specs,
        scratch_shapes=[pltpu.VMEM((2 * 2 * tm * srows, LANE), jnp.uint32), pltpu.SemaphoreType.DMA((2,))],
    )
    return pl.pallas_call(
        functools.partial(_combine_kernel, write_h=write_h),
        grid_spec=grid_spec, out_shape=out_shape,
        compiler_params=_cparams("arbitrary"),
        name="moe_combine",
    )(dest, h, y2d, rt, nw)


def _invert_kernel(dest_ref, code_ref, *, tb):
    n_slots, p_total = dest_ref.shape[0], code_ref.shape[0]

    def init(i, c):
        base = n_slots + ((i * DMA_UNROLL) & (tb - 1))
        for k in range(DMA_UNROLL):
            code_ref[i * DMA_UNROLL + k] = base + k
        return c
    lax.fori_loop(0, p_total // DMA_UNROLL, init, 0)

    def put(i, c):
        for k in range(DMA_UNROLL):
            s = i * DMA_UNROLL + k
            code_ref[dest_ref[s]] = s
        return c
    lax.fori_loop(0, n_slots // DMA_UNROLL, put, 0)


def _invert(dest, p_total, tb):
    assert tb & (tb - 1) == 0
    return pl.pallas_call(
        functools.partial(_invert_kernel, tb=tb),
        in_specs=[pl.BlockSpec(memory_space=pltpu.SMEM)],
        out_specs=pl.BlockSpec(memory_space=pltpu.SMEM),
        out_shape=jax.ShapeDtypeStruct((p_total,), jnp.int32),
        name="invert_slots",
    )(dest)


def _dispatch_tables(route_t, counts, n):
    tb = MOE_BLOCK
    e = route_t[0:2, :].astype(jnp.int32)
    rank = route_t[4:6, :].astype(jnp.int32)
    cnt = counts[0, N_GROUPS:N_GROUPS + N_EXPERTS].astype(jnp.int32)
    padded = ((cnt + tb - 1) // tb) * tb
    pends = jnp.cumsum(padded)
    pstarts = pends - padded
    ids = jnp.arange(N_EXPERTS, dtype=jnp.int32)[:, None, None]
    start_of = jnp.sum(jnp.where(ids == e[None], pstarts[:, None, None], 0), axis=0)
    dest = (start_of + rank).reshape(-1)
    p_total = 2 * n + N_EXPERTS * tb
    nblk = p_total // tb
    code = _invert(dest, p_total, tb)
    bstart = jnp.arange(nblk, dtype=jnp.int32) * tb
    blk_e = jnp.minimum(jnp.sum((bstart[:, None] >= pends[None, :]).astype(jnp.int32), axis=1), N_EXPERTS - 1)
    of_blk = blk_e[:, None] == jnp.arange(N_EXPERTS, dtype=jnp.int32)[None, :]
    cnt_b = jnp.sum(jnp.where(of_blk, cnt[None, :], 0), axis=1)
    pstart_b = jnp.sum(jnp.where(of_blk, pstarts[None, :], 0), axis=1)
    nv = jnp.clip(cnt_b - (bstart - pstart_b), 0, tb)
    nv = jnp.where(bstart < pends[-1], nv, 0).astype(jnp.int32)
    return blk_e, nv, code, dest


def _pad_lanes(v, width):
    return jnp.pad(v, ((0, 0), (0, width - v.shape[1])))


def kernel(x, attn_norm_w, w_in, ssd_conv_w, ssd_conv_b, ssd_dt_bias, ssd_a_log, ssd_d, ssd_norm_w, hgrn_lower_bounds, hgrn_norm_w, w_out, ffn_norm_w, router_group_w, router_group_b, router_expert_w, router_expert_b, expert_w_gate, expert_w_up, expert_w_down, final_norm_w):
    bsz, t, d = x.shape
    n = bsz * t
    depth = w_in.shape[0]

    half = HEAD_DIM // 2
    inv = 1.0 / (ROPE_THETA ** (jnp.arange(half, dtype=F32) / half))
    ang = jnp.arange(t, dtype=F32)[:, None] * inv[None, :]
    cos_t = jnp.concatenate([jnp.cos(ang), jnp.cos(ang)], axis=1)
    sin_t = jnp.concatenate([-jnp.sin(ang), jnp.sin(ang)], axis=1)

    def group_lanes(v):
        v = v.reshape(SSD_GROUPS, SSD_HEADS_PER_GROUP)
        return _pad_lanes(v, LANE).reshape(1, SSD_GROUPS * LANE)

    h = x.reshape(n, d)
    w_in_b = _wprep(jnp.swapaxes(w_in, 1, 2))
    moe = None
    for l in range(depth):
        if moe is None:
            u = _norm_pass(h, attn_norm_w[l][None, :], BF16)
        else:
            u, h = _combine_pass(h, attn_norm_w[l][None, :], *moe, BF16, True)
        p = _inproj(u, w_in_b, l)

        o_ret = _retention(p, cos_t, sin_t, bsz, t)
        o_ssd = _ssd(p, ssd_conv_w[l], ssd_conv_b[l][None, :], group_lanes(ssd_dt_bias[l]),
                     group_lanes(ssd_a_log[l]), jnp.repeat(ssd_d[l], SSD_HEADDIM)[None, :],
                     ssd_norm_w[l][None, :], bsz, t)
        o_hgrn = _hgrn(p, hgrn_lower_bounds, hgrn_norm_w[l][None, :], l, bsz, t)

        w_r = _pad_lanes(jnp.concatenate([router_group_w[l], router_expert_w[l]], axis=1), LANE)
        w_r_hi = w_r.astype(BF16)
        w_r = jnp.concatenate([w_r_hi, (w_r - w_r_hi.astype(F32)).astype(BF16)], axis=1)
        b_r = _pad_lanes(jnp.concatenate([router_group_b[l], router_expert_b[l]])[None, :], LANE)
        h, u2d, logits = _outproj(o_ret, o_ssd, o_hgrn, h, w_out[l].astype(BF16), ffn_norm_w[l][None, :], w_r, b_r)
        route, route_t, counts = _route(logits)
        blk_e, nv, code, dest = _dispatch_tables(route_t, counts, n)
        y2d = _ffn(blk_e, nv, code, u2d, expert_w_gate, expert_w_up, expert_w_down, l)
        moe = (y2d, route, dest)

    (out,) = _combine_pass(h, final_norm_w[None, :], *moe, F32, False)
    return out.reshape(bsz, t, d)
```

```python
import functools
import math

import jax
import jax.numpy as jnp
from jax import lax
from jax.experimental import pallas as pl
from jax.experimental.pallas import tpu as pltpu

F32 = jnp.float32
BF16 = jnp.bfloat16
HIGHEST = lax.Precision.HIGHEST

V7X_VMEM_BYTES = 64 * 1024 * 1024
VMEM_LIMIT = V7X_VMEM_BYTES - 8 * 1024 * 1024
LANE = 128

EPS = 1e-6
LOG2E = 1.4426950408889634
LB_FLOOR = 1e-30
ROPE_THETA = 10000.0

RET_HEADS = 4
HEAD_DIM = 128
SSD_HEADDIM = 64
SSD_GROUPS = 2
SSD_HEADS_PER_GROUP = 8
SSD_CONV = 4
HGRN_HEADS = 4
CHUNK = 128
N_GROUPS = 4
EXPERTS_PER_GROUP = 8
N_EXPERTS = N_GROUPS * EXPERTS_PER_GROUP
MOE_BLOCK = 256
DMA_UNROLL = 8
ROUTE_TILE = 512
SSD_GROUP_WIDTH = SSD_HEADS_PER_GROUP * SSD_HEADDIM
P_Z = 4 * RET_HEADS * HEAD_DIM
P_X = P_Z + SSD_GROUPS * SSD_GROUP_WIDTH
P_B = P_X + SSD_GROUPS * SSD_GROUP_WIDTH
P_C = P_B + SSD_GROUPS * HEAD_DIM
DT_LO = P_C + SSD_GROUPS * HEAD_DIM
DT_HI = DT_LO + SSD_GROUPS * SSD_HEADS_PER_GROUP


def _cparams(*sem):
    return pltpu.CompilerParams(dimension_semantics=sem, vmem_limit_bytes=VMEM_LIMIT)


def _rms(x, w=None):
    y = x * lax.rsqrt(jnp.mean(x * x, axis=-1, keepdims=True) + EPS)
    return y if w is None else y * w


def _sigmoid(x):
    return 1.0 / (1.0 + jnp.exp(-x))


def _silu(x):
    return x * _sigmoid(x)


def _dot(a, b):
    return jnp.dot(a, b, preferred_element_type=F32)


def _dot_nt(a, b):
    return lax.dot_general(a, b, (((1,), (1,)), ((), ())), preferred_element_type=F32)


def _dot_tn(a, b):
    return lax.dot_general(a, b, (((0,), (0,)), ((), ())), preferred_element_type=F32)


def _tri_incl():
    ii = lax.broadcasted_iota(jnp.int32, (CHUNK, CHUNK), 0)
    jj = lax.broadcasted_iota(jnp.int32, (CHUNK, CHUNK), 1)
    return ii, jj


def _lane_spread_matrix(width, per_head):
    hh = lax.broadcasted_iota(jnp.int32, (LANE, width), 0)
    cc = lax.broadcasted_iota(jnp.int32, (LANE, width), 1)
    return jnp.where((cc >= hh * per_head) & (cc < (hh + 1) * per_head), 1.0, 0.0).astype(BF16)


def _spread(v, e, terms):
    out = None
    rest = v
    for i in range(terms):
        piece = rest.astype(BF16)
        part = _dot(piece, e)
        out = part if out is None else out + part
        if i + 1 < terms:
            rest = rest - piece.astype(F32)
    return out


def _slab_rows(d):
    return d // LANE


def _pack_bf16_pairs(v):
    half = v.shape[1] // 2
    bits = lax.bitcast_convert_type(v.astype(BF16).astype(F32), jnp.uint32)
    return (bits[:, :half] >> 16) | (bits[:, half:] & jnp.uint32(0xFFFF0000))


def _unpack_bf16_pairs(words):
    return jnp.concatenate([lax.bitcast_convert_type(words << 16, F32),
                            lax.bitcast_convert_type(words & jnp.uint32(0xFFFF0000), F32)], axis=1)


def _slabs_to_rows(ref2d, base, rows, srows):
    return jnp.concatenate([ref2d[pl.ds(base + c, rows, stride=srows), :] for c in range(srows)], axis=1)


def _rows_to_slabs(ref2d, base, val, srows):
    rows = val.shape[0]
    for c in range(srows):
        ref2d[pl.ds(base + c, rows, stride=srows), :] = val[:, c * LANE:(c + 1) * LANE]


WPREP_ROWS = 256


def _wprep_kernel(wt_hbm, out_ref, buf, sem):
    l, j = pl.program_id(0), pl.program_id(1)
    nj = pl.num_programs(1)
    step = l * nj + j
    slot = step % 2
    n_lo = DT_LO // WPREP_ROWS
    n_main = n_lo + (wt_hbm.shape[1] - DT_HI) // WPREP_ROWS

    def block_copy(st, sl):
        jj = st % nj
        src = jnp.where(jj < n_lo, jj * WPREP_ROWS,
                        jnp.where(jj < n_main, DT_HI + (jj - n_lo) * WPREP_ROWS, DT_LO))
        return pltpu.make_async_copy(wt_hbm.at[st // nj, pl.ds(pl.multiple_of(src, 8), WPREP_ROWS), :],
                                     buf.at[sl], sem.at[sl])

    @pl.when(step == 0)
    def _():
        block_copy(0, 0).start()

    @pl.when(step + 1 < pl.num_programs(0) * nj)
    def _():
        block_copy(step + 1, 1 - slot).start()

    block_copy(step, slot).wait()

    @pl.when(j < n_main)
    def _():
        out_ref[...] = buf[slot].astype(BF16)

    @pl.when(j >= n_main)
    def _():
        pad = jnp.zeros((LANE - SSD_HEADS_PER_GROUP, out_ref.shape[1]), F32)
        parts = []
        for g in range(SSD_GROUPS):
            parts += [buf[slot, g * SSD_HEADS_PER_GROUP:(g + 1) * SSD_HEADS_PER_GROUP, :], pad]
        out_ref[...] = jnp.concatenate(parts, axis=0).astype(BF16)


def _wprep(wt):
    depth, nin, d = wt.shape
    nout = nin - (DT_HI - DT_LO) + SSD_GROUPS * LANE
    assert DT_LO % WPREP_ROWS == 0 and (nin - DT_HI) % WPREP_ROWS == 0 and SSD_GROUPS * LANE == WPREP_ROWS
    return pl.pallas_call(
        _wprep_kernel,
        grid=(depth, nout // WPREP_ROWS),
        in_specs=[pl.BlockSpec(memory_space=pl.ANY)],
        out_specs=pl.BlockSpec((None, WPREP_ROWS, d), lambda l, j: (l, j, 0)),
        out_shape=jax.ShapeDtypeStruct((depth, nout, d), BF16),
        scratch_shapes=[pltpu.VMEM((2, WPREP_ROWS, d), F32), pltpu.SemaphoreType.DMA((2,))],
        compiler_params=_cparams("arbitrary", "arbitrary"),
        name="wprep",
    )(wt)


def _norm_kernel(*refs, combine, write_h):
    if combine:
        h_ref, o0_ref, o1_ref, rt_ref, nw_ref = refs[:5]
        outs = refs[5:]
        rows = h_ref.shape[0]
        srows = _slab_rows(h_ref.shape[1] // 2)
        rt = rt_ref[...]
        h = (h_ref[...] + rt[:, 2:3] * _unpack_bf16_pairs(_slabs_to_rows(o0_ref, 0, rows, srows))
             + rt[:, 3:4] * _unpack_bf16_pairs(_slabs_to_rows(o1_ref, 0, rows, srows)))
    else:
        h_ref, nw_ref = refs[:2]
        outs = refs[2:]
        h = h_ref[...]
    u_ref = outs[0]
    u_ref[...] = _rms(h, nw_ref[...]).astype(u_ref.dtype)
    if write_h:
        outs[1][...] = h


def _norm_pass(h, nw, moe, out_dtype, write_h):
    n, d = h.shape
    combine = moe is not None
    tm = min(512, n)
    row = lambda i: (i, 0)
    in_specs = [pl.BlockSpec((tm, d), row)]
    args = [h]
    if combine:
        o2d, rt = moe
        srows = _slab_rows(d // 2)
        nt = n // tm
        in_specs += [pl.BlockSpec((tm * srows, LANE), row),
                     pl.BlockSpec((tm * srows, LANE), lambda i: (nt + i, 0)),
                     pl.BlockSpec((tm, LANE), row)]
        args += [o2d, o2d, rt]
    in_specs.append(pl.BlockSpec((1, d), lambda i: (0, 0)))
    args.append(nw)
    out_shape = [jax.ShapeDtypeStruct((n, d), out_dtype)]
    out_specs = [pl.BlockSpec((tm, d), row)]
    if write_h:
        out_shape.append(jax.ShapeDtypeStruct((n, d), F32))
        out_specs.append(pl.BlockSpec((tm, d), row))
    return pl.pallas_call(
        functools.partial(_norm_kernel, combine=combine, write_h=write_h),
        grid=(n // tm,),
        in_specs=in_specs, out_specs=out_specs, out_shape=out_shape,
        compiler_params=_cparams("arbitrary"),
        name="norm_pass",
    )(*args)


def _inproj_kernel(u_ref, wt_ref, p_ref):
    p_ref[...] = _dot_nt(u_ref[...], wt_ref[...])


def _inproj(u, wt, layer):
    n, d = u.shape
    np_ = wt.shape[1]
    tm = min(1024, n)
    tn = np_ // 3
    return pl.pallas_call(
        _inproj_kernel,
        grid=(np_ // tn, n // tm),
        in_specs=[pl.BlockSpec((tm, d), lambda j, i: (i, 0)),
                  pl.BlockSpec((None, tn, d), lambda j, i: (layer, j, 0))],
        out_specs=pl.BlockSpec((tm, tn), lambda j, i: (i, j)),
        out_shape=jax.ShapeDtypeStruct((n, np_), F32),
        compiler_params=_cparams("arbitrary", "arbitrary"),
        name="inproj",
    )(u, wt)


def _ret_kernel(q_ref, k_ref, v_ref, g_ref, cos_ref, sin_ref, o_ref, s_scr):
    t = q_ref.shape[0]
    ii, jj = _tri_incl()
    causal = ii >= jj
    dist = (ii - jj).astype(F32)
    iif = ii.astype(F32)
    consts = []
    for hd in range(RET_HEADS):
        lg = math.log(1.0 - 2.0 ** (-5.0 - hd))
        dmat = jnp.where(causal, jnp.exp(jnp.where(causal, dist * lg, 0.0)), 0.0)
        ecum = jnp.exp((iif + 1.0) * lg)
        wk = jnp.exp((CHUNK - 1.0 - iif) * lg)
        consts.append((dmat, ecum, wk, math.exp(CHUNK * lg)))
    scale = HEAD_DIM ** -0.5
    s_scr[...] = jnp.zeros_like(s_scr)

    def step(c, carry):
        r = pl.ds(pl.multiple_of(c * CHUNK, CHUNK), CHUNK)
        cs, sn = cos_ref[r, :], sin_ref[r, :]
        for hd in range(RET_HEADS):
            dmat, ecum, wk, elast = consts[hd]
            cols = slice(hd * HEAD_DIM, (hd + 1) * HEAD_DIM)
            q, k = q_ref[r, cols], k_ref[r, cols]
            qr = q * cs + pltpu.roll(q, HEAD_DIM // 2, 1) * sn
            kr = (k * cs + pltpu.roll(k, HEAD_DIM // 2, 1) * sn) * scale
            vb = v_ref[r, cols].astype(BF16)
            s = s_scr[hd]
            scores = _dot_nt(qr.astype(BF16), kr.astype(BF16)) * dmat
            out = _dot(scores.astype(BF16), vb) + _dot((qr * ecum).astype(BF16), s.astype(BF16))
            s_scr[hd] = elast * s + _dot_tn((kr * wk).astype(BF16), vb)
            o_ref[r, cols] = (_silu(g_ref[r, cols]) * _rms(out)).astype(o_ref.dtype)
        return carry

    lax.fori_loop(0, t // CHUNK, step, 0, unroll=2)


def _retention(p, cos_t, sin_t, bsz, t):
    n = bsz * t
    width = RET_HEADS * HEAD_DIM
    blk = lambda off: pl.BlockSpec((t, width), lambda b, off=off: (b, off))
    tab = pl.BlockSpec((t, HEAD_DIM), lambda b: (0, 0))
    return pl.pallas_call(
        _ret_kernel,
        grid=(bsz,),
        in_specs=[blk(0), blk(1), blk(2), blk(3), tab, tab],
        out_specs=pl.BlockSpec((t, width), lambda b: (b, 0)),
        out_shape=jax.ShapeDtypeStruct((n, width), BF16),
        scratch_shapes=[pltpu.VMEM((RET_HEADS, HEAD_DIM, HEAD_DIM), F32)],
        compiler_params=_cparams("arbitrary"),
        name="retention",
    )(p, p, p, p, cos_t, sin_t)


def _ssd_kernel(z_ref, x_ref, b_ref, c_ref, dt_ref, cwx_ref, cwb_ref, cwc_ref, cbx_ref, cbb_ref, cbc_ref,
                dtb_ref, alog_ref, dsk_ref, nw_ref, o_ref, s_scr):
    t = z_ref.shape[0]
    ii, jj = _tri_incl()
    causal = ii >= jj
    tri = jnp.where(causal, 1.0, 0.0).astype(F32)
    lane_lo = lax.broadcasted_iota(jnp.int32, (CHUNK, LANE), 1) < SSD_HEADDIM
    neg_a = -jnp.exp(alog_ref[...])
    gw = x_ref.shape[1]
    e_pairs = _lane_spread_matrix(SSD_HEADS_PER_GROUP * CHUNK, CHUNK)
    e_heads = _lane_spread_matrix(gw, SSD_HEADDIM)
    s_scr[...] = jnp.zeros_like(s_scr)

    def conv(ref, w_ref, bias_ref, c):
        r = pl.ds(pl.multiple_of(c * CHUNK, CHUNK), CHUNK)
        rp = pl.ds(pl.multiple_of(jnp.maximum(c * CHUNK - 8, 0), 8), 8)
        cur = ref[r, :]
        ext = jnp.concatenate([jnp.where(c > 0, ref[rp, :], 0.0), cur], axis=0)
        w = w_ref[...]
        acc = bias_ref[...] + w[SSD_CONV - 1:SSD_CONV, :] * cur
        for lag in range(1, SSD_CONV):
            shifted = pltpu.roll(ext, lag, 0)[8:8 + CHUNK, :]
            acc = acc + w[SSD_CONV - 1 - lag:SSD_CONV - lag, :] * shifted
        return _silu(acc)

    def step(c, carry):
        r = pl.ds(pl.multiple_of(c * CHUNK, CHUNK), CHUNK)
        xs = conv(x_ref, cwx_ref, cbx_ref, c)
        bm = conv(b_ref, cwb_ref, cbb_ref, c).astype(BF16)
        cm = conv(c_ref, cwc_ref, cbc_ref, c).astype(BF16)
        xr = dt_ref[r, :] + dtb_ref[...]
        dt = jnp.maximum(xr, 0.0) + jnp.log1p(jnp.exp(-jnp.abs(xr)))
        la = dt * neg_a
        cum = jnp.dot(tri, la, precision=HIGHEST, preferred_element_type=F32) * LOG2E
        last = cum[CHUNK - 1:CHUNK, :]
        wj = dt * jnp.exp2(last - cum)
        ecum = jnp.exp2(cum)
        cum_t = cum.T
        dt_t = dt.T
        gm = jnp.where(causal, _dot_nt(cm, bm), 0.0)
        s = s_scr[...]
        cs = _dot(cm, s.astype(BF16))
        cix = _spread(cum, e_pairs, 3)
        ecx = _spread(ecum, e_heads, 2)
        wjx = _spread(wj, e_heads, 2)
        ys = []
        for pr in range(SSD_HEADS_PER_GROUP // 2):
            xp = xs[:, pr * LANE:(pr + 1) * LANE]
            acc = None
            for half in range(2):
                hd = 2 * pr + half
                ci = cix[:, hd * CHUNK:(hd + 1) * CHUNK]
                dec = jnp.exp2(jnp.minimum(ci - cum_t[hd:hd + 1, :], 0.0))
                m = gm * dec * dt_t[hd:hd + 1, :]
                sel = lane_lo if half == 0 else jnp.logical_not(lane_lo)
                y = _dot(m.astype(BF16), jnp.where(sel, xp, 0.0).astype(BF16))
                acc = y if acc is None else acc + y
            ys.append(acc)
        y = jnp.concatenate(ys, axis=1) + ecx * cs
        s_scr[...] = ecx[CHUNK - 1:CHUNK, :] * s + _dot_tn(bm, (wjx * xs).astype(BF16))
        y = (y + dsk_ref[...] * xs) * _silu(z_ref[r, :])
        o_ref[r, :] = (_rms(y) * nw_ref[...]).astype(o_ref.dtype)
        return carry

    lax.fori_loop(0, t // CHUNK, step, 0, unroll=2)


def _ssd(p, conv_w, conv_b, dtb, alog, dsk, nw, bsz, t):
    n = bsz * t
    gw = SSD_GROUP_WIDTH
    st = HEAD_DIM
    z_off, x_off = P_Z // gw, P_X // gw
    b_off, c_off = P_B // st, P_C // st
    dt_off = (p.shape[1] - SSD_GROUPS * LANE) // LANE
    xw = SSD_GROUPS * gw
    par = lambda shape, f: pl.BlockSpec(shape, f)
    in_specs = [
        par((t, gw), lambda b, g: (b, z_off + g)),
        par((t, gw), lambda b, g: (b, x_off + g)),
        par((t, st), lambda b, g: (b, b_off + g)),
        par((t, st), lambda b, g: (b, c_off + g)),
        par((t, LANE), lambda b, g: (b, dt_off + g)),
        par((SSD_CONV, gw), lambda b, g: (0, g)),
        par((SSD_CONV, st), lambda b, g: (0, xw // st + g)),
        par((SSD_CONV, st), lambda b, g: (0, xw // st + SSD_GROUPS + g)),
        par((1, gw), lambda b, g: (0, g)),
        par((1, st), lambda b, g: (0, xw // st + g)),
        par((1, st), lambda b, g: (0, xw // st + SSD_GROUPS + g)),
        par((1, LANE), lambda b, g: (0, g)),
        par((1, LANE), lambda b, g: (0, g)),
        par((1, gw), lambda b, g: (0, g)),
        par((1, gw), lambda b, g: (0, g)),
    ]
    return pl.pallas_call(
        _ssd_kernel,
        grid=(bsz, SSD_GROUPS),
        in_specs=in_specs,
        out_specs=pl.BlockSpec((t, gw), lambda b, g: (b, g)),
        out_shape=jax.ShapeDtypeStruct((n, SSD_GROUPS * gw), BF16),
        scratch_shapes=[pltpu.VMEM((st, gw), F32)],
        compiler_params=_cparams("arbitrary", "arbitrary"),
        name="ssd",
    )(p, p, p, p, p, conv_w, conv_w, conv_w, conv_b, conv_b, conv_b, dtb, alog, dsk, nw)


def _hgrn_kernel(q_ref, f_ref, i_ref, g_ref, lb_ref, nw_ref, o_ref, st_scr, cum_scr, *, layer):
    t = q_ref.shape[0]
    lbm = lb_ref[...]
    depth = lbm.shape[0]
    mx = lbm[0:1, :]
    for i in range(1, depth):
        mx = jnp.maximum(mx, lbm[i:i + 1, :])
    ex = [jnp.exp(lbm[i:i + 1, :] - mx) for i in range(depth)]
    den = ex[0]
    for i in range(1, depth):
        den = den + ex[i]
    sm = [e / den for e in ex]
    csum = sm[0]
    for i in range(1, layer + 1):
        csum = csum + sm[i]
    lb = jnp.maximum(csum - sm[0], 0.0)
    lb_floor = jnp.maximum(lb, LB_FLOOR)
    oml = 1.0 - lb

    ii, jj = _tri_incl()
    tri = jnp.where(ii >= jj, 1.0, 0.0).astype(F32)
    lvl = jnp.where(ii > jj, 31 - lax.clz(ii ^ jj), -1)
    eye = ii == jj
    width = q_ref.shape[1]
    hcols = [slice(c0, c0 + HEAD_DIM) for c0 in range(0, width, HEAD_DIM)]
    row = lax.broadcasted_iota(jnp.int32, (CHUNK, width), 0)
    scale = HEAD_DIM ** -0.5
    nlev = int(math.log2(CHUNK))
    st_scr[...] = jnp.zeros_like(st_scr)

    def step(c, carry):
        r = pl.ds(pl.multiple_of(c * CHUNK, CHUNK), CHUNK)
        q = _silu(q_ref[r, :]) * scale
        f = f_ref[r, :]
        ef = jnp.exp(-jnp.abs(f))
        big = 1.0 / (1.0 + ef)
        small = ef * big
        sig = jnp.where(f >= 0.0, big, small)
        sig_m = jnp.where(f >= 0.0, small, big)
        lf = jnp.log(lb_floor + oml * sig)
        kk = oml * sig_m
        vb = i_ref[r, :].astype(BF16)
        cum = jnp.dot(tri, lf, precision=HIGHEST, preferred_element_type=F32) * LOG2E
        cum_scr[...] = cum
        qb, kb = q.astype(BF16), kk.astype(BF16)
        scores = [jnp.where(eye, _dot_nt(qb[:, cs], kb[:, cs]), 0.0) for cs in hcols]
        for lv in range(nlev):
            s = 1 << lv
            if 2 * s >= 8:
                ref = jnp.concatenate(
                    [jnp.broadcast_to(cum_scr[g0 * 2 * s + s - 1:g0 * 2 * s + s, :], (2 * s, width))
                     for g0 in range(CHUNK // (2 * s))], axis=0)
            elif s == 2:
                m4 = row & 3
                ref = jnp.where(m4 == 0, pltpu.roll(cum, CHUNK - 1, 0),
                                jnp.where(m4 == 1, cum,
                                          jnp.where(m4 == 2, pltpu.roll(cum, 1, 0), pltpu.roll(cum, 2, 0))))
            else:
                ref = jnp.where((row & 1) == 1, pltpu.roll(cum, 1, 0), cum)
            dlt = cum - ref
            e = jnp.exp2(jnp.minimum(dlt, -dlt))
            qs = (q * e).astype(BF16)
            ks = (kk * e).astype(BF16)
            scores = [jnp.where(lvl == lv, _dot_nt(qs[:, cs], ks[:, cs]), sc) for cs, sc in zip(hcols, scores)]
        last = cum[CHUNK - 1:CHUNK, :]
        qe = (q * jnp.exp2(cum)).astype(BF16)
        kw = (kk * jnp.exp2(last - cum)).astype(BF16)
        elast = jnp.exp2(last)
        outs = []
        for hd, cs in enumerate(hcols):
            st = st_scr[hd]
            out = _dot(scores[hd].astype(BF16), vb[:, cs]) + _dot_nt(qe[:, cs], st.astype(BF16))
            st_scr[hd] = st * elast[:, cs] + _dot_tn(vb[:, cs], kw[:, cs])
            outs.append(_rms(out))
        o = jnp.concatenate(outs, axis=1) * nw_ref[...]
        o_ref[r, :] = (_silu(g_ref[r, :]) * o).astype(o_ref.dtype)
        return carry

    lax.fori_loop(0, t // CHUNK, step, 0, unroll=2)


def _hgrn(p, lbounds, nw, layer, bsz, t):
    n = bsz * t
    width = HGRN_HEADS * HEAD_DIM
    base = DT_LO // width
    blk = lambda off: pl.BlockSpec((t, width), lambda b, off=off: (b, base + off))
    return pl.pallas_call(
        functools.partial(_hgrn_kernel, layer=layer),
        grid=(bsz,),
        in_specs=[blk(0), blk(1), blk(2), blk(3),
                  pl.BlockSpec((lbounds.shape[0], width), lambda b: (0, 0)),
                  pl.BlockSpec((1, width), lambda b: (0, 0))],
        out_specs=pl.BlockSpec((t, width), lambda b: (b, 0)),
        out_shape=jax.ShapeDtypeStruct((n, width), BF16),
        scratch_shapes=[pltpu.VMEM((HGRN_HEADS, HEAD_DIM, HEAD_DIM), F32), pltpu.VMEM((CHUNK, width), F32)],
        compiler_params=_cparams("arbitrary"),
        name="hgrn2",
    )(p, p, p, p, lbounds, nw)


def _outproj_kernel(a_ref, b_ref, c_ref, h_ref, w_ref, nw_ref, wr_ref, br_ref, hout_ref, u_ref, lg_ref):
    xrows = _slab_rows(h_ref.shape[1] // 2)
    half = h_ref.shape[0] // 2
    accs = []
    for r0 in (0, half):
        rs = slice(r0, r0 + half)
        mix = jnp.concatenate([a_ref[rs, :], b_ref[rs, :], c_ref[rs, :]], axis=1)
        accs.append(_dot(mix, w_ref[...]))
    for r0, acc in zip((0, half), accs):
        rs = slice(r0, r0 + half)
        h = h_ref[rs, :] + acc
        hout_ref[rs, :] = h
        u = _rms(h, nw_ref[...])
        u_hi = u.astype(BF16)
        u_lo = (u - u_hi.astype(F32)).astype(BF16)
        t2 = _dot(u_hi, wr_ref[...])
        lg_ref[rs, :] = t2[:, :LANE] + t2[:, LANE:] + _dot(u_lo, wr_ref[:, :LANE]) + br_ref[...]
        _rows_to_slabs(u_ref, r0 * xrows, _pack_bf16_pairs(u), xrows)


def _outproj(o_a, o_b, o_c, h, w_out, nw, w_r, b_r):
    n, d = h.shape
    tm = min(512, n)
    srows = _slab_rows(d // 2)
    row = lambda i: (i, 0)
    fix = lambda i: (0, 0)
    return pl.pallas_call(
        _outproj_kernel,
        grid=(n // tm,),
        in_specs=[pl.BlockSpec((tm, o_a.shape[1]), row), pl.BlockSpec((tm, o_b.shape[1]), row),
                  pl.BlockSpec((tm, o_c.shape[1]), row), pl.BlockSpec((tm, d), row),
                  pl.BlockSpec(w_out.shape, fix), pl.BlockSpec((1, d), fix),
                  pl.BlockSpec(w_r.shape, fix), pl.BlockSpec((1, LANE), fix)],
        out_specs=[pl.BlockSpec((tm, d), row), pl.BlockSpec((tm * srows, LANE), row),
                   pl.BlockSpec((tm, LANE), row)],
        out_shape=[jax.ShapeDtypeStruct((n, d), F32), jax.ShapeDtypeStruct((n * srows, LANE), jnp.uint32),
                   jax.ShapeDtypeStruct((n, LANE), F32)],
        compiler_params=_cparams("arbitrary"),
        name="outproj",
    )(o_a, o_b, o_c, h, w_out, nw, w_r, b_r)


def _route_kernel(lg_ref, rt_ref, rtt_ref, cnt_ref, carry_scr):
    tr = lg_ref.shape[0]

    @pl.when(pl.program_id(0) == 0)
    def _():
        carry_scr[...] = jnp.zeros_like(carry_scr)

    lg = lg_ref[...]
    lane = lax.broadcasted_iota(jnp.int32, (tr, LANE), 1)
    neg = -jnp.inf
    big = jnp.int32(LANE)

    def first_max(vals):
        m = jnp.max(vals, axis=-1, keepdims=True)
        idx = jnp.min(jnp.where(vals == m, lane, big), axis=-1, keepdims=True)
        return m, idx

    gl = jnp.where(lane < N_GROUPS, lg, neg)
    gmax, gidx = first_max(gl)
    gate = 1.0 / jnp.sum(jnp.exp(gl - gmax), axis=-1, keepdims=True)
    lo = N_GROUPS + EXPERTS_PER_GROUP * gidx
    el = jnp.where((lane >= lo) & (lane < lo + EXPERTS_PER_GROUP), lg, neg)
    m1, i1 = first_max(el)
    m2, i2 = first_max(jnp.where(lane == i1, neg, el))
    e21 = jnp.exp(m2 - m1)
    w1 = gate * (1.0 / (1.0 + e21))
    w2 = gate * (e21 / (1.0 + e21))

    oh1 = lane == i1
    oh2 = lane == i2
    m = jnp.where(oh1 | oh2, 1.0, 0.0)
    ti = lax.broadcasted_iota(jnp.int32, (tr, tr), 0)
    tj = lax.broadcasted_iota(jnp.int32, (tr, tr), 1)
    before = _dot(jnp.where(ti > tj, 1.0, 0.0).astype(BF16), m.astype(BF16)) + carry_scr[0:1, :]
    r1 = jnp.sum(jnp.where(oh1, before, 0.0), axis=-1, keepdims=True)
    r2 = jnp.sum(jnp.where(oh2, before, 0.0), axis=-1, keepdims=True)
    total = carry_scr[0:1, :] + jnp.sum(m, axis=0, keepdims=True)
    carry_scr[...] = jnp.broadcast_to(total, carry_scr.shape)
    cnt_ref[...] = jnp.broadcast_to(total, cnt_ref.shape)

    e1 = (i1 - N_GROUPS).astype(F32)
    e2 = (i2 - N_GROUPS).astype(F32)
    out = jnp.zeros((tr, LANE), F32)
    for pos, val in enumerate((e1, e2, w1, w2, r1, r2)):
        out = jnp.where(lane == pos, val, out)
    rt_ref[...] = out
    rtt_ref[...] = out.T[0:rtt_ref.shape[0], :]


def _route(logits):
    n = logits.shape[0]
    tr = min(ROUTE_TILE, n)
    return pl.pallas_call(
        _route_kernel,
        grid=(n // tr,),
        in_specs=[pl.BlockSpec((tr, LANE), lambda i: (i, 0))],
        out_specs=[pl.BlockSpec((tr, LANE), lambda i: (i, 0)), pl.BlockSpec((8, tr), lambda i: (0, i)),
                   pl.BlockSpec((8, LANE), lambda i: (0, 0))],
        out_shape=[jax.ShapeDtypeStruct((n, LANE), F32), jax.ShapeDtypeStruct((8, n), F32),
                   jax.ShapeDtypeStruct((8, LANE), F32)],
        scratch_shapes=[pltpu.VMEM((8, LANE), F32)],
        compiler_params=_cparams("arbitrary"),
        name="route",
    )(logits)


def _ffn_kernel(blk_e_ref, nv_ref, code_ref, u_hbm, wg_ref, wu_ref, wd_ref, o_hbm,
                xbuf, ybuf, wgb, wub, wdb, gsem, ssem, *, tb, xrows, yrows):
    b = pl.program_id(0)
    n_tok = u_hbm.shape[0] // xrows
    slot = b % 2
    prev = jnp.maximum(b - 1, 0)
    nv = nv_ref[b]
    nv_prev = jnp.where(b > 0, nv_ref[prev], 0)

    def slab(ref, srows, idx, count=1):
        return ref.at[pl.ds(pl.multiple_of(idx * srows, srows), count * srows)]

    def gather_row(blk, sl, r, lane):
        tok = code_ref[blk * tb + r] & (n_tok - 1)
        pltpu.make_async_copy(slab(u_hbm, xrows, tok), slab(xbuf, xrows, sl * tb + r),
                              gsem.at[sl]).start(priority=lane % 2)

    def scatter_row(blk, sl, r, dst, lane):
        pltpu.make_async_copy(slab(ybuf, yrows, sl * tb + r), slab(o_hbm, yrows, dst),
                              ssem.at[sl]).start(priority=lane % 2)

    def wait_gather(sl):
        pltpu.make_async_copy(slab(u_hbm, xrows, 0, tb), slab(xbuf, xrows, sl * tb, tb), gsem.at[sl]).wait()

    def wait_scatter(sl):
        pltpu.make_async_copy(slab(ybuf, yrows, sl * tb, tb), slab(o_hbm, yrows, 0, tb), ssem.at[sl]).wait()

    def row_loop(fn):
        def grp(i, c):
            for k in range(DMA_UNROLL):
                fn(i * DMA_UNROLL + k, k)
            return c
        lax.fori_loop(0, tb // DMA_UNROLL, grp, 0)

    @pl.when(b == 0)
    def _():
        ybuf[...] = jnp.zeros_like(ybuf)
        row_loop(lambda r, k: gather_row(0, 0, r, k))

    @pl.when((nv > 0) & ((b == 0) | (blk_e_ref[b] != blk_e_ref[prev])))
    def _():
        wgb[...] = wg_ref[...].astype(BF16)
        wub[...] = wu_ref[...].astype(BF16)
        wdb[...] = wd_ref[...].astype(BF16)

    @pl.when(nv > 0)
    def _():
        wait_gather(slot)
        words = _slabs_to_rows(xbuf, slot * tb * xrows, tb, xrows)
        x = _unpack_bf16_pairs(words).astype(BF16)
        has_prev = b > 0
        for r in range(tb):
            gather_row(b + 1, 1 - slot, r, r)
            dst = jnp.where(has_prev, code_ref[prev * tb + r], 2 * n_tok + r)
            scatter_row(prev, 1 - slot, r, dst, r)
        hdn = _silu(_dot(x, wgb[...])) * _dot(x, wub[...])
        y = _dot(hdn.astype(BF16), wdb[...])
        _rows_to_slabs(ybuf, slot * tb * yrows, _pack_bf16_pairs(y), yrows)
        wait_scatter(1 - slot)

    @pl.when((nv == 0) & (nv_prev > 0))
    def _():
        wait_gather(slot)
        row_loop(lambda r, k: scatter_row(prev, 1 - slot, r, code_ref[prev * tb + r], k))
        wait_scatter(1 - slot)


def _ffn(blk_e, nv, code, u2d, wg, wu, wd, layer):
    d, de = wg.shape[2], wg.shape[3]
    xrows = yrows = _slab_rows(d // 2)
    n = u2d.shape[0] // xrows
    assert n & (n - 1) == 0, "token index is taken from the slot code by masking"
    nblk = blk_e.shape[0]
    tb = MOE_BLOCK
    grid_spec = pltpu.PrefetchScalarGridSpec(
        num_scalar_prefetch=3,
        grid=(nblk,),
        in_specs=[pl.BlockSpec(memory_space=pl.ANY),
                  pl.BlockSpec((None, None, d, de), lambda b, be, nv, cd: (layer, be[b], 0, 0)),
                  pl.BlockSpec((None, None, d, de), lambda b, be, nv, cd: (layer, be[b], 0, 0)),
                  pl.BlockSpec((None, None, de, d), lambda b, be, nv, cd: (layer, be[b], 0, 0))],
        out_specs=pl.BlockSpec(memory_space=pl.ANY),
        scratch_shapes=[pltpu.VMEM((2 * tb * xrows, LANE), jnp.uint32),
                        pltpu.VMEM((2 * tb * yrows, LANE), jnp.uint32),
                        pltpu.VMEM((d, de), BF16), pltpu.VMEM((d, de), BF16), pltpu.VMEM((de, d), BF16),
                        pltpu.SemaphoreType.DMA((2,)), pltpu.SemaphoreType.DMA((2,))],
    )
    return pl.pallas_call(
        functools.partial(_ffn_kernel, tb=tb, xrows=xrows, yrows=yrows),
        grid_spec=grid_spec,
        out_shape=jax.ShapeDtypeStruct(((2 * n + tb) * yrows, LANE), jnp.uint32),
        compiler_params=_cparams("arbitrary"),
        name="moe_ffn",
    )(blk_e, nv, code, u2d, wg, wu, wd)


def _invert_kernel(dest_ref, code_ref, *, tb):
    n_slots, p_total = dest_ref.shape[0], code_ref.shape[0]

    def init(i, c):
        base = n_slots + ((i * DMA_UNROLL) & (tb - 1))
        for k in range(DMA_UNROLL):
            code_ref[i * DMA_UNROLL + k] = base + k
        return c
    lax.fori_loop(0, p_total // DMA_UNROLL, init, 0)

    def put(i, c):
        for k in range(DMA_UNROLL):
            s = i * DMA_UNROLL + k
            code_ref[dest_ref[s]] = s
        return c
    lax.fori_loop(0, n_slots // DMA_UNROLL, put, 0)


def _invert(dest, p_total, tb):
    assert tb & (tb - 1) == 0
    return pl.pallas_call(
        functools.partial(_invert_kernel, tb=tb),
        in_specs=[pl.BlockSpec(memory_space=pltpu.SMEM)],
        out_specs=pl.BlockSpec(memory_space=pltpu.SMEM),
        out_shape=jax.ShapeDtypeStruct((p_total,), jnp.int32),
        name="invert_slots",
    )(dest)


def _dispatch_tables(route_t, counts, n):
    tb = MOE_BLOCK
    e = route_t[0:2, :].astype(jnp.int32)
    rank = route_t[4:6, :].astype(jnp.int32)
    cnt = counts[0, N_GROUPS:N_GROUPS + N_EXPERTS].astype(jnp.int32)
    padded = ((cnt + tb - 1) // tb) * tb
    pends = jnp.cumsum(padded)
    pstarts = pends - padded
    ids = jnp.arange(N_EXPERTS, dtype=jnp.int32)[:, None, None]
    start_of = jnp.sum(jnp.where(ids == e[None], pstarts[:, None, None], 0), axis=0)
    dest = (start_of + rank).reshape(-1)
    p_total = 2 * n + N_EXPERTS * tb
    nblk = p_total // tb
    code = _invert(dest, p_total, tb)
    bstart = jnp.arange(nblk, dtype=jnp.int32) * tb
    blk_e = jnp.minimum(jnp.sum((bstart[:, None] >= pends[None, :]).astype(jnp.int32), axis=1), N_EXPERTS - 1)
    of_blk = blk_e[:, None] == jnp.arange(N_EXPERTS, dtype=jnp.int32)[None, :]
    cnt_b = jnp.sum(jnp.where(of_blk, cnt[None, :], 0), axis=1)
    pstart_b = jnp.sum(jnp.where(of_blk, pstarts[None, :], 0), axis=1)
    nv = jnp.clip(cnt_b - (bstart - pstart_b), 0, tb)
    nv = jnp.where(bstart < pends[-1], nv, 0).astype(jnp.int32)
    return blk_e, nv, code


def _pad_lanes(v, width):
    return jnp.pad(v, ((0, 0), (0, width - v.shape[1])))


def kernel(x, attn_norm_w, w_in, ssd_conv_w, ssd_conv_b, ssd_dt_bias, ssd_a_log, ssd_d, ssd_norm_w, hgrn_lower_bounds, hgrn_norm_w, w_out, ffn_norm_w, router_group_w, router_group_b, router_expert_w, router_expert_b, expert_w_gate, expert_w_up, expert_w_down, final_norm_w):
    bsz, t, d = x.shape
    n = bsz * t
    depth = w_in.shape[0]

    half = HEAD_DIM // 2
    inv = 1.0 / (ROPE_THETA ** (jnp.arange(half, dtype=F32) / half))
    ang = jnp.arange(t, dtype=F32)[:, None] * inv[None, :]
    cos_t = jnp.concatenate([jnp.cos(ang), jnp.cos(ang)], axis=1)
    sin_t = jnp.concatenate([-jnp.sin(ang), jnp.sin(ang)], axis=1)

    def group_lanes(v):
        v = v.reshape(SSD_GROUPS, SSD_HEADS_PER_GROUP)
        return _pad_lanes(v, LANE).reshape(1, SSD_GROUPS * LANE)

    h = x.reshape(n, d)
    w_in_b = _wprep(jnp.swapaxes(w_in, 1, 2))
    moe = None
    for l in range(depth):
        if moe is None:
            (u,) = _norm_pass(h, attn_norm_w[l][None, :], None, BF16, False)
        else:
            u, h = _norm_pass(h, attn_norm_w[l][None, :], moe, BF16, True)
        p = _inproj(u, w_in_b, l)

        o_ret = _retention(p, cos_t, sin_t, bsz, t)
        o_ssd = _ssd(p, ssd_conv_w[l], ssd_conv_b[l][None, :], group_lanes(ssd_dt_bias[l]),
                     group_lanes(ssd_a_log[l]), jnp.repeat(ssd_d[l], SSD_HEADDIM)[None, :],
                     ssd_norm_w[l][None, :], bsz, t)
        o_hgrn = _hgrn(p, hgrn_lower_bounds, hgrn_norm_w[l][None, :], l, bsz, t)

        w_r = _pad_lanes(jnp.concatenate([router_group_w[l], router_expert_w[l]], axis=1), LANE)
        w_r_hi = w_r.astype(BF16)
        w_r = jnp.concatenate([w_r_hi, (w_r - w_r_hi.astype(F32)).astype(BF16)], axis=1)
        b_r = _pad_lanes(jnp.concatenate([router_group_b[l], router_expert_b[l]])[None, :], LANE)
        h, u2d, logits = _outproj(o_ret, o_ssd, o_hgrn, h, w_out[l].astype(BF16), ffn_norm_w[l][None, :], w_r, b_r)
        route, route_t, counts = _route(logits)
        blk_e, nv, code = _dispatch_tables(route_t, counts, n)
        o2d = _ffn(blk_e, nv, code, u2d, expert_w_gate, expert_w_up, expert_w_down, l)
        moe = (o2d, route)

    (out,) = _norm_pass(h, final_norm_w[None, :], moe, F32, False)
    return out.reshape(bsz, t, d)
```

```python
import functools
import math

import jax
import jax.numpy as jnp
from jax import lax
from jax.experimental import pallas as pl
from jax.experimental.pallas import tpu as pltpu

F32 = jnp.float32
BF16 = jnp.bfloat16
HIGHEST = lax.Precision.HIGHEST

V7X_VMEM_BYTES = 64 * 1024 * 1024
VMEM_LIMIT = V7X_VMEM_BYTES - 8 * 1024 * 1024
LANE = 128

EPS = 1e-6
LOG2E = 1.4426950408889634
LB_FLOOR = 1e-30
ROPE_THETA = 10000.0

RET_HEADS = 4
HEAD_DIM = 128
SSD_HEADDIM = 64
SSD_GROUPS = 2
SSD_HEADS_PER_GROUP = 8
SSD_CONV = 4
HGRN_HEADS = 4
CHUNK = 128
N_GROUPS = 4
EXPERTS_PER_GROUP = 8
N_EXPERTS = N_GROUPS * EXPERTS_PER_GROUP
MOE_BLOCK = 256
DMA_UNROLL = 8
ROUTE_TILE = 512
SSD_GROUP_WIDTH = SSD_HEADS_PER_GROUP * SSD_HEADDIM
P_Z = 4 * RET_HEADS * HEAD_DIM
P_X = P_Z + SSD_GROUPS * SSD_GROUP_WIDTH
P_B = P_X + SSD_GROUPS * SSD_GROUP_WIDTH
P_C = P_B + SSD_GROUPS * HEAD_DIM
DT_LO = P_C + SSD_GROUPS * HEAD_DIM
DT_HI = DT_LO + SSD_GROUPS * SSD_HEADS_PER_GROUP


def _cparams(*sem):
    return pltpu.CompilerParams(dimension_semantics=sem, vmem_limit_bytes=VMEM_LIMIT)


def _rms(x, w=None):
    y = x * lax.rsqrt(jnp.mean(x * x, axis=-1, keepdims=True) + EPS)
    return y if w is None else y * w


def _sigmoid(x):
    return 1.0 / (1.0 + jnp.exp(-x))


def _silu(x):
    return x * _sigmoid(x)


def _dot(a, b):
    return jnp.dot(a, b, preferred_element_type=F32)


def _dot_nt(a, b):
    return lax.dot_general(a, b, (((1,), (1,)), ((), ())), preferred_element_type=F32)


def _dot_tn(a, b):
    return lax.dot_general(a, b, (((0,), (0,)), ((), ())), preferred_element_type=F32)


def _tri_incl():
    ii = lax.broadcasted_iota(jnp.int32, (CHUNK, CHUNK), 0)
    jj = lax.broadcasted_iota(jnp.int32, (CHUNK, CHUNK), 1)
    return ii, jj


def _lane_spread_matrix(width, per_head):
    hh = lax.broadcasted_iota(jnp.int32, (LANE, width), 0)
    cc = lax.broadcasted_iota(jnp.int32, (LANE, width), 1)
    return jnp.where((cc >= hh * per_head) & (cc < (hh + 1) * per_head), 1.0, 0.0).astype(BF16)


def _spread(v, e, terms):
    out = None
    rest = v
    for i in range(terms):
        piece = rest.astype(BF16)
        part = _dot(piece, e)
        out = part if out is None else out + part
        if i + 1 < terms:
            rest = rest - piece.astype(F32)
    return out


def _slab_rows(d):
    return d // LANE


def _pack_bf16_pairs(v):
    half = v.shape[1] // 2
    bits = lax.bitcast_convert_type(v.astype(BF16).astype(F32), jnp.uint32)
    return (bits[:, :half] >> 16) | (bits[:, half:] & jnp.uint32(0xFFFF0000))


def _unpack_bf16_pairs(words):
    return jnp.concatenate([lax.bitcast_convert_type(words << 16, F32),
                            lax.bitcast_convert_type(words & jnp.uint32(0xFFFF0000), F32)], axis=1)


def _slabs_to_rows(ref2d, base, rows, srows):
    return jnp.concatenate([ref2d[pl.ds(base + c, rows, stride=srows), :] for c in range(srows)], axis=1)


def _rows_to_slabs(ref2d, base, val, srows):
    rows = val.shape[0]
    for c in range(srows):
        ref2d[pl.ds(base + c, rows, stride=srows), :] = val[:, c * LANE:(c + 1) * LANE]


WPREP_ROWS = 256


def _wprep_kernel(wt_hbm, out_ref, buf, sem):
    l, j = pl.program_id(0), pl.program_id(1)
    nj = pl.num_programs(1)
    step = l * nj + j
    slot = step % 2
    n_lo = DT_LO // WPREP_ROWS
    n_main = n_lo + (wt_hbm.shape[1] - DT_HI) // WPREP_ROWS

    def block_copy(st, sl):
        jj = st % nj
        src = jnp.where(jj < n_lo, jj * WPREP_ROWS,
                        jnp.where(jj < n_main, DT_HI + (jj - n_lo) * WPREP_ROWS, DT_LO))
        return pltpu.make_async_copy(wt_hbm.at[st // nj, pl.ds(pl.multiple_of(src, 8), WPREP_ROWS), :],
                                     buf.at[sl], sem.at[sl])

    @pl.when(step == 0)
    def _():
        block_copy(0, 0).start()

    @pl.when(step + 1 < pl.num_programs(0) * nj)
    def _():
        block_copy(step + 1, 1 - slot).start()

    block_copy(step, slot).wait()

    @pl.when(j < n_main)
    def _():
        out_ref[...] = buf[slot].astype(BF16)

    @pl.when(j >= n_main)
    def _():
        pad = jnp.zeros((LANE - SSD_HEADS_PER_GROUP, out_ref.shape[1]), F32)
        parts = []
        for g in range(SSD_GROUPS):
            parts += [buf[slot, g * SSD_HEADS_PER_GROUP:(g + 1) * SSD_HEADS_PER_GROUP, :], pad]
        out_ref[...] = jnp.concatenate(parts, axis=0).astype(BF16)


def _wprep(wt):
    depth, nin, d = wt.shape
    nout = nin - (DT_HI - DT_LO) + SSD_GROUPS * LANE
    assert DT_LO % WPREP_ROWS == 0 and (nin - DT_HI) % WPREP_ROWS == 0 and SSD_GROUPS * LANE == WPREP_ROWS
    return pl.pallas_call(
        _wprep_kernel,
        grid=(depth, nout // WPREP_ROWS),
        in_specs=[pl.BlockSpec(memory_space=pl.ANY)],
        out_specs=pl.BlockSpec((None, WPREP_ROWS, d), lambda l, j: (l, j, 0)),
        out_shape=jax.ShapeDtypeStruct((depth, nout, d), BF16),
        scratch_shapes=[pltpu.VMEM((2, WPREP_ROWS, d), F32), pltpu.SemaphoreType.DMA((2,))],
        compiler_params=_cparams("arbitrary", "arbitrary"),
        name="wprep",
    )(wt)


def _norm_kernel(*refs, combine, write_h):
    if combine:
        h_ref, o0_ref, o1_ref, rt_ref, nw_ref = refs[:5]
        outs = refs[5:]
        rows = h_ref.shape[0]
        srows = _slab_rows(h_ref.shape[1] // 2)
        rt = rt_ref[...]
        h = (h_ref[...] + rt[:, 2:3] * _unpack_bf16_pairs(_slabs_to_rows(o0_ref, 0, rows, srows))
             + rt[:, 3:4] * _unpack_bf16_pairs(_slabs_to_rows(o1_ref, 0, rows, srows)))
    else:
        h_ref, nw_ref = refs[:2]
        outs = refs[2:]
        h = h_ref[...]
    u_ref = outs[0]
    u_ref[...] = _rms(h, nw_ref[...]).astype(u_ref.dtype)
    if write_h:
        outs[1][...] = h


def _norm_pass(h, nw, moe, out_dtype, write_h):
    n, d = h.shape
    combine = moe is not None
    tm = min(512, n)
    row = lambda i: (i, 0)
    in_specs = [pl.BlockSpec((tm, d), row)]
    args = [h]
    if combine:
        o2d, rt = moe
        srows = _slab_rows(d // 2)
        nt = n // tm
        in_specs += [pl.BlockSpec((tm * srows, LANE), row),
                     pl.BlockSpec((tm * srows, LANE), lambda i: (nt + i, 0)),
                     pl.BlockSpec((tm, LANE), row)]
        args += [o2d, o2d, rt]
    in_specs.append(pl.BlockSpec((1, d), lambda i: (0, 0)))
    args.append(nw)
    out_shape = [jax.ShapeDtypeStruct((n, d), out_dtype)]
    out_specs = [pl.BlockSpec((tm, d), row)]
    if write_h:
        out_shape.append(jax.ShapeDtypeStruct((n, d), F32))
        out_specs.append(pl.BlockSpec((tm, d), row))
    return pl.pallas_call(
        functools.partial(_norm_kernel, combine=combine, write_h=write_h),
        grid=(n // tm,),
        in_specs=in_specs, out_specs=out_specs, out_shape=out_shape,
        compiler_params=_cparams("arbitrary"),
        name="norm_pass",
    )(*args)


def _inproj_kernel(u_ref, wt_ref, p_ref):
    p_ref[...] = _dot_nt(u_ref[...], wt_ref[...])


def _inproj(u, wt, layer):
    n, d = u.shape
    np_ = wt.shape[1]
    tm = min(1024, n)
    tn = np_ // 3
    return pl.pallas_call(
        _inproj_kernel,
        grid=(np_ // tn, n // tm),
        in_specs=[pl.BlockSpec((tm, d), lambda j, i: (i, 0)),
                  pl.BlockSpec((None, tn, d), lambda j, i: (layer, j, 0))],
        out_specs=pl.BlockSpec((tm, tn), lambda j, i: (i, j)),
        out_shape=jax.ShapeDtypeStruct((n, np_), F32),
        compiler_params=_cparams("arbitrary", "arbitrary"),
        name="inproj",
    )(u, wt)


def _ret_kernel(q_ref, k_ref, v_ref, g_ref, cos_ref, sin_ref, o_ref, s_scr):
    t = q_ref.shape[0]
    ii, jj = _tri_incl()
    causal = ii >= jj
    dist = (ii - jj).astype(F32)
    iif = ii.astype(F32)
    consts = []
    for hd in range(RET_HEADS):
        lg = math.log(1.0 - 2.0 ** (-5.0 - hd))
        dmat = jnp.where(causal, jnp.exp(jnp.where(causal, dist * lg, 0.0)), 0.0)
        ecum = jnp.exp((iif + 1.0) * lg)
        wk = jnp.exp((CHUNK - 1.0 - iif) * lg)
        consts.append((dmat, ecum, wk, math.exp(CHUNK * lg)))
    scale = HEAD_DIM ** -0.5
    s_scr[...] = jnp.zeros_like(s_scr)

    def step(c, carry):
        r = pl.ds(pl.multiple_of(c * CHUNK, CHUNK), CHUNK)
        cs, sn = cos_ref[r, :], sin_ref[r, :]
        for hd in range(RET_HEADS):
            dmat, ecum, wk, elast = consts[hd]
            cols = slice(hd * HEAD_DIM, (hd + 1) * HEAD_DIM)
            q, k = q_ref[r, cols], k_ref[r, cols]
            qr = q * cs + pltpu.roll(q, HEAD_DIM // 2, 1) * sn
            kr = (k * cs + pltpu.roll(k, HEAD_DIM // 2, 1) * sn) * scale
            vb = v_ref[r, cols].astype(BF16)
            s = s_scr[hd]
            scores = _dot_nt(qr.astype(BF16), kr.astype(BF16)) * dmat
            out = _dot(scores.astype(BF16), vb) + _dot((qr * ecum).astype(BF16), s.astype(BF16))
            s_scr[hd] = elast * s + _dot_tn((kr * wk).astype(BF16), vb)
            o_ref[r, cols] = (_silu(g_ref[r, cols]) * _rms(out)).astype(o_ref.dtype)
        return carry

    lax.fori_loop(0, t // CHUNK, step, 0, unroll=4)


def _retention(p, cos_t, sin_t, bsz, t):
    n = bsz * t
    width = RET_HEADS * HEAD_DIM
    blk = lambda off: pl.BlockSpec((t, width), lambda b, off=off: (b, off))
    tab = pl.BlockSpec((t, HEAD_DIM), lambda b: (0, 0))
    return pl.pallas_call(
        _ret_kernel,
        grid=(bsz,),
        in_specs=[blk(0), blk(1), blk(2), blk(3), tab, tab],
        out_specs=pl.BlockSpec((t, width), lambda b: (b, 0)),
        out_shape=jax.ShapeDtypeStruct((n, width), BF16),
        scratch_shapes=[pltpu.VMEM((RET_HEADS, HEAD_DIM, HEAD_DIM), F32)],
        compiler_params=_cparams("arbitrary"),
        name="retention",
    )(p, p, p, p, cos_t, sin_t)


def _ssd_kernel(z_ref, x_ref, b_ref, c_ref, dt_ref, cwx_ref, cwb_ref, cwc_ref, cbx_ref, cbb_ref, cbc_ref,
                dtb_ref, alog_ref, dsk_ref, nw_ref, o_ref, s_scr):
    t = z_ref.shape[0]
    ii, jj = _tri_incl()
    causal = ii >= jj
    tri = jnp.where(causal, 1.0, 0.0).astype(F32)
    lane_lo = lax.broadcasted_iota(jnp.int32, (CHUNK, LANE), 1) < SSD_HEADDIM
    neg_a = -jnp.exp(alog_ref[...])
    gw = x_ref.shape[1]
    e_pairs = _lane_spread_matrix(SSD_HEADS_PER_GROUP * CHUNK, CHUNK)
    e_heads = _lane_spread_matrix(gw, SSD_HEADDIM)
    s_scr[...] = jnp.zeros_like(s_scr)

    def conv(ref, w_ref, bias_ref, c):
        r = pl.ds(pl.multiple_of(c * CHUNK, CHUNK), CHUNK)
        rp = pl.ds(pl.multiple_of(jnp.maximum(c * CHUNK - 8, 0), 8), 8)
        cur = ref[r, :]
        ext = jnp.concatenate([jnp.where(c > 0, ref[rp, :], 0.0), cur], axis=0)
        w = w_ref[...]
        acc = bias_ref[...] + w[SSD_CONV - 1:SSD_CONV, :] * cur
        for lag in range(1, SSD_CONV):
            shifted = pltpu.roll(ext, lag, 0)[8:8 + CHUNK, :]
            acc = acc + w[SSD_CONV - 1 - lag:SSD_CONV - lag, :] * shifted
        return _silu(acc)

    def step(c, carry):
        r = pl.ds(pl.multiple_of(c * CHUNK, CHUNK), CHUNK)
        xs = conv(x_ref, cwx_ref, cbx_ref, c)
        bm = conv(b_ref, cwb_ref, cbb_ref, c).astype(BF16)
        cm = conv(c_ref, cwc_ref, cbc_ref, c).astype(BF16)
        xr = dt_ref[r, :] + dtb_ref[...]
        dt = jnp.maximum(xr, 0.0) + jnp.log1p(jnp.exp(-jnp.abs(xr)))
        la = dt * neg_a
        cum = jnp.dot(tri, la, precision=HIGHEST, preferred_element_type=F32) * LOG2E
        last = cum[CHUNK - 1:CHUNK, :]
        wj = dt * jnp.exp2(last - cum)
        ecum = jnp.exp2(cum)
        cum_t = cum.T
        dt_t = dt.T
        gm = jnp.where(causal, _dot_nt(cm, bm), 0.0)
        s = s_scr[...]
        cs = _dot(cm, s.astype(BF16))
        cix = _spread(cum, e_pairs, 3)
        ecx = _spread(ecum, e_heads, 2)
        wjx = _spread(wj, e_heads, 2)
        ys = []
        for pr in range(SSD_HEADS_PER_GROUP // 2):
            xp = xs[:, pr * LANE:(pr + 1) * LANE]
            acc = None
            for half in range(2):
                hd = 2 * pr + half
                ci = cix[:, hd * CHUNK:(hd + 1) * CHUNK]
                dec = jnp.exp2(jnp.minimum(ci - cum_t[hd:hd + 1, :], 0.0))
                m = gm * dec * dt_t[hd:hd + 1, :]
                sel = lane_lo if half == 0 else jnp.logical_not(lane_lo)
                y = _dot(m.astype(BF16), jnp.where(sel, xp, 0.0).astype(BF16))
                acc = y if acc is None else acc + y
            ys.append(acc)
        y = jnp.concatenate(ys, axis=1) + ecx * cs
        s_scr[...] = ecx[CHUNK - 1:CHUNK, :] * s + _dot_tn(bm, (wjx * xs).astype(BF16))
        y = (y + dsk_ref[...] * xs) * _silu(z_ref[r, :])
        o_ref[r, :] = (_rms(y) * nw_ref[...]).astype(o_ref.dtype)
        return carry

    lax.fori_loop(0, t // CHUNK, step, 0, unroll=4)


def _ssd(p, conv_w, conv_b, dtb, alog, dsk, nw, bsz, t):
    n = bsz * t
    gw = SSD_GROUP_WIDTH
    st = HEAD_DIM
    z_off, x_off = P_Z // gw, P_X // gw
    b_off, c_off = P_B // st, P_C // st
    dt_off = (p.shape[1] - SSD_GROUPS * LANE) // LANE
    xw = SSD_GROUPS * gw
    par = lambda shape, f: pl.BlockSpec(shape, f)
    in_specs = [
        par((t, gw), lambda b, g: (b, z_off + g)),
        par((t, gw), lambda b, g: (b, x_off + g)),
        par((t, st), lambda b, g: (b, b_off + g)),
        par((t, st), lambda b, g: (b, c_off + g)),
        par((t, LANE), lambda b, g: (b, dt_off + g)),
        par((SSD_CONV, gw), lambda b, g: (0, g)),
        par((SSD_CONV, st), lambda b, g: (0, xw // st + g)),
        par((SSD_CONV, st), lambda b, g: (0, xw // st + SSD_GROUPS + g)),
        par((1, gw), lambda b, g: (0, g)),
        par((1, st), lambda b, g: (0, xw // st + g)),
        par((1, st), lambda b, g: (0, xw // st + SSD_GROUPS + g)),
        par((1, LANE), lambda b, g: (0, g)),
        par((1, LANE), lambda b, g: (0, g)),
        par((1, gw), lambda b, g: (0, g)),
        par((1, gw), lambda b, g: (0, g)),
    ]
    return pl.pallas_call(
        _ssd_kernel,
        grid=(bsz, SSD_GROUPS),
        in_specs=in_specs,
        out_specs=pl.BlockSpec((t, gw), lambda b, g: (b, g)),
        out_shape=jax.ShapeDtypeStruct((n, SSD_GROUPS * gw), BF16),
        scratch_shapes=[pltpu.VMEM((st, gw), F32)],
        compiler_params=_cparams("arbitrary", "arbitrary"),
        name="ssd",
    )(p, p, p, p, p, conv_w, conv_w, conv_w, conv_b, conv_b, conv_b, dtb, alog, dsk, nw)


def _hgrn_kernel(q_ref, f_ref, i_ref, g_ref, lb_ref, nw_ref, o_ref, st_scr, cum_scr, *, layer):
    t = q_ref.shape[0]
    lbm = lb_ref[...]
    depth = lbm.shape[0]
    mx = lbm[0:1, :]
    for i in range(1, depth):
        mx = jnp.maximum(mx, lbm[i:i + 1, :])
    ex = [jnp.exp(lbm[i:i + 1, :] - mx) for i in range(depth)]
    den = ex[0]
    for i in range(1, depth):
        den = den + ex[i]
    sm = [e / den for e in ex]
    csum = sm[0]
    for i in range(1, layer + 1):
        csum = csum + sm[i]
    lb = jnp.maximum(csum - sm[0], 0.0)
    lb_floor = jnp.maximum(lb, LB_FLOOR)
    oml = 1.0 - lb

    ii, jj = _tri_incl()
    tri = jnp.where(ii >= jj, 1.0, 0.0).astype(F32)
    lvl = jnp.where(ii > jj, 31 - lax.clz(ii ^ jj), -1)
    eye = ii == jj
    width = q_ref.shape[1]
    hcols = [slice(c0, c0 + HEAD_DIM) for c0 in range(0, width, HEAD_DIM)]
    row = lax.broadcasted_iota(jnp.int32, (CHUNK, width), 0)
    scale = HEAD_DIM ** -0.5
    nlev = int(math.log2(CHUNK))
    st_scr[...] = jnp.zeros_like(st_scr)

    def step(c, carry):
        r = pl.ds(pl.multiple_of(c * CHUNK, CHUNK), CHUNK)
        q = _silu(q_ref[r, :]) * scale
        f = f_ref[r, :]
        ef = jnp.exp(-jnp.abs(f))
        big = 1.0 / (1.0 + ef)
        small = ef * big
        sig = jnp.where(f >= 0.0, big, small)
        sig_m = jnp.where(f >= 0.0, small, big)
        lf = jnp.log(lb_floor + oml * sig)
        kk = oml * sig_m
        vb = i_ref[r, :].astype(BF16)
        cum = jnp.dot(tri, lf, precision=HIGHEST, preferred_element_type=F32) * LOG2E
        cum_scr[...] = cum
        qb, kb = q.astype(BF16), kk.astype(BF16)
        scores = [jnp.where(eye, _dot_nt(qb[:, cs], kb[:, cs]), 0.0) for cs in hcols]
        for lv in range(nlev):
            s = 1 << lv
            if 2 * s >= 8:
                ref = jnp.concatenate(
                    [jnp.broadcast_to(cum_scr[g0 * 2 * s + s - 1:g0 * 2 * s + s, :], (2 * s, width))
                     for g0 in range(CHUNK // (2 * s))], axis=0)
            elif s == 2:
                m4 = row & 3
                ref = jnp.where(m4 == 0, pltpu.roll(cum, CHUNK - 1, 0),
                                jnp.where(m4 == 1, cum,
                                          jnp.where(m4 == 2, pltpu.roll(cum, 1, 0), pltpu.roll(cum, 2, 0))))
            else:
                ref = jnp.where((row & 1) == 1, pltpu.roll(cum, 1, 0), cum)
            dlt = cum - ref
            e = jnp.exp2(jnp.minimum(dlt, -dlt))
            qs = (q * e).astype(BF16)
            ks = (kk * e).astype(BF16)
            scores = [jnp.where(lvl == lv, _dot_nt(qs[:, cs], ks[:, cs]), sc) for cs, sc in zip(hcols, scores)]
        last = cum[CHUNK - 1:CHUNK, :]
        qe = (q * jnp.exp2(cum)).astype(BF16)
        kw = (kk * jnp.exp2(last - cum)).astype(BF16)
        elast = jnp.exp2(last)
        outs = []
        for hd, cs in enumerate(hcols):
            st = st_scr[hd]
            out = _dot(scores[hd].astype(BF16), vb[:, cs]) + _dot_nt(qe[:, cs], st.astype(BF16))
            st_scr[hd] = st * elast[:, cs] + _dot_tn(vb[:, cs], kw[:, cs])
            outs.append(_rms(out))
        o = jnp.concatenate(outs, axis=1) * nw_ref[...]
        o_ref[r, :] = (_silu(g_ref[r, :]) * o).astype(o_ref.dtype)
        return carry

    lax.fori_loop(0, t // CHUNK, step, 0, unroll=4)


def _hgrn(p, lbounds, nw, layer, bsz, t):
    n = bsz * t
    width = HGRN_HEADS * HEAD_DIM
    base = DT_LO // width
    blk = lambda off: pl.BlockSpec((t, width), lambda b, off=off: (b, base + off))
    return pl.pallas_call(
        functools.partial(_hgrn_kernel, layer=layer),
        grid=(bsz,),
        in_specs=[blk(0), blk(1), blk(2), blk(3),
                  pl.BlockSpec((lbounds.shape[0], width), lambda b: (0, 0)),
                  pl.BlockSpec((1, width), lambda b: (0, 0))],
        out_specs=pl.BlockSpec((t, width), lambda b: (b, 0)),
        out_shape=jax.ShapeDtypeStruct((n, width), BF16),
        scratch_shapes=[pltpu.VMEM((HGRN_HEADS, HEAD_DIM, HEAD_DIM), F32), pltpu.VMEM((CHUNK, width), F32)],
        compiler_params=_cparams("arbitrary"),
        name="hgrn2",
    )(p, p, p, p, lbounds, nw)


def _outproj_kernel(a_ref, b_ref, c_ref, h_ref, w_ref, nw_ref, wr_ref, br_ref, hout_ref, u_ref, lg_ref):
    xrows = _slab_rows(h_ref.shape[1] // 2)
    half = h_ref.shape[0] // 2
    accs = []
    for r0 in (0, half):
        rs = slice(r0, r0 + half)
        mix = jnp.concatenate([a_ref[rs, :], b_ref[rs, :], c_ref[rs, :]], axis=1)
        accs.append(_dot(mix, w_ref[...]))
    for r0, acc in zip((0, half), accs):
        rs = slice(r0, r0 + half)
        h = h_ref[rs, :] + acc
        hout_ref[rs, :] = h
        u = _rms(h, nw_ref[...])
        u_hi = u.astype(BF16)
        u_lo = (u - u_hi.astype(F32)).astype(BF16)
        t2 = _dot(u_hi, wr_ref[...])
        lg_ref[rs, :] = t2[:, :LANE] + t2[:, LANE:] + _dot(u_lo, wr_ref[:, :LANE]) + br_ref[...]
        _rows_to_slabs(u_ref, r0 * xrows, _pack_bf16_pairs(u), xrows)


def _outproj(o_a, o_b, o_c, h, w_out, nw, w_r, b_r):
    n, d = h.shape
    tm = min(512, n)
    srows = _slab_rows(d // 2)
    row = lambda i: (i, 0)
    fix = lambda i: (0, 0)
    return pl.pallas_call(
        _outproj_kernel,
        grid=(n // tm,),
        in_specs=[pl.BlockSpec((tm, o_a.shape[1]), row), pl.BlockSpec((tm, o_b.shape[1]), row),
                  pl.BlockSpec((tm, o_c.shape[1]), row), pl.BlockSpec((tm, d), row),
                  pl.BlockSpec(w_out.shape, fix), pl.BlockSpec((1, d), fix),
                  pl.BlockSpec(w_r.shape, fix), pl.BlockSpec((1, LANE), fix)],
        out_specs=[pl.BlockSpec((tm, d), row), pl.BlockSpec((tm * srows, LANE), row),
                   pl.BlockSpec((tm, LANE), row)],
        out_shape=[jax.ShapeDtypeStruct((n, d), F32), jax.ShapeDtypeStruct((n * srows, LANE), jnp.uint32),
                   jax.ShapeDtypeStruct((n, LANE), F32)],
        compiler_params=_cparams("arbitrary"),
        name="outproj",
    )(o_a, o_b, o_c, h, w_out, nw, w_r, b_r)


def _route_kernel(lg_ref, rt_ref, rtt_ref, cnt_ref, carry_scr):
    tr = lg_ref.shape[0]

    @pl.when(pl.program_id(0) == 0)
    def _():
        carry_scr[...] = jnp.zeros_like(carry_scr)

    lg = lg_ref[...]
    lane = lax.broadcasted_iota(jnp.int32, (tr, LANE), 1)
    neg = -jnp.inf
    big = jnp.int32(LANE)

    def first_max(vals):
        m = jnp.max(vals, axis=-1, keepdims=True)
        idx = jnp.min(jnp.where(vals == m, lane, big), axis=-1, keepdims=True)
        return m, idx

    gl = jnp.where(lane < N_GROUPS, lg, neg)
    gmax, gidx = first_max(gl)
    gate = 1.0 / jnp.sum(jnp.exp(gl - gmax), axis=-1, keepdims=True)
    lo = N_GROUPS + EXPERTS_PER_GROUP * gidx
    el = jnp.where((lane >= lo) & (lane < lo + EXPERTS_PER_GROUP), lg, neg)
    m1, i1 = first_max(el)
    m2, i2 = first_max(jnp.where(lane == i1, neg, el))
    e21 = jnp.exp(m2 - m1)
    w1 = gate * (1.0 / (1.0 + e21))
    w2 = gate * (e21 / (1.0 + e21))

    oh1 = lane == i1
    oh2 = lane == i2
    m = jnp.where(oh1 | oh2, 1.0, 0.0)
    ti = lax.broadcasted_iota(jnp.int32, (tr, tr), 0)
    tj = lax.broadcasted_iota(jnp.int32, (tr, tr), 1)
    before = _dot(jnp.where(ti > tj, 1.0, 0.0).astype(BF16), m.astype(BF16)) + carry_scr[0:1, :]
    r1 = jnp.sum(jnp.where(oh1, before, 0.0), axis=-1, keepdims=True)
    r2 = jnp.sum(jnp.where(oh2, before, 0.0), axis=-1, keepdims=True)
    total = carry_scr[0:1, :] + jnp.sum(m, axis=0, keepdims=True)
    carry_scr[...] = jnp.broadcast_to(total, carry_scr.shape)
    cnt_ref[...] = jnp.broadcast_to(total, cnt_ref.shape)

    e1 = (i1 - N_GROUPS).astype(F32)
    e2 = (i2 - N_GROUPS).astype(F32)
    out = jnp.zeros((tr, LANE), F32)
    for pos, val in enumerate((e1, e2, w1, w2, r1, r2)):
        out = jnp.where(lane == pos, val, out)
    rt_ref[...] = out
    rtt_ref[...] = out.T[0:rtt_ref.shape[0], :]


def _route(logits):
    n = logits.shape[0]
    tr = min(ROUTE_TILE, n)
    return pl.pallas_call(
        _route_kernel,
        grid=(n // tr,),
        in_specs=[pl.BlockSpec((tr, LANE), lambda i: (i, 0))],
        out_specs=[pl.BlockSpec((tr, LANE), lambda i: (i, 0)), pl.BlockSpec((8, tr), lambda i: (0, i)),
                   pl.BlockSpec((8, LANE), lambda i: (0, 0))],
        out_shape=[jax.ShapeDtypeStruct((n, LANE), F32), jax.ShapeDtypeStruct((8, n), F32),
                   jax.ShapeDtypeStruct((8, LANE), F32)],
        scratch_shapes=[pltpu.VMEM((8, LANE), F32)],
        compiler_params=_cparams("arbitrary"),
        name="route",
    )(logits)


def _ffn_kernel(blk_e_ref, nv_ref, code_ref, u_hbm, wg_ref, wu_ref, wd_ref, o_hbm,
                xbuf, ybuf, wgb, wub, wdb, gsem, ssem, *, tb, xrows, yrows):
    b = pl.program_id(0)
    n_tok = u_hbm.shape[0] // xrows
    slot = b % 2
    prev = jnp.maximum(b - 1, 0)
    nv = nv_ref[b]
    nv_prev = jnp.where(b > 0, nv_ref[prev], 0)

    def slab(ref, srows, idx, count=1):
        return ref.at[pl.ds(pl.multiple_of(idx * srows, srows), count * srows)]

    def gather_row(blk, sl, r, lane):
        tok = code_ref[blk * tb + r] & (n_tok - 1)
        pltpu.make_async_copy(slab(u_hbm, xrows, tok), slab(xbuf, xrows, sl * tb + r),
                              gsem.at[sl]).start(priority=lane % 2)

    def scatter_row(blk, sl, r, dst, lane):
        pltpu.make_async_copy(slab(ybuf, yrows, sl * tb + r), slab(o_hbm, yrows, dst),
                              ssem.at[sl]).start(priority=lane % 2)

    def wait_gather(sl):
        pltpu.make_async_copy(slab(u_hbm, xrows, 0, tb), slab(xbuf, xrows, sl * tb, tb), gsem.at[sl]).wait()

    def wait_scatter(sl):
        pltpu.make_async_copy(slab(ybuf, yrows, sl * tb, tb), slab(o_hbm, yrows, 0, tb), ssem.at[sl]).wait()

    def row_loop(fn):
        def grp(i, c):
            for k in range(DMA_UNROLL):
                fn(i * DMA_UNROLL + k, k)
            return c
        lax.fori_loop(0, tb // DMA_UNROLL, grp, 0)

    @pl.when(b == 0)
    def _():
        ybuf[...] = jnp.zeros_like(ybuf)
        row_loop(lambda r, k: gather_row(0, 0, r, k))

    @pl.when((nv > 0) & ((b == 0) | (blk_e_ref[b] != blk_e_ref[prev])))
    def _():
        wgb[...] = wg_ref[...].astype(BF16)
        wub[...] = wu_ref[...].astype(BF16)
        wdb[...] = wd_ref[...].astype(BF16)

    @pl.when(nv > 0)
    def _():
        wait_gather(slot)
        words = _slabs_to_rows(xbuf, slot * tb * xrows, tb, xrows)
        x = _unpack_bf16_pairs(words).astype(BF16)
        has_prev = b > 0
        for r in range(tb):
            gather_row(b + 1, 1 - slot, r, r)
            dst = jnp.where(has_prev, code_ref[prev * tb + r], 2 * n_tok + r)
            scatter_row(prev, 1 - slot, r, dst, r)
        hdn = _silu(_dot(x, wgb[...])) * _dot(x, wub[...])
        y = _dot(hdn.astype(BF16), wdb[...])
        _rows_to_slabs(ybuf, slot * tb * yrows, _pack_bf16_pairs(y), yrows)
        wait_scatter(1 - slot)

    @pl.when((nv == 0) & (nv_prev > 0))
    def _():
        wait_gather(slot)
        row_loop(lambda r, k: scatter_row(prev, 1 - slot, r, code_ref[prev * tb + r], k))
        wait_scatter(1 - slot)


def _ffn(blk_e, nv, code, u2d, wg, wu, wd, layer):
    d, de = wg.shape[2], wg.shape[3]
    xrows = yrows = _slab_rows(d // 2)
    n = u2d.shape[0] // xrows
    assert n & (n - 1) == 0, "token index is taken from the slot code by masking"
    nblk = blk_e.shape[0]
    tb = MOE_BLOCK
    grid_spec = pltpu.PrefetchScalarGridSpec(
        num_scalar_prefetch=3,
        grid=(nblk,),
        in_specs=[pl.BlockSpec(memory_space=pl.ANY),
                  pl.BlockSpec((None, None, d, de), lambda b, be, nv, cd: (layer, be[b], 0, 0)),
                  pl.BlockSpec((None, None, d, de), lambda b, be, nv, cd: (layer, be[b], 0, 0)),
                  pl.BlockSpec((None, None, de, d), lambda b, be, nv, cd: (layer, be[b], 0, 0))],
        out_specs=pl.BlockSpec(memory_space=pl.ANY),
        scratch_shapes=[pltpu.VMEM((2 * tb * xrows, LANE), jnp.uint32),
                        pltpu.VMEM((2 * tb * yrows, LANE), jnp.uint32),
                        pltpu.VMEM((d, de), BF16), pltpu.VMEM((d, de), BF16), pltpu.VMEM((de, d), BF16),
                        pltpu.SemaphoreType.DMA((2,)), pltpu.SemaphoreType.DMA((2,))],
    )
    return pl.pallas_call(
        functools.partial(_ffn_kernel, tb=tb, xrows=xrows, yrows=yrows),
        grid_spec=grid_spec,
        out_shape=jax.ShapeDtypeStruct(((2 * n + tb) * yrows, LANE), jnp.uint32),
        compiler_params=_cparams("arbitrary"),
        name="moe_ffn",
    )(blk_e, nv, code, u2d, wg, wu, wd)


def _invert_kernel(dest_ref, code_ref, *, tb):
    n_slots, p_total = dest_ref.shape[0], code_ref.shape[0]

    def init(i, c):
        base = n_slots + ((i * DMA_UNROLL) & (tb - 1))
        for k in range(DMA_UNROLL):
            code_ref[i * DMA_UNROLL + k] = base + k
        return c
    lax.fori_loop(0, p_total // DMA_UNROLL, init, 0)

    def put(i, c):
        for k in range(DMA_UNROLL):
            s = i * DMA_UNROLL + k
            code_ref[dest_ref[s]] = s
        return c
    lax.fori_loop(0, n_slots // DMA_UNROLL, put, 0)


def _invert(dest, p_total, tb):
    assert tb & (tb - 1) == 0
    return pl.pallas_call(
        functools.partial(_invert_kernel, tb=tb),
        in_specs=[pl.BlockSpec(memory_space=pltpu.SMEM)],
        out_specs=pl.BlockSpec(memory_space=pltpu.SMEM),
        out_shape=jax.ShapeDtypeStruct((p_total,), jnp.int32),
        name="invert_slots",
    )(dest)


def _dispatch_tables(route_t, counts, n):
    tb = MOE_BLOCK
    e = route_t[0:2, :].astype(jnp.int32)
    rank = route_t[4:6, :].astype(jnp.int32)
    cnt = counts[0, N_GROUPS:N_GROUPS + N_EXPERTS].astype(jnp.int32)
    padded = ((cnt + tb - 1) // tb) * tb
    pends = jnp.cumsum(padded)
    pstarts = pends - padded
    ids = jnp.arange(N_EXPERTS, dtype=jnp.int32)[:, None, None]
    start_of = jnp.sum(jnp.where(ids == e[None], pstarts[:, None, None], 0), axis=0)
    dest = (start_of + rank).reshape(-1)
    p_total = 2 * n + N_EXPERTS * tb
    nblk = p_total // tb
    code = _invert(dest, p_total, tb)
    bstart = jnp.arange(nblk, dtype=jnp.int32) * tb
    blk_e = jnp.minimum(jnp.sum((bstart[:, None] >= pends[None, :]).astype(jnp.int32), axis=1), N_EXPERTS - 1)
    of_blk = blk_e[:, None] == jnp.arange(N_EXPERTS, dtype=jnp.int32)[None, :]
    cnt_b = jnp.sum(jnp.where(of_blk, cnt[None, :], 0), axis=1)
    pstart_b = jnp.sum(jnp.where(of_blk, pstarts[None, :], 0), axis=1)
    nv = jnp.clip(cnt_b - (bstart - pstart_b), 0, tb)
    nv = jnp.where(bstart < pends[-1], nv, 0).astype(jnp.int32)
    return blk_e, nv, code


def _pad_lanes(v, width):
    return jnp.pad(v, ((0, 0), (0, width - v.shape[1])))


def kernel(x, attn_norm_w, w_in, ssd_conv_w, ssd_conv_b, ssd_dt_bias, ssd_a_log, ssd_d, ssd_norm_w, hgrn_lower_bounds, hgrn_norm_w, w_out, ffn_norm_w, router_group_w, router_group_b, router_expert_w, router_expert_b, expert_w_gate, expert_w_up, expert_w_down, final_norm_w):
    bsz, t, d = x.shape
    n = bsz * t
    depth = w_in.shape[0]

    half = HEAD_DIM // 2
    inv = 1.0 / (ROPE_THETA ** (jnp.arange(half, dtype=F32) / half))
    ang = jnp.arange(t, dtype=F32)[:, None] * inv[None, :]
    cos_t = jnp.concatenate([jnp.cos(ang), jnp.cos(ang)], axis=1)
    sin_t = jnp.concatenate([-jnp.sin(ang), jnp.sin(ang)], axis=1)

    def group_lanes(v):
        v = v.reshape(SSD_GROUPS, SSD_HEADS_PER_GROUP)
        return _pad_lanes(v, LANE).reshape(1, SSD_GROUPS * LANE)

    h = x.reshape(n, d)
    w_in_b = _wprep(jnp.swapaxes(w_in, 1, 2))
    moe = None
    for l in range(depth):
        if moe is None:
            (u,) = _norm_pass(h, attn_norm_w[l][None, :], None, BF16, False)
        else:
            u, h = _norm_pass(h, attn_norm_w[l][None, :], moe, BF16, True)
        p = _inproj(u, w_in_b, l)

        o_ret = _retention(p, cos_t, sin_t, bsz, t)
        o_ssd = _ssd(p, ssd_conv_w[l], ssd_conv_b[l][None, :], group_lanes(ssd_dt_bias[l]),
                     group_lanes(ssd_a_log[l]), jnp.repeat(ssd_d[l], SSD_HEADDIM)[None, :],
                     ssd_norm_w[l][None, :], bsz, t)
        o_hgrn = _hgrn(p, hgrn_lower_bounds, hgrn_norm_w[l][None, :], l, bsz, t)

        w_r = _pad_lanes(jnp.concatenate([router_group_w[l], router_expert_w[l]], axis=1), LANE)
        w_r_hi = w_r.astype(BF16)
        w_r = jnp.concatenate([w_r_hi, (w_r - w_r_hi.astype(F32)).astype(BF16)], axis=1)
        b_r = _pad_lanes(jnp.concatenate([router_group_b[l], router_expert_b[l]])[None, :], LANE)
        h, u2d, logits = _outproj(o_ret, o_ssd, o_hgrn, h, w_out[l].astype(BF16), ffn_norm_w[l][None, :], w_r, b_r)
        route, route_t, counts = _route(logits)
        blk_e, nv, code = _dispatch_tables(route_t, counts, n)
        o2d = _ffn(blk_e, nv, code, u2d, expert_w_gate, expert_w_up, expert_w_down, l)
        moe = (o2d, route)

    (out,) = _norm_pass(h, final_norm_w[None, :], moe, F32, False)
    return out.reshape(bsz, t, d)
```

```python
import functools
import math

import jax
import jax.numpy as jnp
from jax import lax
from jax.experimental import pallas as pl
from jax.experimental.pallas import tpu as pltpu

F32 = jnp.float32
BF16 = jnp.bfloat16
HIGHEST = lax.Precision.HIGHEST

V7X_VMEM_BYTES = 64 * 1024 * 1024
VMEM_LIMIT = V7X_VMEM_BYTES - 8 * 1024 * 1024
LANE = 128

EPS = 1e-6
LOG2E = 1.4426950408889634
LB_FLOOR = 1e-30
ROPE_THETA = 10000.0

RET_HEADS = 4
HEAD_DIM = 128
SSD_HEADDIM = 64
SSD_GROUPS = 2
SSD_HEADS_PER_GROUP = 8
SSD_CONV = 4
HGRN_HEADS = 4
CHUNK = 128
N_GROUPS = 4
EXPERTS_PER_GROUP = 8
N_EXPERTS = N_GROUPS * EXPERTS_PER_GROUP
MOE_BLOCK = 256
DMA_UNROLL = 8
ROUTE_TILE = 512
SSD_GROUP_WIDTH = SSD_HEADS_PER_GROUP * SSD_HEADDIM
P_Z = 4 * RET_HEADS * HEAD_DIM
P_X = P_Z + SSD_GROUPS * SSD_GROUP_WIDTH
P_B = P_X + SSD_GROUPS * SSD_GROUP_WIDTH
P_C = P_B + SSD_GROUPS * HEAD_DIM
DT_LO = P_C + SSD_GROUPS * HEAD_DIM
DT_HI = DT_LO + SSD_GROUPS * SSD_HEADS_PER_GROUP


def _cparams(*sem):
    return pltpu.CompilerParams(dimension_semantics=sem, vmem_limit_bytes=VMEM_LIMIT)


def _rms(x, w=None):
    y = x * lax.rsqrt(jnp.mean(x * x, axis=-1, keepdims=True) + EPS)
    return y if w is None else y * w


def _sigmoid(x):
    return 1.0 / (1.0 + jnp.exp(-x))


def _silu(x):
    return x * _sigmoid(x)


def _dot(a, b):
    return jnp.dot(a, b, preferred_element_type=F32)


def _dot_nt(a, b):
    return lax.dot_general(a, b, (((1,), (1,)), ((), ())), preferred_element_type=F32)


def _dot_tn(a, b):
    return lax.dot_general(a, b, (((0,), (0,)), ((), ())), preferred_element_type=F32)


def _tri_incl():
    ii = lax.broadcasted_iota(jnp.int32, (CHUNK, CHUNK), 0)
    jj = lax.broadcasted_iota(jnp.int32, (CHUNK, CHUNK), 1)
    return ii, jj


def _lane_spread_matrix(width, per_head):
    hh = lax.broadcasted_iota(jnp.int32, (LANE, width), 0)
    cc = lax.broadcasted_iota(jnp.int32, (LANE, width), 1)
    return jnp.where((cc >= hh * per_head) & (cc < (hh + 1) * per_head), 1.0, 0.0).astype(BF16)


def _spread(v, e, terms):
    out = None
    rest = v
    for i in range(terms):
        piece = rest.astype(BF16)
        part = _dot(piece, e)
        out = part if out is None else out + part
        if i + 1 < terms:
            rest = rest - piece.astype(F32)
    return out


def _slab_rows(d):
    return d // LANE


def _pack_bf16_pairs(v):
    half = v.shape[1] // 2
    bits = lax.bitcast_convert_type(v.astype(BF16).astype(F32), jnp.uint32)
    return (bits[:, :half] >> 16) | (bits[:, half:] & jnp.uint32(0xFFFF0000))


def _unpack_bf16_pairs(words):
    return jnp.concatenate([lax.bitcast_convert_type(words << 16, F32),
                            lax.bitcast_convert_type(words & jnp.uint32(0xFFFF0000), F32)], axis=1)


def _slabs_to_rows(ref2d, base, rows, srows):
    return jnp.concatenate([ref2d[pl.ds(base + c, rows, stride=srows), :] for c in range(srows)], axis=1)


def _rows_to_slabs(ref2d, base, val, srows):
    rows = val.shape[0]
    for c in range(srows):
        ref2d[pl.ds(base + c, rows, stride=srows), :] = val[:, c * LANE:(c + 1) * LANE]


WPREP_ROWS = 256


def _wprep_kernel(wt_hbm, out_ref, buf, sem):
    l, j = pl.program_id(0), pl.program_id(1)
    nj = pl.num_programs(1)
    step = l * nj + j
    slot = step % 2
    n_lo = DT_LO // WPREP_ROWS
    n_main = n_lo + (wt_hbm.shape[1] - DT_HI) // WPREP_ROWS

    def block_copy(st, sl):
        jj = st % nj
        src = jnp.where(jj < n_lo, jj * WPREP_ROWS,
                        jnp.where(jj < n_main, DT_HI + (jj - n_lo) * WPREP_ROWS, DT_LO))
        return pltpu.make_async_copy(wt_hbm.at[st // nj, pl.ds(pl.multiple_of(src, 8), WPREP_ROWS), :],
                                     buf.at[sl], sem.at[sl])

    @pl.when(step == 0)
    def _():
        block_copy(0, 0).start()

    @pl.when(step + 1 < pl.num_programs(0) * nj)
    def _():
        block_copy(step + 1, 1 - slot).start()

    block_copy(step, slot).wait()

    @pl.when(j < n_main)
    def _():
        out_ref[...] = buf[slot].astype(BF16)

    @pl.when(j >= n_main)
    def _():
        pad = jnp.zeros((LANE - SSD_HEADS_PER_GROUP, out_ref.shape[1]), F32)
        parts = []
        for g in range(SSD_GROUPS):
            parts += [buf[slot, g * SSD_HEADS_PER_GROUP:(g + 1) * SSD_HEADS_PER_GROUP, :], pad]
        out_ref[...] = jnp.concatenate(parts, axis=0).astype(BF16)


def _wprep(wt):
    depth, nin, d = wt.shape
    nout = nin - (DT_HI - DT_LO) + SSD_GROUPS * LANE
    assert DT_LO % WPREP_ROWS == 0 and (nin - DT_HI) % WPREP_ROWS == 0 and SSD_GROUPS * LANE == WPREP_ROWS
    return pl.pallas_call(
        _wprep_kernel,
        grid=(depth, nout // WPREP_ROWS),
        in_specs=[pl.BlockSpec(memory_space=pl.ANY)],
        out_specs=pl.BlockSpec((None, WPREP_ROWS, d), lambda l, j: (l, j, 0)),
        out_shape=jax.ShapeDtypeStruct((depth, nout, d), BF16),
        scratch_shapes=[pltpu.VMEM((2, WPREP_ROWS, d), F32), pltpu.SemaphoreType.DMA((2,))],
        compiler_params=_cparams("arbitrary", "arbitrary"),
        name="wprep",
    )(wt)


def _norm_kernel(*refs, combine, write_h):
    if combine:
        h_ref, o0_ref, o1_ref, rt_ref, nw_ref = refs[:5]
        outs = refs[5:]
        rows = h_ref.shape[0]
        srows = _slab_rows(h_ref.shape[1] // 2)
        rt = rt_ref[...]
        h = (h_ref[...] + rt[:, 2:3] * _unpack_bf16_pairs(_slabs_to_rows(o0_ref, 0, rows, srows))
             + rt[:, 3:4] * _unpack_bf16_pairs(_slabs_to_rows(o1_ref, 0, rows, srows)))
    else:
        h_ref, nw_ref = refs[:2]
        outs = refs[2:]
        h = h_ref[...]
    u_ref = outs[0]
    u_ref[...] = _rms(h, nw_ref[...]).astype(u_ref.dtype)
    if write_h:
        outs[1][...] = h


def _norm_pass(h, nw, moe, out_dtype, write_h):
    n, d = h.shape
    combine = moe is not None
    tm = min(512, n)
    row = lambda i: (i, 0)
    in_specs = [pl.BlockSpec((tm, d), row)]
    args = [h]
    if combine:
        o2d, rt = moe
        srows = _slab_rows(d // 2)
        nt = n // tm
        in_specs += [pl.BlockSpec((tm * srows, LANE), row),
                     pl.BlockSpec((tm * srows, LANE), lambda i: (nt + i, 0)),
                     pl.BlockSpec((tm, LANE), row)]
        args += [o2d, o2d, rt]
    in_specs.append(pl.BlockSpec((1, d), lambda i: (0, 0)))
    args.append(nw)
    out_shape = [jax.ShapeDtypeStruct((n, d), out_dtype)]
    out_specs = [pl.BlockSpec((tm, d), row)]
    if write_h:
        out_shape.append(jax.ShapeDtypeStruct((n, d), F32))
        out_specs.append(pl.BlockSpec((tm, d), row))
    return pl.pallas_call(
        functools.partial(_norm_kernel, combine=combine, write_h=write_h),
        grid=(n // tm,),
        in_specs=in_specs, out_specs=out_specs, out_shape=out_shape,
        compiler_params=_cparams("arbitrary"),
        name="norm_pass",
    )(*args)


def _inproj_kernel(u_ref, wt_ref, p_ref):
    p_ref[...] = _dot_nt(u_ref[...], wt_ref[...])


def _inproj(u, wt, layer):
    n, d = u.shape
    np_ = wt.shape[1]
    tm = min(1024, n)
    tn = np_ // 3
    return pl.pallas_call(
        _inproj_kernel,
        grid=(np_ // tn, n // tm),
        in_specs=[pl.BlockSpec((tm, d), lambda j, i: (i, 0)),
                  pl.BlockSpec((None, tn, d), lambda j, i: (layer, j, 0))],
        out_specs=pl.BlockSpec((tm, tn), lambda j, i: (i, j)),
        out_shape=jax.ShapeDtypeStruct((n, np_), F32),
        compiler_params=_cparams("arbitrary", "arbitrary"),
        name="inproj",
    )(u, wt)


def _ret_kernel(q_ref, k_ref, v_ref, g_ref, cos_ref, sin_ref, o_ref, s_scr):
    t = q_ref.shape[0]
    ii, jj = _tri_incl()
    causal = ii >= jj
    dist = (ii - jj).astype(F32)
    iif = ii.astype(F32)
    consts = []
    for hd in range(RET_HEADS):
        lg = math.log(1.0 - 2.0 ** (-5.0 - hd))
        dmat = jnp.where(causal, jnp.exp(jnp.where(causal, dist * lg, 0.0)), 0.0)
        ecum = jnp.exp((iif + 1.0) * lg)
        wk = jnp.exp((CHUNK - 1.0 - iif) * lg)
        consts.append((dmat, ecum, wk, math.exp(CHUNK * lg)))
    scale = HEAD_DIM ** -0.5
    s_scr[...] = jnp.zeros_like(s_scr)

    def step(c, carry):
        r = pl.ds(pl.multiple_of(c * CHUNK, CHUNK), CHUNK)
        cs, sn = cos_ref[r, :], sin_ref[r, :]
        for hd in range(RET_HEADS):
            dmat, ecum, wk, elast = consts[hd]
            cols = slice(hd * HEAD_DIM, (hd + 1) * HEAD_DIM)
            q, k = q_ref[r, cols], k_ref[r, cols]
            qr = q * cs + pltpu.roll(q, HEAD_DIM // 2, 1) * sn
            kr = (k * cs + pltpu.roll(k, HEAD_DIM // 2, 1) * sn) * scale
            vb = v_ref[r, cols].astype(BF16)
            s = s_scr[hd]
            scores = _dot_nt(qr.astype(BF16), kr.astype(BF16)) * dmat
            out = _dot(scores.astype(BF16), vb) + _dot((qr * ecum).astype(BF16), s.astype(BF16))
            s_scr[hd] = elast * s + _dot_tn((kr * wk).astype(BF16), vb)
            o_ref[r, cols] = (_silu(g_ref[r, cols]) * _rms(out)).astype(o_ref.dtype)
        return carry

    lax.fori_loop(0, t // CHUNK, step, 0, unroll=8)


def _retention(p, cos_t, sin_t, bsz, t):
    n = bsz * t
    width = RET_HEADS * HEAD_DIM
    blk = lambda off: pl.BlockSpec((t, width), lambda b, off=off: (b, off))
    tab = pl.BlockSpec((t, HEAD_DIM), lambda b: (0, 0))
    return pl.pallas_call(
        _ret_kernel,
        grid=(bsz,),
        in_specs=[blk(0), blk(1), blk(2), blk(3), tab, tab],
        out_specs=pl.BlockSpec((t, width), lambda b: (b, 0)),
        out_shape=jax.ShapeDtypeStruct((n, width), BF16),
        scratch_shapes=[pltpu.VMEM((RET_HEADS, HEAD_DIM, HEAD_DIM), F32)],
        compiler_params=_cparams("arbitrary"),
        name="retention",
    )(p, p, p, p, cos_t, sin_t)


def _ssd_kernel(z_ref, x_ref, b_ref, c_ref, dt_ref, cwx_ref, cwb_ref, cwc_ref, cbx_ref, cbb_ref, cbc_ref,
                dtb_ref, alog_ref, dsk_ref, nw_ref, o_ref, s_scr):
    t = z_ref.shape[0]
    ii, jj = _tri_incl()
    causal = ii >= jj
    tri = jnp.where(causal, 1.0, 0.0).astype(F32)
    lane_lo = lax.broadcasted_iota(jnp.int32, (CHUNK, LANE), 1) < SSD_HEADDIM
    neg_a = -jnp.exp(alog_ref[...])
    gw = x_ref.shape[1]
    e_pairs = _lane_spread_matrix(SSD_HEADS_PER_GROUP * CHUNK, CHUNK)
    e_heads = _lane_spread_matrix(gw, SSD_HEADDIM)
    s_scr[...] = jnp.zeros_like(s_scr)

    def conv(ref, w_ref, bias_ref, c):
        r = pl.ds(pl.multiple_of(c * CHUNK, CHUNK), CHUNK)
        rp = pl.ds(pl.multiple_of(jnp.maximum(c * CHUNK - 8, 0), 8), 8)
        cur = ref[r, :]
        ext = jnp.concatenate([jnp.where(c > 0, ref[rp, :], 0.0), cur], axis=0)
        w = w_ref[...]
        acc = bias_ref[...] + w[SSD_CONV - 1:SSD_CONV, :] * cur
        for lag in range(1, SSD_CONV):
            shifted = pltpu.roll(ext, lag, 0)[8:8 + CHUNK, :]
            acc = acc + w[SSD_CONV - 1 - lag:SSD_CONV - lag, :] * shifted
        return _silu(acc)

    def step(c, carry):
        r = pl.ds(pl.multiple_of(c * CHUNK, CHUNK), CHUNK)
        xs = conv(x_ref, cwx_ref, cbx_ref, c)
        bm = conv(b_ref, cwb_ref, cbb_ref, c).astype(BF16)
        cm = conv(c_ref, cwc_ref, cbc_ref, c).astype(BF16)
        xr = dt_ref[r, :] + dtb_ref[...]
        dt = jnp.maximum(xr, 0.0) + jnp.log1p(jnp.exp(-jnp.abs(xr)))
        la = dt * neg_a
        cum = jnp.dot(tri, la, precision=HIGHEST, preferred_element_type=F32) * LOG2E
        last = cum[CHUNK - 1:CHUNK, :]
        wj = dt * jnp.exp2(last - cum)
        ecum = jnp.exp2(cum)
        cum_t = cum.T
        dt_t = dt.T
        gm = jnp.where(causal, _dot_nt(cm, bm), 0.0)
        s = s_scr[...]
        cs = _dot(cm, s.astype(BF16))
        cix = _spread(cum, e_pairs, 3)
        ecx = _spread(ecum, e_heads, 2)
        wjx = _spread(wj, e_heads, 2)
        ys = []
        for pr in range(SSD_HEADS_PER_GROUP // 2):
            xp = xs[:, pr * LANE:(pr + 1) * LANE]
            acc = None
            for half in range(2):
                hd = 2 * pr + half
                ci = cix[:, hd * CHUNK:(hd + 1) * CHUNK]
                dec = jnp.exp2(jnp.minimum(ci - cum_t[hd:hd + 1, :], 0.0))
                m = gm * dec * dt_t[hd:hd + 1, :]
                sel = lane_lo if half == 0 else jnp.logical_not(lane_lo)
                y = _dot(m.astype(BF16), jnp.where(sel, xp, 0.0).astype(BF16))
                acc = y if acc is None else acc + y
            ys.append(acc)
        y = jnp.concatenate(ys, axis=1) + ecx * cs
        s_scr[...] = ecx[CHUNK - 1:CHUNK, :] * s + _dot_tn(bm, (wjx * xs).astype(BF16))
        y = (y + dsk_ref[...] * xs) * _silu(z_ref[r, :])
        o_ref[r, :] = (_rms(y) * nw_ref[...]).astype(o_ref.dtype)
        return carry

    lax.fori_loop(0, t // CHUNK, step, 0, unroll=8)


def _ssd(p, conv_w, conv_b, dtb, alog, dsk, nw, bsz, t):
    n = bsz * t
    gw = SSD_GROUP_WIDTH
    st = HEAD_DIM
    z_off, x_off = P_Z // gw, P_X // gw
    b_off, c_off = P_B // st, P_C // st
    dt_off = (p.shape[1] - SSD_GROUPS * LANE) // LANE
    xw = SSD_GROUPS * gw
    par = lambda shape, f: pl.BlockSpec(shape, f)
    in_specs = [
        par((t, gw), lambda b, g: (b, z_off + g)),
        par((t, gw), lambda b, g: (b, x_off + g)),
        par((t, st), lambda b, g: (b, b_off + g)),
        par((t, st), lambda b, g: (b, c_off + g)),
        par((t, LANE), lambda b, g: (b, dt_off + g)),
        par((SSD_CONV, gw), lambda b, g: (0, g)),
        par((SSD_CONV, st), lambda b, g: (0, xw // st + g)),
        par((SSD_CONV, st), lambda b, g: (0, xw // st + SSD_GROUPS + g)),
        par((1, gw), lambda b, g: (0, g)),
        par((1, st), lambda b, g: (0, xw // st + g)),
        par((1, st), lambda b, g: (0, xw // st + SSD_GROUPS + g)),
        par((1, LANE), lambda b, g: (0, g)),
        par((1, LANE), lambda b, g: (0, g)),
        par((1, gw), lambda b, g: (0, g)),
        par((1, gw), lambda b, g: (0, g)),
    ]
    return pl.pallas_call(
        _ssd_kernel,
        grid=(bsz, SSD_GROUPS),
        in_specs=in_specs,
        out_specs=pl.BlockSpec((t, gw), lambda b, g: (b, g)),
        out_shape=jax.ShapeDtypeStruct((n, SSD_GROUPS * gw), BF16),
        scratch_shapes=[pltpu.VMEM((st, gw), F32)],
        compiler_params=_cparams("arbitrary", "arbitrary"),
        name="ssd",
    )(p, p, p, p, p, conv_w, conv_w, conv_w, conv_b, conv_b, conv_b, dtb, alog, dsk, nw)


def _hgrn_kernel(q_ref, f_ref, i_ref, g_ref, lb_ref, nw_ref, o_ref, st_scr, cum_scr, *, layer):
    t = q_ref.shape[0]
    lbm = lb_ref[...]
    depth = lbm.shape[0]
    mx = lbm[0:1, :]
    for i in range(1, depth):
        mx = jnp.maximum(mx, lbm[i:i + 1, :])
    ex = [jnp.exp(lbm[i:i + 1, :] - mx) for i in range(depth)]
    den = ex[0]
    for i in range(1, depth):
        den = den + ex[i]
    sm = [e / den for e in ex]
    csum = sm[0]
    for i in range(1, layer + 1):
        csum = csum + sm[i]
    lb = jnp.maximum(csum - sm[0], 0.0)
    lb_floor = jnp.maximum(lb, LB_FLOOR)
    oml = 1.0 - lb

    ii, jj = _tri_incl()
    tri = jnp.where(ii >= jj, 1.0, 0.0).astype(F32)
    lvl = jnp.where(ii > jj, 31 - lax.clz(ii ^ jj), -1)
    eye = ii == jj
    width = q_ref.shape[1]
    hcols = [slice(c0, c0 + HEAD_DIM) for c0 in range(0, width, HEAD_DIM)]
    row = lax.broadcasted_iota(jnp.int32, (CHUNK, width), 0)
    scale = HEAD_DIM ** -0.5
    nlev = int(math.log2(CHUNK))
    st_scr[...] = jnp.zeros_like(st_scr)

    def step(c, carry):
        r = pl.ds(pl.multiple_of(c * CHUNK, CHUNK), CHUNK)
        q = _silu(q_ref[r, :]) * scale
        f = f_ref[r, :]
        ef = jnp.exp(-jnp.abs(f))
        big = 1.0 / (1.0 + ef)
        small = ef * big
        sig = jnp.where(f >= 0.0, big, small)
        sig_m = jnp.where(f >= 0.0, small, big)
        lf = jnp.log(lb_floor + oml * sig)
        kk = oml * sig_m
        vb = i_ref[r, :].astype(BF16)
        cum = jnp.dot(tri, lf, precision=HIGHEST, preferred_element_type=F32) * LOG2E
        cum_scr[...] = cum
        qb, kb = q.astype(BF16), kk.astype(BF16)
        scores = [jnp.where(eye, _dot_nt(qb[:, cs], kb[:, cs]), 0.0) for cs in hcols]
        for lv in range(nlev):
            s = 1 << lv
            if 2 * s >= 8:
                ref = jnp.concatenate(
                    [jnp.broadcast_to(cum_scr[g0 * 2 * s + s - 1:g0 * 2 * s + s, :], (2 * s, width))
                     for g0 in range(CHUNK // (2 * s))], axis=0)
            elif s == 2:
                m4 = row & 3
                ref = jnp.where(m4 == 0, pltpu.roll(cum, CHUNK - 1, 0),
                                jnp.where(m4 == 1, cum,
                                          jnp.where(m4 == 2, pltpu.roll(cum, 1, 0), pltpu.roll(cum, 2, 0))))
            else:
                ref = jnp.where((row & 1) == 1, pltpu.roll(cum, 1, 0), cum)
            dlt = cum - ref
            e = jnp.exp2(jnp.minimum(dlt, -dlt))
            qs = (q * e).astype(BF16)
            ks = (kk * e).astype(BF16)
            scores = [jnp.where(lvl == lv, _dot_nt(qs[:, cs], ks[:, cs]), sc) for cs, sc in zip(hcols, scores)]
        last = cum[CHUNK - 1:CHUNK, :]
        qe = (q * jnp.exp2(cum)).astype(BF16)
        kw = (kk * jnp.exp2(last - cum)).astype(BF16)
        elast = jnp.exp2(last)
        outs = []
        for hd, cs in enumerate(hcols):
            st = st_scr[hd]
            out = _dot(scores[hd].astype(BF16), vb[:, cs]) + _dot_nt(qe[:, cs], st.astype(BF16))
            st_scr[hd] = st * elast[:, cs] + _dot_tn(vb[:, cs], kw[:, cs])
            outs.append(_rms(out))
        o = jnp.concatenate(outs, axis=1) * nw_ref[...]
        o_ref[r, :] = (_silu(g_ref[r, :]) * o).astype(o_ref.dtype)
        return carry

    lax.fori_loop(0, t // CHUNK, step, 0, unroll=4)


def _hgrn(p, lbounds, nw, layer, bsz, t):
    n = bsz * t
    width = HGRN_HEADS * HEAD_DIM
    base = DT_LO // width
    blk = lambda off: pl.BlockSpec((t, width), lambda b, off=off: (b, base + off))
    return pl.pallas_call(
        functools.partial(_hgrn_kernel, layer=layer),
        grid=(bsz,),
        in_specs=[blk(0), blk(1), blk(2), blk(3),
                  pl.BlockSpec((lbounds.shape[0], width), lambda b: (0, 0)),
                  pl.BlockSpec((1, width), lambda b: (0, 0))],
        out_specs=pl.BlockSpec((t, width), lambda b: (b, 0)),
        out_shape=jax.ShapeDtypeStruct((n, width), BF16),
        scratch_shapes=[pltpu.VMEM((HGRN_HEADS, HEAD_DIM, HEAD_DIM), F32), pltpu.VMEM((CHUNK, width), F32)],
        compiler_params=_cparams("arbitrary"),
        name="hgrn2",
    )(p, p, p, p, lbounds, nw)


def _outproj_kernel(a_ref, b_ref, c_ref, h_ref, w_ref, nw_ref, wr_ref, br_ref, hout_ref, u_ref, lg_ref):
    xrows = _slab_rows(h_ref.shape[1] // 2)
    half = h_ref.shape[0] // 2
    accs = []
    for r0 in (0, half):
        rs = slice(r0, r0 + half)
        mix = jnp.concatenate([a_ref[rs, :], b_ref[rs, :], c_ref[rs, :]], axis=1)
        accs.append(_dot(mix, w_ref[...]))
    for r0, acc in zip((0, half), accs):
        rs = slice(r0, r0 + half)
        h = h_ref[rs, :] + acc
        hout_ref[rs, :] = h
        u = _rms(h, nw_ref[...])
        u_hi = u.astype(BF16)
        u_lo = (u - u_hi.astype(F32)).astype(BF16)
        t2 = _dot(u_hi, wr_ref[...])
        lg_ref[rs, :] = t2[:, :LANE] + t2[:, LANE:] + _dot(u_lo, wr_ref[:, :LANE]) + br_ref[...]
        _rows_to_slabs(u_ref, r0 * xrows, _pack_bf16_pairs(u), xrows)


def _outproj(o_a, o_b, o_c, h, w_out, nw, w_r, b_r):
    n, d = h.shape
    tm = min(512, n)
    srows = _slab_rows(d // 2)
    row = lambda i: (i, 0)
    fix = lambda i: (0, 0)
    return pl.pallas_call(
        _outproj_kernel,
        grid=(n // tm,),
        in_specs=[pl.BlockSpec((tm, o_a.shape[1]), row), pl.BlockSpec((tm, o_b.shape[1]), row),
                  pl.BlockSpec((tm, o_c.shape[1]), row), pl.BlockSpec((tm, d), row),
                  pl.BlockSpec(w_out.shape, fix), pl.BlockSpec((1, d), fix),
                  pl.BlockSpec(w_r.shape, fix), pl.BlockSpec((1, LANE), fix)],
        out_specs=[pl.BlockSpec((tm, d), row), pl.BlockSpec((tm * srows, LANE), row),
                   pl.BlockSpec((tm, LANE), row)],
        out_shape=[jax.ShapeDtypeStruct((n, d), F32), jax.ShapeDtypeStruct((n * srows, LANE), jnp.uint32),
                   jax.ShapeDtypeStruct((n, LANE), F32)],
        compiler_params=_cparams("arbitrary"),
        name="outproj",
    )(o_a, o_b, o_c, h, w_out, nw, w_r, b_r)


def _route_kernel(lg_ref, rt_ref, rtt_ref, cnt_ref, carry_scr):
    tr = lg_ref.shape[0]

    @pl.when(pl.program_id(0) == 0)
    def _():
        carry_scr[...] = jnp.zeros_like(carry_scr)

    lg = lg_ref[...]
    lane = lax.broadcasted_iota(jnp.int32, (tr, LANE), 1)
    neg = -jnp.inf
    big = jnp.int32(LANE)

    def first_max(vals):
        m = jnp.max(vals, axis=-1, keepdims=True)
        idx = jnp.min(jnp.where(vals == m, lane, big), axis=-1, keepdims=True)
        return m, idx

    gl = jnp.where(lane < N_GROUPS, lg, neg)
    gmax, gidx = first_max(gl)
    gate = 1.0 / jnp.sum(jnp.exp(gl - gmax), axis=-1, keepdims=True)
    lo = N_GROUPS + EXPERTS_PER_GROUP * gidx
    el = jnp.where((lane >= lo) & (lane < lo + EXPERTS_PER_GROUP), lg, neg)
    m1, i1 = first_max(el)
    m2, i2 = first_max(jnp.where(lane == i1, neg, el))
    e21 = jnp.exp(m2 - m1)
    w1 = gate * (1.0 / (1.0 + e21))
    w2 = gate * (e21 / (1.0 + e21))

    oh1 = lane == i1
    oh2 = lane == i2
    m = jnp.where(oh1 | oh2, 1.0, 0.0)
    ti = lax.broadcasted_iota(jnp.int32, (tr, tr), 0)
    tj = lax.broadcasted_iota(jnp.int32, (tr, tr), 1)
    before = _dot(jnp.where(ti > tj, 1.0, 0.0).astype(BF16), m.astype(BF16)) + carry_scr[0:1, :]
    r1 = jnp.sum(jnp.where(oh1, before, 0.0), axis=-1, keepdims=True)
    r2 = jnp.sum(jnp.where(oh2, before, 0.0), axis=-1, keepdims=True)
    total = carry_scr[0:1, :] + jnp.sum(m, axis=0, keepdims=True)
    carry_scr[...] = jnp.broadcast_to(total, carry_scr.shape)
    cnt_ref[...] = jnp.broadcast_to(total, cnt_ref.shape)

    e1 = (i1 - N_GROUPS).astype(F32)
    e2 = (i2 - N_GROUPS).astype(F32)
    out = jnp.zeros((tr, LANE), F32)
    for pos, val in enumerate((e1, e2, w1, w2, r1, r2)):
        out = jnp.where(lane == pos, val, out)
    rt_ref[...] = out
    rtt_ref[...] = out.T[0:rtt_ref.shape[0], :]


def _route(logits):
    n = logits.shape[0]
    tr = min(ROUTE_TILE, n)
    return pl.pallas_call(
        _route_kernel,
        grid=(n // tr,),
        in_specs=[pl.BlockSpec((tr, LANE), lambda i: (i, 0))],
        out_specs=[pl.BlockSpec((tr, LANE), lambda i: (i, 0)), pl.BlockSpec((8, tr), lambda i: (0, i)),
                   pl.BlockSpec((8, LANE), lambda i: (0, 0))],
        out_shape=[jax.ShapeDtypeStruct((n, LANE), F32), jax.ShapeDtypeStruct((8, n), F32),
                   jax.ShapeDtypeStruct((8, LANE), F32)],
        scratch_shapes=[pltpu.VMEM((8, LANE), F32)],
        compiler_params=_cparams("arbitrary"),
        name="route",
    )(logits)


def _ffn_kernel(blk_e_ref, nv_ref, code_ref, u_hbm, wg_ref, wu_ref, wd_ref, o_hbm,
                xbuf, ybuf, wgb, wub, wdb, gsem, ssem, *, tb, xrows, yrows):
    b = pl.program_id(0)
    n_tok = u_hbm.shape[0] // xrows
    slot = b % 2
    prev = jnp.maximum(b - 1, 0)
    nv = nv_ref[b]
    nv_prev = jnp.where(b > 0, nv_ref[prev], 0)

    def slab(ref, srows, idx, count=1):
        return ref.at[pl.ds(pl.multiple_of(idx * srows, srows), count * srows)]

    def gather_row(blk, sl, r, lane):
        tok = code_ref[blk * tb + r] & (n_tok - 1)
        pltpu.make_async_copy(slab(u_hbm, xrows, tok), slab(xbuf, xrows, sl * tb + r),
                              gsem.at[sl]).start(priority=lane % 2)

    def scatter_row(blk, sl, r, dst, lane):
        pltpu.make_async_copy(slab(ybuf, yrows, sl * tb + r), slab(o_hbm, yrows, dst),
                              ssem.at[sl]).start(priority=lane % 2)

    def wait_gather(sl):
        pltpu.make_async_copy(slab(u_hbm, xrows, 0, tb), slab(xbuf, xrows, sl * tb, tb), gsem.at[sl]).wait()

    def wait_scatter(sl):
        pltpu.make_async_copy(slab(ybuf, yrows, sl * tb, tb), slab(o_hbm, yrows, 0, tb), ssem.at[sl]).wait()

    def row_loop(fn):
        def grp(i, c):
            for k in range(DMA_UNROLL):
                fn(i * DMA_UNROLL + k, k)
            return c
        lax.fori_loop(0, tb // DMA_UNROLL, grp, 0)

    @pl.when(b == 0)
    def _():
        ybuf[...] = jnp.zeros_like(ybuf)
        row_loop(lambda r, k: gather_row(0, 0, r, k))

    @pl.when((nv > 0) & ((b == 0) | (blk_e_ref[b] != blk_e_ref[prev])))
    def _():
        wgb[...] = wg_ref[...].astype(BF16)
        wub[...] = wu_ref[...].astype(BF16)
        wdb[...] = wd_ref[...].astype(BF16)

    @pl.when(nv > 0)
    def _():
        wait_gather(slot)
        words = _slabs_to_rows(xbuf, slot * tb * xrows, tb, xrows)
        x = _unpack_bf16_pairs(words).astype(BF16)
        has_prev = b > 0
        for r in range(tb):
            gather_row(b + 1, 1 - slot, r, r)
            dst = jnp.where(has_prev, code_ref[prev * tb + r], 2 * n_tok + r)
            scatter_row(prev, 1 - slot, r, dst, r)
        hdn = _silu(_dot(x, wgb[...])) * _dot(x, wub[...])
        y = _dot(hdn.astype(BF16), wdb[...])
        _rows_to_slabs(ybuf, slot * tb * yrows, _pack_bf16_pairs(y), yrows)
        wait_scatter(1 - slot)

    @pl.when((nv == 0) & (nv_prev > 0))
    def _():
        wait_gather(slot)
        row_loop(lambda r, k: scatter_row(prev, 1 - slot, r, code_ref[prev * tb + r], k))
        wait_scatter(1 - slot)


def _ffn(blk_e, nv, code, u2d, wg, wu, wd, layer):
    d, de = wg.shape[2], wg.shape[3]
    xrows = yrows = _slab_rows(d // 2)
    n = u2d.shape[0] // xrows
    assert n & (n - 1) == 0, "token index is taken from the slot code by masking"
    nblk = blk_e.shape[0]
    tb = MOE_BLOCK
    grid_spec = pltpu.PrefetchScalarGridSpec(
        num_scalar_prefetch=3,
        grid=(nblk,),
        in_specs=[pl.BlockSpec(memory_space=pl.ANY),
                  pl.BlockSpec((None, None, d, de), lambda b, be, nv, cd: (layer, be[b], 0, 0)),
                  pl.BlockSpec((None, None, d, de), lambda b, be, nv, cd: (layer, be[b], 0, 0)),
                  pl.BlockSpec((None, None, de, d), lambda b, be, nv, cd: (layer, be[b], 0, 0))],
        out_specs=pl.BlockSpec(memory_space=pl.ANY),
        scratch_shapes=[pltpu.VMEM((2 * tb * xrows, LANE), jnp.uint32),
                        pltpu.VMEM((2 * tb * yrows, LANE), jnp.uint32),
                        pltpu.VMEM((d, de), BF16), pltpu.VMEM((d, de), BF16), pltpu.VMEM((de, d), BF16),
                        pltpu.SemaphoreType.DMA((2,)), pltpu.SemaphoreType.DMA((2,))],
    )
    return pl.pallas_call(
        functools.partial(_ffn_kernel, tb=tb, xrows=xrows, yrows=yrows),
        grid_spec=grid_spec,
        out_shape=jax.ShapeDtypeStruct(((2 * n + tb) * yrows, LANE), jnp.uint32),
        compiler_params=_cparams("arbitrary"),
        name="moe_ffn",
    )(blk_e, nv, code, u2d, wg, wu, wd)


def _invert_kernel(dest_ref, code_ref, *, tb):
    n_slots, p_total = dest_ref.shape[0], code_ref.shape[0]

    def init(i, c):
        base = n_slots + ((i * DMA_UNROLL) & (tb - 1))
        for k in range(DMA_UNROLL):
            code_ref[i * DMA_UNROLL + k] = base + k
        return c
    lax.fori_loop(0, p_total // DMA_UNROLL, init, 0)

    def put(i, c):
        for k in range(DMA_UNROLL):
            s = i * DMA_UNROLL + k
            code_ref[dest_ref[s]] = s
        return c
    lax.fori_loop(0, n_slots // DMA_UNROLL, put, 0)


def _invert(dest, p_total, tb):
    assert tb & (tb - 1) == 0
    return pl.pallas_call(
        functools.partial(_invert_kernel, tb=tb),
        in_specs=[pl.BlockSpec(memory_space=pltpu.SMEM)],
        out_specs=pl.BlockSpec(memory_space=pltpu.SMEM),
        out_shape=jax.ShapeDtypeStruct((p_total,), jnp.int32),
        name="invert_slots",
    )(dest)


def _dispatch_tables(route_t, counts, n):
    tb = MOE_BLOCK
    e = route_t[0:2, :].astype(jnp.int32)
    rank = route_t[4:6, :].astype(jnp.int32)
    cnt = counts[0, N_GROUPS:N_GROUPS + N_EXPERTS].astype(jnp.int32)
    padded = ((cnt + tb - 1) // tb) * tb
    pends = jnp.cumsum(padded)
    pstarts = pends - padded
    ids = jnp.arange(N_EXPERTS, dtype=jnp.int32)[:, None, None]
    start_of = jnp.sum(jnp.where(ids == e[None], pstarts[:, None, None], 0), axis=0)
    dest = (start_of + rank).reshape(-1)
    p_total = 2 * n + N_EXPERTS * tb
    nblk = p_total // tb
    code = _invert(dest, p_total, tb)
    bstart = jnp.arange(nblk, dtype=jnp.int32) * tb
    blk_e = jnp.minimum(jnp.sum((bstart[:, None] >= pends[None, :]).astype(jnp.int32), axis=1), N_EXPERTS - 1)
    of_blk = blk_e[:, None] == jnp.arange(N_EXPERTS, dtype=jnp.int32)[None, :]
    cnt_b = jnp.sum(jnp.where(of_blk, cnt[None, :], 0), axis=1)
    pstart_b = jnp.sum(jnp.where(of_blk, pstarts[None, :], 0), axis=1)
    nv = jnp.clip(cnt_b - (bstart - pstart_b), 0, tb)
    nv = jnp.where(bstart < pends[-1], nv, 0).astype(jnp.int32)
    return blk_e, nv, code


def _pad_lanes(v, width):
    return jnp.pad(v, ((0, 0), (0, width - v.shape[1])))


def kernel(x, attn_norm_w, w_in, ssd_conv_w, ssd_conv_b, ssd_dt_bias, ssd_a_log, ssd_d, ssd_norm_w, hgrn_lower_bounds, hgrn_norm_w, w_out, ffn_norm_w, router_group_w, router_group_b, router_expert_w, router_expert_b, expert_w_gate, expert_w_up, expert_w_down, final_norm_w):
    bsz, t, d = x.shape
    n = bsz * t
    depth = w_in.shape[0]

    half = HEAD_DIM // 2
    inv = 1.0 / (ROPE_THETA ** (jnp.arange(half, dtype=F32) / half))
    ang = jnp.arange(t, dtype=F32)[:, None] * inv[None, :]
    cos_t = jnp.concatenate([jnp.cos(ang), jnp.cos(ang)], axis=1)
    sin_t = jnp.concatenate([-jnp.sin(ang), jnp.sin(ang)], axis=1)

    def group_lanes(v):
        v = v.reshape(SSD_GROUPS, SSD_HEADS_PER_GROUP)
        return _pad_lanes(v, LANE).reshape(1, SSD_GROUPS * LANE)

    h = x.reshape(n, d)
    w_in_b = _wprep(jnp.swapaxes(w_in, 1, 2))
    moe = None
    for l in range(depth):
        if moe is None:
            (u,) = _norm_pass(h, attn_norm_w[l][None, :], None, BF16, False)
        else:
            u, h = _norm_pass(h, attn_norm_w[l][None, :], moe, BF16, True)
        p = _inproj(u, w_in_b, l)

        o_ret = _retention(p, cos_t, sin_t, bsz, t)
        o_ssd = _ssd(p, ssd_conv_w[l], ssd_conv_b[l][None, :], group_lanes(ssd_dt_bias[l]),
                     group_lanes(ssd_a_log[l]), jnp.repeat(ssd_d[l], SSD_HEADDIM)[None, :],
                     ssd_norm_w[l][None, :], bsz, t)
        o_hgrn = _hgrn(p, hgrn_lower_bounds, hgrn_norm_w[l][None, :], l, bsz, t)

        w_r = _pad_lanes(jnp.concatenate([router_group_w[l], router_expert_w[l]], axis=1), LANE)
        w_r_hi = w_r.astype(BF16)
        w_r = jnp.concatenate([w_r_hi, (w_r - w_r_hi.astype(F32)).astype(BF16)], axis=1)
        b_r = _pad_lanes(jnp.concatenate([router_group_b[l], router_expert_b[l]])[None, :], LANE)
        h, u2d, logits = _outproj(o_ret, o_ssd, o_hgrn, h, w_out[l].astype(BF16), ffn_norm_w[l][None, :], w_r, b_r)
        route, route_t, counts = _route(logits)
        blk_e, nv, code = _dispatch_tables(route_t, counts, n)
        o2d = _ffn(blk_e, nv, code, u2d, expert_w_gate, expert_w_up, expert_w_down, l)
        moe = (o2d, route)

    (out,) = _norm_pass(h, final_norm_w[None, :], moe, F32, False)
    return out.reshape(bsz, t, d)
```
